```python
import jax, jax.numpy as jnp
from jax import lax
import numpy as np

D_MODEL = 2048
BATCH = 4
SEQ = 2048
DEPTH = 1
DEC_BATCH = 2
DEC_SEQ = 4096
PAST_LEN = 128

HEAD_DIM = 128
N_Q_HEADS = 16
N_KV_HEADS = 4
WINDOW = 128
BLOCK = 128
ROPE_DIM = HEAD_DIM // 4
ROPE_THETA = 500000.0
N_REC_HEADS = 16
REC_KEY_DIM = 128
REC_VAL_DIM = 128
CHUNK = 32
D_FF = -(-8 * D_MODEL // (3 * 256)) * 256
RMS_EPS = 1e-6

D_ATT = N_Q_HEADS * HEAD_DIM
D_KV = N_KV_HEADS * HEAD_DIM
D_REC_K = N_REC_HEADS * REC_KEY_DIM
D_REC_V = N_REC_HEADS * REC_VAL_DIM
SPLIT_SIZES = (D_ATT, D_KV, D_KV, D_REC_K, D_REC_K, D_REC_K, D_REC_V, D_REC_V, D_MODEL, D_MODEL)
D_IN = sum(SPLIT_SIZES)
SPLIT_POINTS = [int(v) for v in np.cumsum(SPLIT_SIZES)[:-1]]

kernel_name = "hybrid_window_gqa_hgrn2_encoder"


def rms_norm(x, gain):
    xf = x.astype(jnp.float32)
    y = xf * lax.rsqrt(jnp.mean(xf * xf, axis=-1, keepdims=True) + RMS_EPS)
    return (y * gain.astype(jnp.float32)).astype(x.dtype)


def partial_rope(x):
    L = x.shape[1]
    half = ROPE_DIM // 2
    inv_freq = ROPE_THETA ** (-jnp.arange(half, dtype=jnp.float32) / half)
    ang = jnp.arange(L, dtype=jnp.float32)[:, None] * inv_freq[None, :]
    cos = jnp.cos(ang)[None, :, None, :]
    sin = jnp.sin(ang)[None, :, None, :]
    xf = x.astype(jnp.float32)
    x1 = xf[..., :half]
    x2 = xf[..., half:ROPE_DIM]
    out = jnp.concatenate([x1 * cos - x2 * sin, x2 * cos + x1 * sin, xf[..., ROPE_DIM:]], axis=-1)
    return out.astype(x.dtype)


def window_attention(q, k, v, sink):
    B, L = q.shape[0], q.shape[1]
    nb = L // BLOCK
    G = N_Q_HEADS // N_KV_HEADS
    q = partial_rope(q)
    k = partial_rope(k)
    qb = q.reshape(B, nb, BLOCK, N_KV_HEADS, G, HEAD_DIM)
    pad = ((0, 0), (BLOCK, BLOCK), (0, 0), (0, 0))
    key_idx = jnp.arange(nb)[:, None] * BLOCK + jnp.arange(3 * BLOCK)[None, :]
    kb = jnp.pad(k, pad)[:, key_idx]
    vb = jnp.pad(v, pad)[:, key_idx]
    s = jnp.einsum('bnqhgd,bnkhd->bnhgqk', qb, kb, preferred_element_type=jnp.float32) * (HEAD_DIM ** -0.5)
    q_pos = jnp.arange(nb)[:, None] * BLOCK + jnp.arange(BLOCK)[None, :]
    k_pos = (key_idx - BLOCK)[:, None, :]
    valid = (jnp.abs(q_pos[:, :, None] - k_pos) <= WINDOW) & (k_pos >= 0) & (k_pos < L)
    s = jnp.where(valid[None, :, None, None], s, -jnp.inf)
    sink_b = sink.astype(jnp.float32).reshape(1, 1, N_KV_HEADS, G, 1, 1)
    m = jnp.maximum(jnp.max(s, axis=-1, keepdims=True), sink_b)
    p = jnp.exp(s - m)
    denom = jnp.sum(p, axis=-1, keepdims=True) + jnp.exp(sink_b - m)
    p = (p / denom).astype(v.dtype)
    o = jnp.einsum('bnhgqk,bnkhd->bnqhgd', p, vb)
    return o.reshape(B, L, D_ATT)


def hgrn2_direction(q, k, logf, i):
    B, L, H, DK = q.shape
    DV = i.shape[-1]
    N = L // CHUNK
    qc = q.reshape(B, N, CHUNK, H, DK)
    kc = k.reshape(B, N, CHUNK, H, DK)
    ic = i.reshape(B, N, CHUNK, H, DV)
    b = jnp.cumsum(logf.reshape(B, N, CHUNK, H, DK), axis=2)
    b_ref = b[:, :, CHUNK // 2 - 1:CHUNK // 2]
    b_last = b[:, :, -1]
    q_in = qc * jnp.exp(b - b_ref)
    k_in = kc * jnp.exp(b_ref - b)
    A = jnp.einsum('bnthd,bnshd->bnhts', q_in, k_in)
    lower = jnp.tril(jnp.ones((CHUNK, CHUNK), dtype=bool))
    A = jnp.where(lower, A, 0.0)
    o_intra = jnp.einsum('bnhts,bnshv->bnthv', A, ic)
    k_state = kc * jnp.exp(b_last[:, :, None] - b)
    dS = jnp.einsum('bnshd,bnshv->bnhdv', k_state, ic)
    decay = jnp.exp(b_last)

    def step(S, inp):
        d, ds = inp
        return d[..., None] * S + ds, S

    S0 = jnp.zeros((B, H, DK, DV), dtype=jnp.float32)
    _, S_prev = lax.scan(step, S0, (jnp.moveaxis(decay, 1, 0), jnp.moveaxis(dS, 1, 0)))
    S_prev = jnp.moveaxis(S_prev, 0, 1)
    o_inter = jnp.einsum('bnthd,bnhdv->bnthv', qc * jnp.exp(b), S_prev)
    return (o_intra + o_inter).reshape(B, L, H, DV)


def bidirectional_hgrn2(q_r, zf_fwd, zf_bwd, i_r, lb):
    B, L = q_r.shape[0], q_r.shape[1]
    lbh = lb.reshape(N_REC_HEADS, REC_KEY_DIM)
    q = jax.nn.silu(q_r.astype(jnp.float32)).reshape(B, L, N_REC_HEADS, REC_KEY_DIM)
    i = i_r.astype(jnp.float32).reshape(B, L, N_REC_HEADS, REC_VAL_DIM)

    def gates(z):
        z = z.astype(jnp.float32).reshape(B, L, N_REC_HEADS, REC_KEY_DIM)
        f = lbh + (1.0 - lbh) * jax.nn.sigmoid(z)
        k = (1.0 - lbh) * jax.nn.sigmoid(-z)
        return k, jnp.log(f)

    k_f, logf_f = gates(zf_fwd)
    k_b, logf_b = gates(zf_bwd)
    o_fwd = hgrn2_direction(q, k_f, logf_f, i)
    flip = lambda t: jnp.flip(t, axis=1)
    o_bwd = flip(hgrn2_direction(flip(q), flip(k_b), flip(logf_b), flip(i)))
    return o_fwd + o_bwd


def encoder_layer(x, w_in, sink, rec_norm, lb, w_out, norm_mix_pre, norm_mix_post,
                  norm_ffn_pre, norm_ffn_post, w_gate, w_up, w_down):
    B, L, _ = x.shape
    xn = rms_norm(x, norm_mix_pre)
    proj = xn @ w_in
    q_a, k_a, v_a, q_r, zf_fwd, zf_bwd, i_r, g_r, gate_a, gate_r = jnp.split(proj, SPLIT_POINTS, axis=-1)
    attn = window_attention(q_a.reshape(B, L, N_Q_HEADS, HEAD_DIM),
                            k_a.reshape(B, L, N_KV_HEADS, HEAD_DIM),
                            v_a.reshape(B, L, N_KV_HEADS, HEAD_DIM), sink)
    rec = bidirectional_hgrn2(q_r, zf_fwd, zf_bwd, i_r, lb).astype(x.dtype)
    rec = rms_norm(rec, rec_norm).reshape(B, L, D_REC_V) * jax.nn.silu(g_r)
    merged = jax.nn.sigmoid(gate_a) * attn + jax.nn.sigmoid(gate_r) * rec
    h = x + rms_norm(merged @ w_out, norm_mix_post)
    hn = rms_norm(h, norm_ffn_pre)
    ffn = (jax.nn.silu(hn @ w_gate) * (hn @ w_up)) @ w_down
    return h + rms_norm(ffn, norm_ffn_post)


def setup_inputs(seed: int = 0) -> dict:
    key = jax.random.key(seed)
    ks = jax.random.split(key, 14)
    f32 = jnp.float32
    nrm = lambda k, shape, scale: jax.random.normal(k, shape, f32) * scale
    gain = lambda k, shape: 1.0 + 0.1 * jax.random.normal(k, shape, f32)
    return {
        "x_prompt": jax.random.normal(ks[0], (BATCH, SEQ, D_MODEL), f32),
        "x_sample": jax.random.normal(ks[1], (DEC_BATCH, DEC_SEQ, D_MODEL), f32),
        "w_in": nrm(ks[2], (DEPTH, D_MODEL, D_IN), D_MODEL ** -0.5),
        "sink": nrm(ks[3], (DEPTH, N_Q_HEADS), 0.5),
        "rec_norm": gain(ks[4], (DEPTH, N_REC_HEADS, REC_VAL_DIM)),
        "lb_logits": nrm(ks[5], (DEPTH + 1, D_REC_K), 0.1),
        "w_out": nrm(ks[6], (DEPTH, D_MODEL, D_MODEL), D_MODEL ** -0.5),
        "norm_mix_pre": gain(ks[7], (DEPTH, D_MODEL)),
        "norm_mix_post": gain(ks[8], (DEPTH, D_MODEL)),
        "norm_ffn_pre": gain(ks[9], (DEPTH, D_MODEL)),
        "norm_ffn_post": gain(ks[10], (DEPTH, D_MODEL)),
        "w_gate": nrm(ks[11], (DEPTH, D_MODEL, D_FF), D_MODEL ** -0.5),
        "w_up": nrm(ks[12], (DEPTH, D_MODEL, D_FF), D_MODEL ** -0.5),
        "w_down": nrm(ks[13], (DEPTH, D_FF, D_MODEL), D_FF ** -0.5),
    }


def reference(x_prompt, x_sample, w_in, sink, rec_norm, lb_logits, w_out, norm_mix_pre,
              norm_mix_post, norm_ffn_pre, norm_ffn_post, w_gate, w_up, w_down):
    lb_all = jnp.cumsum(jax.nn.softmax(lb_logits.astype(jnp.float32), axis=0), axis=0)
    y_prompt = x_prompt
    y_sample = x_sample
    for l in range(DEPTH):
        layer_args = (w_in[l], sink[l], rec_norm[l], lb_all[l], w_out[l], norm_mix_pre[l],
                      norm_mix_post[l], norm_ffn_pre[l], norm_ffn_post[l], w_gate[l], w_up[l], w_down[l])
        y_prompt = encoder_layer(y_prompt, *layer_args)
        y_sample = encoder_layer(y_sample, *layer_args)
    return (y_prompt, y_sample)
```

```python
import functools

import jax
import jax.numpy as jnp
import numpy as np
from jax import lax
from jax.experimental import pallas as pl
from jax.experimental.pallas import tpu as pltpu

F32 = jnp.float32
BF16 = jnp.bfloat16

D_MODEL = 2048
DEPTH = 1
HEAD_DIM = 128
N_Q_HEADS = 16
N_KV_HEADS = 4
GROUP = N_Q_HEADS // N_KV_HEADS
WINDOW = 128
BLOCK = 128
ROPE_DIM = HEAD_DIM // 4
ROPE_HALF = ROPE_DIM // 2
ROPE_THETA = 500000.0
N_REC_HEADS = 16
REC_KEY_DIM = 128
REC_VAL_DIM = 128
CHUNK = 32
D_FF = -(-8 * D_MODEL // (3 * 256)) * 256
RMS_EPS = 1e-6

D_ATT = N_Q_HEADS * HEAD_DIM
D_KV = N_KV_HEADS * HEAD_DIM
D_REC_K = N_REC_HEADS * REC_KEY_DIM
D_REC_V = N_REC_HEADS * REC_VAL_DIM
SPLIT_SIZES = (D_ATT, D_KV, D_KV, D_REC_K, D_REC_K, D_REC_K, D_REC_V, D_REC_V, D_MODEL, D_MODEL)
D_IN = sum(SPLIT_SIZES)
(OFF_Q, OFF_K, OFF_V, OFF_QR, OFF_ZF, OFF_ZB, OFF_IR, OFF_GR, OFF_GA, OFF_GT) = (
    int(v) for v in np.concatenate([[0], np.cumsum(SPLIT_SIZES)[:-1]]))

V7X_VMEM_CEILING = 56 * 1024 * 1024
LANES = 128

TM_IN, TN_IN = 1024, 1024
ROWS_EW = 256
TQ_ATT = 512
TM_OUT = 256
TM_FFN, TF_FFN = 512, 512


def _vmem_limit(nbytes):
    return int(min(V7X_VMEM_CEILING, nbytes * 1.25 + (4 << 20)))


def _sigmoid(x):
    return 1.0 / (1.0 + jnp.exp(-x))


def _dot_nt(a, b):
    return lax.dot_general(a, b, (((1,), (1,)), ((), ())), preferred_element_type=F32)


def _dot_tn(a, b):
    return lax.dot_general(a, b, (((0,), (0,)), ((), ())), preferred_element_type=F32)


def _inproj_kernel(x_ref, g_ref, cos_ref, sa_ref, sb_ref, w_ref, o_ref, xn_ref):
    j = pl.program_id(1)
    n_row_tiles = TM_IN // ROWS_EW

    @pl.when(j == 0)
    def _():
        def body(r, c):
            rows = pl.ds(pl.multiple_of(r * ROWS_EW, ROWS_EW), ROWS_EW)
            x = x_ref[rows, :]
            ms = jnp.mean(x * x, axis=-1, keepdims=True)
            xn_ref[rows, :] = ((x * lax.rsqrt(ms + RMS_EPS)) * g_ref[...]).astype(BF16)
            return c
        lax.fori_loop(0, n_row_tiles, body, 0)

    o_ref[...] = jnp.dot(xn_ref[...], w_ref[...], preferred_element_type=F32)

    def rope_heads(n_heads):
        def body(r, c):
            rows = pl.ds(pl.multiple_of(r * ROWS_EW, ROWS_EW), ROWS_EW)
            cos = cos_ref[rows, :]
            sa = sa_ref[rows, :]
            sb = sb_ref[rows, :]
            for hh in range(n_heads):
                cs = slice(hh * HEAD_DIM, (hh + 1) * HEAD_DIM)
                y = o_ref[rows, cs]
                o_ref[rows, cs] = (y * cos + pltpu.roll(y, HEAD_DIM - ROPE_HALF, 1) * sa
                                   + pltpu.roll(y, ROPE_HALF, 1) * sb)
            return c
        lax.fori_loop(0, n_row_tiles, body, 0)

    q_tiles = D_ATT // TN_IN

    @pl.when(j < q_tiles)
    def _():
        rope_heads(TN_IN // HEAD_DIM)

    @pl.when(j == q_tiles)
    def _():
        rope_heads(N_KV_HEADS)


def _in_projection(x, gain, cos_t, sa_t, sb_t, w_bf16):
    T = x.shape[0]
    assert OFF_K == (D_ATT // TN_IN) * TN_IN and D_KV <= TN_IN
    nbytes = (2 * TM_IN * D_MODEL * 4 + 2 * D_MODEL * TN_IN * 2 + 2 * TM_IN * TN_IN * 4
              + TM_IN * D_MODEL * 2 + 6 * TM_IN * LANES * 4)
    return pl.pallas_call(
        _inproj_kernel,
        grid=(T // TM_IN, D_IN // TN_IN),
        in_specs=[
            pl.BlockSpec((TM_IN, D_MODEL), lambda i, j: (i, 0)),
            pl.BlockSpec((1, D_MODEL), lambda i, j: (0, 0)),
            pl.BlockSpec((TM_IN, LANES), lambda i, j: (i, 0)),
            pl.BlockSpec((TM_IN, LANES), lambda i, j: (i, 0)),
            pl.BlockSpec((TM_IN, LANES), lambda i, j: (i, 0)),
            pl.BlockSpec((D_MODEL, TN_IN), lambda i, j: (0, j)),
        ],
        out_specs=pl.BlockSpec((TM_IN, TN_IN), lambda i, j: (i, j)),
        out_shape=jax.ShapeDtypeStruct((T, D_IN), F32),
        scratch_shapes=[pltpu.VMEM((TM_IN, D_MODEL), BF16)],
        compiler_params=pltpu.CompilerParams(
            dimension_semantics=("parallel", "arbitrary"),
            vmem_limit_bytes=_vmem_limit(nbytes)),
        name="in_projection",
    )(x, gain, cos_t, sa_t, sb_t, w_bf16)


def _attn_kernel(seq_tiles, sink_ref, q_ref, kc_ref, vc_ref, kp_ref, kn_ref, vp_ref, vn_ref,
                 ga_ref, o_ref):
    t = pl.program_id(0)
    h = pl.program_id(1)
    (p0, per_p), (s0, per_s) = seq_tiles
    local = jnp.where(t < s0, lax.rem(t - p0, per_p), lax.rem(t - s0, per_s))
    per = jnp.where(t < s0, per_p, per_s)
    lo = jnp.where(local != 0, 0, BLOCK)
    hi = jnp.where(local != per - 1, 3 * BLOCK, 2 * BLOCK)

    k_all = jnp.concatenate([kp_ref[...], kc_ref[...], kn_ref[...]], axis=0).astype(BF16)
    v_all = jnp.concatenate([vp_ref[...], vc_ref[...], vn_ref[...]], axis=0).astype(BF16)
    rows_q = GROUP * BLOCK
    qi = lax.broadcasted_iota(jnp.int32, (rows_q, 3 * BLOCK), 0) & (BLOCK - 1)
    kj = lax.broadcasted_iota(jnp.int32, (rows_q, 3 * BLOCK), 1)
    band = (kj >= qi) & (kj <= qi + 2 * WINDOW)
    sink_col = jnp.concatenate(
        [jnp.full((BLOCK, 1), sink_ref[GROUP * h + g], F32) for g in range(GROUP)], axis=0)
    scale = HEAD_DIM ** -0.5
    n_blocks = TQ_ATT // BLOCK
    for b in range(n_blocks):
        rs = slice(b * BLOCK, (b + 1) * BLOCK)
        q = jnp.concatenate(
            [q_ref[rs, g * HEAD_DIM:(g + 1) * HEAD_DIM] for g in range(GROUP)], axis=0).astype(BF16)
        kb = k_all[b * BLOCK:(b + 3) * BLOCK]
        vb = v_all[b * BLOCK:(b + 3) * BLOCK]
        s = _dot_nt(q, kb) * scale
        valid = band
        if b == 0:
            valid = valid & (kj >= lo)
        if b == n_blocks - 1:
            valid = valid & (kj < hi)
        s = jnp.where(valid, s, -jnp.inf)
        m = jnp.maximum(jnp.max(s, axis=-1, keepdims=True), sink_col)
        p = jnp.exp(s - m)
        denom = jnp.sum(p, axis=-1, keepdims=True) + jnp.exp(sink_col - m)
        o = jnp.dot(p.astype(BF16), vb, preferred_element_type=F32) * (1.0 / denom)
        for g in range(GROUP):
            cs = slice(g * HEAD_DIM, (g + 1) * HEAD_DIM)
            gate = _sigmoid(ga_ref[rs, cs])
            o_ref[rs, cs] = (gate * o[g * BLOCK:(g + 1) * BLOCK]).astype(o_ref.dtype)


def _window_attention(proj, sink, seq_tiles):
    T = proj.shape[0]
    n_tiles = T // TQ_ATT
    bpt = TQ_ATT // BLOCK
    last_blk = T // BLOCK - 1
    gw = GROUP * HEAD_DIM
    kcol, vcol, gcol = OFF_K // HEAD_DIM, OFF_V // HEAD_DIM, OFF_GA // gw
    nbytes = 2 * (2 * TQ_ATT * gw * 4 + 2 * TQ_ATT * HEAD_DIM * 4 + 4 * BLOCK * HEAD_DIM * 4
                  + TQ_ATT * gw * 2) + 8 * GROUP * BLOCK * 3 * BLOCK * 4
    return pl.pallas_call(
        functools.partial(_attn_kernel, seq_tiles),
        grid=(n_tiles, N_KV_HEADS),
        in_specs=[
            pl.BlockSpec(memory_space=pltpu.SMEM),
            pl.BlockSpec((TQ_ATT, gw), lambda t, h: (t, h)),
            pl.BlockSpec((TQ_ATT, HEAD_DIM), lambda t, h: (t, kcol + h)),
            pl.BlockSpec((TQ_ATT, HEAD_DIM), lambda t, h: (t, vcol + h)),
            pl.BlockSpec((BLOCK, HEAD_DIM), lambda t, h: (jnp.maximum(t * bpt - 1, 0), kcol + h)),
            pl.BlockSpec((BLOCK, HEAD_DIM), lambda t, h: (jnp.minimum((t + 1) * bpt, last_blk), kcol + h)),
            pl.BlockSpec((BLOCK, HEAD_DIM), lambda t, h: (jnp.maximum(t * bpt - 1, 0), vcol + h)),
            pl.BlockSpec((BLOCK, HEAD_DIM), lambda t, h: (jnp.minimum((t + 1) * bpt, last_blk), vcol + h)),
            pl.BlockSpec((TQ_ATT, gw), lambda t, h: (t, gcol + h)),
        ],
        out_specs=pl.BlockSpec((TQ_ATT, gw), lambda t, h: (t, h)),
        out_shape=jax.ShapeDtypeStruct((T, D_ATT), BF16),
        compiler_params=pltpu.CompilerParams(
            dimension_semantics=("parallel", "arbitrary"),
            vmem_limit_bytes=_vmem_limit(nbytes)),
        name="window_attention",
    )(sink, proj, proj, proj, proj, proj, proj, proj, proj)


def _hgrn_kernel(L, *refs):
    (lb_ref, gain_ref, qr_ref, zf_ref, zb_ref, ir_ref, gr_ref, gt_ref) = refs[:8]
    o_ref = refs[-13]
    (i_s, qin_f, kin_f, kst_f, qdc_f, qin_b, kin_b, kst_b, qdc_b, dec_f, dec_b, acc) = refs[-12:]
    R = ROWS_EW
    n_chunks = L // CHUNK
    cpt = R // CHUNK
    lb = lb_ref[0]
    one_m_lb = 1.0 - lb
    rin = lax.broadcasted_iota(jnp.int32, (R, LANES), 0) & (CHUNK - 1)

    def prep(r, c):
        rows = pl.ds(pl.multiple_of(r * R, R), R)
        qraw = qr_ref[rows, :]
        q3 = (qraw * _sigmoid(qraw)).reshape(cpt, CHUNK, LANES)
        i_s[rows, :] = ir_ref[rows, :].astype(BF16)
        acc[rows, :] = jnp.zeros((R, LANES), F32)
        for z_ref, fwd, (qin, kin, kst, qdc, dec) in (
                (zf_ref, True, (qin_f, kin_f, kst_f, qdc_f, dec_f)),
                (zb_ref, False, (qin_b, kin_b, kst_b, qdc_b, dec_b))):
            z = z_ref[rows, :]
            e = jnp.exp(-jnp.abs(z))
            inv = 1.0 / (1.0 + e)
            s_hi, s_lo = inv, e * inv
            sg = jnp.where(z >= 0, s_hi, s_lo)
            sgn = jnp.where(z >= 0, s_lo, s_hi)
            logf = jnp.log(lb + one_m_lb * sg)
            k3 = (one_m_lb * sgn).reshape(cpt, CHUNK, LANES)
            b = logf
            for s in (1, 2, 4, 8, 16):
                b = b + jnp.where(rin >= s, pltpu.roll(b, s, 0), 0.0)
            b3 = b.reshape(cpt, CHUNK, LANES)
            tot = b3[:, CHUNK - 1:CHUNK, :]
            if fwd:
                c3 = b3
                cref = b3[:, CHUNK // 2 - 1:CHUNK // 2, :]
            else:
                c3 = tot - b3 + logf.reshape(cpt, CHUNK, LANES)
                cref = c3[:, CHUNK // 2:CHUNK // 2 + 1, :]
            shape3 = (cpt, CHUNK, LANES)
            qin[rows, :] = (q3 * jnp.exp(c3 - cref)).reshape(R, LANES).astype(BF16)
            kin[rows, :] = (k3 * jnp.exp(cref - c3)).reshape(R, LANES).astype(BF16)
            kst[rows, :] = (k3 * jnp.exp(tot - c3)).reshape(R, LANES).astype(BF16)
            qdc[rows, :] = (q3 * jnp.exp(c3)).reshape(R, LANES).astype(BF16)
            dec[rows, :] = jnp.broadcast_to(jnp.exp(tot), shape3).reshape(R, LANES)
        return c

    lax.fori_loop(0, L // R, prep, 0)

    ti = lax.broadcasted_iota(jnp.int32, (CHUNK, CHUNK), 0)
    si = lax.broadcasted_iota(jnp.int32, (CHUNK, CHUNK), 1)

    def direction_step(rows, mask, st, qin, kin, kst, qdc, dec):
        i_c = i_s[rows, :]
        a = jnp.where(mask, _dot_nt(qin[rows, :], kin[rows, :]), 0.0)
        o = jnp.dot(a.astype(BF16), i_c, preferred_element_type=F32) + _dot_nt(qdc[rows, :], st.astype(BF16))
        acc[rows, :] += o
        return dec[pl.ds(rows.start, 1), :] * st + _dot_tn(i_c, kst[rows, :])

    def scan(n, carry):
        st_f, st_b = carry
        rows_f = pl.ds(pl.multiple_of(n * CHUNK, CHUNK), CHUNK)
        rows_b = pl.ds(pl.multiple_of((n_chunks - 1 - n) * CHUNK, CHUNK), CHUNK)
        st_f = direction_step(rows_f, ti >= si, st_f, qin_f, kin_f, kst_f, qdc_f, dec_f)
        st_b = direction_step(rows_b, ti <= si, st_b, qin_b, kin_b, kst_b, qdc_b, dec_b)
        return st_f, st_b

    zero_state = jnp.zeros((REC_VAL_DIM, REC_KEY_DIM), F32)
    lax.fori_loop(0, n_chunks, scan, (zero_state, zero_state))

    gain = gain_ref[0]

    def finish(r, c):
        rows = pl.ds(pl.multiple_of(r * R, R), R)
        rec = acc[rows, :]
        ms = jnp.mean(rec * rec, axis=-1, keepdims=True)
        y = (rec * lax.rsqrt(ms + RMS_EPS)) * gain
        g = gr_ref[rows, :]
        y = y * (g * _sigmoid(g))
        o_ref[rows, :] = (_sigmoid(gt_ref[rows, :]) * y).astype(o_ref.dtype)
        return c

    lax.fori_loop(0, L // R, finish, 0)


def _hgrn2(proj, lb3, gain3, L, row_blk0, n_seq, prev_out):
    T = proj.shape[0]
    col = lambda off: off // LANES

    def pspec(off):
        return pl.BlockSpec((L, LANES), lambda b, h: (row_blk0 + b, col(off) + h))

    in_specs = [
        pl.BlockSpec((1, 1, LANES), lambda b, h: (h, 0, 0)),
        pl.BlockSpec((1, 1, LANES), lambda b, h: (h, 0, 0)),
        pspec(OFF_QR), pspec(OFF_ZF), pspec(OFF_ZB), pspec(OFF_IR), pspec(OFF_GR), pspec(OFF_GT),
    ]
    args = [lb3, gain3, proj, proj, proj, proj, proj, proj]
    aliases = {}
    if prev_out is not None:
        in_specs.append(pl.BlockSpec(memory_space=pl.ANY))
        args.append(prev_out)
        aliases = {len(args) - 1: 0}
    nbytes = 2 * 6 * L * LANES * 4 + 9 * L * LANES * 2 + 3 * L * LANES * 4 + 2 * L * LANES * 2
    return pl.pallas_call(
        functools.partial(_hgrn_kernel, L),
        grid=(n_seq, N_REC_HEADS),
        in_specs=in_specs,
        out_specs=pl.BlockSpec((L, LANES), lambda b, h: (row_blk0 + b, h)),
        out_shape=jax.ShapeDtypeStruct((T, D_REC_V), BF16),
        scratch_shapes=[pltpu.VMEM((L, LANES), BF16)] * 9 + [pltpu.VMEM((L, LANES), F32)] * 3,
        input_output_aliases=aliases,
        compiler_params=pltpu.CompilerParams(
            dimension_semantics=("parallel", "arbitrary"),
            vmem_limit_bytes=_vmem_limit(nbytes)),
        name="hgrn2_L%d" % L,
    )(*args)


def _outproj_kernel(a_ref, r_ref, x_ref, w_ref, g_ref, h_ref):
    merged = (a_ref[...].astype(F32) + r_ref[...].astype(F32)).astype(BF16)
    y = jnp.dot(merged, w_ref[...], preferred_element_type=F32)
    ms = jnp.mean(y * y, axis=-1, keepdims=True)
    h_ref[...] = x_ref[...] + (y * lax.rsqrt(ms + RMS_EPS)) * g_ref[...]


def _out_projection(attn, rec, x, w_bf16, gain):
    T = x.shape[0]
    nbytes = (2 * 2 * TM_OUT * D_MODEL * 2 + 2 * 2 * TM_OUT * D_MODEL * 4 + 2 * D_MODEL * D_MODEL * 2
              + 2 * TM_OUT * D_MODEL * 4)
    return pl.pallas_call(
        _outproj_kernel,
        grid=(T // TM_OUT,),
        in_specs=[
            pl.BlockSpec((TM_OUT, D_MODEL), lambda i: (i, 0)),
            pl.BlockSpec((TM_OUT, D_MODEL), lambda i: (i, 0)),
            pl.BlockSpec((TM_OUT, D_MODEL), lambda i: (i, 0)),
            pl.BlockSpec((D_MODEL, D_MODEL), lambda i: (0, 0)),
            pl.BlockSpec((1, D_MODEL), lambda i: (0, 0)),
        ],
        out_specs=pl.BlockSpec((TM_OUT, D_MODEL), lambda i: (i, 0)),
        out_shape=jax.ShapeDtypeStruct((T, D_MODEL), F32),
        compiler_params=pltpu.CompilerParams(
            dimension_semantics=("parallel",),
            vmem_limit_bytes=_vmem_limit(nbytes)),
        name="out_projection",
    )(attn, rec, x, w_bf16, gain)


def _ffn_kernel(h_ref, gpre_ref, gpost_ref, wg_ref, wu_ref, wd_ref, o_ref, hn_ref):
    j = pl.program_id(1)
    n_row_tiles = TM_FFN // ROWS_EW

    @pl.when(j == 0)
    def _():
        def body(r, c):
            rows = pl.ds(pl.multiple_of(r * ROWS_EW, ROWS_EW), ROWS_EW)
            h = h_ref[rows, :]
            ms = jnp.mean(h * h, axis=-1, keepdims=True)
            hn_ref[rows, :] = ((h * lax.rsqrt(ms + RMS_EPS)) * gpre_ref[...]).astype(BF16)
            o_ref[rows, :] = jnp.zeros((ROWS_EW, D_MODEL), F32)
            return c
        lax.fori_loop(0, n_row_tiles, body, 0)

    hn = hn_ref[...]
    g = jnp.dot(hn, wg_ref[...], preferred_element_type=F32)
    u = jnp.dot(hn, wu_ref[...], preferred_element_type=F32)
    act = ((g * _sigmoid(g)) * u).astype(BF16)
    o_ref[...] += jnp.dot(act, wd_ref[...], preferred_element_type=F32)

    @pl.when(j == pl.num_programs(1) - 1)
    def _():
        def body(r, c):
            rows = pl.ds(pl.multiple_of(r * ROWS_EW, ROWS_EW), ROWS_EW)
            y = o_ref[rows, :]
            ms = jnp.mean(y * y, axis=-1, keepdims=True)
            o_ref[rows, :] = h_ref[rows, :] + (y * lax.rsqrt(ms + RMS_EPS)) * gpost_ref[...]
            return c
        lax.fori_loop(0, n_row_tiles, body, 0)


def _ffn(h, gpre, gpost, wg, wu, wd):
    T = h.shape[0]
    nbytes = (2 * 2 * TM_FFN * D_MODEL * 4 + TM_FFN * D_MODEL * 2 + 2 * 3 * D_MODEL * TF_FFN * 2
              + 3 * TM_FFN * TF_FFN * 4 + TM_FFN * D_MODEL * 4)
    return pl.pallas_call(
        _ffn_kernel,
        grid=(T // TM_FFN, D_FF // TF_FFN),
        in_specs=[
            pl.BlockSpec((TM_FFN, D_MODEL), lambda i, j: (i, 0)),
            pl.BlockSpec((1, D_MODEL), lambda i, j: (0, 0)),
            pl.BlockSpec((1, D_MODEL), lambda i, j: (0, 0)),
            pl.BlockSpec((D_MODEL, TF_FFN), lambda i, j: (0, j)),
            pl.BlockSpec((D_MODEL, TF_FFN), lambda i, j: (0, j)),
            pl.BlockSpec((TF_FFN, D_MODEL), lambda i, j: (j, 0)),
        ],
        out_specs=pl.BlockSpec((TM_FFN, D_MODEL), lambda i, j: (i, 0)),
        out_shape=jax.ShapeDtypeStruct((T, D_MODEL), F32),
        scratch_shapes=[pltpu.VMEM((TM_FFN, D_MODEL), BF16)],
        compiler_params=pltpu.CompilerParams(
            dimension_semantics=("parallel", "arbitrary"),
            vmem_limit_bytes=_vmem_limit(nbytes)),
        name="swiglu_ffn",
    )(h, gpre, gpost, wg, wu, wd)


def _rope_tables(seq_shapes):
    pos = jnp.concatenate([jnp.tile(jnp.arange(L, dtype=F32), B) for B, L in seq_shapes])
    inv_freq = ROPE_THETA ** (-jnp.arange(ROPE_HALF, dtype=F32) / ROPE_HALF)
    ang = pos[:, None] * inv_freq[None, :]
    cos, sin = jnp.cos(ang), jnp.sin(ang)
    T = pos.shape[0]
    rest = HEAD_DIM - ROPE_DIM
    cos_t = jnp.concatenate([cos, cos, jnp.ones((T, rest), F32)], axis=1)
    sa_t = jnp.concatenate([-sin, jnp.zeros((T, HEAD_DIM - ROPE_HALF), F32)], axis=1)
    sb_t = jnp.concatenate([jnp.zeros((T, ROPE_HALF), F32), sin, jnp.zeros((T, rest), F32)], axis=1)
    return cos_t, sa_t, sb_t


def _encoder_layer(x, seq_shapes, tables, w_in, sink, rec_norm, lb, w_out, norm_mix_pre, norm_mix_post,
                   norm_ffn_pre, norm_ffn_post, w_gate, w_up, w_down):
    row = lambda v: v.astype(F32).reshape(1, -1)
    proj = _in_projection(x, row(norm_mix_pre), *tables, w_in.astype(BF16))

    seq_tiles = []
    first = 0
    for B, L in seq_shapes:
        assert L % TQ_ATT == 0 and L % ROWS_EW == 0
        seq_tiles.append((first, L // TQ_ATT))
        first += B * L // TQ_ATT
    attn = _window_attention(proj, sink.astype(F32), tuple(seq_tiles))

    lb3 = lb.astype(F32).reshape(N_REC_HEADS, 1, REC_KEY_DIM)
    gain3 = rec_norm.astype(F32).reshape(N_REC_HEADS, 1, REC_VAL_DIM)
    rec = None
    first_row = 0
    for B, L in seq_shapes:
        assert first_row % L == 0
        rec = _hgrn2(proj, lb3, gain3, L, first_row // L, B, rec)
        first_row += B * L

    h = _out_projection(attn, rec, x, w_out.astype(BF16), row(norm_mix_post))
    return _ffn(h, row(norm_ffn_pre), row(norm_ffn_post),
                w_gate.astype(BF16), w_up.astype(BF16), w_down.astype(BF16))


def kernel(x_prompt, x_sample, w_in, sink, rec_norm, lb_logits, w_out, norm_mix_pre, norm_mix_post,
           norm_ffn_pre, norm_ffn_post, w_gate, w_up, w_down):
    lb_all = jnp.cumsum(jax.nn.softmax(lb_logits.astype(F32), axis=0), axis=0)
    seq_shapes = (x_prompt.shape[:2], x_sample.shape[:2])
    n_prompt = x_prompt.shape[0] * x_prompt.shape[1]
    x = jnp.concatenate([x_prompt.reshape(-1, D_MODEL), x_sample.reshape(-1, D_MODEL)], axis=0)
    tables = _rope_tables(seq_shapes)
    for l in range(DEPTH):
        x = _encoder_layer(x, seq_shapes, tables, w_in[l], sink[l], rec_norm[l], lb_all[l], w_out[l],
                           norm_mix_pre[l], norm_mix_post[l], norm_ffn_pre[l], norm_ffn_post[l],
                           w_gate[l], w_up[l], w_down[l])
    return (x[:n_prompt].reshape(x_prompt.shape), x[n_prompt:].reshape(x_sample.shape))
```

```python
import functools

import jax
import jax.numpy as jnp
import numpy as np
from jax import lax
from jax.experimental import pallas as pl
from jax.experimental.pallas import tpu as pltpu

F32 = jnp.float32
BF16 = jnp.bfloat16

D_MODEL = 2048
DEPTH = 1
HEAD_DIM = 128
N_Q_HEADS = 16
N_KV_HEADS = 4
GROUP = N_Q_HEADS // N_KV_HEADS
WINDOW = 128
BLOCK = 128
ROPE_DIM = HEAD_DIM // 4
ROPE_HALF = ROPE_DIM // 2
ROPE_THETA = 500000.0
N_REC_HEADS = 16
REC_KEY_DIM = 128
REC_VAL_DIM = 128
CHUNK = 32
D_FF = -(-8 * D_MODEL // (3 * 256)) * 256
RMS_EPS = 1e-6

D_ATT = N_Q_HEADS * HEAD_DIM
D_KV = N_KV_HEADS * HEAD_DIM
D_REC_K = N_REC_HEADS * REC_KEY_DIM
D_REC_V = N_REC_HEADS * REC_VAL_DIM
SPLIT_SIZES = (D_ATT, D_KV, D_KV, D_REC_K, D_REC_K, D_REC_K, D_REC_V, D_REC_V, D_MODEL, D_MODEL)
D_IN = sum(SPLIT_SIZES)
(OFF_Q, OFF_K, OFF_V, OFF_QR, OFF_ZF, OFF_ZB, OFF_IR, OFF_GR, OFF_GA, OFF_GT) = (
    int(v) for v in np.concatenate([[0], np.cumsum(SPLIT_SIZES)[:-1]]))

V7X_VMEM_CEILING = 56 * 1024 * 1024
LANES = 128

TM_IN, TN_IN = 1024, 1024
ROWS_EW = 256
TQ_ATT = 512
SEG_REC = 512
HB_REC = 4
TM_OUT = 256
TM_FFN, TF_FFN = 512, 512


def _vmem_limit(nbytes):
    return int(min(V7X_VMEM_CEILING, nbytes * 1.25 + (4 << 20)))


def _sigmoid(x):
    return 1.0 / (1.0 + jnp.exp(-x))


def _dot_nt(a, b):
    return lax.dot_general(a, b, (((1,), (1,)), ((), ())), preferred_element_type=F32)


def _dot_tn(a, b):
    return lax.dot_general(a, b, (((0,), (0,)), ((), ())), preferred_element_type=F32)


def _segment_position(g, seq_groups):
    first, local, per = None, None, None
    for g0, n in reversed(seq_groups):
        loc = lax.rem(g - g0, n)
        fst = g - loc
        if first is None:
            first, local, per = fst, loc, n
        else:
            here = g < nxt
            first = jnp.where(here, fst, first)
            local = jnp.where(here, loc, local)
            per = jnp.where(here, n, per)
        nxt = g0
    return first, local, per


def _inproj_kernel(x_ref, g_ref, cos_ref, sa_ref, sb_ref, w_ref, o_ref, xn_ref):
    j = pl.program_id(1)
    n_row_tiles = TM_IN // ROWS_EW

    @pl.when(j == 0)
    def _():
        def body(r, c):
            rows = pl.ds(pl.multiple_of(r * ROWS_EW, ROWS_EW), ROWS_EW)
            x = x_ref[rows, :]
            ms = jnp.mean(x * x, axis=-1, keepdims=True)
            xn_ref[rows, :] = ((x * lax.rsqrt(ms + RMS_EPS)) * g_ref[...]).astype(BF16)
            return c
        lax.fori_loop(0, n_row_tiles, body, 0)

    o_ref[...] = jnp.dot(xn_ref[...], w_ref[...], preferred_element_type=F32)

    def rope_heads(n_heads):
        def body(r, c):
            rows = pl.ds(pl.multiple_of(r * ROWS_EW, ROWS_EW), ROWS_EW)
            cos = cos_ref[rows, :]
            sa = sa_ref[rows, :]
            sb = sb_ref[rows, :]
            for hh in range(n_heads):
                cs = slice(hh * HEAD_DIM, (hh + 1) * HEAD_DIM)
                y = o_ref[rows, cs]
                o_ref[rows, cs] = (y * cos + pltpu.roll(y, HEAD_DIM - ROPE_HALF, 1) * sa
                                   + pltpu.roll(y, ROPE_HALF, 1) * sb)
            return c
        lax.fori_loop(0, n_row_tiles, body, 0)

    q_tiles = D_ATT // TN_IN

    @pl.when(j < q_tiles)
    def _():
        rope_heads(TN_IN // HEAD_DIM)

    @pl.when(j == q_tiles)
    def _():
        rope_heads(N_KV_HEADS)


def _in_projection(x, gain, cos_t, sa_t, sb_t, w_bf16):
    T = x.shape[0]
    assert OFF_K == (D_ATT // TN_IN) * TN_IN and D_KV <= TN_IN
    nbytes = (2 * TM_IN * D_MODEL * 4 + 2 * D_MODEL * TN_IN * 2 + 2 * TM_IN * TN_IN * 4
              + TM_IN * D_MODEL * 2 + 6 * TM_IN * LANES * 4)
    return pl.pallas_call(
        _inproj_kernel,
        grid=(T // TM_IN, D_IN // TN_IN),
        in_specs=[
            pl.BlockSpec((TM_IN, D_MODEL), lambda i, j: (i, 0)),
            pl.BlockSpec((1, D_MODEL), lambda i, j: (0, 0)),
            pl.BlockSpec((TM_IN, LANES), lambda i, j: (i, 0)),
            pl.BlockSpec((TM_IN, LANES), lambda i, j: (i, 0)),
            pl.BlockSpec((TM_IN, LANES), lambda i, j: (i, 0)),
            pl.BlockSpec((D_MODEL, TN_IN), lambda i, j: (0, j)),
        ],
        out_specs=pl.BlockSpec((TM_IN, TN_IN), lambda i, j: (i, j)),
        out_shape=jax.ShapeDtypeStruct((T, D_IN), F32),
        scratch_shapes=[pltpu.VMEM((TM_IN, D_MODEL), BF16)],
        compiler_params=pltpu.CompilerParams(
            dimension_semantics=("parallel", "arbitrary"),
            vmem_limit_bytes=_vmem_limit(nbytes)),
        name="in_projection",
    )(x, gain, cos_t, sa_t, sb_t, w_bf16)


def _attn_kernel(seq_groups, sink_ref, q_ref, kc_ref, vc_ref, kp_ref, kn_ref, vp_ref, vn_ref,
                 ga_ref, o_ref):
    t = pl.program_id(0)
    h = pl.program_id(1)
    _, local, per = _segment_position(t, seq_groups)
    lo = jnp.where(local != 0, 0, BLOCK)
    hi = jnp.where(local != per - 1, 3 * BLOCK, 2 * BLOCK)

    k_all = jnp.concatenate([kp_ref[...], kc_ref[...], kn_ref[...]], axis=0).astype(BF16)
    v_all = jnp.concatenate([vp_ref[...], vc_ref[...], vn_ref[...]], axis=0).astype(BF16)
    rows_q = GROUP * BLOCK
    qi = lax.broadcasted_iota(jnp.int32, (rows_q, 3 * BLOCK), 0) & (BLOCK - 1)
    kj = lax.broadcasted_iota(jnp.int32, (rows_q, 3 * BLOCK), 1)
    band = (kj >= qi) & (kj <= qi + 2 * WINDOW)
    sink_col = jnp.concatenate(
        [jnp.full((BLOCK, 1), sink_ref[GROUP * h + g], F32) for g in range(GROUP)], axis=0)
    scale = HEAD_DIM ** -0.5
    n_blocks = TQ_ATT // BLOCK
    for b in range(n_blocks):
        rs = slice(b * BLOCK, (b + 1) * BLOCK)
        q = jnp.concatenate(
            [q_ref[rs, g * HEAD_DIM:(g + 1) * HEAD_DIM] for g in range(GROUP)], axis=0).astype(BF16)
        kb = k_all[b * BLOCK:(b + 3) * BLOCK]
        vb = v_all[b * BLOCK:(b + 3) * BLOCK]
        s = _dot_nt(q, kb) * scale
        valid = band
        if b == 0:
            valid = valid & (kj >= lo)
        if b == n_blocks - 1:
            valid = valid & (kj < hi)
        s = jnp.where(valid, s, -jnp.inf)
        m = jnp.maximum(jnp.max(s, axis=-1, keepdims=True), sink_col)
        p = jnp.exp(s - m)
        denom = jnp.sum(p, axis=-1, keepdims=True) + jnp.exp(sink_col - m)
        o = jnp.dot(p.astype(BF16), vb, preferred_element_type=F32) * (1.0 / denom)
        for g in range(GROUP):
            cs = slice(g * HEAD_DIM, (g + 1) * HEAD_DIM)
            gate = _sigmoid(ga_ref[rs, cs])
            o_ref[rs, cs] = (gate * o[g * BLOCK:(g + 1) * BLOCK]).astype(o_ref.dtype)


def _window_attention(proj, sink, seq_groups):
    T = proj.shape[0]
    n_tiles = T // TQ_ATT
    bpt = TQ_ATT // BLOCK
    last_blk = T // BLOCK - 1
    gw = GROUP * HEAD_DIM
    kcol, vcol, gcol = OFF_K // HEAD_DIM, OFF_V // HEAD_DIM, OFF_GA // gw
    nbytes = 2 * (2 * TQ_ATT * gw * 4 + 2 * TQ_ATT * HEAD_DIM * 4 + 4 * BLOCK * HEAD_DIM * 4
                  + TQ_ATT * gw * 2) + 8 * GROUP * BLOCK * 3 * BLOCK * 4
    return pl.pallas_call(
        functools.partial(_attn_kernel, seq_groups),
        grid=(n_tiles, N_KV_HEADS),
        in_specs=[
            pl.BlockSpec(memory_space=pltpu.SMEM),
            pl.BlockSpec((TQ_ATT, gw), lambda t, h: (t, h)),
            pl.BlockSpec((TQ_ATT, HEAD_DIM), lambda t, h: (t, kcol + h)),
            pl.BlockSpec((TQ_ATT, HEAD_DIM), lambda t, h: (t, vcol + h)),
            pl.BlockSpec((BLOCK, HEAD_DIM), lambda t, h: (jnp.maximum(t * bpt - 1, 0), kcol + h)),
            pl.BlockSpec((BLOCK, HEAD_DIM), lambda t, h: (jnp.minimum((t + 1) * bpt, last_blk), kcol + h)),
            pl.BlockSpec((BLOCK, HEAD_DIM), lambda t, h: (jnp.maximum(t * bpt - 1, 0), vcol + h)),
            pl.BlockSpec((BLOCK, HEAD_DIM), lambda t, h: (jnp.minimum((t + 1) * bpt, last_blk), vcol + h)),
            pl.BlockSpec((TQ_ATT, gw), lambda t, h: (t, gcol + h)),
        ],
        out_specs=pl.BlockSpec((TQ_ATT, gw), lambda t, h: (t, h)),
        out_shape=jax.ShapeDtypeStruct((T, D_ATT), BF16),
        compiler_params=pltpu.CompilerParams(
            dimension_semantics=("parallel", "arbitrary"),
            vmem_limit_bytes=_vmem_limit(nbytes)),
        name="window_attention",
    )(sink, proj, proj, proj, proj, proj, proj, proj, proj)


def _hgrn_kernel(seq_groups, lb_ref, qrf_ref, irf_ref, zf_ref, qrb_ref, irb_ref, zb_ref,
                 of_ref, ob_ref, st_ref):
    g = pl.program_id(1)
    _, local, _ = _segment_position(g, seq_groups)

    @pl.when(local == 0)
    def _():
        st_ref[...] = jnp.zeros(st_ref.shape, F32)

    n_chunks = SEG_REC // CHUNK
    rin = lax.broadcasted_iota(jnp.int32, (CHUNK, LANES), 0)
    ti = lax.broadcasted_iota(jnp.int32, (CHUNK, CHUNK), 0)
    si = lax.broadcasted_iota(jnp.int32, (CHUNK, CHUNK), 1)

    def chain(d, hh, rows, qr_ref, ir_ref, z_ref, o_ref):
        fwd = d == 0
        cs = slice(hh * LANES, (hh + 1) * LANES)
        lb = lb_ref[:, cs]
        one_m_lb = 1.0 - lb
        qraw = qr_ref[rows, cs]
        q = qraw * _sigmoid(qraw)
        i_c = ir_ref[rows, cs].astype(BF16)
        z = z_ref[rows, cs]
        e = jnp.exp(-jnp.abs(z))
        inv = 1.0 / (1.0 + e)
        s_hi, s_lo = inv, e * inv
        sg = jnp.where(z >= 0, s_hi, s_lo)
        sgn = jnp.where(z >= 0, s_lo, s_hi)
        logf = jnp.log(lb + one_m_lb * sg)
        k = one_m_lb * sgn
        b = logf
        for s in (1, 2, 4, 8, 16):
            b = b + jnp.where(rin >= s, pltpu.roll(b, s, 0), 0.0)
        tot = b[CHUNK - 1:CHUNK, :]
        if fwd:
            c = b
            cref = b[CHUNK // 2 - 1:CHUNK // 2, :]
        else:
            c = tot - b + logf
            cref = c[CHUNK // 2:CHUNK // 2 + 1, :]
        q_in = q * jnp.exp(c - cref)
        k_in = k * jnp.exp(cref - c)
        k_st = (k_in * jnp.exp(tot - cref)).astype(BF16)
        q_dc = (q_in * jnp.exp(cref)).astype(BF16)
        a = jnp.where((ti >= si) if fwd else (ti <= si), _dot_nt(q_in.astype(BF16), k_in.astype(BF16)), 0.0)
        st = st_ref[d, hh]
        o = jnp.dot(a.astype(BF16), i_c, preferred_element_type=F32) + _dot_nt(q_dc, st.astype(BF16))
        o_ref[rows, cs] = o.astype(o_ref.dtype)
        st_ref[d, hh] = jnp.exp(tot) * st + _dot_tn(i_c, k_st)

    def scan(n, carry):
        rows_f = pl.ds(pl.multiple_of(n * CHUNK, CHUNK), CHUNK)
        rows_b = pl.ds(pl.multiple_of((n_chunks - 1 - n) * CHUNK, CHUNK), CHUNK)
        for hh in range(HB_REC):
            chain(0, hh, rows_f, qrf_ref, irf_ref, zf_ref, of_ref)
            chain(1, hh, rows_b, qrb_ref, irb_ref, zb_ref, ob_ref)
        return carry

    lax.fori_loop(0, n_chunks, scan, 0, unroll=2)


def _hgrn2(proj, lb_row, seq_groups):
    T = proj.shape[0]
    w = HB_REC * LANES
    col = lambda off: off // w

    def mirror(g):
        first, local, per = _segment_position(g, seq_groups)
        return first + per - 1 - local

    def fspec(off):
        return pl.BlockSpec((SEG_REC, w), lambda hb, g: (g, col(off) + hb))

    def bspec(off):
        return pl.BlockSpec((SEG_REC, w), lambda hb, g: (mirror(g), col(off) + hb))

    nbytes = 2 * 6 * SEG_REC * w * 4 + 2 * 2 * SEG_REC * w * 2 + 2 * HB_REC * LANES * LANES * 4
    out = jax.ShapeDtypeStruct((T, D_REC_V), BF16)
    return pl.pallas_call(
        functools.partial(_hgrn_kernel, seq_groups),
        grid=(N_REC_HEADS // HB_REC, T // SEG_REC),
        in_specs=[
            pl.BlockSpec((1, w), lambda hb, g: (0, hb)),
            fspec(OFF_QR), fspec(OFF_IR), fspec(OFF_ZF),
            bspec(OFF_QR), bspec(OFF_IR), bspec(OFF_ZB),
        ],
        out_specs=[pl.BlockSpec((SEG_REC, w), lambda hb, g: (g, hb)),
                   pl.BlockSpec((SEG_REC, w), lambda hb, g: (mirror(g), hb))],
        out_shape=[out, out],
        scratch_shapes=[pltpu.VMEM((2, HB_REC, REC_VAL_DIM, REC_KEY_DIM), F32)],
        compiler_params=pltpu.CompilerParams(
            dimension_semantics=("parallel", "arbitrary"),
            vmem_limit_bytes=_vmem_limit(nbytes)),
        name="hgrn2",
    )(lb_row, proj, proj, proj, proj, proj, proj)


def _outproj_kernel(a_ref, of_ref, ob_ref, grl_ref, grh_ref, gtl_ref, gth_ref, x_ref, w_ref, rg_ref, g_ref,
                    h_ref, m_ref):
    half = D_REC_V // 2
    for hh in range(N_REC_HEADS):
        cs = slice(hh * REC_VAL_DIM, (hh + 1) * REC_VAL_DIM)
        lo = hh * REC_VAL_DIM < half
        hs = cs if lo else slice(hh * REC_VAL_DIM - half, (hh + 1) * REC_VAL_DIM - half)
        rec = of_ref[:, cs].astype(F32) + ob_ref[:, cs].astype(F32)
        ms = jnp.mean(rec * rec, axis=-1, keepdims=True)
        y = (rec * lax.rsqrt(ms + RMS_EPS)) * rg_ref[:, cs]
        gr = (grl_ref if lo else grh_ref)[:, hs]
        gt = (gtl_ref if lo else gth_ref)[:, hs]
        y = _sigmoid(gt) * (y * (gr * _sigmoid(gr)))
        m_ref[:, cs] = (a_ref[:, cs].astype(F32) + y).astype(BF16)
    y = jnp.dot(m_ref[...], w_ref[...], preferred_element_type=F32)
    ms = jnp.mean(y * y, axis=-1, keepdims=True)
    h_ref[...] = x_ref[...] + (y * lax.rsqrt(ms + RMS_EPS)) * g_ref[...]


def _out_projection(attn, rec_f, rec_b, proj, x, w_bf16, rec_gain, gain):
    T = x.shape[0]
    half = D_REC_V // 2
    assert OFF_GR % half == 0 and OFF_GT % half == 0
    tok = lambda i: (i, 0)
    nbytes = (2 * 3 * TM_OUT * D_MODEL * 2 + 2 * 4 * TM_OUT * half * 4 + 2 * 2 * TM_OUT * D_MODEL * 4
              + 2 * D_MODEL * D_MODEL * 2 + TM_OUT * D_MODEL * 2 + 2 * TM_OUT * D_MODEL * 4)
    return pl.pallas_call(
        _outproj_kernel,
        grid=(T // TM_OUT,),
        in_specs=[
            pl.BlockSpec((TM_OUT, D_MODEL), tok),
            pl.BlockSpec((TM_OUT, D_REC_V), tok),
            pl.BlockSpec((TM_OUT, D_REC_V), tok),
            pl.BlockSpec((TM_OUT, half), lambda i: (i, OFF_GR // half)),
            pl.BlockSpec((TM_OUT, half), lambda i: (i, OFF_GR // half + 1)),
            pl.BlockSpec((TM_OUT, half), lambda i: (i, OFF_GT // half)),
            pl.BlockSpec((TM_OUT, half), lambda i: (i, OFF_GT // half + 1)),
            pl.BlockSpec((TM_OUT, D_MODEL), tok),
            pl.BlockSpec((D_MODEL, D_MODEL), lambda i: (0, 0)),
            pl.BlockSpec((1, D_REC_V), lambda i: (0, 0)),
            pl.BlockSpec((1, D_MODEL), lambda i: (0, 0)),
        ],
        out_specs=pl.BlockSpec((TM_OUT, D_MODEL), tok),
        out_shape=jax.ShapeDtypeStruct((T, D_MODEL), F32),
        scratch_shapes=[pltpu.VMEM((TM_OUT, D_MODEL), BF16)],
        compiler_params=pltpu.CompilerParams(
            dimension_semantics=("parallel",),
            vmem_limit_bytes=_vmem_limit(nbytes)),
        name="out_projection",
    )(attn, rec_f, rec_b, proj, proj, proj, proj, x, w_bf16, rec_gain, gain)


def _ffn_kernel(h_ref, gpre_ref, gpost_ref, wg_ref, wu_ref, wd_ref, o_ref, hn_ref):
    j = pl.program_id(1)
    n_row_tiles = TM_FFN // ROWS_EW

    @pl.when(j == 0)
    def _():
        def body(r, c):
            rows = pl.ds(pl.multiple_of(r * ROWS_EW, ROWS_EW), ROWS_EW)
            h = h_ref[rows, :]
            ms = jnp.mean(h * h, axis=-1, keepdims=True)
            hn_ref[rows, :] = ((h * lax.rsqrt(ms + RMS_EPS)) * gpre_ref[...]).astype(BF16)
            o_ref[rows, :] = jnp.zeros((ROWS_EW, D_MODEL), F32)
            return c
        lax.fori_loop(0, n_row_tiles, body, 0)

    hn = hn_ref[...]
    g = jnp.dot(hn, wg_ref[...], preferred_element_type=F32)
    u = jnp.dot(hn, wu_ref[...], preferred_element_type=F32)
    act = ((g * _sigmoid(g)) * u).astype(BF16)
    o_ref[...] += jnp.dot(act, wd_ref[...], preferred_element_type=F32)

    @pl.when(j == pl.num_programs(1) - 1)
    def _():
        def body(r, c):
            rows = pl.ds(pl.multiple_of(r * ROWS_EW, ROWS_EW), ROWS_EW)
            y = o_ref[rows, :]
            ms = jnp.mean(y * y, axis=-1, keepdims=True)
            o_ref[rows, :] = h_ref[rows, :] + (y * lax.rsqrt(ms + RMS_EPS)) * gpost_ref[...]
            return c
        lax.fori_loop(0, n_row_tiles, body, 0)


def _ffn(h, gpre, gpost, wg, wu, wd):
    T = h.shape[0]
    nbytes = (2 * 2 * TM_FFN * D_MODEL * 4 + TM_FFN * D_MODEL * 2 + 2 * 3 * D_MODEL * TF_FFN * 2
              + 3 * TM_FFN * TF_FFN * 4 + TM_FFN * D_MODEL * 4)
    return pl.pallas_call(
        _ffn_kernel,
        grid=(T // TM_FFN, D_FF // TF_FFN),
        in_specs=[
            pl.BlockSpec((TM_FFN, D_MODEL), lambda i, j: (i, 0)),
            pl.BlockSpec((1, D_MODEL), lambda i, j: (0, 0)),
            pl.BlockSpec((1, D_MODEL), lambda i, j: (0, 0)),
            pl.BlockSpec((D_MODEL, TF_FFN), lambda i, j: (0, j)),
            pl.BlockSpec((D_MODEL, TF_FFN), lambda i, j: (0, j)),
            pl.BlockSpec((TF_FFN, D_MODEL), lambda i, j: (j, 0)),
        ],
        out_specs=pl.BlockSpec((TM_FFN, D_MODEL), lambda i, j: (i, 0)),
        out_shape=jax.ShapeDtypeStruct((T, D_MODEL), F32),
        scratch_shapes=[pltpu.VMEM((TM_FFN, D_MODEL), BF16)],
        compiler_params=pltpu.CompilerParams(
            dimension_semantics=("parallel", "arbitrary"),
            vmem_limit_bytes=_vmem_limit(nbytes)),
        name="swiglu_ffn",
    )(h, gpre, gpost, wg, wu, wd)


def _rope_tables(seq_shapes):
    pos = jnp.concatenate([jnp.tile(jnp.arange(L, dtype=F32), B) for B, L in seq_shapes])
    inv_freq = ROPE_THETA ** (-jnp.arange(ROPE_HALF, dtype=F32) / ROPE_HALF)
    ang = pos[:, None] * inv_freq[None, :]
    cos, sin = jnp.cos(ang), jnp.sin(ang)
    T = pos.shape[0]
    rest = HEAD_DIM - ROPE_DIM
    cos_t = jnp.concatenate([cos, cos, jnp.ones((T, rest), F32)], axis=1)
    sa_t = jnp.concatenate([-sin, jnp.zeros((T, HEAD_DIM - ROPE_HALF), F32)], axis=1)
    sb_t = jnp.concatenate([jnp.zeros((T, ROPE_HALF), F32), sin, jnp.zeros((T, rest), F32)], axis=1)
    return cos_t, sa_t, sb_t


def _seq_groups(seq_shapes, seg):
    groups, first = [], 0
    for B, L in seq_shapes:
        assert L % seg == 0
        groups.append((first, L // seg))
        first += B * L // seg
    return tuple(groups)


def _encoder_layer(x, seq_shapes, tables, w_in, sink, rec_norm, lb, w_out, norm_mix_pre, norm_mix_post,
                   norm_ffn_pre, norm_ffn_post, w_gate, w_up, w_down):
    row = lambda v: v.astype(F32).reshape(1, -1)
    proj = _in_projection(x, row(norm_mix_pre), *tables, w_in.astype(BF16))
    attn = _window_attention(proj, sink.astype(F32), _seq_groups(seq_shapes, TQ_ATT))
    rec_f, rec_b = _hgrn2(proj, row(lb), _seq_groups(seq_shapes, SEG_REC))
    h = _out_projection(attn, rec_f, rec_b, proj, x, w_out.astype(BF16), row(rec_norm), row(norm_mix_post))
    return _ffn(h, row(norm_ffn_pre), row(norm_ffn_post),
                w_gate.astype(BF16), w_up.astype(BF16), w_down.astype(BF16))


def kernel(x_prompt, x_sample, w_in, sink, rec_norm, lb_logits, w_out, norm_mix_pre, norm_mix_post,
           norm_ffn_pre, norm_ffn_post, w_gate, w_up, w_down):
    lb_all = jnp.cumsum(jax.nn.softmax(lb_logits.astype(F32), axis=0), axis=0)
    seq_shapes = (x_prompt.shape[:2], x_sample.shape[:2])
    n_prompt = x_prompt.shape[0] * x_prompt.shape[1]
    x = jnp.concatenate([x_prompt.reshape(-1, D_MODEL), x_sample.reshape(-1, D_MODEL)], axis=0)
    tables = _rope_tables(seq_shapes)
    for l in range(DEPTH):
        x = _encoder_layer(x, seq_shapes, tables, w_in[l], sink[l], rec_norm[l], lb_all[l], w_out[l],
                           norm_mix_pre[l], norm_mix_post[l], norm_ffn_pre[l], norm_ffn_post[l],
                           w_gate[l], w_up[l], w_down[l])
    return (x[:n_prompt].reshape(x_prompt.shape), x[n_prompt:].reshape(x_sample.shape))
```

```python
import functools

import jax
import jax.numpy as jnp
import numpy as np
from jax import lax
from jax.experimental import pallas as pl
from jax.experimental.pallas import tpu as pltpu

F32 = jnp.float32
BF16 = jnp.bfloat16

D_MODEL = 2048
DEPTH = 1
HEAD_DIM = 128
N_Q_HEADS = 16
N_KV_HEADS = 4
GROUP = N_Q_HEADS // N_KV_HEADS
WINDOW = 128
BLOCK = 128
ROPE_DIM = HEAD_DIM // 4
ROPE_HALF = ROPE_DIM // 2
ROPE_THETA = 500000.0
N_REC_HEADS = 16
REC_KEY_DIM = 128
REC_VAL_DIM = 128
CHUNK = 32
D_FF = -(-8 * D_MODEL // (3 * 256)) * 256
RMS_EPS = 1e-6
LOG2E = 1.4426950408889634

D_ATT = N_Q_HEADS * HEAD_DIM
D_KV = N_KV_HEADS * HEAD_DIM
D_REC_K = N_REC_HEADS * REC_KEY_DIM
D_REC_V = N_REC_HEADS * REC_VAL_DIM
SPLIT_SIZES = (D_ATT, D_KV, D_KV, D_REC_K, D_REC_K, D_REC_K, D_REC_V, D_REC_V, D_MODEL, D_MODEL)
D_IN = sum(SPLIT_SIZES)
(OFF_Q, OFF_K, OFF_V, OFF_QR, OFF_ZF, OFF_ZB, OFF_IR, OFF_GR, OFF_GA, OFF_GT) = (
    int(v) for v in np.concatenate([[0], np.cumsum(SPLIT_SIZES)[:-1]]))
D_GATES = OFF_IR - OFF_ZF
D_SLAB = D_IN - D_GATES
A_Q, A_K, A_V, A_QR = OFF_Q, OFF_K, OFF_V, OFF_QR
A_IR, A_GR, A_GA, A_GT = (OFF_IR - D_GATES, OFF_GR - D_GATES, OFF_GA - D_GATES, OFF_GT - D_GATES)

V7X_VMEM_CEILING = 56 * 1024 * 1024
LANES = 128

TM_NORM = 512
TM_IN, TN_IN = 1024, 1024
ROWS_EW = 256
SLAB_ATT = 32
TQ_ATT = 512
SEG_REC = 512
HB_REC = 4
TM_OUT = 256
TM_FFN, TF_FFN = 512, 512


def _vmem_limit(nbytes):
    return int(min(V7X_VMEM_CEILING, nbytes * 1.25 + (4 << 20)))


def _sigmoid(x):
    return 1.0 / (1.0 + jnp.exp(-x))


def _dot_nt(a, b):
    return lax.dot_general(a, b, (((1,), (1,)), ((), ())), preferred_element_type=F32)


def _dot_tn(a, b):
    return lax.dot_general(a, b, (((0,), (0,)), ((), ())), preferred_element_type=F32)


def _row_tiles(n_rows, body):
    def step(r, c):
        body(pl.ds(pl.multiple_of(r * ROWS_EW, ROWS_EW), ROWS_EW))
        return c
    lax.fori_loop(0, n_rows // ROWS_EW, step, 0)


def _segment_position(g, seq_groups):
    first, local, per = None, None, None
    for g0, n in reversed(seq_groups):
        loc = lax.rem(g - g0, n)
        fst = g - loc
        if first is None:
            first, local, per = fst, loc, n
        else:
            here = g < nxt
            first = jnp.where(here, fst, first)
            local = jnp.where(here, loc, local)
            per = jnp.where(here, n, per)
        nxt = g0
    return first, local, per


def _two_source_specs(tm, n_first):
    return (pl.BlockSpec((tm, D_MODEL), lambda i, *_: (jnp.minimum(i, n_first - 1), 0)),
            pl.BlockSpec((tm, D_MODEL), lambda i, *_: (jnp.maximum(i - n_first, 0), 0)))


def _norm_kernel(n_first, xa_ref, xb_ref, g_ref, o_ref):
    def run(x_ref):
        def body(rows):
            x = x_ref[rows, :]
            ms = jnp.mean(x * x, axis=-1, keepdims=True)
            o_ref[rows, :] = ((x * lax.rsqrt(ms + RMS_EPS)) * g_ref[...]).astype(o_ref.dtype)
        _row_tiles(TM_NORM, body)

    @pl.when(pl.program_id(0) < n_first)
    def _():
        run(xa_ref)

    @pl.when(pl.program_id(0) >= n_first)
    def _():
        run(xb_ref)


def _stream_norm(xa, xb, gain):
    T = xa.shape[0] + xb.shape[0]
    n_first = xa.shape[0] // TM_NORM
    nbytes = 2 * 2 * TM_NORM * D_MODEL * 4 + 2 * TM_NORM * D_MODEL * 2
    return pl.pallas_call(
        functools.partial(_norm_kernel, n_first),
        grid=(T // TM_NORM,),
        in_specs=[*_two_source_specs(TM_NORM, n_first), pl.BlockSpec((1, D_MODEL), lambda i: (0, 0))],
        out_specs=pl.BlockSpec((TM_NORM, D_MODEL), lambda i: (i, 0)),
        out_shape=jax.ShapeDtypeStruct((T, D_MODEL), BF16),
        compiler_params=pltpu.CompilerParams(
            dimension_semantics=("parallel",), vmem_limit_bytes=_vmem_limit(nbytes)),
        name="stream_norm",
    )(xa, xb, gain)


def _inproj_kernel(kinds, col_tiles_ref, *refs):
    del col_tiles_ref
    with_epilogue = any(k != "plain" for k in kinds)
    if with_epilogue:
        xn_ref, cos_ref, sa_ref, sb_ref, w_ref, o_ref, wb_ref, y_ref = refs
    else:
        xn_ref, w_ref, o_ref, wb_ref = refs
    j = pl.program_id(0)

    @pl.when(pl.program_id(1) == 0)
    def _():
        def body(rows):
            wb_ref[rows, :] = w_ref[rows, :].astype(BF16)
        _row_tiles(D_MODEL, body)

    def matmul():
        return jnp.dot(xn_ref[...], wb_ref[...], preferred_element_type=F32)

    def rope(n_rope_heads):
        def body(rows):
            cos, sa, sb = cos_ref[rows, :], sa_ref[rows, :], sb_ref[rows, :]
            for hh in range(TN_IN // HEAD_DIM):
                cs = slice(hh * HEAD_DIM, (hh + 1) * HEAD_DIM)
                y = y_ref[rows, cs]
                if hh < n_rope_heads:
                    y = (y * cos + pltpu.roll(y, HEAD_DIM - ROPE_HALF, 1) * sa
                         + pltpu.roll(y, ROPE_HALF, 1) * sb)
                o_ref[rows, cs] = y.astype(o_ref.dtype)
        _row_tiles(TM_IN, body)

    def silu():
        def body(rows):
            for cc in range(TN_IN // LANES):
                cs = slice(cc * LANES, (cc + 1) * LANES)
                y = y_ref[rows, cs]
                o_ref[rows, cs] = (y * _sigmoid(y)).astype(o_ref.dtype)
        _row_tiles(TM_IN, body)

    for kind in sorted(set(kinds)):
        tiles = [t for t, k in enumerate(kinds) if k == kind]
        cond = functools.reduce(jnp.logical_or, [j == t for t in tiles])

        @pl.when(cond)
        def _(kind=kind):
            if kind == "plain":
                o_ref[...] = matmul().astype(o_ref.dtype)
            else:
                y_ref[...] = matmul()
                if kind == "rope":
                    rope(TN_IN // HEAD_DIM)
                elif kind == "rope_k":
                    rope(N_KV_HEADS)
                else:
                    silu()


def _in_projection(xn, w_in, tables, table_groups, col_tiles, kinds, out_dtype, name):
    T = xn.shape[0]
    n_col = len(col_tiles)
    col_tiles = jnp.asarray(col_tiles, jnp.int32)
    with_epilogue = any(k != "plain" for k in kinds)
    last_table_tile = max([t for t, k in enumerate(kinds) if k.startswith("rope")], default=-1)
    out_bytes = jnp.dtype(out_dtype).itemsize

    def table_map(j, i, ct):
        _, local, _ = _segment_position(i, table_groups)
        return (jnp.where(j <= last_table_tile, local, 0), 0)

    in_specs = [pl.BlockSpec((TM_IN, D_MODEL), lambda j, i, ct: (i, 0))]
    args = [xn]
    scratch = [pltpu.VMEM((D_MODEL, TN_IN), BF16)]
    nbytes = (2 * TM_IN * D_MODEL * 2 + 2 * D_MODEL * TN_IN * 4 + D_MODEL * TN_IN * 2
              + 2 * TM_IN * TN_IN * out_bytes + TM_IN * TN_IN * 4)
    if with_epilogue:
        in_specs += [pl.BlockSpec((TM_IN, LANES), table_map)] * 3
        args += list(tables)
        scratch.append(pltpu.VMEM((TM_IN, TN_IN), F32))
        nbytes += 2 * 3 * TM_IN * LANES * 4 + TM_IN * TN_IN * 4
    in_specs.append(pl.BlockSpec((D_MODEL, TN_IN), lambda j, i, ct: (0, ct[j])))
    args.append(w_in)
    return pl.pallas_call(
        functools.partial(_inproj_kernel, kinds),
        grid_spec=pltpu.PrefetchScalarGridSpec(
            num_scalar_prefetch=1,
            grid=(n_col, T // TM_IN),
            in_specs=in_specs,
            out_specs=pl.BlockSpec((TM_IN, TN_IN), lambda j, i, ct: (i, j)),
            scratch_shapes=scratch),
        out_shape=jax.ShapeDtypeStruct((T, n_col * TN_IN), out_dtype),
        compiler_params=pltpu.CompilerParams(
            dimension_semantics=("arbitrary", "arbitrary"),
            vmem_limit_bytes=_vmem_limit(nbytes)),
        name=name,
    )(col_tiles, *args)


def _attn_kernel(seq_groups, sink_ref, q_ref, kc_ref, vc_ref, kp_ref, kn_ref, vp_ref, vn_ref,
                 ga_ref, o_ref, s_all, p_all, r_all):
    t = pl.program_id(0)
    h = pl.program_id(1)
    _, local, per = _segment_position(t, seq_groups)
    has_prev = local != 0
    has_next = local != per - 1

    k_all = jnp.concatenate([kp_ref[...], kc_ref[...], kn_ref[...]], axis=0)
    v_all = jnp.concatenate([vp_ref[...], vc_ref[...], vn_ref[...]], axis=0)
    qi = lax.broadcasted_iota(jnp.int32, (BLOCK, BLOCK), 0)
    kj = lax.broadcasted_iota(jnp.int32, (BLOCK, BLOCK), 1)
    neg = jnp.float32(-jnp.inf)
    bias_prev = jnp.where(kj >= qi, 0.0, neg)
    bias_next = jnp.where(kj <= qi, 0.0, neg)
    bias_prev0 = jnp.where(has_prev, bias_prev, neg)
    bias_next_last = jnp.where(has_next, bias_next, neg)
    scale2 = HEAD_DIM ** -0.5 * LOG2E
    n_blocks = TQ_ATT // BLOCK
    for b in range(n_blocks):
        rs = slice(b * BLOCK, (b + 1) * BLOCK)
        kb = k_all[b * BLOCK:(b + 3) * BLOCK]
        vb = v_all[b * BLOCK:(b + 3) * BLOCK]
        bp = bias_prev0 if b == 0 else bias_prev
        bn = bias_next_last if b == n_blocks - 1 else bias_next
        heads = [slice(g * HEAD_DIM, (g + 1) * HEAD_DIM) for g in range(GROUP)]
        s_ref, p_ref, r_ref = s_all.at[b], p_all.at[b], r_all.at[b]
        for g, cs in enumerate(heads):
            s_ref[g] = _dot_nt(q_ref[rs, cs], kb)
        for g, cs in enumerate(heads):
            sink2 = sink_ref[GROUP * h + g] * LOG2E
            for r in range(BLOCK // SLAB_ATT):
                rr = slice(r * SLAB_ATT, (r + 1) * SLAB_ATT)
                s = s_ref[g, rr, :] * scale2
                s_p = s[:, :BLOCK] + bp[rr]
                s_c = s[:, BLOCK:2 * BLOCK]
                s_n = s[:, 2 * BLOCK:] + bn[rr]
                m = jnp.maximum(jnp.max(jnp.maximum(jnp.maximum(s_p, s_c), s_n), axis=-1, keepdims=True), sink2)
                p_p, p_c, p_n = jnp.exp2(s_p - m), jnp.exp2(s_c - m), jnp.exp2(s_n - m)
                denom = jnp.sum(p_p + p_c + p_n, axis=-1, keepdims=True) + jnp.exp2(sink2 - m)
                p_ref[g, rr, :] = jnp.concatenate([p_p, p_c, p_n], axis=1).astype(BF16)
                r_ref[g, rr, :] = jnp.broadcast_to(1.0 / denom, (SLAB_ATT, LANES))
        for g, cs in enumerate(heads):
            o = jnp.dot(p_ref[g], vb, preferred_element_type=F32) * r_ref[g]
            o_ref[rs, cs] = (_sigmoid(ga_ref[rs, cs].astype(F32)) * o).astype(o_ref.dtype)


def _window_attention(slab, sink, seq_groups):
    T = slab.shape[0]
    n_tiles = T // TQ_ATT
    bpt = TQ_ATT // BLOCK
    last_blk = T // BLOCK - 1
    gw = GROUP * HEAD_DIM
    kcol, vcol, gcol = A_K // HEAD_DIM, A_V // HEAD_DIM, A_GA // gw
    nbytes = (2 * (3 * TQ_ATT * gw * 2 + 2 * TQ_ATT * HEAD_DIM * 2 + 4 * BLOCK * HEAD_DIM * 2)
              + bpt * GROUP * BLOCK * (3 * BLOCK * 6 + LANES * 4))
    return pl.pallas_call(
        functools.partial(_attn_kernel, seq_groups),
        grid=(n_tiles, N_KV_HEADS),
        in_specs=[
            pl.BlockSpec(memory_space=pltpu.SMEM),
            pl.BlockSpec((TQ_ATT, gw), lambda t, h: (t, h)),
            pl.BlockSpec((TQ_ATT, HEAD_DIM), lambda t, h: (t, kcol + h)),
            pl.BlockSpec((TQ_ATT, HEAD_DIM), lambda t, h: (t, vcol + h)),
            pl.BlockSpec((BLOCK, HEAD_DIM), lambda t, h: (jnp.maximum(t * bpt - 1, 0), kcol + h)),
            pl.BlockSpec((BLOCK, HEAD_DIM), lambda t, h: (jnp.minimum((t + 1) * bpt, last_blk), kcol + h)),
            pl.BlockSpec((BLOCK, HEAD_DIM), lambda t, h: (jnp.maximum(t * bpt - 1, 0), vcol + h)),
            pl.BlockSpec((BLOCK, HEAD_DIM), lambda t, h: (jnp.minimum((t + 1) * bpt, last_blk), vcol + h)),
            pl.BlockSpec((TQ_ATT, gw), lambda t, h: (t, gcol + h)),
        ],
        out_specs=pl.BlockSpec((TQ_ATT, gw), lambda t, h: (t, h)),
        out_shape=jax.ShapeDtypeStruct((T, D_ATT), BF16),
        scratch_shapes=[pltpu.VMEM((bpt, GROUP, BLOCK, 3 * BLOCK), F32),
                        pltpu.VMEM((bpt, GROUP, BLOCK, 3 * BLOCK), BF16),
                        pltpu.VMEM((bpt, GROUP, BLOCK, LANES), F32)],
        compiler_params=pltpu.CompilerParams(
            dimension_semantics=("parallel", "arbitrary"),
            vmem_limit_bytes=_vmem_limit(nbytes)),
        name="window_attention",
    )(sink, slab, slab, slab, slab, slab, slab, slab, slab)


def _hgrn_kernel(seq_groups, lb_ref, qf_ref, if_ref, zf_ref, qb_ref, ib_ref, zb_ref,
                 of_ref, ob_ref, st_ref, stb_ref, ds_ref, ops_ref, dec_ref, a_ref):
    g = pl.program_id(1)
    _, local, _ = _segment_position(g, seq_groups)

    @pl.when(local == 0)
    def _():
        st_ref[...] = jnp.zeros(st_ref.shape, F32)
        stb_ref[...] = jnp.zeros(stb_ref.shape, BF16)

    n_chunks = SEG_REC // CHUNK
    rin = lax.broadcasted_iota(jnp.int32, (CHUNK, LANES), 0)
    ti = lax.broadcasted_iota(jnp.int32, (CHUNK, CHUNK), 0)
    si = lax.broadcasted_iota(jnp.int32, (CHUNK, CHUNK), 1)

    def prepare(d, hh, rows, q_ref, i_ref, z_ref):
        fwd = d == 0
        cs = slice(hh * LANES, (hh + 1) * LANES)
        lb = lb_ref[:, cs]
        one_m_lb = 1.0 - lb
        q = q_ref[rows, cs].astype(F32)
        gate = one_m_lb * _sigmoid(z_ref[rows, cs])
        k = one_m_lb - gate
        logf = jnp.log2(lb + gate)
        b = logf
        for s in (1, 2, 4, 8, 16):
            b = b + jnp.where(rin >= s, pltpu.roll(b, s, 0), 0.0)
        tot = b[CHUNK - 1:CHUNK, :]
        if fwd:
            c = b
            cref = b[CHUNK // 2 - 1:CHUNK // 2, :]
        else:
            c = tot - b + logf
            cref = c[CHUNK // 2:CHUNK // 2 + 1, :]
        q_in = q * jnp.exp2(c - cref)
        k_in = k * jnp.exp2(cref - c)
        ops_ref[d, hh, 0] = q_in.astype(BF16)
        ops_ref[d, hh, 1] = k_in.astype(BF16)
        ops_ref[d, hh, 2] = (k_in * jnp.exp2(tot - cref)).astype(BF16)
        ops_ref[d, hh, 3] = (q_in * jnp.exp2(cref)).astype(BF16)
        ops_ref[d, hh, 4] = i_ref[rows, cs]
        dec_ref[d, hh] = jnp.broadcast_to(jnp.exp2(tot), (8, LANES))

    def intra(d, hh):
        a_ref[d, hh] = _dot_nt(ops_ref[d, hh, 0], ops_ref[d, hh, 1])

    def output(d, hh, rows, o_ref):
        cs = slice(hh * LANES, (hh + 1) * LANES)
        a = jnp.where((ti >= si) if d == 0 else (ti <= si), a_ref[d, hh], 0.0)
        o = (jnp.dot(a.astype(BF16), ops_ref[d, hh, 4], preferred_element_type=F32)
             + _dot_nt(ops_ref[d, hh, 3], stb_ref[d, hh]))
        o_ref[rows, cs] = o.astype(o_ref.dtype)
        ds_ref[d, hh] = _dot_tn(ops_ref[d, hh, 4], ops_ref[d, hh, 2])

    def update(d, hh):
        dec = dec_ref[d, hh, 0:1, :]
        for r in range(REC_VAL_DIM // CHUNK):
            rr = slice(r * CHUNK, (r + 1) * CHUNK)
            new = dec * st_ref[d, hh, rr, :] + ds_ref[d, hh, rr, :]
            st_ref[d, hh, rr, :] = new
            stb_ref[d, hh, rr, :] = new.astype(BF16)

    def scan(n, carry):
        rows_f = pl.ds(pl.multiple_of(n * CHUNK, CHUNK), CHUNK)
        rows_b = pl.ds(pl.multiple_of((n_chunks - 1 - n) * CHUNK, CHUNK), CHUNK)
        for hh in range(HB_REC):
            prepare(0, hh, rows_f, qf_ref, if_ref, zf_ref)
            prepare(1, hh, rows_b, qb_ref, ib_ref, zb_ref)
        for hh in range(HB_REC):
            intra(0, hh)
            intra(1, hh)
        for hh in range(HB_REC):
            output(0, hh, rows_f, of_ref)
            output(1, hh, rows_b, ob_ref)
        for hh in range(HB_REC):
            update(0, hh)
            update(1, hh)
        return carry

    lax.fori_loop(0, n_chunks, scan, 0, unroll=2)


def _hgrn2(slab, gates, lb_row, seq_groups):
    T = slab.shape[0]
    w = HB_REC * LANES

    def mirror(g):
        first, local, per = _segment_position(g, seq_groups)
        return first + per - 1 - local

    def fspec(off):
        return pl.BlockSpec((SEG_REC, w), lambda hb, g: (g, off // w + hb))

    def bspec(off):
        return pl.BlockSpec((SEG_REC, w), lambda hb, g: (mirror(g), off // w + hb))

    state = (2, HB_REC, REC_VAL_DIM, REC_KEY_DIM)
    nbytes = (2 * (4 * SEG_REC * w * 2 + 2 * SEG_REC * w * 4 + 2 * SEG_REC * w * 2)
              + 2 * HB_REC * (REC_VAL_DIM * REC_KEY_DIM * 10 + 5 * CHUNK * LANES * 2 + 8 * LANES * 4))
    out = jax.ShapeDtypeStruct((T, D_REC_V), BF16)
    return pl.pallas_call(
        functools.partial(_hgrn_kernel, seq_groups),
        grid=(N_REC_HEADS // HB_REC, T // SEG_REC),
        in_specs=[
            pl.BlockSpec((1, w), lambda hb, g: (0, hb)),
            fspec(A_QR), fspec(A_IR), fspec(0),
            bspec(A_QR), bspec(A_IR), bspec(D_REC_K),
        ],
        out_specs=[pl.BlockSpec((SEG_REC, w), lambda hb, g: (g, hb)),
                   pl.BlockSpec((SEG_REC, w), lambda hb, g: (mirror(g), hb))],
        out_shape=[out, out],
        scratch_shapes=[pltpu.VMEM(state, F32), pltpu.VMEM(state, BF16), pltpu.VMEM(state, F32),
                        pltpu.VMEM((2, HB_REC, 5, CHUNK, LANES), BF16),
                        pltpu.VMEM((2, HB_REC, 8, LANES), F32),
                        pltpu.VMEM((2, HB_REC, CHUNK, CHUNK), F32)],
        compiler_params=pltpu.CompilerParams(
            dimension_semantics=("parallel", "arbitrary"),
            vmem_limit_bytes=_vmem_limit(nbytes)),
        name="hgrn2",
    )(lb_row, slab, slab, gates, slab, slab, gates)


def _outproj_kernel(n_first, a_ref, of_ref, ob_ref, grl_ref, grh_ref, gtl_ref, gth_ref, xa_ref, xb_ref,
                    w_ref, rg_ref, g_ref, h_ref, m_ref):
    half = D_REC_V // 2
    for hh in range(N_REC_HEADS):
        cs = slice(hh * REC_VAL_DIM, (hh + 1) * REC_VAL_DIM)
        lo = hh * REC_VAL_DIM < half
        hs = cs if lo else slice(hh * REC_VAL_DIM - half, (hh + 1) * REC_VAL_DIM - half)
        rec = of_ref[:, cs].astype(F32) + ob_ref[:, cs].astype(F32)
        ms = jnp.mean(rec * rec, axis=-1, keepdims=True)
        y = (rec * lax.rsqrt(ms + RMS_EPS)) * rg_ref[:, cs]
        gr = (grl_ref if lo else grh_ref)[:, hs].astype(F32)
        gt = (gtl_ref if lo else gth_ref)[:, hs].astype(F32)
        y = _sigmoid(gt) * (y * (gr * _sigmoid(gr)))
        m_ref[:, cs] = (a_ref[:, cs].astype(F32) + y).astype(BF16)
    y = jnp.dot(m_ref[...], w_ref[...], preferred_element_type=F32)
    ms = jnp.mean(y * y, axis=-1, keepdims=True)
    h_ref[...] = (y * lax.rsqrt(ms + RMS_EPS)) * g_ref[...]

    @pl.when(pl.program_id(0) < n_first)
    def _():
        h_ref[...] += xa_ref[...]

    @pl.when(pl.program_id(0) >= n_first)
    def _():
        h_ref[...] += xb_ref[...]


def _out_projection(attn, rec_f, rec_b, slab, xa, xb, w_bf16, rec_gain, gain):
    T = attn.shape[0]
    n_first = xa.shape[0] // TM_OUT
    half = D_REC_V // 2
    assert A_GR % half == 0 and A_GT % half == 0
    tok = lambda i: (i, 0)
    nbytes = (2 * 3 * TM_OUT * D_MODEL * 2 + 2 * 4 * TM_OUT * half * 2 + 2 * 3 * TM_OUT * D_MODEL * 4
              + 2 * D_MODEL * D_MODEL * 2 + TM_OUT * D_MODEL * 2 + 2 * TM_OUT * D_MODEL * 4)
    return pl.pallas_call(
        functools.partial(_outproj_kernel, n_first),
        grid=(T // TM_OUT,),
        in_specs=[
            pl.BlockSpec((TM_OUT, D_MODEL), tok),
            pl.BlockSpec((TM_OUT, D_REC_V), tok),
            pl.BlockSpec((TM_OUT, D_REC_V), tok),
            pl.BlockSpec((TM_OUT, half), lambda i: (i, A_GR // half)),
            pl.BlockSpec((TM_OUT, half), lambda i: (i, A_GR // half + 1)),
            pl.BlockSpec((TM_OUT, half), lambda i: (i, A_GT // half)),
            pl.BlockSpec((TM_OUT, half), lambda i: (i, A_GT // half + 1)),
            *_two_source_specs(TM_OUT, n_first),
            pl.BlockSpec((D_MODEL, D_MODEL), lambda i: (0, 0)),
            pl.BlockSpec((1, D_REC_V), lambda i: (0, 0)),
            pl.BlockSpec((1, D_MODEL), lambda i: (0, 0)),
        ],
        out_specs=pl.BlockSpec((TM_OUT, D_MODEL), tok),
        out_shape=jax.ShapeDtypeStruct((T, D_MODEL), F32),
        scratch_shapes=[pltpu.VMEM((TM_OUT, D_MODEL), BF16)],
        compiler_params=pltpu.CompilerParams(
            dimension_semantics=("parallel",),
            vmem_limit_bytes=_vmem_limit(nbytes)),
        name="out_projection",
    )(attn, rec_f, rec_b, slab, slab, slab, slab, xa, xb, w_bf16, rec_gain, gain)


def _ffn_kernel(n_first, h_ref, gpre_ref, gpost_ref, wg_ref, wu_ref, wd_ref, oa_ref, ob_ref, hn_ref, acc_ref):
    i = pl.program_id(0)
    j = pl.program_id(1)

    @pl.when(j == 0)
    def _():
        def body(rows):
            h = h_ref[rows, :]
            ms = jnp.mean(h * h, axis=-1, keepdims=True)
            hn_ref[rows, :] = ((h * lax.rsqrt(ms + RMS_EPS)) * gpre_ref[...]).astype(BF16)
            acc_ref[rows, :] = jnp.zeros((ROWS_EW, D_MODEL), F32)
        _row_tiles(TM_FFN, body)

    hn = hn_ref[...]
    g = jnp.dot(hn, wg_ref[...], preferred_element_type=F32)
    u = jnp.dot(hn, wu_ref[...], preferred_element_type=F32)
    act = ((g * _sigmoid(g)) * u).astype(BF16)
    acc_ref[...] += jnp.dot(act, wd_ref[...], preferred_element_type=F32)

    def finish(o_ref):
        def body(rows):
            y = acc_ref[rows, :]
            ms = jnp.mean(y * y, axis=-1, keepdims=True)
            o_ref[rows, :] = h_ref[rows, :] + (y * lax.rsqrt(ms + RMS_EPS)) * gpost_ref[...]
        _row_tiles(TM_FFN, body)

    last = j == pl.num_programs(1) - 1

    @pl.when(jnp.logical_and(last, i < n_first))
    def _():
        finish(oa_ref)

    @pl.when(jnp.logical_and(last, i >= n_first))
    def _():
        finish(ob_ref)


def _ffn(h, n_tokens_first, gpre, gpost, wg, wu, wd):
    T = h.shape[0]
    n_first = n_tokens_first // TM_FFN
    nbytes = (2 * 3 * TM_FFN * D_MODEL * 4 + TM_FFN * D_MODEL * 2 + 2 * 3 * D_MODEL * TF_FFN * 2
              + 3 * TM_FFN * TF_FFN * 4 + 2 * TM_FFN * D_MODEL * 4)
    return pl.pallas_call(
        functools.partial(_ffn_kernel, n_first),
        grid=(T // TM_FFN, D_FF // TF_FFN),
        in_specs=[
            pl.BlockSpec((TM_FFN, D_MODEL), lambda i, j: (i, 0)),
            pl.BlockSpec((1, D_MODEL), lambda i, j: (0, 0)),
            pl.BlockSpec((1, D_MODEL), lambda i, j: (0, 0)),
            pl.BlockSpec((D_MODEL, TF_FFN), lambda i, j: (0, j)),
            pl.BlockSpec((D_MODEL, TF_FFN), lambda i, j: (0, j)),
            pl.BlockSpec((TF_FFN, D_MODEL), lambda i, j: (j, 0)),
        ],
        out_specs=list(_two_source_specs(TM_FFN, n_first)),
        out_shape=[jax.ShapeDtypeStruct((n_tokens_first, D_MODEL), F32),
                   jax.ShapeDtypeStruct((T - n_tokens_first, D_MODEL), F32)],
        scratch_shapes=[pltpu.VMEM((TM_FFN, D_MODEL), BF16), pltpu.VMEM((TM_FFN, D_MODEL), F32)],
        compiler_params=pltpu.CompilerParams(
            dimension_semantics=("arbitrary", "arbitrary"),
            vmem_limit_bytes=_vmem_limit(nbytes)),
        name="swiglu_ffn",
    )(h, gpre, gpost, wg, wu, wd)


def _rope_tables(max_len):
    pos = jnp.arange(max_len, dtype=F32)
    inv_freq = ROPE_THETA ** (-jnp.arange(ROPE_HALF, dtype=F32) / ROPE_HALF)
    ang = pos[:, None] * inv_freq[None, :]
    cos, sin = jnp.cos(ang), jnp.sin(ang)
    rest = HEAD_DIM - ROPE_DIM
    cos_t = jnp.concatenate([cos, cos, jnp.ones((max_len, rest), F32)], axis=1)
    sa_t = jnp.concatenate([-sin, jnp.zeros((max_len, HEAD_DIM - ROPE_HALF), F32)], axis=1)
    sb_t = jnp.concatenate([jnp.zeros((max_len, ROPE_HALF), F32), sin, jnp.zeros((max_len, rest), F32)], axis=1)
    return cos_t, sa_t, sb_t


def _seq_groups(seq_shapes, seg):
    groups, first = [], 0
    for B, L in seq_shapes:
        assert L % seg == 0
        groups.append((first, L // seg))
        first += B * L // seg
    return tuple(groups)


def _encoder_layer(xa, xb, seq_shapes, tables, w_in, sink, rec_norm, lb, w_out, norm_mix_pre, norm_mix_post,
                   norm_ffn_pre, norm_ffn_post, w_gate, w_up, w_down):
    row = lambda v: v.astype(F32).reshape(1, -1)
    xn = _stream_norm(xa, xb, row(norm_mix_pre))

    assert all(off % TN_IN == 0 for off in (OFF_K, OFF_QR, OFF_ZF, OFF_IR)) and 2 * D_KV == TN_IN
    gate_tiles = list(range(OFF_ZF // TN_IN, OFF_IR // TN_IN))
    slab_tiles = [t for t in range(D_IN // TN_IN) if t not in gate_tiles]
    kind_of = lambda t: ("rope" if t < OFF_K // TN_IN else "rope_k" if t == OFF_K // TN_IN
                         else "silu" if OFF_QR // TN_IN <= t < OFF_ZF // TN_IN else "plain")
    table_groups = _seq_groups(seq_shapes, TM_IN)
    slab = _in_projection(xn, w_in, tables, table_groups, slab_tiles, tuple(kind_of(t) for t in slab_tiles),
                          BF16, "in_projection")
    gates = _in_projection(xn, w_in, None, None, gate_tiles, ("plain",) * len(gate_tiles),
                           F32, "gate_projection")

    attn = _window_attention(slab, sink.astype(F32), _seq_groups(seq_shapes, TQ_ATT))
    rec_f, rec_b = _hgrn2(slab, gates, row(lb), _seq_groups(seq_shapes, SEG_REC))
    h = _out_projection(attn, rec_f, rec_b, slab, xa, xb, w_out.astype(BF16), row(rec_norm),
                        row(norm_mix_post))
    return _ffn(h, xa.shape[0], row(norm_ffn_pre), row(norm_ffn_post),
                w_gate.astype(BF16), w_up.astype(BF16), w_down.astype(BF16))


def kernel(x_prompt, x_sample, w_in, sink, rec_norm, lb_logits, w_out, norm_mix_pre, norm_mix_post,
           norm_ffn_pre, norm_ffn_post, w_gate, w_up, w_down):
    lb_all = jnp.cumsum(jax.nn.softmax(lb_logits.astype(F32), axis=0), axis=0)
    seq_shapes = (x_prompt.shape[:2], x_sample.shape[:2])
    xa = x_prompt.reshape(-1, D_MODEL)
    xb = x_sample.reshape(-1, D_MODEL)
    tables = _rope_tables(max(L for _, L in seq_shapes))
    for l in range(DEPTH):
        xa, xb = _encoder_layer(xa, xb, seq_shapes, tables, w_in[l], sink[l], rec_norm[l], lb_all[l], w_out[l],
                                norm_mix_pre[l], norm_mix_post[l], norm_ffn_pre[l], norm_ffn_post[l],
                                w_gate[l], w_up[l], w_down[l])
    return (xa.reshape(x_prompt.shape), xb.reshape(x_sample.shape))
```

```python
import functools

import jax
import jax.numpy as jnp
import numpy as np
from jax import lax
from jax.experimental import pallas as pl
from jax.experimental.pallas import tpu as pltpu

F32 = jnp.float32
BF16 = jnp.bfloat16

D_MODEL = 2048
DEPTH = 1
HEAD_DIM = 128
N_Q_HEADS = 16
N_KV_HEADS = 4
GROUP = N_Q_HEADS // N_KV_HEADS
WINDOW = 128
BLOCK = 128
ROPE_DIM = HEAD_DIM // 4
ROPE_HALF = ROPE_DIM // 2
ROPE_THETA = 500000.0
N_REC_HEADS = 16
REC_KEY_DIM = 128
REC_VAL_DIM = 128
CHUNK = 32
D_FF = -(-8 * D_MODEL // (3 * 256)) * 256
RMS_EPS = 1e-6
LOG2E = 1.4426950408889634

D_ATT = N_Q_HEADS * HEAD_DIM
D_KV = N_KV_HEADS * HEAD_DIM
D_REC_K = N_REC_HEADS * REC_KEY_DIM
D_REC_V = N_REC_HEADS * REC_VAL_DIM
SPLIT_SIZES = (D_ATT, D_KV, D_KV, D_REC_K, D_REC_K, D_REC_K, D_REC_V, D_REC_V, D_MODEL, D_MODEL)
D_IN = sum(SPLIT_SIZES)
(OFF_Q, OFF_K, OFF_V, OFF_QR, OFF_ZF, OFF_ZB, OFF_IR, OFF_GR, OFF_GA, OFF_GT) = (
    int(v) for v in np.concatenate([[0], np.cumsum(SPLIT_SIZES)[:-1]]))
D_GATES = OFF_IR - OFF_ZF
D_SLAB = D_IN - D_GATES
A_Q, A_K, A_V, A_QR = OFF_Q, OFF_K, OFF_V, OFF_QR
A_IR, A_GR, A_GA, A_GT = (OFF_IR - D_GATES, OFF_GR - D_GATES, OFF_GA - D_GATES, OFF_GT - D_GATES)

V7X_VMEM_CEILING = 56 * 1024 * 1024
LANES = 128

TM_NORM = 512
TM_IN, TN_IN = 1024, 1024
SUB_IN = 256
ROWS_EW = 256
SLAB_ATT = 32
TQ_ATT = 512
SEG_REC = 1024
HB_REC = 4
TM_OUT = 256
KC_OUT = 512
TM_FFN, TF_FFN = 512, 512


def _vmem_limit(nbytes):
    return int(min(V7X_VMEM_CEILING, nbytes * 1.25 + (4 << 20)))


def _sigmoid(x):
    return 1.0 / (1.0 + jnp.exp(-x))


def _dot_nt(a, b):
    return lax.dot_general(a, b, (((1,), (1,)), ((), ())), preferred_element_type=F32)


def _dot_tn(a, b):
    return lax.dot_general(a, b, (((0,), (0,)), ((), ())), preferred_element_type=F32)


def _rows(start, size):
    return pl.ds(start if isinstance(start, int) else pl.multiple_of(start, size), size)


def _row_tiles(n_rows, body):
    def step(r, c):
        body(_rows(r * ROWS_EW, ROWS_EW))
        return c
    lax.fori_loop(0, n_rows // ROWS_EW, step, 0)


def _segment_position(g, seq_groups):
    first, local, per = None, None, None
    for g0, n in reversed(seq_groups):
        loc = lax.rem(g - g0, n)
        fst = g - loc
        if first is None:
            first, local, per = fst, loc, n
        else:
            here = g < nxt
            first = jnp.where(here, fst, first)
            local = jnp.where(here, loc, local)
            per = jnp.where(here, n, per)
        nxt = g0
    return first, local, per


def _two_source_specs(tm, n_first):
    return (pl.BlockSpec((tm, D_MODEL), lambda i, *_: (jnp.minimum(i, n_first - 1), 0)),
            pl.BlockSpec((tm, D_MODEL), lambda i, *_: (jnp.maximum(i - n_first, 0), 0)))


def _norm_kernel(n_first, xa_ref, xb_ref, g_ref, o_ref):
    def run(x_ref):
        def body(rows):
            x = x_ref[rows, :]
            ms = jnp.mean(x * x, axis=-1, keepdims=True)
            o_ref[rows, :] = ((x * lax.rsqrt(ms + RMS_EPS)) * g_ref[...]).astype(o_ref.dtype)
        _row_tiles(TM_NORM, body)

    @pl.when(pl.program_id(0) < n_first)
    def _():
        run(xa_ref)

    @pl.when(pl.program_id(0) >= n_first)
    def _():
        run(xb_ref)


def _stream_norm(xa, xb, gain):
    T = xa.shape[0] + xb.shape[0]
    n_first = xa.shape[0] // TM_NORM
    nbytes = 2 * 2 * TM_NORM * D_MODEL * 4 + 2 * TM_NORM * D_MODEL * 2
    return pl.pallas_call(
        functools.partial(_norm_kernel, n_first),
        grid=(T // TM_NORM,),
        in_specs=[*_two_source_specs(TM_NORM, n_first), pl.BlockSpec((1, D_MODEL), lambda i: (0, 0))],
        out_specs=pl.BlockSpec((TM_NORM, D_MODEL), lambda i: (i, 0)),
        out_shape=jax.ShapeDtypeStruct((T, D_MODEL), BF16),
        compiler_params=pltpu.CompilerParams(
            dimension_semantics=("parallel",), vmem_limit_bytes=_vmem_limit(nbytes)),
        name="stream_norm",
    )(xa, xb, gain)


def _inproj_kernel(kinds, col_tiles_ref, *refs):
    del col_tiles_ref
    with_epilogue = any(k != "plain" for k in kinds)
    if with_epilogue:
        xn_ref, cos_ref, sa_ref, sb_ref, w_ref, o_ref, wb_ref, y_ref = refs
    else:
        xn_ref, w_ref, o_ref, wb_ref = refs
    j = pl.program_id(0)

    @pl.when(pl.program_id(1) == 0)
    def _():
        def body(rows):
            wb_ref[rows, :] = w_ref[rows, :].astype(BF16)
        _row_tiles(D_MODEL, body)

    def transform(kind, y, rows, lane_tile):
        if kind == "silu":
            return y * _sigmoid(y)
        if kind == "rope_k" and lane_tile >= N_KV_HEADS:
            return y
        return (y * cos_ref[rows, :] + pltpu.roll(y, HEAD_DIM - ROPE_HALF, 1) * sa_ref[rows, :]
                + pltpu.roll(y, ROPE_HALF, 1) * sb_ref[rows, :])

    def epilogue(kind, sub):
        for r in range(TM_IN // ROWS_EW):
            rows = slice(r * ROWS_EW, (r + 1) * ROWS_EW)
            for cc in range(SUB_IN // LANES):
                lane_tile = sub * (SUB_IN // LANES) + cc
                y = y_ref[sub, rows, cc * LANES:(cc + 1) * LANES]
                o_ref[rows, lane_tile * LANES:(lane_tile + 1) * LANES] = (
                    transform(kind, y, rows, lane_tile).astype(o_ref.dtype))

    for kind in sorted(set(kinds)):
        tiles = [t for t, k in enumerate(kinds) if k == kind]
        cond = functools.reduce(jnp.logical_or, [j == t for t in tiles])

        @pl.when(cond)
        def _(kind=kind):
            if kind == "plain":
                o_ref[...] = jnp.dot(xn_ref[...], wb_ref[...], preferred_element_type=F32).astype(o_ref.dtype)
            else:
                n_sub = TN_IN // SUB_IN
                for sub in range(n_sub):
                    y_ref[sub] = jnp.dot(xn_ref[...], wb_ref[:, sub * SUB_IN:(sub + 1) * SUB_IN],
                                         preferred_element_type=F32)
                    if sub > 0:
                        epilogue(kind, sub - 1)
                epilogue(kind, n_sub - 1)


def _in_projection(xn, w_in, tables, table_groups, col_tiles, kinds, out_dtype, name):
    T = xn.shape[0]
    n_col = len(col_tiles)
    col_tiles = jnp.asarray(col_tiles, jnp.int32)
    with_epilogue = any(k != "plain" for k in kinds)
    last_table_tile = max([t for t, k in enumerate(kinds) if k.startswith("rope")], default=-1)
    out_bytes = jnp.dtype(out_dtype).itemsize

    def table_map(j, i, ct):
        _, local, _ = _segment_position(i, table_groups)
        return (jnp.where(j <= last_table_tile, local, 0), 0)

    in_specs = [pl.BlockSpec((TM_IN, D_MODEL), lambda j, i, ct: (i, 0))]
    args = [xn]
    scratch = [pltpu.VMEM((D_MODEL, TN_IN), BF16)]
    nbytes = (2 * TM_IN * D_MODEL * 2 + 2 * D_MODEL * TN_IN * 4 + D_MODEL * TN_IN * 2
              + 2 * TM_IN * TN_IN * out_bytes + TM_IN * TN_IN * 4)
    if with_epilogue:
        in_specs += [pl.BlockSpec((TM_IN, LANES), table_map)] * 3
        args += list(tables)
        scratch.append(pltpu.VMEM((TN_IN // SUB_IN, TM_IN, SUB_IN), F32))
        nbytes += 2 * 3 * TM_IN * LANES * 4 + TM_IN * TN_IN * 4
    in_specs.append(pl.BlockSpec((D_MODEL, TN_IN), lambda j, i, ct: (0, ct[j])))
    args.append(w_in)
    return pl.pallas_call(
        functools.partial(_inproj_kernel, kinds),
        grid_spec=pltpu.PrefetchScalarGridSpec(
            num_scalar_prefetch=1,
            grid=(n_col, T // TM_IN),
            in_specs=in_specs,
            out_specs=pl.BlockSpec((TM_IN, TN_IN), lambda j, i, ct: (i, j)),
            scratch_shapes=scratch),
        out_shape=jax.ShapeDtypeStruct((T, n_col * TN_IN), out_dtype),
        compiler_params=pltpu.CompilerParams(
            dimension_semantics=("arbitrary", "arbitrary"),
            vmem_limit_bytes=_vmem_limit(nbytes)),
        name=name,
    )(col_tiles, *args)


def _attn_kernel(seq_groups, sink_ref, q_ref, kc_ref, vc_ref, kp_ref, kn_ref, vp_ref, vn_ref,
                 ga_ref, o_ref, s_all, p_all, r_all):
    t = pl.program_id(0)
    h = pl.program_id(1)
    _, local, per = _segment_position(t, seq_groups)
    has_prev = local != 0
    has_next = local != per - 1

    k_all = jnp.concatenate([kp_ref[...], kc_ref[...], kn_ref[...]], axis=0)
    v_all = jnp.concatenate([vp_ref[...], vc_ref[...], vn_ref[...]], axis=0)
    qi = lax.broadcasted_iota(jnp.int32, (BLOCK, BLOCK), 0)
    kj = lax.broadcasted_iota(jnp.int32, (BLOCK, BLOCK), 1)
    neg = jnp.float32(-jnp.inf)
    bias_prev = jnp.where(kj >= qi, 0.0, neg)
    bias_next = jnp.where(kj <= qi, 0.0, neg)
    bias_prev0 = jnp.where(has_prev, bias_prev, neg)
    bias_next_last = jnp.where(has_next, bias_next, neg)
    scale2 = HEAD_DIM ** -0.5 * LOG2E
    n_blocks = TQ_ATT // BLOCK
    for b in range(n_blocks):
        rs = slice(b * BLOCK, (b + 1) * BLOCK)
        kb = k_all[b * BLOCK:(b + 3) * BLOCK]
        vb = v_all[b * BLOCK:(b + 3) * BLOCK]
        bp = bias_prev0 if b == 0 else bias_prev
        bn = bias_next_last if b == n_blocks - 1 else bias_next
        heads = [slice(g * HEAD_DIM, (g + 1) * HEAD_DIM) for g in range(GROUP)]
        s_ref, p_ref, r_ref = s_all.at[b], p_all.at[b], r_all.at[b]
        for g, cs in enumerate(heads):
            s_ref[g] = _dot_nt(q_ref[rs, cs], kb)
        for g, cs in enumerate(heads):
            sink2 = sink_ref[GROUP * h + g] * LOG2E
            for r in range(BLOCK // SLAB_ATT):
                rr = slice(r * SLAB_ATT, (r + 1) * SLAB_ATT)
                s = s_ref[g, rr, :] * scale2
                s_p = s[:, :BLOCK] + bp[rr]
                s_c = s[:, BLOCK:2 * BLOCK]
                s_n = s[:, 2 * BLOCK:] + bn[rr]
                m = jnp.maximum(jnp.max(jnp.maximum(jnp.maximum(s_p, s_c), s_n), axis=-1, keepdims=True), sink2)
                p_p, p_c, p_n = jnp.exp2(s_p - m), jnp.exp2(s_c - m), jnp.exp2(s_n - m)
                denom = jnp.sum(p_p + p_c + p_n, axis=-1, keepdims=True) + jnp.exp2(sink2 - m)
                p_ref[g, rr, :] = jnp.concatenate([p_p, p_c, p_n], axis=1).astype(BF16)
                r_ref[g, rr, :] = jnp.broadcast_to(1.0 / denom, (SLAB_ATT, LANES))
        for g, cs in enumerate(heads):
            o = jnp.dot(p_ref[g], vb, preferred_element_type=F32) * r_ref[g]
            o_ref[rs, cs] = (_sigmoid(ga_ref[rs, cs].astype(F32)) * o).astype(o_ref.dtype)


def _window_attention(slab, sink, seq_groups):
    T = slab.shape[0]
    n_tiles = T // TQ_ATT
    bpt = TQ_ATT // BLOCK
    last_blk = T // BLOCK - 1
    gw = GROUP * HEAD_DIM
    kcol, vcol, gcol = A_K // HEAD_DIM, A_V // HEAD_DIM, A_GA // gw
    nbytes = (2 * (3 * TQ_ATT * gw * 2 + 2 * TQ_ATT * HEAD_DIM * 2 + 4 * BLOCK * HEAD_DIM * 2)
              + bpt * GROUP * BLOCK * (3 * BLOCK * 6 + LANES * 4))
    return pl.pallas_call(
        functools.partial(_attn_kernel, seq_groups),
        grid=(n_tiles, N_KV_HEADS),
        in_specs=[
            pl.BlockSpec(memory_space=pltpu.SMEM),
            pl.BlockSpec((TQ_ATT, gw), lambda t, h: (t, h)),
            pl.BlockSpec((TQ_ATT, HEAD_DIM), lambda t, h: (t, kcol + h)),
            pl.BlockSpec((TQ_ATT, HEAD_DIM), lambda t, h: (t, vcol + h)),
            pl.BlockSpec((BLOCK, HEAD_DIM), lambda t, h: (jnp.maximum(t * bpt - 1, 0), kcol + h)),
            pl.BlockSpec((BLOCK, HEAD_DIM), lambda t, h: (jnp.minimum((t + 1) * bpt, last_blk), kcol + h)),
            pl.BlockSpec((BLOCK, HEAD_DIM), lambda t, h: (jnp.maximum(t * bpt - 1, 0), vcol + h)),
            pl.BlockSpec((BLOCK, HEAD_DIM), lambda t, h: (jnp.minimum((t + 1) * bpt, last_blk), vcol + h)),
            pl.BlockSpec((TQ_ATT, gw), lambda t, h: (t, gcol + h)),
        ],
        out_specs=pl.BlockSpec((TQ_ATT, gw), lambda t, h: (t, h)),
        out_shape=jax.ShapeDtypeStruct((T, D_ATT), BF16),
        scratch_shapes=[pltpu.VMEM((bpt, GROUP, BLOCK, 3 * BLOCK), F32),
                        pltpu.VMEM((bpt, GROUP, BLOCK, 3 * BLOCK), BF16),
                        pltpu.VMEM((bpt, GROUP, BLOCK, LANES), F32)],
        compiler_params=pltpu.CompilerParams(
            dimension_semantics=("parallel", "arbitrary"),
            vmem_limit_bytes=_vmem_limit(nbytes)),
        name="window_attention",
    )(sink, slab, slab, slab, slab, slab, slab, slab, slab)


def _hgrn_kernel(seq_groups, lb_ref, qf_ref, if_ref, zf_ref, qb_ref, ib_ref, zb_ref,
                 of_ref, ob_ref, st_ref, stb_ref, ds_ref, ops_ref, dec_ref, a_ref):
    g = pl.program_id(1)
    _, local, _ = _segment_position(g, seq_groups)

    @pl.when(local == 0)
    def _():
        st_ref[...] = jnp.zeros(st_ref.shape, F32)
        stb_ref[...] = jnp.zeros(stb_ref.shape, BF16)

    n_chunks = SEG_REC // CHUNK
    rin = lax.broadcasted_iota(jnp.int32, (CHUNK, LANES), 0)
    ti = lax.broadcasted_iota(jnp.int32, (CHUNK, CHUNK), 0)
    si = lax.broadcasted_iota(jnp.int32, (CHUNK, CHUNK), 1)

    def prepare(d, hh, rows, q_ref, i_ref, z_ref):
        cs = slice(hh * LANES, (hh + 1) * LANES)
        lb = lb_ref[:, cs]
        one_m_lb = 1.0 - lb
        q = q_ref[rows, cs].astype(F32)
        gate = one_m_lb * _sigmoid(z_ref[rows, cs])
        k = one_m_lb - gate
        logf = jnp.log2(lb + gate)
        b = logf
        for s in (1, 2, 4, 8, 16):
            b = b + jnp.where(rin >= s, pltpu.roll(b, s, 0), 0.0)
        tot = b[CHUNK - 1:CHUNK, :]
        if d == 0:
            c = b
            cref = b[CHUNK // 2 - 1:CHUNK // 2, :]
        else:
            c = tot - b + logf
            cref = c[CHUNK // 2:CHUNK // 2 + 1, :]
        q_in = q * jnp.exp2(c - cref)
        k_in = k * jnp.exp2(cref - c)
        ops_ref[d, hh, 0] = q_in.astype(BF16)
        ops_ref[d, hh, 1] = k_in.astype(BF16)
        ops_ref[d, hh, 2] = (k_in * jnp.exp2(tot - cref)).astype(BF16)
        ops_ref[d, hh, 3] = (q_in * jnp.exp2(cref)).astype(BF16)
        ops_ref[d, hh, 4] = i_ref[rows, cs]
        dec_ref[d, hh] = jnp.broadcast_to(jnp.exp2(tot), (8, LANES))

    def intra(d, hh):
        a_ref[d, hh] = _dot_nt(ops_ref[d, hh, 0], ops_ref[d, hh, 1])

    def output(d, hh, rows, o_ref):
        cs = slice(hh * LANES, (hh + 1) * LANES)
        a = jnp.where((ti >= si) if d == 0 else (ti <= si), a_ref[d, hh], 0.0)
        o = (jnp.dot(a.astype(BF16), ops_ref[d, hh, 4], preferred_element_type=F32)
             + _dot_nt(ops_ref[d, hh, 3], stb_ref[d, hh]))
        o_ref[rows, cs] = o.astype(o_ref.dtype)
        ds_ref[d, hh] = _dot_tn(ops_ref[d, hh, 4], ops_ref[d, hh, 2])

    def update(d, hh):
        dec = dec_ref[d, hh, 0:1, :]
        for r in range(REC_VAL_DIM // CHUNK):
            rr = slice(r * CHUNK, (r + 1) * CHUNK)
            new = dec * st_ref[d, hh, rr, :] + ds_ref[d, hh, rr, :]
            st_ref[d, hh, rr, :] = new
            stb_ref[d, hh, rr, :] = new.astype(BF16)

    def scan(n, carry):
        rows_f = _rows(n * CHUNK, CHUNK)
        rows_b = _rows((n_chunks - 1 - n) * CHUNK, CHUNK)
        for hh in range(HB_REC):
            prepare(0, hh, rows_f, qf_ref, if_ref, zf_ref)
            prepare(1, hh, rows_b, qb_ref, ib_ref, zb_ref)
        for hh in range(HB_REC):
            intra(0, hh)
            intra(1, hh)
        for hh in range(HB_REC):
            output(0, hh, rows_f, of_ref)
            output(1, hh, rows_b, ob_ref)
        for hh in range(HB_REC):
            update(0, hh)
            update(1, hh)
        return carry

    lax.fori_loop(0, n_chunks, scan, 0, unroll=2)


def _hgrn2(slab, gates, lb_row, seq_groups):
    T = slab.shape[0]
    w = HB_REC * LANES

    def mirror(g):
        first, local, per = _segment_position(g, seq_groups)
        return first + per - 1 - local

    def fspec(off):
        return pl.BlockSpec((SEG_REC, w), lambda hb, g: (g, off // w + hb))

    def bspec(off):
        return pl.BlockSpec((SEG_REC, w), lambda hb, g: (mirror(g), off // w + hb))

    state = (2, HB_REC, REC_VAL_DIM, REC_KEY_DIM)
    nbytes = (2 * (4 * SEG_REC * w * 2 + 2 * SEG_REC * w * 4 + 2 * SEG_REC * w * 2)
              + 2 * HB_REC * (REC_VAL_DIM * REC_KEY_DIM * 10 + 5 * CHUNK * LANES * 2 + 8 * LANES * 4))
    out = jax.ShapeDtypeStruct((T, D_REC_V), BF16)
    return pl.pallas_call(
        functools.partial(_hgrn_kernel, seq_groups),
        grid=(N_REC_HEADS // HB_REC, T // SEG_REC),
        in_specs=[
            pl.BlockSpec((1, w), lambda hb, g: (0, hb)),
            fspec(A_QR), fspec(A_IR), fspec(0),
            bspec(A_QR), bspec(A_IR), bspec(D_REC_K),
        ],
        out_specs=[pl.BlockSpec((SEG_REC, w), lambda hb, g: (g, hb)),
                   pl.BlockSpec((SEG_REC, w), lambda hb, g: (mirror(g), hb))],
        out_shape=[out, out],
        scratch_shapes=[pltpu.VMEM(state, F32), pltpu.VMEM(state, BF16), pltpu.VMEM(state, F32),
                        pltpu.VMEM((2, HB_REC, 5, CHUNK, LANES), BF16),
                        pltpu.VMEM((2, HB_REC, 8, LANES), F32),
                        pltpu.VMEM((2, HB_REC, CHUNK, CHUNK), F32)],
        compiler_params=pltpu.CompilerParams(
            dimension_semantics=("parallel", "arbitrary"),
            vmem_limit_bytes=_vmem_limit(nbytes)),
        name="hgrn2",
    )(lb_row, slab, slab, gates, slab, slab, gates)


def _outproj_kernel(n_first, a_ref, of_ref, ob_ref, grl_ref, grh_ref, gtl_ref, gth_ref, xa_ref, xb_ref,
                    w_ref, rg_ref, g_ref, h_ref, m_ref):
    half = D_REC_V // 2
    heads_per_chunk = KC_OUT // REC_VAL_DIM
    y = None
    for c in range(D_MODEL // KC_OUT):
        for hh in range(c * heads_per_chunk, (c + 1) * heads_per_chunk):
            cs = slice(hh * REC_VAL_DIM, (hh + 1) * REC_VAL_DIM)
            lo = hh * REC_VAL_DIM < half
            hs = cs if lo else slice(hh * REC_VAL_DIM - half, (hh + 1) * REC_VAL_DIM - half)
            rec = of_ref[:, cs].astype(F32) + ob_ref[:, cs].astype(F32)
            ms = jnp.mean(rec * rec, axis=-1, keepdims=True)
            r = (rec * lax.rsqrt(ms + RMS_EPS)) * rg_ref[:, cs]
            gr = (grl_ref if lo else grh_ref)[:, hs].astype(F32)
            gt = (gtl_ref if lo else gth_ref)[:, hs].astype(F32)
            r = _sigmoid(gt) * (r * (gr * _sigmoid(gr)))
            m_ref[:, cs] = (a_ref[:, cs].astype(F32) + r).astype(BF16)
        ks = slice(c * KC_OUT, (c + 1) * KC_OUT)
        part = jnp.dot(m_ref[:, ks], w_ref[ks, :], preferred_element_type=F32)
        y = part if y is None else y + part
    ms = jnp.mean(y * y, axis=-1, keepdims=True)
    h_ref[...] = (y * lax.rsqrt(ms + RMS_EPS)) * g_ref[...]

    @pl.when(pl.program_id(0) < n_first)
    def _():
        h_ref[...] += xa_ref[...]

    @pl.when(pl.program_id(0) >= n_first)
    def _():
        h_ref[...] += xb_ref[...]


def _out_projection(attn, rec_f, rec_b, slab, xa, xb, w_bf16, rec_gain, gain):
    T = attn.shape[0]
    n_first = xa.shape[0] // TM_OUT
    half = D_REC_V // 2
    assert A_GR % half == 0 and A_GT % half == 0
    tok = lambda i: (i, 0)
    nbytes = (2 * 3 * TM_OUT * D_MODEL * 2 + 2 * 4 * TM_OUT * half * 2 + 2 * 3 * TM_OUT * D_MODEL * 4
              + 2 * D_MODEL * D_MODEL * 2 + TM_OUT * D_MODEL * 2 + 2 * TM_OUT * D_MODEL * 4)
    return pl.pallas_call(
        functools.partial(_outproj_kernel, n_first),
        grid=(T // TM_OUT,),
        in_specs=[
            pl.BlockSpec((TM_OUT, D_MODEL), tok),
            pl.BlockSpec((TM_OUT, D_REC_V), tok),
            pl.BlockSpec((TM_OUT, D_REC_V), tok),
            pl.BlockSpec((TM_OUT, half), lambda i: (i, A_GR // half)),
            pl.BlockSpec((TM_OUT, half), lambda i: (i, A_GR // half + 1)),
            pl.BlockSpec((TM_OUT, half), lambda i: (i, A_GT // half)),
            pl.BlockSpec((TM_OUT, half), lambda i: (i, A_GT // half + 1)),
            *_two_source_specs(TM_OUT, n_first),
            pl.BlockSpec((D_MODEL, D_MODEL), lambda i: (0, 0)),
            pl.BlockSpec((1, D_REC_V), lambda i: (0, 0)),
            pl.BlockSpec((1, D_MODEL), lambda i: (0, 0)),
        ],
        out_specs=pl.BlockSpec((TM_OUT, D_MODEL), tok),
        out_shape=jax.ShapeDtypeStruct((T, D_MODEL), F32),
        scratch_shapes=[pltpu.VMEM((TM_OUT, D_MODEL), BF16)],
        compiler_params=pltpu.CompilerParams(
            dimension_semantics=("parallel",),
            vmem_limit_bytes=_vmem_limit(nbytes)),
        name="out_projection",
    )(attn, rec_f, rec_b, slab, slab, slab, slab, xa, xb, w_bf16, rec_gain, gain)


def _ffn_kernel(n_first, h_ref, gpre_ref, gpost_ref, wg_ref, wu_ref, wd_ref, oa_ref, ob_ref, hn_ref, acc_ref):
    i = pl.program_id(0)
    j = pl.program_id(1)

    @pl.when(j == 0)
    def _():
        def body(rows):
            h = h_ref[rows, :]
            ms = jnp.mean(h * h, axis=-1, keepdims=True)
            hn_ref[rows, :] = ((h * lax.rsqrt(ms + RMS_EPS)) * gpre_ref[...]).astype(BF16)
            acc_ref[rows, :] = jnp.zeros((ROWS_EW, D_MODEL), F32)
        _row_tiles(TM_FFN, body)

    hn = hn_ref[...]
    g = jnp.dot(hn, wg_ref[...], preferred_element_type=F32)
    u = jnp.dot(hn, wu_ref[...], preferred_element_type=F32)
    act = ((g * _sigmoid(g)) * u).astype(BF16)
    acc_ref[...] += jnp.dot(act, wd_ref[...], preferred_element_type=F32)

    def finish(o_ref):
        def body(rows):
            y = acc_ref[rows, :]
            ms = jnp.mean(y * y, axis=-1, keepdims=True)
            o_ref[rows, :] = h_ref[rows, :] + (y * lax.rsqrt(ms + RMS_EPS)) * gpost_ref[...]
        _row_tiles(TM_FFN, body)

    last = j == pl.num_programs(1) - 1

    @pl.when(jnp.logical_and(last, i < n_first))
    def _():
        finish(oa_ref)

    @pl.when(jnp.logical_and(last, i >= n_first))
    def _():
        finish(ob_ref)


def _ffn(h, n_tokens_first, gpre, gpost, wg, wu, wd):
    T = h.shape[0]
    n_first = n_tokens_first // TM_FFN
    nbytes = (2 * 3 * TM_FFN * D_MODEL * 4 + TM_FFN * D_MODEL * 2 + 2 * 3 * D_MODEL * TF_FFN * 2
              + 3 * TM_FFN * TF_FFN * 4 + 2 * TM_FFN * D_MODEL * 4)
    return pl.pallas_call(
        functools.partial(_ffn_kernel, n_first),
        grid=(T // TM_FFN, D_FF // TF_FFN),
        in_specs=[
            pl.BlockSpec((TM_FFN, D_MODEL), lambda i, j: (i, 0)),
            pl.BlockSpec((1, D_MODEL), lambda i, j: (0, 0)),
            pl.BlockSpec((1, D_MODEL), lambda i, j: (0, 0)),
            pl.BlockSpec((D_MODEL, TF_FFN), lambda i, j: (0, j)),
            pl.BlockSpec((D_MODEL, TF_FFN), lambda i, j: (0, j)),
            pl.BlockSpec((TF_FFN, D_MODEL), lambda i, j: (j, 0)),
        ],
        out_specs=list(_two_source_specs(TM_FFN, n_first)),
        out_shape=[jax.ShapeDtypeStruct((n_tokens_first, D_MODEL), F32),
                   jax.ShapeDtypeStruct((T - n_tokens_first, D_MODEL), F32)],
        scratch_shapes=[pltpu.VMEM((TM_FFN, D_MODEL), BF16), pltpu.VMEM((TM_FFN, D_MODEL), F32)],
        compiler_params=pltpu.CompilerParams(
            dimension_semantics=("arbitrary", "arbitrary"),
            vmem_limit_bytes=_vmem_limit(nbytes)),
        name="swiglu_ffn",
    )(h, gpre, gpost, wg, wu, wd)


def _rope_tables(max_len):
    pos = jnp.arange(max_len, dtype=F32)
    inv_freq = ROPE_THETA ** (-jnp.arange(ROPE_HALF, dtype=F32) / ROPE_HALF)
    ang = pos[:, None] * inv_freq[None, :]
    cos, sin = jnp.cos(ang), jnp.sin(ang)
    rest = HEAD_DIM - ROPE_DIM
    cos_t = jnp.concatenate([cos, cos, jnp.ones((max_len, rest), F32)], axis=1)
    sa_t = jnp.concatenate([-sin, jnp.zeros((max_len, HEAD_DIM - ROPE_HALF), F32)], axis=1)
    sb_t = jnp.concatenate([jnp.zeros((max_len, ROPE_HALF), F32), sin, jnp.zeros((max_len, rest), F32)], axis=1)
    return cos_t, sa_t, sb_t


def _seq_groups(seq_shapes, seg):
    groups, first = [], 0
    for B, L in seq_shapes:
        assert L % seg == 0
        groups.append((first, L // seg))
        first += B * L // seg
    return tuple(groups)


def _encoder_layer(xa, xb, seq_shapes, tables, w_in, sink, rec_norm, lb, w_out, norm_mix_pre, norm_mix_post,
                   norm_ffn_pre, norm_ffn_post, w_gate, w_up, w_down):
    row = lambda v: v.astype(F32).reshape(1, -1)
    xn = _stream_norm(xa, xb, row(norm_mix_pre))

    assert all(off % TN_IN == 0 for off in (OFF_K, OFF_QR, OFF_ZF, OFF_IR)) and 2 * D_KV == TN_IN
    gate_tiles = list(range(OFF_ZF // TN_IN, OFF_IR // TN_IN))
    slab_tiles = [t for t in range(D_IN // TN_IN) if t not in gate_tiles]
    kind_of = lambda t: ("rope" if t < OFF_K // TN_IN else "rope_k" if t == OFF_K // TN_IN
                         else "silu" if OFF_QR // TN_IN <= t < OFF_ZF // TN_IN else "plain")
    table_groups = _seq_groups(seq_shapes, TM_IN)
    slab = _in_projection(xn, w_in, tables, table_groups, slab_tiles, tuple(kind_of(t) for t in slab_tiles),
                          BF16, "in_projection")
    gates = _in_projection(xn, w_in, None, None, gate_tiles, ("plain",) * len(gate_tiles),
                           F32, "gate_projection")

    attn = _window_attention(slab, sink.astype(F32), _seq_groups(seq_shapes, TQ_ATT))
    rec_f, rec_b = _hgrn2(slab, gates, row(lb), _seq_groups(seq_shapes, SEG_REC))
    h = _out_projection(attn, rec_f, rec_b, slab, xa, xb, w_out.astype(BF16), row(rec_norm),
                        row(norm_mix_post))
    return _ffn(h, xa.shape[0], row(norm_ffn_pre), row(norm_ffn_post),
                w_gate.astype(BF16), w_up.astype(BF16), w_down.astype(BF16))


def kernel(x_prompt, x_sample, w_in, sink, rec_norm, lb_logits, w_out, norm_mix_pre, norm_mix_post,
           norm_ffn_pre, norm_ffn_post, w_gate, w_up, w_down):
    lb_all = jnp.cumsum(jax.nn.softmax(lb_logits.astype(F32), axis=0), axis=0)
    seq_shapes = (x_prompt.shape[:2], x_sample.shape[:2])
    xa = x_prompt.reshape(-1, D_MODEL)
    xb = x_sample.reshape(-1, D_MODEL)
    tables = _rope_tables(max(L for _, L in seq_shapes))
    for l in range(DEPTH):
        xa, xb = _encoder_layer(xa, xb, seq_shapes, tables, w_in[l], sink[l], rec_norm[l], lb_all[l], w_out[l],
                                norm_mix_pre[l], norm_mix_post[l], norm_ffn_pre[l], norm_ffn_post[l],
                                w_gate[l], w_up[l], w_down[l])
    return (xa.reshape(x_prompt.shape), xb.reshape(x_sample.shape))
```

```python
import functools

import jax
import jax.numpy as jnp
import numpy as np
from jax import lax
from jax.experimental import pallas as pl
from jax.experimental.pallas import tpu as pltpu

F32 = jnp.float32
BF16 = jnp.bfloat16

D_MODEL = 2048
DEPTH = 1
HEAD_DIM = 128
N_Q_HEADS = 16
N_KV_HEADS = 4
GROUP = N_Q_HEADS // N_KV_HEADS
WINDOW = 128
BLOCK = 128
ROPE_DIM = HEAD_DIM // 4
ROPE_HALF = ROPE_DIM // 2
ROPE_THETA = 500000.0
N_REC_HEADS = 16
REC_KEY_DIM = 128
REC_VAL_DIM = 128
CHUNK = 32
D_FF = -(-8 * D_MODEL // (3 * 256)) * 256
RMS_EPS = 1e-6
LOG2E = 1.4426950408889634

D_ATT = N_Q_HEADS * HEAD_DIM
D_KV = N_KV_HEADS * HEAD_DIM
D_REC_K = N_REC_HEADS * REC_KEY_DIM
D_REC_V = N_REC_HEADS * REC_VAL_DIM
SPLIT_SIZES = (D_ATT, D_KV, D_KV, D_REC_K, D_REC_K, D_REC_K, D_REC_V, D_REC_V, D_MODEL, D_MODEL)
D_IN = sum(SPLIT_SIZES)
(OFF_Q, OFF_K, OFF_V, OFF_QR, OFF_ZF, OFF_ZB, OFF_IR, OFF_GR, OFF_GA, OFF_GT) = (
    int(v) for v in np.concatenate([[0], np.cumsum(SPLIT_SIZES)[:-1]]))
D_GATES = OFF_IR - OFF_ZF
D_SLAB = D_IN - D_GATES
A_Q, A_K, A_V, A_QR = OFF_Q, OFF_K, OFF_V, OFF_QR
A_IR, A_GR, A_GA, A_GT = (OFF_IR - D_GATES, OFF_GR - D_GATES, OFF_GA - D_GATES, OFF_GT - D_GATES)

V7X_VMEM_CEILING = 56 * 1024 * 1024
LANES = 128

TM_NORM = 512
TM_IN, TN_IN = 1024, 1024
SUB_IN = 256
ROWS_EW = 256
SLAB_ATT = 32
TQ_ATT = 512
SEG_REC = 1024
HB_REC = 4
TM_OUT = 256
KC_OUT = 512
TM_FFN, TF_FFN = 512, 512


def _vmem_limit(nbytes):
    return int(min(V7X_VMEM_CEILING, nbytes * 1.25 + (4 << 20)))


def _sigmoid(x):
    return 1.0 / (1.0 + jnp.exp(-x))


def _dot_nt(a, b):
    return lax.dot_general(a, b, (((1,), (1,)), ((), ())), preferred_element_type=F32)


def _dot_tn(a, b):
    return lax.dot_general(a, b, (((0,), (0,)), ((), ())), preferred_element_type=F32)


def _rows(start, size):
    return pl.ds(start if isinstance(start, int) else pl.multiple_of(start, size), size)


def _row_tiles(n_rows, body):
    def step(r, c):
        body(_rows(r * ROWS_EW, ROWS_EW))
        return c
    lax.fori_loop(0, n_rows // ROWS_EW, step, 0)


def _segment_position(g, seq_groups):
    first, local, per = None, None, None
    for g0, n in reversed(seq_groups):
        loc = lax.rem(g - g0, n)
        fst = g - loc
        if first is None:
            first, local, per = fst, loc, n
        else:
            here = g < nxt
            first = jnp.where(here, fst, first)
            local = jnp.where(here, loc, local)
            per = jnp.where(here, n, per)
        nxt = g0
    return first, local, per


def _two_source_specs(tm, n_first):
    return (pl.BlockSpec((tm, D_MODEL), lambda i, *_: (jnp.minimum(i, n_first - 1), 0)),
            pl.BlockSpec((tm, D_MODEL), lambda i, *_: (jnp.maximum(i - n_first, 0), 0)))


def _norm_kernel(n_first, xa_ref, xb_ref, g_ref, o_ref):
    def run(x_ref):
        def body(rows):
            x = x_ref[rows, :]
            ms = jnp.mean(x * x, axis=-1, keepdims=True)
            o_ref[rows, :] = ((x * lax.rsqrt(ms + RMS_EPS)) * g_ref[...]).astype(o_ref.dtype)
        _row_tiles(TM_NORM, body)

    @pl.when(pl.program_id(0) < n_first)
    def _():
        run(xa_ref)

    @pl.when(pl.program_id(0) >= n_first)
    def _():
        run(xb_ref)


def _stream_norm(xa, xb, gain):
    T = xa.shape[0] + xb.shape[0]
    n_first = xa.shape[0] // TM_NORM
    nbytes = 2 * 2 * TM_NORM * D_MODEL * 4 + 2 * TM_NORM * D_MODEL * 2
    return pl.pallas_call(
        functools.partial(_norm_kernel, n_first),
        grid=(T // TM_NORM,),
        in_specs=[*_two_source_specs(TM_NORM, n_first), pl.BlockSpec((1, D_MODEL), lambda i: (0, 0))],
        out_specs=pl.BlockSpec((TM_NORM, D_MODEL), lambda i: (i, 0)),
        out_shape=jax.ShapeDtypeStruct((T, D_MODEL), BF16),
        compiler_params=pltpu.CompilerParams(
            dimension_semantics=("parallel",), vmem_limit_bytes=_vmem_limit(nbytes)),
        name="stream_norm",
    )(xa, xb, gain)


def _inproj_kernel(kinds, col_tiles_ref, *refs):
    del col_tiles_ref
    with_epilogue = any(k != "plain" for k in kinds)
    if with_epilogue:
        xn_ref, cos_ref, sa_ref, sb_ref, w_ref, o_ref, wb_ref, y_ref = refs
    else:
        xn_ref, w_ref, o_ref, wb_ref = refs
    j = pl.program_id(0)

    @pl.when(pl.program_id(1) == 0)
    def _():
        def body(rows):
            wb_ref[rows, :] = w_ref[rows, :].astype(BF16)
        _row_tiles(D_MODEL, body)

    def transform(kind, y, rows, lane_tile):
        if kind == "silu":
            return y * _sigmoid(y)
        if kind == "rope_k" and lane_tile >= N_KV_HEADS:
            return y
        return (y * cos_ref[rows, :] + pltpu.roll(y, HEAD_DIM - ROPE_HALF, 1) * sa_ref[rows, :]
                + pltpu.roll(y, ROPE_HALF, 1) * sb_ref[rows, :])

    def epilogue(kind, sub):
        for r in range(TM_IN // ROWS_EW):
            rows = slice(r * ROWS_EW, (r + 1) * ROWS_EW)
            for cc in range(SUB_IN // LANES):
                lane_tile = sub * (SUB_IN // LANES) + cc
                y = y_ref[sub, rows, cc * LANES:(cc + 1) * LANES]
                o_ref[rows, lane_tile * LANES:(lane_tile + 1) * LANES] = (
                    transform(kind, y, rows, lane_tile).astype(o_ref.dtype))

    for kind in sorted(set(kinds)):
        tiles = [t for t, k in enumerate(kinds) if k == kind]
        cond = functools.reduce(jnp.logical_or, [j == t for t in tiles])

        @pl.when(cond)
        def _(kind=kind):
            if kind == "plain":
                o_ref[...] = jnp.dot(xn_ref[...], wb_ref[...], preferred_element_type=F32).astype(o_ref.dtype)
            else:
                n_sub = TN_IN // SUB_IN
                for sub in range(n_sub):
                    y_ref[sub] = jnp.dot(xn_ref[...], wb_ref[:, sub * SUB_IN:(sub + 1) * SUB_IN],
                                         preferred_element_type=F32)
                    if sub > 0:
                        epilogue(kind, sub - 1)
                epilogue(kind, n_sub - 1)


def _in_projection(xn, w_in, tables, table_groups, col_tiles, kinds, out_dtype, name):
    T = xn.shape[0]
    n_col = len(col_tiles)
    col_tiles = jnp.asarray(col_tiles, jnp.int32)
    with_epilogue = any(k != "plain" for k in kinds)
    last_table_tile = max([t for t, k in enumerate(kinds) if k.startswith("rope")], default=-1)
    out_bytes = jnp.dtype(out_dtype).itemsize

    def table_map(j, i, ct):
        _, local, _ = _segment_position(i, table_groups)
        return (jnp.where(j <= last_table_tile, local, 0), 0)

    in_specs = [pl.BlockSpec((TM_IN, D_MODEL), lambda j, i, ct: (i, 0))]
    args = [xn]
    scratch = [pltpu.VMEM((D_MODEL, TN_IN), BF16)]
    nbytes = (2 * TM_IN * D_MODEL * 2 + 2 * D_MODEL * TN_IN * 4 + D_MODEL * TN_IN * 2
              + 2 * TM_IN * TN_IN * out_bytes + TM_IN * TN_IN * 4)
    if with_epilogue:
        in_specs += [pl.BlockSpec((TM_IN, LANES), table_map)] * 3
        args += list(tables)
        scratch.append(pltpu.VMEM((TN_IN // SUB_IN, TM_IN, SUB_IN), F32))
        nbytes += 2 * 3 * TM_IN * LANES * 4 + TM_IN * TN_IN * 4
    in_specs.append(pl.BlockSpec((D_MODEL, TN_IN), lambda j, i, ct: (0, ct[j])))
    args.append(w_in)
    return pl.pallas_call(
        functools.partial(_inproj_kernel, kinds),
        grid_spec=pltpu.PrefetchScalarGridSpec(
            num_scalar_prefetch=1,
            grid=(n_col, T // TM_IN),
            in_specs=in_specs,
            out_specs=pl.BlockSpec((TM_IN, TN_IN), lambda j, i, ct: (i, j)),
            scratch_shapes=scratch),
        out_shape=jax.ShapeDtypeStruct((T, n_col * TN_IN), out_dtype),
        compiler_params=pltpu.CompilerParams(
            dimension_semantics=("arbitrary", "arbitrary"),
            vmem_limit_bytes=_vmem_limit(nbytes)),
        name=name,
    )(col_tiles, *args)


def _attn_kernel(seq_groups, sink_ref, q_ref, kc_ref, vc_ref, kp_ref, kn_ref, vp_ref, vn_ref,
                 ga_ref, o_ref, s_all, p_all, r_all):
    t = pl.program_id(0)
    h = pl.program_id(1)
    _, local, per = _segment_position(t, seq_groups)
    has_prev = local != 0
    has_next = local != per - 1

    k_all = jnp.concatenate([kp_ref[...], kc_ref[...], kn_ref[...]], axis=0)
    v_all = jnp.concatenate([vp_ref[...], vc_ref[...], vn_ref[...]], axis=0)
    qi = lax.broadcasted_iota(jnp.int32, (BLOCK, BLOCK), 0)
    kj = lax.broadcasted_iota(jnp.int32, (BLOCK, BLOCK), 1)
    neg = jnp.float32(-jnp.inf)
    bias_prev = jnp.where(kj >= qi, 0.0, neg)
    bias_next = jnp.where(kj <= qi, 0.0, neg)
    bias_prev0 = jnp.where(has_prev, bias_prev, neg)
    bias_next_last = jnp.where(has_next, bias_next, neg)
    scale2 = HEAD_DIM ** -0.5 * LOG2E
    n_blocks = TQ_ATT // BLOCK
    for b in range(n_blocks):
        rs = slice(b * BLOCK, (b + 1) * BLOCK)
        kb = k_all[b * BLOCK:(b + 3) * BLOCK]
        vb = v_all[b * BLOCK:(b + 3) * BLOCK]
        bp = bias_prev0 if b == 0 else bias_prev
        bn = bias_next_last if b == n_blocks - 1 else bias_next
        heads = [slice(g * HEAD_DIM, (g + 1) * HEAD_DIM) for g in range(GROUP)]
        s_ref, p_ref, r_ref = s_all.at[b], p_all.at[b], r_all.at[b]
        for g, cs in enumerate(heads):
            s_ref[g] = _dot_nt(q_ref[rs, cs], kb)
        for g, cs in enumerate(heads):
            sink2 = sink_ref[GROUP * h + g] * LOG2E
            for r in range(BLOCK // SLAB_ATT):
                rr = slice(r * SLAB_ATT, (r + 1) * SLAB_ATT)
                s = s_ref[g, rr, :] * scale2
                s_p = s[:, :BLOCK] + bp[rr]
                s_c = s[:, BLOCK:2 * BLOCK]
                s_n = s[:, 2 * BLOCK:] + bn[rr]
                m = jnp.maximum(jnp.max(jnp.maximum(jnp.maximum(s_p, s_c), s_n), axis=-1, keepdims=True), sink2)
                p_p, p_c, p_n = jnp.exp2(s_p - m), jnp.exp2(s_c - m), jnp.exp2(s_n - m)
                denom = jnp.sum(p_p + p_c + p_n, axis=-1, keepdims=True) + jnp.exp2(sink2 - m)
                p_ref[g, rr, :] = jnp.concatenate([p_p, p_c, p_n], axis=1).astype(BF16)
                r_ref[g, rr, :] = jnp.broadcast_to(1.0 / denom, (SLAB_ATT, LANES))
        for g, cs in enumerate(heads):
            o = jnp.dot(p_ref[g], vb, preferred_element_type=F32) * r_ref[g]
            o_ref[rs, cs] = (_sigmoid(ga_ref[rs, cs].astype(F32)) * o).astype(o_ref.dtype)


def _window_attention(slab, sink, seq_groups):
    T = slab.shape[0]
    n_tiles = T // TQ_ATT
    bpt = TQ_ATT // BLOCK
    last_blk = T // BLOCK - 1
    gw = GROUP * HEAD_DIM
    kcol, vcol, gcol = A_K // HEAD_DIM, A_V // HEAD_DIM, A_GA // gw
    nbytes = (2 * (3 * TQ_ATT * gw * 2 + 2 * TQ_ATT * HEAD_DIM * 2 + 4 * BLOCK * HEAD_DIM * 2)
              + bpt * GROUP * BLOCK * (3 * BLOCK * 6 + LANES * 4))
    return pl.pallas_call(
        functools.partial(_attn_kernel, seq_groups),
        grid=(n_tiles, N_KV_HEADS),
        in_specs=[
            pl.BlockSpec(memory_space=pltpu.SMEM),
            pl.BlockSpec((TQ_ATT, gw), lambda t, h: (t, h)),
            pl.BlockSpec((TQ_ATT, HEAD_DIM), lambda t, h: (t, kcol + h)),
            pl.BlockSpec((TQ_ATT, HEAD_DIM), lambda t, h: (t, vcol + h)),
            pl.BlockSpec((BLOCK, HEAD_DIM), lambda t, h: (jnp.maximum(t * bpt - 1, 0), kcol + h)),
            pl.BlockSpec((BLOCK, HEAD_DIM), lambda t, h: (jnp.minimum((t + 1) * bpt, last_blk), kcol + h)),
            pl.BlockSpec((BLOCK, HEAD_DIM), lambda t, h: (jnp.maximum(t * bpt - 1, 0), vcol + h)),
            pl.BlockSpec((BLOCK, HEAD_DIM), lambda t, h: (jnp.minimum((t + 1) * bpt, last_blk), vcol + h)),
            pl.BlockSpec((TQ_ATT, gw), lambda t, h: (t, gcol + h)),
        ],
        out_specs=pl.BlockSpec((TQ_ATT, gw), lambda t, h: (t, h)),
        out_shape=jax.ShapeDtypeStruct((T, D_ATT), BF16),
        scratch_shapes=[pltpu.VMEM((bpt, GROUP, BLOCK, 3 * BLOCK), F32),
                        pltpu.VMEM((bpt, GROUP, BLOCK, 3 * BLOCK), BF16),
                        pltpu.VMEM((bpt, GROUP, BLOCK, LANES), F32)],
        compiler_params=pltpu.CompilerParams(
            dimension_semantics=("parallel", "arbitrary"),
            vmem_limit_bytes=_vmem_limit(nbytes)),
        name="window_attention",
    )(sink, slab, slab, slab, slab, slab, slab, slab, slab)


def _hgrn_kernel(seq_groups, lb_ref, qf_ref, if_ref, zf_ref, qb_ref, ib_ref, zb_ref,
                 of_ref, ob_ref, st_ref, stb_ref, ds_ref, ops_ref, kst_ref, i2_ref, dec_ref, a_ref):
    g = pl.program_id(1)
    _, local, _ = _segment_position(g, seq_groups)

    @pl.when(local == 0)
    def _():
        st_ref[...] = jnp.zeros(st_ref.shape, F32)
        stb_ref[...] = jnp.zeros(stb_ref.shape, BF16)
        kst_ref[...] = jnp.zeros(kst_ref.shape, BF16)

    n_chunks = SEG_REC // CHUNK
    rin = lax.broadcasted_iota(jnp.int32, (CHUNK, LANES), 0)
    ti = lax.broadcasted_iota(jnp.int32, (CHUNK, CHUNK), 0)
    si = lax.broadcasted_iota(jnp.int32, (CHUNK, CHUNK), 1)

    def prepare(d, hh, rows, q_ref, i_ref, z_ref):
        cs = slice(hh * LANES, (hh + 1) * LANES)
        lb = lb_ref[:, cs]
        one_m_lb = 1.0 - lb
        q = q_ref[rows, cs].astype(F32)
        gate = one_m_lb * _sigmoid(z_ref[rows, cs])
        k = one_m_lb - gate
        logf = jnp.log2(lb + gate)
        b = logf
        for s in (1, 2, 4, 8, 16):
            b = b + jnp.where(rin >= s, pltpu.roll(b, s, 0), 0.0)
        tot = b[CHUNK - 1:CHUNK, :]
        if d == 0:
            c = b
            cref = b[CHUNK // 2 - 1:CHUNK // 2, :]
        else:
            c = tot - b + logf
            cref = c[CHUNK // 2:CHUNK // 2 + 1, :]
        q_in = q * jnp.exp2(c - cref)
        k_in = k * jnp.exp2(cref - c)
        p, half = hh // 2, hh % 2
        rb = slice(half * CHUNK, (half + 1) * CHUNK)
        ops_ref[d, hh, 0] = q_in.astype(BF16)
        ops_ref[d, hh, 1] = k_in.astype(BF16)
        ops_ref[d, hh, 2] = (q_in * jnp.exp2(cref)).astype(BF16)
        kst_ref[d, p, rb, half * LANES:(half + 1) * LANES] = (
            k_in * jnp.exp2(tot - cref)).astype(BF16)
        i2_ref[d, p, rb, :] = i_ref[rows, cs]
        dec_ref[d, hh] = jnp.broadcast_to(jnp.exp2(tot), (8, LANES))

    def intra(d, hh):
        a_ref[d, hh] = _dot_nt(ops_ref[d, hh, 0], ops_ref[d, hh, 1])

    def output(d, hh, rows, o_ref):
        cs = slice(hh * LANES, (hh + 1) * LANES)
        half = hh % 2
        a = jnp.where((ti >= si) if d == 0 else (ti <= si), a_ref[d, hh], 0.0)
        i_c = i2_ref[d, hh // 2, half * CHUNK:(half + 1) * CHUNK, :]
        o = (jnp.dot(a.astype(BF16), i_c, preferred_element_type=F32)
             + _dot_nt(ops_ref[d, hh, 2], stb_ref[d, hh]))
        o_ref[rows, cs] = o.astype(o_ref.dtype)

    def delta(d, p):
        ds_ref[d, p] = _dot_tn(i2_ref[d, p], kst_ref[d, p])

    def update(d, hh):
        dec = dec_ref[d, hh, 0:1, :]
        ls = slice((hh % 2) * LANES, (hh % 2 + 1) * LANES)
        for r in range(REC_VAL_DIM // CHUNK):
            rr = slice(r * CHUNK, (r + 1) * CHUNK)
            new = dec * st_ref[d, hh, rr, :] + ds_ref[d, hh // 2, rr, ls]
            st_ref[d, hh, rr, :] = new
            stb_ref[d, hh, rr, :] = new.astype(BF16)

    def scan(n, carry):
        rows_f = _rows(n * CHUNK, CHUNK)
        rows_b = _rows((n_chunks - 1 - n) * CHUNK, CHUNK)
        for hh in range(HB_REC):
            prepare(0, hh, rows_f, qf_ref, if_ref, zf_ref)
            prepare(1, hh, rows_b, qb_ref, ib_ref, zb_ref)
        for hh in range(HB_REC):
            intra(0, hh)
            intra(1, hh)
        for hh in range(HB_REC):
            output(0, hh, rows_f, of_ref)
            output(1, hh, rows_b, ob_ref)
        for p in range(HB_REC // 2):
            delta(0, p)
            delta(1, p)
        for hh in range(HB_REC):
            update(0, hh)
            update(1, hh)
        return carry

    lax.fori_loop(0, n_chunks, scan, 0, unroll=2)


def _hgrn2(slab, gates, lb_row, seq_groups):
    T = slab.shape[0]
    w = HB_REC * LANES

    def mirror(g):
        first, local, per = _segment_position(g, seq_groups)
        return first + per - 1 - local

    def fspec(off):
        return pl.BlockSpec((SEG_REC, w), lambda hb, g: (g, off // w + hb))

    def bspec(off):
        return pl.BlockSpec((SEG_REC, w), lambda hb, g: (mirror(g), off // w + hb))

    assert HB_REC % 2 == 0
    state = (2, HB_REC, REC_VAL_DIM, REC_KEY_DIM)
    pairs = (2, HB_REC // 2)
    nbytes = (2 * (4 * SEG_REC * w * 2 + 2 * SEG_REC * w * 4 + 2 * SEG_REC * w * 2)
              + 2 * HB_REC * (REC_VAL_DIM * REC_KEY_DIM * 10 + 6 * CHUNK * LANES * 2 + 8 * LANES * 4))
    out = jax.ShapeDtypeStruct((T, D_REC_V), BF16)
    return pl.pallas_call(
        functools.partial(_hgrn_kernel, seq_groups),
        grid=(N_REC_HEADS // HB_REC, T // SEG_REC),
        in_specs=[
            pl.BlockSpec((1, w), lambda hb, g: (0, hb)),
            fspec(A_QR), fspec(A_IR), fspec(0),
            bspec(A_QR), bspec(A_IR), bspec(D_REC_K),
        ],
        out_specs=[pl.BlockSpec((SEG_REC, w), lambda hb, g: (g, hb)),
                   pl.BlockSpec((SEG_REC, w), lambda hb, g: (mirror(g), hb))],
        out_shape=[out, out],
        scratch_shapes=[pltpu.VMEM(state, F32), pltpu.VMEM(state, BF16),
                        pltpu.VMEM((*pairs, REC_VAL_DIM, 2 * REC_KEY_DIM), F32),
                        pltpu.VMEM((2, HB_REC, 3, CHUNK, LANES), BF16),
                        pltpu.VMEM((*pairs, 2 * CHUNK, 2 * LANES), BF16),
                        pltpu.VMEM((*pairs, 2 * CHUNK, LANES), BF16),
                        pltpu.VMEM((2, HB_REC, 8, LANES), F32),
                        pltpu.VMEM((2, HB_REC, CHUNK, CHUNK), F32)],
        compiler_params=pltpu.CompilerParams(
            dimension_semantics=("parallel", "arbitrary"),
            vmem_limit_bytes=_vmem_limit(nbytes)),
        name="hgrn2",
    )(lb_row, slab, slab, gates, slab, slab, gates)


def _outproj_kernel(n_first, a_ref, of_ref, ob_ref, grl_ref, grh_ref, gtl_ref, gth_ref, xa_ref, xb_ref,
                    w_ref, rg_ref, g_ref, h_ref, m_ref):
    half = D_REC_V // 2
    heads_per_chunk = KC_OUT // REC_VAL_DIM
    y = None
    for c in range(D_MODEL // KC_OUT):
        for hh in range(c * heads_per_chunk, (c + 1) * heads_per_chunk):
            cs = slice(hh * REC_VAL_DIM, (hh + 1) * REC_VAL_DIM)
            lo = hh * REC_VAL_DIM < half
            hs = cs if lo else slice(hh * REC_VAL_DIM - half, (hh + 1) * REC_VAL_DIM - half)
            rec = of_ref[:, cs].astype(F32) + ob_ref[:, cs].astype(F32)
            ms = jnp.mean(rec * rec, axis=-1, keepdims=True)
            r = (rec * lax.rsqrt(ms + RMS_EPS)) * rg_ref[:, cs]
            gr = (grl_ref if lo else grh_ref)[:, hs].astype(F32)
            gt = (gtl_ref if lo else gth_ref)[:, hs].astype(F32)
            r = _sigmoid(gt) * (r * (gr * _sigmoid(gr)))
            m_ref[:, cs] = (a_ref[:, cs].astype(F32) + r).astype(BF16)
        ks = slice(c * KC_OUT, (c + 1) * KC_OUT)
        part = jnp.dot(m_ref[:, ks], w_ref[ks, :], preferred_element_type=F32)
        y = part if y is None else y + part
    ms = jnp.mean(y * y, axis=-1, keepdims=True)
    h_ref[...] = (y * lax.rsqrt(ms + RMS_EPS)) * g_ref[...]

    @pl.when(pl.program_id(0) < n_first)
    def _():
        h_ref[...] += xa_ref[...]

    @pl.when(pl.program_id(0) >= n_first)
    def _():
        h_ref[...] += xb_ref[...]


def _out_projection(attn, rec_f, rec_b, slab, xa, xb, w_bf16, rec_gain, gain):
    T = attn.shape[0]
    n_first = xa.shape[0] // TM_OUT
    half = D_REC_V // 2
    assert A_GR % half == 0 and A_GT % half == 0
    tok = lambda i: (i, 0)
    nbytes = (2 * 3 * TM_OUT * D_MODEL * 2 + 2 * 4 * TM_OUT * half * 2 + 2 * 3 * TM_OUT * D_MODEL * 4
              + 2 * D_MODEL * D_MODEL * 2 + TM_OUT * D_MODEL * 2 + 2 * TM_OUT * D_MODEL * 4)
    return pl.pallas_call(
        functools.partial(_outproj_kernel, n_first),
        grid=(T // TM_OUT,),
        in_specs=[
            pl.BlockSpec((TM_OUT, D_MODEL), tok),
            pl.BlockSpec((TM_OUT, D_REC_V), tok),
            pl.BlockSpec((TM_OUT, D_REC_V), tok),
            pl.BlockSpec((TM_OUT, half), lambda i: (i, A_GR // half)),
            pl.BlockSpec((TM_OUT, half), lambda i: (i, A_GR // half + 1)),
            pl.BlockSpec((TM_OUT, half), lambda i: (i, A_GT // half)),
            pl.BlockSpec((TM_OUT, half), lambda i: (i, A_GT // half + 1)),
            *_two_source_specs(TM_OUT, n_first),
            pl.BlockSpec((D_MODEL, D_MODEL), lambda i: (0, 0)),
            pl.BlockSpec((1, D_REC_V), lambda i: (0, 0)),
            pl.BlockSpec((1, D_MODEL), lambda i: (0, 0)),
        ],
        out_specs=pl.BlockSpec((TM_OUT, D_MODEL), tok),
        out_shape=jax.ShapeDtypeStruct((T, D_MODEL), F32),
        scratch_shapes=[pltpu.VMEM((TM_OUT, D_MODEL), BF16)],
        compiler_params=pltpu.CompilerParams(
            dimension_semantics=("parallel",),
            vmem_limit_bytes=_vmem_limit(nbytes)),
        name="out_projection",
    )(attn, rec_f, rec_b, slab, slab, slab, slab, xa, xb, w_bf16, rec_gain, gain)


def _ffn_kernel(n_first, h_ref, gpre_ref, gpost_ref, wg_ref, wu_ref, wd_ref, oa_ref, ob_ref, hn_ref, acc_ref):
    i = pl.program_id(0)
    j = pl.program_id(1)

    @pl.when(j == 0)
    def _():
        def body(rows):
            h = h_ref[rows, :]
            ms = jnp.mean(h * h, axis=-1, keepdims=True)
            hn_ref[rows, :] = ((h * lax.rsqrt(ms + RMS_EPS)) * gpre_ref[...]).astype(BF16)
            acc_ref[rows, :] = jnp.zeros((ROWS_EW, D_MODEL), F32)
        _row_tiles(TM_FFN, body)

    hn = hn_ref[...]
    g = jnp.dot(hn, wg_ref[...], preferred_element_type=F32)
    u = jnp.dot(hn, wu_ref[...], preferred_element_type=F32)
    act = ((g * _sigmoid(g)) * u).astype(BF16)
    acc_ref[...] += jnp.dot(act, wd_ref[...], preferred_element_type=F32)

    def finish(o_ref):
        def body(rows):
            y = acc_ref[rows, :]
            ms = jnp.mean(y * y, axis=-1, keepdims=True)
            o_ref[rows, :] = h_ref[rows, :] + (y * lax.rsqrt(ms + RMS_EPS)) * gpost_ref[...]
        _row_tiles(TM_FFN, body)

    last = j == pl.num_programs(1) - 1

    @pl.when(jnp.logical_and(last, i < n_first))
    def _():
        finish(oa_ref)

    @pl.when(jnp.logical_and(last, i >= n_first))
    def _():
        finish(ob_ref)


def _ffn(h, n_tokens_first, gpre, gpost, wg, wu, wd):
    T = h.shape[0]
    n_first = n_tokens_first // TM_FFN
    nbytes = (2 * 3 * TM_FFN * D_MODEL * 4 + TM_FFN * D_MODEL * 2 + 2 * 3 * D_MODEL * TF_FFN * 2
              + 3 * TM_FFN * TF_FFN * 4 + 2 * TM_FFN * D_MODEL * 4)
    return pl.pallas_call(
        functools.partial(_ffn_kernel, n_first),
        grid=(T // TM_FFN, D_FF // TF_FFN),
        in_specs=[
            pl.BlockSpec((TM_FFN, D_MODEL), lambda i, j: (i, 0)),
            pl.BlockSpec((1, D_MODEL), lambda i, j: (0, 0)),
            pl.BlockSpec((1, D_MODEL), lambda i, j: (0, 0)),
            pl.BlockSpec((D_MODEL, TF_FFN), lambda i, j: (0, j)),
            pl.BlockSpec((D_MODEL, TF_FFN), lambda i, j: (0, j)),
            pl.BlockSpec((TF_FFN, D_MODEL), lambda i, j: (j, 0)),
        ],
        out_specs=list(_two_source_specs(TM_FFN, n_first)),
        out_shape=[jax.ShapeDtypeStruct((n_tokens_first, D_MODEL), F32),
                   jax.ShapeDtypeStruct((T - n_tokens_first, D_MODEL), F32)],
        scratch_shapes=[pltpu.VMEM((TM_FFN, D_MODEL), BF16), pltpu.VMEM((TM_FFN, D_MODEL), F32)],
        compiler_params=pltpu.CompilerParams(
            dimension_semantics=("arbitrary", "arbitrary"),
            vmem_limit_bytes=_vmem_limit(nbytes)),
        name="swiglu_ffn",
    )(h, gpre, gpost, wg, wu, wd)


def _rope_tables(max_len):
    pos = jnp.arange(max_len, dtype=F32)
    inv_freq = ROPE_THETA ** (-jnp.arange(ROPE_HALF, dtype=F32) / ROPE_HALF)
    ang = pos[:, None] * inv_freq[None, :]
    cos, sin = jnp.cos(ang), jnp.sin(ang)
    rest = HEAD_DIM - ROPE_DIM
    cos_t = jnp.concatenate([cos, cos, jnp.ones((max_len, rest), F32)], axis=1)
    sa_t = jnp.concatenate([-sin, jnp.zeros((max_len, HEAD_DIM - ROPE_HALF), F32)], axis=1)
    sb_t = jnp.concatenate([jnp.zeros((max_len, ROPE_HALF), F32), sin, jnp.zeros((max_len, rest), F32)], axis=1)
    return cos_t, sa_t, sb_t


def _seq_groups(seq_shapes, seg):
    groups, first = [], 0
    for B, L in seq_shapes:
        assert L % seg == 0
        groups.append((first, L // seg))
        first += B * L // seg
    return tuple(groups)


def _encoder_layer(xa, xb, seq_shapes, tables, w_in, sink, rec_norm, lb, w_out, norm_mix_pre, norm_mix_post,
                   norm_ffn_pre, norm_ffn_post, w_gate, w_up, w_down):
    row = lambda v: v.astype(F32).reshape(1, -1)
    xn = _stream_norm(xa, xb, row(norm_mix_pre))

    assert all(off % TN_IN == 0 for off in (OFF_K, OFF_QR, OFF_ZF, OFF_IR)) and 2 * D_KV == TN_IN
    gate_tiles = list(range(OFF_ZF // TN_IN, OFF_IR // TN_IN))
    slab_tiles = [t for t in range(D_IN // TN_IN) if t not in gate_tiles]
    kind_of = lambda t: ("rope" if t < OFF_K // TN_IN else "rope_k" if t == OFF_K // TN_IN
                         else "silu" if OFF_QR // TN_IN <= t < OFF_ZF // TN_IN else "plain")
    table_groups = _seq_groups(seq_shapes, TM_IN)
    slab = _in_projection(xn, w_in, tables, table_groups, slab_tiles, tuple(kind_of(t) for t in slab_tiles),
                          BF16, "in_projection")
    gates = _in_projection(xn, w_in, None, None, gate_tiles, ("plain",) * len(gate_tiles),
                           F32, "gate_projection")

    attn = _window_attention(slab, sink.astype(F32), _seq_groups(seq_shapes, TQ_ATT))
    rec_f, rec_b = _hgrn2(slab, gates, row(lb), _seq_groups(seq_shapes, SEG_REC))
    h = _out_projection(attn, rec_f, rec_b, slab, xa, xb, w_out.astype(BF16), row(rec_norm),
                        row(norm_mix_post))
    return _ffn(h, xa.shape[0], row(norm_ffn_pre), row(norm_ffn_post),
                w_gate.astype(BF16), w_up.astype(BF16), w_down.astype(BF16))


def kernel(x_prompt, x_sample, w_in, sink, rec_norm, lb_logits, w_out, norm_mix_pre, norm_mix_post,
           norm_ffn_pre, norm_ffn_post, w_gate, w_up, w_down):
    lb_all = jnp.cumsum(jax.nn.softmax(lb_logits.astype(F32), axis=0), axis=0)
    seq_shapes = (x_prompt.shape[:2], x_sample.shape[:2])
    xa = x_prompt.reshape(-1, D_MODEL)
    xb = x_sample.reshape(-1, D_MODEL)
    tables = _rope_tables(max(L for _, L in seq_shapes))
    for l in range(DEPTH):
        xa, xb = _encoder_layer(xa, xb, seq_shapes, tables, w_in[l], sink[l], rec_norm[l], lb_all[l], w_out[l],
                                norm_mix_pre[l], norm_mix_post[l], norm_ffn_pre[l], norm_ffn_post[l],
                                w_gate[l], w_up[l], w_down[l])
    return (xa.reshape(x_prompt.shape), xb.reshape(x_sample.shape))
```

```python
import functools

import jax
import jax.numpy as jnp
import numpy as np
from jax import lax
from jax.experimental import pallas as pl
from jax.experimental.pallas import tpu as pltpu

F32 = jnp.float32
BF16 = jnp.bfloat16

D_MODEL = 2048
DEPTH = 1
HEAD_DIM = 128
N_Q_HEADS = 16
N_KV_HEADS = 4
GROUP = N_Q_HEADS // N_KV_HEADS
WINDOW = 128
BLOCK = 128
ROPE_DIM = HEAD_DIM // 4
ROPE_HALF = ROPE_DIM // 2
ROPE_THETA = 500000.0
N_REC_HEADS = 16
REC_KEY_DIM = 128
REC_VAL_DIM = 128
CHUNK = 32
D_FF = -(-8 * D_MODEL // (3 * 256)) * 256
RMS_EPS = 1e-6
LOG2E = 1.4426950408889634

D_ATT = N_Q_HEADS * HEAD_DIM
D_KV = N_KV_HEADS * HEAD_DIM
D_REC_K = N_REC_HEADS * REC_KEY_DIM
D_REC_V = N_REC_HEADS * REC_VAL_DIM
SPLIT_SIZES = (D_ATT, D_KV, D_KV, D_REC_K, D_REC_K, D_REC_K, D_REC_V, D_REC_V, D_MODEL, D_MODEL)
D_IN = sum(SPLIT_SIZES)
(OFF_Q, OFF_K, OFF_V, OFF_QR, OFF_ZF, OFF_ZB, OFF_IR, OFF_GR, OFF_GA, OFF_GT) = (
    int(v) for v in np.concatenate([[0], np.cumsum(SPLIT_SIZES)[:-1]]))
D_GATES = OFF_IR - OFF_ZF
D_SLAB = D_IN - D_GATES
A_Q, A_K, A_V, A_QR = OFF_Q, OFF_K, OFF_V, OFF_QR
A_IR, A_GR, A_GA, A_GT = (OFF_IR - D_GATES, OFF_GR - D_GATES, OFF_GA - D_GATES, OFF_GT - D_GATES)

V7X_VMEM_CEILING = 56 * 1024 * 1024
LANES = 128

TM_NORM = 512
TM_IN, TN_IN = 1024, 1024
SUB_IN = 256
ROWS_EW = 256
SLAB_ATT = 32
TQ_ATT = 512
SEG_REC = 512
HB_REC = 8
TM_OUT = 256
KC_OUT = 512
TM_FFN, TF_FFN = 512, 512


def _vmem_limit(nbytes):
    return int(min(V7X_VMEM_CEILING, nbytes * 1.25 + (4 << 20)))


def _sigmoid(x):
    return 1.0 / (1.0 + jnp.exp(-x))


def _dot_nt(a, b):
    return lax.dot_general(a, b, (((1,), (1,)), ((), ())), preferred_element_type=F32)


def _dot_tn(a, b):
    return lax.dot_general(a, b, (((0,), (0,)), ((), ())), preferred_element_type=F32)


def _rows(start, size):
    return pl.ds(start if isinstance(start, int) else pl.multiple_of(start, size), size)


def _row_tiles(n_rows, body):
    def step(r, c):
        body(_rows(r * ROWS_EW, ROWS_EW))
        return c
    lax.fori_loop(0, n_rows // ROWS_EW, step, 0)


def _segment_position(g, seq_groups):
    first, local, per = None, None, None
    for g0, n in reversed(seq_groups):
        loc = lax.rem(g - g0, n)
        fst = g - loc
        if first is None:
            first, local, per = fst, loc, n
        else:
            here = g < nxt
            first = jnp.where(here, fst, first)
            local = jnp.where(here, loc, local)
            per = jnp.where(here, n, per)
        nxt = g0
    return first, local, per


def _two_source_specs(tm, n_first):
    return (pl.BlockSpec((tm, D_MODEL), lambda i, *_: (jnp.minimum(i, n_first - 1), 0)),
            pl.BlockSpec((tm, D_MODEL), lambda i, *_: (jnp.maximum(i - n_first, 0), 0)))


def _norm_kernel(n_first, xa_ref, xb_ref, g_ref, o_ref):
    def run(x_ref):
        def body(rows):
            x = x_ref[rows, :]
            ms = jnp.mean(x * x, axis=-1, keepdims=True)
            o_ref[rows, :] = ((x * lax.rsqrt(ms + RMS_EPS)) * g_ref[...]).astype(o_ref.dtype)
        _row_tiles(TM_NORM, body)

    @pl.when(pl.program_id(0) < n_first)
    def _():
        run(xa_ref)

    @pl.when(pl.program_id(0) >= n_first)
    def _():
        run(xb_ref)


def _stream_norm(xa, xb, gain):
    T = xa.shape[0] + xb.shape[0]
    n_first = xa.shape[0] // TM_NORM
    nbytes = 2 * 2 * TM_NORM * D_MODEL * 4 + 2 * TM_NORM * D_MODEL * 2
    return pl.pallas_call(
        functools.partial(_norm_kernel, n_first),
        grid=(T // TM_NORM,),
        in_specs=[*_two_source_specs(TM_NORM, n_first), pl.BlockSpec((1, D_MODEL), lambda i: (0, 0))],
        out_specs=pl.BlockSpec((TM_NORM, D_MODEL), lambda i: (i, 0)),
        out_shape=jax.ShapeDtypeStruct((T, D_MODEL), BF16),
        compiler_params=pltpu.CompilerParams(
            dimension_semantics=("parallel",), vmem_limit_bytes=_vmem_limit(nbytes)),
        name="stream_norm",
    )(xa, xb, gain)


def _inproj_kernel(kinds, col_tiles_ref, *refs):
    del col_tiles_ref
    with_epilogue = any(k != "plain" for k in kinds)
    if with_epilogue:
        xn_ref, cos_ref, sa_ref, sb_ref, w_ref, o_ref, wb_ref, y_ref = refs
    else:
        xn_ref, w_ref, o_ref, wb_ref = refs
    j = pl.program_id(0)

    @pl.when(pl.program_id(1) == 0)
    def _():
        def body(rows):
            wb_ref[rows, :] = w_ref[rows, :].astype(BF16)
        _row_tiles(D_MODEL, body)

    def transform(kind, y, rows, lane_tile):
        if kind == "silu":
            return y * _sigmoid(y)
        if kind == "rope_k" and lane_tile >= N_KV_HEADS:
            return y
        return (y * cos_ref[rows, :] + pltpu.roll(y, HEAD_DIM - ROPE_HALF, 1) * sa_ref[rows, :]
                + pltpu.roll(y, ROPE_HALF, 1) * sb_ref[rows, :])

    def epilogue(kind, sub):
        for r in range(TM_IN // ROWS_EW):
            rows = slice(r * ROWS_EW, (r + 1) * ROWS_EW)
            for cc in range(SUB_IN // LANES):
                lane_tile = sub * (SUB_IN // LANES) + cc
                y = y_ref[sub, rows, cc * LANES:(cc + 1) * LANES]
                o_ref[rows, lane_tile * LANES:(lane_tile + 1) * LANES] = (
                    transform(kind, y, rows, lane_tile).astype(o_ref.dtype))

    for kind in sorted(set(kinds)):
        tiles = [t for t, k in enumerate(kinds) if k == kind]
        cond = functools.reduce(jnp.logical_or, [j == t for t in tiles])

        @pl.when(cond)
        def _(kind=kind):
            if kind == "plain":
                o_ref[...] = jnp.dot(xn_ref[...], wb_ref[...], preferred_element_type=F32).astype(o_ref.dtype)
            else:
                n_sub = TN_IN // SUB_IN
                for sub in range(n_sub):
                    y_ref[sub] = jnp.dot(xn_ref[...], wb_ref[:, sub * SUB_IN:(sub + 1) * SUB_IN],
                                         preferred_element_type=F32)
                    if sub > 0:
                        epilogue(kind, sub - 1)
                epilogue(kind, n_sub - 1)


def _in_projection(xn, w_in, tables, table_groups, col_tiles, kinds, out_dtype, name):
    T = xn.shape[0]
    n_col = len(col_tiles)
    col_tiles = jnp.asarray(col_tiles, jnp.int32)
    with_epilogue = any(k != "plain" for k in kinds)
    last_table_tile = max([t for t, k in enumerate(kinds) if k.startswith("rope")], default=-1)
    out_bytes = jnp.dtype(out_dtype).itemsize

    def table_map(j, i, ct):
        _, local, _ = _segment_position(i, table_groups)
        return (jnp.where(j <= last_table_tile, local, 0), 0)

    in_specs = [pl.BlockSpec((TM_IN, D_MODEL), lambda j, i, ct: (i, 0))]
    args = [xn]
    scratch = [pltpu.VMEM((D_MODEL, TN_IN), BF16)]
    nbytes = (2 * TM_IN * D_MODEL * 2 + 2 * D_MODEL * TN_IN * 4 + D_MODEL * TN_IN * 2
              + 2 * TM_IN * TN_IN * out_bytes + TM_IN * TN_IN * 4)
    if with_epilogue:
        in_specs += [pl.BlockSpec((TM_IN, LANES), table_map)] * 3
        args += list(tables)
        scratch.append(pltpu.VMEM((TN_IN // SUB_IN, TM_IN, SUB_IN), F32))
        nbytes += 2 * 3 * TM_IN * LANES * 4 + TM_IN * TN_IN * 4
    in_specs.append(pl.BlockSpec((D_MODEL, TN_IN), lambda j, i, ct: (0, ct[j])))
    args.append(w_in)
    return pl.pallas_call(
        functools.partial(_inproj_kernel, kinds),
        grid_spec=pltpu.PrefetchScalarGridSpec(
            num_scalar_prefetch=1,
            grid=(n_col, T // TM_IN),
            in_specs=in_specs,
            out_specs=pl.BlockSpec((TM_IN, TN_IN), lambda j, i, ct: (i, j)),
            scratch_shapes=scratch),
        out_shape=jax.ShapeDtypeStruct((T, n_col * TN_IN), out_dtype),
        compiler_params=pltpu.CompilerParams(
            dimension_semantics=("arbitrary", "arbitrary"),
            vmem_limit_bytes=_vmem_limit(nbytes)),
        name=name,
    )(col_tiles, *args)


def _attn_kernel(seq_groups, sink_ref, q_ref, kc_ref, vc_ref, kp_ref, kn_ref, vp_ref, vn_ref,
                 ga_ref, o_ref, s_all, p_all, r_all):
    t = pl.program_id(0)
    h = pl.program_id(1)
    _, local, per = _segment_position(t, seq_groups)
    has_prev = local != 0
    has_next = local != per - 1

    k_all = jnp.concatenate([kp_ref[...], kc_ref[...], kn_ref[...]], axis=0)
    v_all = jnp.concatenate([vp_ref[...], vc_ref[...], vn_ref[...]], axis=0)
    qi = lax.broadcasted_iota(jnp.int32, (BLOCK, BLOCK), 0)
    kj = lax.broadcasted_iota(jnp.int32, (BLOCK, BLOCK), 1)
    neg = jnp.float32(-jnp.inf)
    bias_prev = jnp.where(kj >= qi, 0.0, neg)
    bias_next = jnp.where(kj <= qi, 0.0, neg)
    bias_prev0 = jnp.where(has_prev, bias_prev, neg)
    bias_next_last = jnp.where(has_next, bias_next, neg)
    scale2 = HEAD_DIM ** -0.5 * LOG2E
    n_blocks = TQ_ATT // BLOCK
    for b in range(n_blocks):
        rs = slice(b * BLOCK, (b + 1) * BLOCK)
        kb = k_all[b * BLOCK:(b + 3) * BLOCK]
        vb = v_all[b * BLOCK:(b + 3) * BLOCK]
        bp = bias_prev0 if b == 0 else bias_prev
        bn = bias_next_last if b == n_blocks - 1 else bias_next
        heads = [slice(g * HEAD_DIM, (g + 1) * HEAD_DIM) for g in range(GROUP)]
        s_ref, p_ref, r_ref = s_all.at[b], p_all.at[b], r_all.at[b]
        for g, cs in enumerate(heads):
            s_ref[g] = _dot_nt(q_ref[rs, cs], kb)
        for g, cs in enumerate(heads):
            sink2 = sink_ref[GROUP * h + g] * LOG2E
            for r in range(BLOCK // SLAB_ATT):
                rr = slice(r * SLAB_ATT, (r + 1) * SLAB_ATT)
                s = s_ref[g, rr, :] * scale2
                s_p = s[:, :BLOCK] + bp[rr]
                s_c = s[:, BLOCK:2 * BLOCK]
                s_n = s[:, 2 * BLOCK:] + bn[rr]
                m = jnp.maximum(jnp.max(jnp.maximum(jnp.maximum(s_p, s_c), s_n), axis=-1, keepdims=True), sink2)
                p_p, p_c, p_n = jnp.exp2(s_p - m), jnp.exp2(s_c - m), jnp.exp2(s_n - m)
                denom = jnp.sum(p_p + p_c + p_n, axis=-1, keepdims=True) + jnp.exp2(sink2 - m)
                p_ref[g, rr, :] = jnp.concatenate([p_p, p_c, p_n], axis=1).astype(BF16)
                r_ref[g, rr, :] = jnp.broadcast_to(1.0 / denom, (SLAB_ATT, LANES))
        for g, cs in enumerate(heads):
            o = jnp.dot(p_ref[g], vb, preferred_element_type=F32) * r_ref[g]
            o_ref[rs, cs] = (_sigmoid(ga_ref[rs, cs].astype(F32)) * o).astype(o_ref.dtype)


def _window_attention(slab, sink, seq_groups):
    T = slab.shape[0]
    n_tiles = T // TQ_ATT
    bpt = TQ_ATT // BLOCK
    last_blk = T // BLOCK - 1
    gw = GROUP * HEAD_DIM
    kcol, vcol, gcol = A_K // HEAD_DIM, A_V // HEAD_DIM, A_GA // gw
    nbytes = (2 * (3 * TQ_ATT * gw * 2 + 2 * TQ_ATT * HEAD_DIM * 2 + 4 * BLOCK * HEAD_DIM * 2)
              + bpt * GROUP * BLOCK * (3 * BLOCK * 6 + LANES * 4))
    return pl.pallas_call(
        functools.partial(_attn_kernel, seq_groups),
        grid=(n_tiles, N_KV_HEADS),
        in_specs=[
            pl.BlockSpec(memory_space=pltpu.SMEM),
            pl.BlockSpec((TQ_ATT, gw), lambda t, h: (t, h)),
            pl.BlockSpec((TQ_ATT, HEAD_DIM), lambda t, h: (t, kcol + h)),
            pl.BlockSpec((TQ_ATT, HEAD_DIM), lambda t, h: (t, vcol + h)),
            pl.BlockSpec((BLOCK, HEAD_DIM), lambda t, h: (jnp.maximum(t * bpt - 1, 0), kcol + h)),
            pl.BlockSpec((BLOCK, HEAD_DIM), lambda t, h: (jnp.minimum((t + 1) * bpt, last_blk), kcol + h)),
            pl.BlockSpec((BLOCK, HEAD_DIM), lambda t, h: (jnp.maximum(t * bpt - 1, 0), vcol + h)),
            pl.BlockSpec((BLOCK, HEAD_DIM), lambda t, h: (jnp.minimum((t + 1) * bpt, last_blk), vcol + h)),
            pl.BlockSpec((TQ_ATT, gw), lambda t, h: (t, gcol + h)),
        ],
        out_specs=pl.BlockSpec((TQ_ATT, gw), lambda t, h: (t, h)),
        out_shape=jax.ShapeDtypeStruct((T, D_ATT), BF16),
        scratch_shapes=[pltpu.VMEM((bpt, GROUP, BLOCK, 3 * BLOCK), F32),
                        pltpu.VMEM((bpt, GROUP, BLOCK, 3 * BLOCK), BF16),
                        pltpu.VMEM((bpt, GROUP, BLOCK, LANES), F32)],
        compiler_params=pltpu.CompilerParams(
            dimension_semantics=("parallel", "arbitrary"),
            vmem_limit_bytes=_vmem_limit(nbytes)),
        name="window_attention",
    )(sink, slab, slab, slab, slab, slab, slab, slab, slab)


def _hgrn_kernel(seq_groups, lb_ref, qf_ref, if_ref, zf_ref, qb_ref, ib_ref, zb_ref,
                 of_ref, ob_ref, st_ref, stb_ref, ds_ref, ops_ref, dec_ref, a_ref):
    g = pl.program_id(1)
    _, local, _ = _segment_position(g, seq_groups)

    @pl.when(local == 0)
    def _():
        st_ref[...] = jnp.zeros(st_ref.shape, F32)
        stb_ref[...] = jnp.zeros(stb_ref.shape, BF16)

    n_chunks = SEG_REC // CHUNK
    rin = lax.broadcasted_iota(jnp.int32, (CHUNK, LANES), 0)
    ti = lax.broadcasted_iota(jnp.int32, (CHUNK, CHUNK), 0)
    si = lax.broadcasted_iota(jnp.int32, (CHUNK, CHUNK), 1)

    def prepare(d, hh, rows, q_ref, i_ref, z_ref):
        cs = slice(hh * LANES, (hh + 1) * LANES)
        lb = lb_ref[:, cs]
        one_m_lb = 1.0 - lb
        q = q_ref[rows, cs].astype(F32)
        gate = one_m_lb * _sigmoid(z_ref[rows, cs])
        k = one_m_lb - gate
        logf = jnp.log2(lb + gate)
        b = logf
        for s in (1, 2, 4, 8, 16):
            b = b + jnp.where(rin >= s, pltpu.roll(b, s, 0), 0.0)
        tot = b[CHUNK - 1:CHUNK, :]
        if d == 0:
            c = b
            cref = b[CHUNK // 2 - 1:CHUNK // 2, :]
        else:
            c = tot - b + logf
            cref = c[CHUNK // 2:CHUNK // 2 + 1, :]
        q_in = q * jnp.exp2(c - cref)
        k_in = k * jnp.exp2(cref - c)
        ops_ref[d, hh, 0] = q_in.astype(BF16)
        ops_ref[d, hh, 1] = k_in.astype(BF16)
        ops_ref[d, hh, 2] = (k_in * jnp.exp2(tot - cref)).astype(BF16)
        ops_ref[d, hh, 3] = (q_in * jnp.exp2(cref)).astype(BF16)
        ops_ref[d, hh, 4] = i_ref[rows, cs]
        dec_ref[d, hh] = jnp.broadcast_to(jnp.exp2(tot), (8, LANES))

    def intra(d, hh):
        a_ref[d, hh] = _dot_nt(ops_ref[d, hh, 0], ops_ref[d, hh, 1])

    def output(d, hh, rows, o_ref):
        cs = slice(hh * LANES, (hh + 1) * LANES)
        a = jnp.where((ti >= si) if d == 0 else (ti <= si), a_ref[d, hh], 0.0)
        o = (jnp.dot(a.astype(BF16), ops_ref[d, hh, 4], preferred_element_type=F32)
             + _dot_nt(ops_ref[d, hh, 3], stb_ref[d, hh]))
        o_ref[rows, cs] = o.astype(o_ref.dtype)
        ds_ref[d, hh] = _dot_tn(ops_ref[d, hh, 4], ops_ref[d, hh, 2])

    def update(d, hh):
        dec = dec_ref[d, hh, 0:1, :]
        for r in range(REC_VAL_DIM // CHUNK):
            rr = slice(r * CHUNK, (r + 1) * CHUNK)
            new = dec * st_ref[d, hh, rr, :] + ds_ref[d, hh, rr, :]
            st_ref[d, hh, rr, :] = new
            stb_ref[d, hh, rr, :] = new.astype(BF16)

    def scan(n, carry):
        rows_f = _rows(n * CHUNK, CHUNK)
        rows_b = _rows((n_chunks - 1 - n) * CHUNK, CHUNK)
        for hh in range(HB_REC):
            prepare(0, hh, rows_f, qf_ref, if_ref, zf_ref)
            prepare(1, hh, rows_b, qb_ref, ib_ref, zb_ref)
        for hh in range(HB_REC):
            intra(0, hh)
            intra(1, hh)
        for hh in range(HB_REC):
            output(0, hh, rows_f, of_ref)
            output(1, hh, rows_b, ob_ref)
        for hh in range(HB_REC):
            update(0, hh)
            update(1, hh)
        return carry

    lax.fori_loop(0, n_chunks, scan, 0, unroll=2)


def _hgrn2(slab, gates, lb_row, seq_groups):
    T = slab.shape[0]
    w = HB_REC * LANES

    def mirror(g):
        first, local, per = _segment_position(g, seq_groups)
        return first + per - 1 - local

    def fspec(off):
        return pl.BlockSpec((SEG_REC, w), lambda hb, g: (g, off // w + hb))

    def bspec(off):
        return pl.BlockSpec((SEG_REC, w), lambda hb, g: (mirror(g), off // w + hb))

    state = (2, HB_REC, REC_VAL_DIM, REC_KEY_DIM)
    nbytes = (2 * (4 * SEG_REC * w * 2 + 2 * SEG_REC * w * 4 + 2 * SEG_REC * w * 2)
              + 2 * HB_REC * (REC_VAL_DIM * REC_KEY_DIM * 10 + 5 * CHUNK * LANES * 2 + 8 * LANES * 4))
    out = jax.ShapeDtypeStruct((T, D_REC_V), BF16)
    return pl.pallas_call(
        functools.partial(_hgrn_kernel, seq_groups),
        grid=(N_REC_HEADS // HB_REC, T // SEG_REC),
        in_specs=[
            pl.BlockSpec((1, w), lambda hb, g: (0, hb)),
            fspec(A_QR), fspec(A_IR), fspec(0),
            bspec(A_QR), bspec(A_IR), bspec(D_REC_K),
        ],
        out_specs=[pl.BlockSpec((SEG_REC, w), lambda hb, g: (g, hb)),
                   pl.BlockSpec((SEG_REC, w), lambda hb, g: (mirror(g), hb))],
        out_shape=[out, out],
        scratch_shapes=[pltpu.VMEM(state, F32), pltpu.VMEM(state, BF16), pltpu.VMEM(state, F32),
                        pltpu.VMEM((2, HB_REC, 5, CHUNK, LANES), BF16),
                        pltpu.VMEM((2, HB_REC, 8, LANES), F32),
                        pltpu.VMEM((2, HB_REC, CHUNK, CHUNK), F32)],
        compiler_params=pltpu.CompilerParams(
            dimension_semantics=("parallel", "arbitrary"),
            vmem_limit_bytes=_vmem_limit(nbytes)),
        name="hgrn2",
    )(lb_row, slab, slab, gates, slab, slab, gates)


def _outproj_kernel(n_first, a_ref, of_ref, ob_ref, grl_ref, grh_ref, gtl_ref, gth_ref, xa_ref, xb_ref,
                    w_ref, rg_ref, g_ref, h_ref, m_ref):
    half = D_REC_V // 2
    heads_per_chunk = KC_OUT // REC_VAL_DIM
    y = None
    for c in range(D_MODEL // KC_OUT):
        for hh in range(c * heads_per_chunk, (c + 1) * heads_per_chunk):
            cs = slice(hh * REC_VAL_DIM, (hh + 1) * REC_VAL_DIM)
            lo = hh * REC_VAL_DIM < half
            hs = cs if lo else slice(hh * REC_VAL_DIM - half, (hh + 1) * REC_VAL_DIM - half)
            rec = of_ref[:, cs].astype(F32) + ob_ref[:, cs].astype(F32)
            ms = jnp.mean(rec * rec, axis=-1, keepdims=True)
            r = (rec * lax.rsqrt(ms + RMS_EPS)) * rg_ref[:, cs]
            gr = (grl_ref if lo else grh_ref)[:, hs].astype(F32)
            gt = (gtl_ref if lo else gth_ref)[:, hs].astype(F32)
            r = _sigmoid(gt) * (r * (gr * _sigmoid(gr)))
            m_ref[:, cs] = (a_ref[:, cs].astype(F32) + r).astype(BF16)
        ks = slice(c * KC_OUT, (c + 1) * KC_OUT)
        part = jnp.dot(m_ref[:, ks], w_ref[ks, :], preferred_element_type=F32)
        y = part if y is None else y + part
    ms = jnp.mean(y * y, axis=-1, keepdims=True)
    h_ref[...] = (y * lax.rsqrt(ms + RMS_EPS)) * g_ref[...]

    @pl.when(pl.program_id(0) < n_first)
    def _():
        h_ref[...] += xa_ref[...]

    @pl.when(pl.program_id(0) >= n_first)
    def _():
        h_ref[...] += xb_ref[...]


def _out_projection(attn, rec_f, rec_b, slab, xa, xb, w_bf16, rec_gain, gain):
    T = attn.shape[0]
    n_first = xa.shape[0] // TM_OUT
    half = D_REC_V // 2
    assert A_GR % half == 0 and A_GT % half == 0
    tok = lambda i: (i, 0)
    nbytes = (2 * 3 * TM_OUT * D_MODEL * 2 + 2 * 4 * TM_OUT * half * 2 + 2 * 3 * TM_OUT * D_MODEL * 4
              + 2 * D_MODEL * D_MODEL * 2 + TM_OUT * D_MODEL * 2 + 2 * TM_OUT * D_MODEL * 4)
    return pl.pallas_call(
        functools.partial(_outproj_kernel, n_first),
        grid=(T // TM_OUT,),
        in_specs=[
            pl.BlockSpec((TM_OUT, D_MODEL), tok),
            pl.BlockSpec((TM_OUT, D_REC_V), tok),
            pl.BlockSpec((TM_OUT, D_REC_V), tok),
            pl.BlockSpec((TM_OUT, half), lambda i: (i, A_GR // half)),
            pl.BlockSpec((TM_OUT, half), lambda i: (i, A_GR // half + 1)),
            pl.BlockSpec((TM_OUT, half), lambda i: (i, A_GT // half)),
            pl.BlockSpec((TM_OUT, half), lambda i: (i, A_GT // half + 1)),
            *_two_source_specs(TM_OUT, n_first),
            pl.BlockSpec((D_MODEL, D_MODEL), lambda i: (0, 0)),
            pl.BlockSpec((1, D_REC_V), lambda i: (0, 0)),
            pl.BlockSpec((1, D_MODEL), lambda i: (0, 0)),
        ],
        out_specs=pl.BlockSpec((TM_OUT, D_MODEL), tok),
        out_shape=jax.ShapeDtypeStruct((T, D_MODEL), F32),
        scratch_shapes=[pltpu.VMEM((TM_OUT, D_MODEL), BF16)],
        compiler_params=pltpu.CompilerParams(
            dimension_semantics=("parallel",),
            vmem_limit_bytes=_vmem_limit(nbytes)),
        name="out_projection",
    )(attn, rec_f, rec_b, slab, slab, slab, slab, xa, xb, w_bf16, rec_gain, gain)


def _ffn_kernel(n_first, h_ref, gpre_ref, gpost_ref, wg_ref, wu_ref, wd_ref, oa_ref, ob_ref, hn_ref, acc_ref):
    i = pl.program_id(0)
    j = pl.program_id(1)

    @pl.when(j == 0)
    def _():
        def body(rows):
            h = h_ref[rows, :]
            ms = jnp.mean(h * h, axis=-1, keepdims=True)
            hn_ref[rows, :] = ((h * lax.rsqrt(ms + RMS_EPS)) * gpre_ref[...]).astype(BF16)
            acc_ref[rows, :] = jnp.zeros((ROWS_EW, D_MODEL), F32)
        _row_tiles(TM_FFN, body)

    hn = hn_ref[...]
    g = jnp.dot(hn, wg_ref[...], preferred_element_type=F32)
    u = jnp.dot(hn, wu_ref[...], preferred_element_type=F32)
    act = ((g * _sigmoid(g)) * u).astype(BF16)
    acc_ref[...] += jnp.dot(act, wd_ref[...], preferred_element_type=F32)

    def finish(o_ref):
        def body(rows):
            y = acc_ref[rows, :]
            ms = jnp.mean(y * y, axis=-1, keepdims=True)
            o_ref[rows, :] = h_ref[rows, :] + (y * lax.rsqrt(ms + RMS_EPS)) * gpost_ref[...]
        _row_tiles(TM_FFN, body)

    last = j == pl.num_programs(1) - 1

    @pl.when(jnp.logical_and(last, i < n_first))
    def _():
        finish(oa_ref)

    @pl.when(jnp.logical_and(last, i >= n_first))
    def _():
        finish(ob_ref)


def _ffn(h, n_tokens_first, gpre, gpost, wg, wu, wd):
    T = h.shape[0]
    n_first = n_tokens_first // TM_FFN
    nbytes = (2 * 3 * TM_FFN * D_MODEL * 4 + TM_FFN * D_MODEL * 2 + 2 * 3 * D_MODEL * TF_FFN * 2
              + 3 * TM_FFN * TF_FFN * 4 + 2 * TM_FFN * D_MODEL * 4)
    return pl.pallas_call(
        functools.partial(_ffn_kernel, n_first),
        grid=(T // TM_FFN, D_FF // TF_FFN),
        in_specs=[
            pl.BlockSpec((TM_FFN, D_MODEL), lambda i, j: (i, 0)),
            pl.BlockSpec((1, D_MODEL), lambda i, j: (0, 0)),
            pl.BlockSpec((1, D_MODEL), lambda i, j: (0, 0)),
            pl.BlockSpec((D_MODEL, TF_FFN), lambda i, j: (0, j)),
            pl.BlockSpec((D_MODEL, TF_FFN), lambda i, j: (0, j)),
            pl.BlockSpec((TF_FFN, D_MODEL), lambda i, j: (j, 0)),
        ],
        out_specs=list(_two_source_specs(TM_FFN, n_first)),
        out_shape=[jax.ShapeDtypeStruct((n_tokens_first, D_MODEL), F32),
                   jax.ShapeDtypeStruct((T - n_tokens_first, D_MODEL), F32)],
        scratch_shapes=[pltpu.VMEM((TM_FFN, D_MODEL), BF16), pltpu.VMEM((TM_FFN, D_MODEL), F32)],
        compiler_params=pltpu.CompilerParams(
            dimension_semantics=("arbitrary", "arbitrary"),
            vmem_limit_bytes=_vmem_limit(nbytes)),
        name="swiglu_ffn",
    )(h, gpre, gpost, wg, wu, wd)


def _rope_tables(max_len):
    pos = jnp.arange(max_len, dtype=F32)
    inv_freq = ROPE_THETA ** (-jnp.arange(ROPE_HALF, dtype=F32) / ROPE_HALF)
    ang = pos[:, None] * inv_freq[None, :]
    cos, sin = jnp.cos(ang), jnp.sin(ang)
    rest = HEAD_DIM - ROPE_DIM
    cos_t = jnp.concatenate([cos, cos, jnp.ones((max_len, rest), F32)], axis=1)
    sa_t = jnp.concatenate([-sin, jnp.zeros((max_len, HEAD_DIM - ROPE_HALF), F32)], axis=1)
    sb_t = jnp.concatenate([jnp.zeros((max_len, ROPE_HALF), F32), sin, jnp.zeros((max_len, rest), F32)], axis=1)
    return cos_t, sa_t, sb_t


def _seq_groups(seq_shapes, seg):
    groups, first = [], 0
    for B, L in seq_shapes:
        assert L % seg == 0
        groups.append((first, L // seg))
        first += B * L // seg
    return tuple(groups)


def _encoder_layer(xa, xb, seq_shapes, tables, w_in, sink, rec_norm, lb, w_out, norm_mix_pre, norm_mix_post,
                   norm_ffn_pre, norm_ffn_post, w_gate, w_up, w_down):
    row = lambda v: v.astype(F32).reshape(1, -1)
    xn = _stream_norm(xa, xb, row(norm_mix_pre))

    assert all(off % TN_IN == 0 for off in (OFF_K, OFF_QR, OFF_ZF, OFF_IR)) and 2 * D_KV == TN_IN
    gate_tiles = list(range(OFF_ZF // TN_IN, OFF_IR // TN_IN))
    slab_tiles = [t for t in range(D_IN // TN_IN) if t not in gate_tiles]
    kind_of = lambda t: ("rope" if t < OFF_K // TN_IN else "rope_k" if t == OFF_K // TN_IN
                         else "silu" if OFF_QR // TN_IN <= t < OFF_ZF // TN_IN else "plain")
    table_groups = _seq_groups(seq_shapes, TM_IN)
    slab = _in_projection(xn, w_in, tables, table_groups, slab_tiles, tuple(kind_of(t) for t in slab_tiles),
                          BF16, "in_projection")
    gates = _in_projection(xn, w_in, None, None, gate_tiles, ("plain",) * len(gate_tiles),
                           F32, "gate_projection")

    attn = _window_attention(slab, sink.astype(F32), _seq_groups(seq_shapes, TQ_ATT))
    rec_f, rec_b = _hgrn2(slab, gates, row(lb), _seq_groups(seq_shapes, SEG_REC))
    h = _out_projection(attn, rec_f, rec_b, slab, xa, xb, w_out.astype(BF16), row(rec_norm),
                        row(norm_mix_post))
    return _ffn(h, xa.shape[0], row(norm_ffn_pre), row(norm_ffn_post),
                w_gate.astype(BF16), w_up.astype(BF16), w_down.astype(BF16))


def kernel(x_prompt, x_sample, w_in, sink, rec_norm, lb_logits, w_out, norm_mix_pre, norm_mix_post,
           norm_ffn_pre, norm_ffn_post, w_gate, w_up, w_down):
    lb_all = jnp.cumsum(jax.nn.softmax(lb_logits.astype(F32), axis=0), axis=0)
    seq_shapes = (x_prompt.shape[:2], x_sample.shape[:2])
    xa = x_prompt.reshape(-1, D_MODEL)
    xb = x_sample.reshape(-1, D_MODEL)
    tables = _rope_tables(max(L for _, L in seq_shapes))
    for l in range(DEPTH):
        xa, xb = _encoder_layer(xa, xb, seq_shapes, tables, w_in[l], sink[l], rec_norm[l], lb_all[l], w_out[l],
                                norm_mix_pre[l], norm_mix_post[l], norm_ffn_pre[l], norm_ffn_post[l],
                                w_gate[l], w_up[l], w_down[l])
    return (xa.reshape(x_prompt.shape), xb.reshape(x_sample.shape))
```

```python
import functools

import jax
import jax.numpy as jnp
import numpy as np
from jax import lax
from jax.experimental import pallas as pl
from jax.experimental.pallas import tpu as pltpu

F32 = jnp.float32
BF16 = jnp.bfloat16

D_MODEL = 2048
DEPTH = 1
HEAD_DIM = 128
N_Q_HEADS = 16
N_KV_HEADS = 4
GROUP = N_Q_HEADS // N_KV_HEADS
WINDOW = 128
BLOCK = 128
ROPE_DIM = HEAD_DIM // 4
ROPE_HALF = ROPE_DIM // 2
ROPE_THETA = 500000.0
N_REC_HEADS = 16
REC_KEY_DIM = 128
REC_VAL_DIM = 128
CHUNK = 32
D_FF = -(-8 * D_MODEL // (3 * 256)) * 256
RMS_EPS = 1e-6
LOG2E = 1.4426950408889634
Q_SCALE = HEAD_DIM ** -0.5 * LOG2E

D_ATT = N_Q_HEADS * HEAD_DIM
D_KV = N_KV_HEADS * HEAD_DIM
D_REC_K = N_REC_HEADS * REC_KEY_DIM
D_REC_V = N_REC_HEADS * REC_VAL_DIM
SPLIT_SIZES = (D_ATT, D_KV, D_KV, D_REC_K, D_REC_K, D_REC_K, D_REC_V, D_REC_V, D_MODEL, D_MODEL)
D_IN = sum(SPLIT_SIZES)
(OFF_Q, OFF_K, OFF_V, OFF_QR, OFF_ZF, OFF_ZB, OFF_IR, OFF_GR, OFF_GA, OFF_GT) = (
    int(v) for v in np.concatenate([[0], np.cumsum(SPLIT_SIZES)[:-1]]))
D_GATES = OFF_IR - OFF_ZF
D_SLAB = D_IN - D_GATES
A_Q, A_K, A_V, A_QR = OFF_Q, OFF_K, OFF_V, OFF_QR
A_IR, A_GR, A_GA, A_GT = (OFF_IR - D_GATES, OFF_GR - D_GATES, OFF_GA - D_GATES, OFF_GT - D_GATES)

V7X_VMEM_CEILING = 56 * 1024 * 1024
LANES = 128

TM_NORM = 512
TM_IN, TN_IN = 1024, 1024
SUB_IN = 256
ROWS_EW = 256
SLAB_ATT = 32
TQ_ATT = 1024
SEG_REC = 1024
HB_REC = 8
TM_OUT = 256
KC_OUT = 512
TM_FFN, TF_FFN = 512, 512


def _vmem_limit(nbytes):
    return int(min(V7X_VMEM_CEILING, nbytes * 1.25 + (4 << 20)))


def _sigmoid(x):
    return 1.0 / (1.0 + jnp.exp(-x))


def _dot_nt(a, b):
    return lax.dot_general(a, b, (((1,), (1,)), ((), ())), preferred_element_type=F32)


def _dot_tn(a, b):
    return lax.dot_general(a, b, (((0,), (0,)), ((), ())), preferred_element_type=F32)


def _rows(start, size):
    return pl.ds(start if isinstance(start, int) else pl.multiple_of(start, size), size)


def _row_tiles(n_rows, body):
    def step(r, c):
        body(_rows(r * ROWS_EW, ROWS_EW))
        return c
    lax.fori_loop(0, n_rows // ROWS_EW, step, 0)


def _segment_position(g, seq_groups):
    first, local, per = None, None, None
    for g0, n in reversed(seq_groups):
        loc = lax.rem(g - g0, n)
        fst = g - loc
        if first is None:
            first, local, per = fst, loc, n
        else:
            here = g < nxt
            first = jnp.where(here, fst, first)
            local = jnp.where(here, loc, local)
            per = jnp.where(here, n, per)
        nxt = g0
    return first, local, per


def _two_source_specs(tm, n_first):
    return (pl.BlockSpec((tm, D_MODEL), lambda i, *_: (jnp.minimum(i, n_first - 1), 0)),
            pl.BlockSpec((tm, D_MODEL), lambda i, *_: (jnp.maximum(i - n_first, 0), 0)))


def _norm_kernel(n_first, xa_ref, xb_ref, g_ref, o_ref):
    def run(x_ref):
        def body(rows):
            x = x_ref[rows, :]
            ms = jnp.mean(x * x, axis=-1, keepdims=True)
            o_ref[rows, :] = ((x * lax.rsqrt(ms + RMS_EPS)) * g_ref[...]).astype(o_ref.dtype)
        _row_tiles(TM_NORM, body)

    @pl.when(pl.program_id(0) < n_first)
    def _():
        run(xa_ref)

    @pl.when(pl.program_id(0) >= n_first)
    def _():
        run(xb_ref)


def _stream_norm(xa, xb, gain):
    T = xa.shape[0] + xb.shape[0]
    n_first = xa.shape[0] // TM_NORM
    nbytes = 2 * 2 * TM_NORM * D_MODEL * 4 + 2 * TM_NORM * D_MODEL * 2
    return pl.pallas_call(
        functools.partial(_norm_kernel, n_first),
        grid=(T // TM_NORM,),
        in_specs=[*_two_source_specs(TM_NORM, n_first), pl.BlockSpec((1, D_MODEL), lambda i: (0, 0))],
        out_specs=pl.BlockSpec((TM_NORM, D_MODEL), lambda i: (i, 0)),
        out_shape=jax.ShapeDtypeStruct((T, D_MODEL), BF16),
        compiler_params=pltpu.CompilerParams(
            dimension_semantics=("parallel",), vmem_limit_bytes=_vmem_limit(nbytes)),
        name="stream_norm",
    )(xa, xb, gain)


def _inproj_kernel(kinds, col_tiles_ref, *refs):
    del col_tiles_ref
    with_epilogue = any(k != "plain" for k in kinds)
    if with_epilogue:
        xn_ref, cos_ref, sa_ref, sb_ref, w_ref, o_ref, wb_ref, y_ref = refs
    else:
        xn_ref, w_ref, o_ref, wb_ref = refs
    j = pl.program_id(0)

    @pl.when(pl.program_id(1) == 0)
    def _():
        def body(rows):
            wb_ref[rows, :] = w_ref[rows, :].astype(BF16)
        _row_tiles(D_MODEL, body)

    def transform(kind, y, rows, lane_tile):
        if kind == "silu":
            return y * _sigmoid(y)
        if kind == "rope_k" and lane_tile >= N_KV_HEADS:
            return y
        y = (y * cos_ref[rows, :] + pltpu.roll(y, HEAD_DIM - ROPE_HALF, 1) * sa_ref[rows, :]
             + pltpu.roll(y, ROPE_HALF, 1) * sb_ref[rows, :])
        return y * Q_SCALE if kind == "rope" else y

    def epilogue(kind, sub):
        for r in range(TM_IN // ROWS_EW):
            rows = slice(r * ROWS_EW, (r + 1) * ROWS_EW)
            for cc in range(SUB_IN // LANES):
                lane_tile = sub * (SUB_IN // LANES) + cc
                y = y_ref[sub, rows, cc * LANES:(cc + 1) * LANES]
                o_ref[rows, lane_tile * LANES:(lane_tile + 1) * LANES] = (
                    transform(kind, y, rows, lane_tile).astype(o_ref.dtype))

    for kind in sorted(set(kinds)):
        tiles = [t for t, k in enumerate(kinds) if k == kind]
        cond = functools.reduce(jnp.logical_or, [j == t for t in tiles])

        @pl.when(cond)
        def _(kind=kind):
            if kind == "plain":
                o_ref[...] = jnp.dot(xn_ref[...], wb_ref[...], preferred_element_type=F32).astype(o_ref.dtype)
            else:
                n_sub = TN_IN // SUB_IN
                for sub in range(n_sub):
                    y_ref[sub] = jnp.dot(xn_ref[...], wb_ref[:, sub * SUB_IN:(sub + 1) * SUB_IN],
                                         preferred_element_type=F32)
                    if sub > 0:
                        epilogue(kind, sub - 1)
                epilogue(kind, n_sub - 1)


def _in_projection(xn, w_in, tables, table_groups, col_tiles, kinds, out_dtype, name):
    T = xn.shape[0]
    n_col = len(col_tiles)
    col_tiles = jnp.asarray(col_tiles, jnp.int32)
    with_epilogue = any(k != "plain" for k in kinds)
    last_table_tile = max([t for t, k in enumerate(kinds) if k.startswith("rope")], default=-1)
    out_bytes = jnp.dtype(out_dtype).itemsize

    def table_map(j, i, ct):
        _, local, _ = _segment_position(i, table_groups)
        return (jnp.where(j <= last_table_tile, local, 0), 0)

    in_specs = [pl.BlockSpec((TM_IN, D_MODEL), lambda j, i, ct: (i, 0))]
    args = [xn]
    scratch = [pltpu.VMEM((D_MODEL, TN_IN), BF16)]
    nbytes = (2 * TM_IN * D_MODEL * 2 + 2 * D_MODEL * TN_IN * 4 + D_MODEL * TN_IN * 2
              + 2 * TM_IN * TN_IN * out_bytes + TM_IN * TN_IN * 4)
    if with_epilogue:
        in_specs += [pl.BlockSpec((TM_IN, LANES), table_map)] * 3
        args += list(tables)
        scratch.append(pltpu.VMEM((TN_IN // SUB_IN, TM_IN, SUB_IN), F32))
        nbytes += 2 * 3 * TM_IN * LANES * 4 + TM_IN * TN_IN * 4
    in_specs.append(pl.BlockSpec((D_MODEL, TN_IN), lambda j, i, ct: (0, ct[j])))
    args.append(w_in)
    return pl.pallas_call(
        functools.partial(_inproj_kernel, kinds),
        grid_spec=pltpu.PrefetchScalarGridSpec(
            num_scalar_prefetch=1,
            grid=(n_col, T // TM_IN),
            in_specs=in_specs,
            out_specs=pl.BlockSpec((TM_IN, TN_IN), lambda j, i, ct: (i, j)),
            scratch_shapes=scratch),
        out_shape=jax.ShapeDtypeStruct((T, n_col * TN_IN), out_dtype),
        compiler_params=pltpu.CompilerParams(
            dimension_semantics=("arbitrary", "arbitrary"),
            vmem_limit_bytes=_vmem_limit(nbytes)),
        name=name,
    )(col_tiles, *args)


def _attn_kernel(seq_groups, sink_ref, q_ref, kc_ref, vc_ref, kp_ref, kn_ref, vp_ref, vn_ref,
                 ga_ref, o_ref, s_all, p_all, r_all):
    t = pl.program_id(0)
    h = pl.program_id(1)
    _, local, per = _segment_position(t, seq_groups)
    has_prev = local != 0
    has_next = local != per - 1

    k_all = jnp.concatenate([kp_ref[...], kc_ref[...], kn_ref[...]], axis=0)
    v_all = jnp.concatenate([vp_ref[...], vc_ref[...], vn_ref[...]], axis=0)
    qi = lax.broadcasted_iota(jnp.int32, (BLOCK, BLOCK), 0)
    kj = lax.broadcasted_iota(jnp.int32, (BLOCK, BLOCK), 1)
    neg = jnp.float32(-jnp.inf)
    bias_prev = jnp.where(kj >= qi, 0.0, neg)
    bias_next = jnp.where(kj <= qi, 0.0, neg)
    bias_prev0 = jnp.where(has_prev, bias_prev, neg)
    bias_next_last = jnp.where(has_next, bias_next, neg)
    n_blocks = TQ_ATT // BLOCK
    for b in range(n_blocks):
        rs = slice(b * BLOCK, (b + 1) * BLOCK)
        kb = k_all[b * BLOCK:(b + 3) * BLOCK]
        vb = v_all[b * BLOCK:(b + 3) * BLOCK]
        bp = bias_prev0 if b == 0 else bias_prev
        bn = bias_next_last if b == n_blocks - 1 else bias_next
        heads = [slice(g * HEAD_DIM, (g + 1) * HEAD_DIM) for g in range(GROUP)]
        s_ref, p_ref, r_ref = s_all.at[b], p_all.at[b], r_all.at[b]
        for g, cs in enumerate(heads):
            s_ref[g] = _dot_nt(q_ref[rs, cs], kb)
        for g, cs in enumerate(heads):
            sink2 = sink_ref[GROUP * h + g] * LOG2E
            for r in range(BLOCK // SLAB_ATT):
                rr = slice(r * SLAB_ATT, (r + 1) * SLAB_ATT)
                s = s_ref[g, rr, :]
                s_p = s[:, :BLOCK] + bp[rr]
                s_c = s[:, BLOCK:2 * BLOCK]
                s_n = s[:, 2 * BLOCK:] + bn[rr]
                m = jnp.maximum(jnp.max(jnp.maximum(jnp.maximum(s_p, s_c), s_n), axis=-1, keepdims=True), sink2)
                p_p, p_c, p_n = jnp.exp2(s_p - m), jnp.exp2(s_c - m), jnp.exp2(s_n - m)
                denom = jnp.sum(p_p + p_c + p_n, axis=-1, keepdims=True) + jnp.exp2(sink2 - m)
                p_ref[g, rr, :] = jnp.concatenate([p_p, p_c, p_n], axis=1).astype(BF16)
                r_ref[g, rr, :] = jnp.broadcast_to(1.0 / denom, (SLAB_ATT, LANES))
        for g, cs in enumerate(heads):
            o = jnp.dot(p_ref[g], vb, preferred_element_type=F32) * r_ref[g]
            o_ref[rs, cs] = (_sigmoid(ga_ref[rs, cs].astype(F32)) * o).astype(o_ref.dtype)


def _window_attention(slab, sink, seq_groups):
    T = slab.shape[0]
    n_tiles = T // TQ_ATT
    bpt = TQ_ATT // BLOCK
    last_blk = T // BLOCK - 1
    gw = GROUP * HEAD_DIM
    kcol, vcol, gcol = A_K // HEAD_DIM, A_V // HEAD_DIM, A_GA // gw
    nbytes = (2 * (3 * TQ_ATT * gw * 2 + 2 * TQ_ATT * HEAD_DIM * 2 + 4 * BLOCK * HEAD_DIM * 2)
              + bpt * GROUP * BLOCK * (3 * BLOCK * 6 + LANES * 4))
    return pl.pallas_call(
        functools.partial(_attn_kernel, seq_groups),
        grid=(n_tiles, N_KV_HEADS),
        in_specs=[
            pl.BlockSpec(memory_space=pltpu.SMEM),
            pl.BlockSpec((TQ_ATT, gw), lambda t, h: (t, h)),
            pl.BlockSpec((TQ_ATT, HEAD_DIM), lambda t, h: (t, kcol + h)),
            pl.BlockSpec((TQ_ATT, HEAD_DIM), lambda t, h: (t, vcol + h)),
            pl.BlockSpec((BLOCK, HEAD_DIM), lambda t, h: (jnp.maximum(t * bpt - 1, 0), kcol + h)),
            pl.BlockSpec((BLOCK, HEAD_DIM), lambda t, h: (jnp.minimum((t + 1) * bpt, last_blk), kcol + h)),
            pl.BlockSpec((BLOCK, HEAD_DIM), lambda t, h: (jnp.maximum(t * bpt - 1, 0), vcol + h)),
            pl.BlockSpec((BLOCK, HEAD_DIM), lambda t, h: (jnp.minimum((t + 1) * bpt, last_blk), vcol + h)),
            pl.BlockSpec((TQ_ATT, gw), lambda t, h: (t, gcol + h)),
        ],
        out_specs=pl.BlockSpec((TQ_ATT, gw), lambda t, h: (t, h)),
        out_shape=jax.ShapeDtypeStruct((T, D_ATT), BF16),
        scratch_shapes=[pltpu.VMEM((bpt, GROUP, BLOCK, 3 * BLOCK), F32),
                        pltpu.VMEM((bpt, GROUP, BLOCK, 3 * BLOCK), BF16),
                        pltpu.VMEM((bpt, GROUP, BLOCK, LANES), F32)],
        compiler_params=pltpu.CompilerParams(
            dimension_semantics=("parallel", "arbitrary"),
            vmem_limit_bytes=_vmem_limit(nbytes)),
        name="window_attention",
    )(sink, slab, slab, slab, slab, slab, slab, slab, slab)


def _hgrn_kernel(seq_groups, lb_ref, qf_ref, if_ref, zf_ref, qb_ref, ib_ref, zb_ref,
                 of_ref, ob_ref, st_ref, stb_ref, ds_ref, ops_ref, dec_ref, a_ref):
    g = pl.program_id(1)
    _, local, _ = _segment_position(g, seq_groups)

    @pl.when(local == 0)
    def _():
        st_ref[...] = jnp.zeros(st_ref.shape, F32)
        stb_ref[...] = jnp.zeros(stb_ref.shape, BF16)

    n_chunks = SEG_REC // CHUNK
    rin = lax.broadcasted_iota(jnp.int32, (CHUNK, LANES), 0)
    ti = lax.broadcasted_iota(jnp.int32, (CHUNK, CHUNK), 0)
    si = lax.broadcasted_iota(jnp.int32, (CHUNK, CHUNK), 1)

    def prepare(d, hh, rows, q_ref, i_ref, z_ref):
        cs = slice(hh * LANES, (hh + 1) * LANES)
        lb = lb_ref[:, cs]
        one_m_lb = 1.0 - lb
        q = q_ref[rows, cs].astype(F32)
        gate = one_m_lb * _sigmoid(z_ref[rows, cs])
        k = one_m_lb - gate
        logf = jnp.log2(lb + gate)
        b = logf
        for s in (1, 2, 4, 8, 16):
            b = b + jnp.where(rin >= s, pltpu.roll(b, s, 0), 0.0)
        tot = b[CHUNK - 1:CHUNK, :]
        if d == 0:
            c = b
            cref = b[CHUNK // 2 - 1:CHUNK // 2, :]
        else:
            c = tot - b + logf
            cref = c[CHUNK // 2:CHUNK // 2 + 1, :]
        q_in = q * jnp.exp2(c - cref)
        k_in = k * jnp.exp2(cref - c)
        ops_ref[d, hh, 0] = q_in.astype(BF16)
        ops_ref[d, hh, 1] = k_in.astype(BF16)
        ops_ref[d, hh, 2] = (k_in * jnp.exp2(tot - cref)).astype(BF16)
        ops_ref[d, hh, 3] = (q_in * jnp.exp2(cref)).astype(BF16)
        ops_ref[d, hh, 4] = i_ref[rows, cs]
        dec_ref[d, hh] = jnp.broadcast_to(jnp.exp2(tot), (8, LANES))

    def intra(d, hh):
        a_ref[d, hh] = _dot_nt(ops_ref[d, hh, 0], ops_ref[d, hh, 1])

    def output(d, hh, rows, o_ref):
        cs = slice(hh * LANES, (hh + 1) * LANES)
        a = jnp.where((ti >= si) if d == 0 else (ti <= si), a_ref[d, hh], 0.0)
        o = (jnp.dot(a.astype(BF16), ops_ref[d, hh, 4], preferred_element_type=F32)
             + _dot_nt(ops_ref[d, hh, 3], stb_ref[d, hh]))
        o_ref[rows, cs] = o.astype(o_ref.dtype)
        ds_ref[d, hh] = _dot_tn(ops_ref[d, hh, 4], ops_ref[d, hh, 2])

    def update(d, hh):
        dec = dec_ref[d, hh, 0:1, :]
        for r in range(REC_VAL_DIM // CHUNK):
            rr = slice(r * CHUNK, (r + 1) * CHUNK)
            new = dec * st_ref[d, hh, rr, :] + ds_ref[d, hh, rr, :]
            st_ref[d, hh, rr, :] = new
            stb_ref[d, hh, rr, :] = new.astype(BF16)

    def scan(n, carry):
        rows_f = _rows(n * CHUNK, CHUNK)
        rows_b = _rows((n_chunks - 1 - n) * CHUNK, CHUNK)
        for hh in range(HB_REC):
            prepare(0, hh, rows_f, qf_ref, if_ref, zf_ref)
            prepare(1, hh, rows_b, qb_ref, ib_ref, zb_ref)
        for hh in range(HB_REC):
            intra(0, hh)
            intra(1, hh)
        for hh in range(HB_REC):
            output(0, hh, rows_f, of_ref)
            output(1, hh, rows_b, ob_ref)
        for hh in range(HB_REC):
            update(0, hh)
            update(1, hh)
        return carry

    lax.fori_loop(0, n_chunks, scan, 0, unroll=2)


def _hgrn2(slab, gates, lb_row, seq_groups):
    T = slab.shape[0]
    w = HB_REC * LANES

    def mirror(g):
        first, local, per = _segment_position(g, seq_groups)
        return first + per - 1 - local

    def fspec(off):
        return pl.BlockSpec((SEG_REC, w), lambda hb, g: (g, off // w + hb))

    def bspec(off):
        return pl.BlockSpec((SEG_REC, w), lambda hb, g: (mirror(g), off // w + hb))

    state = (2, HB_REC, REC_VAL_DIM, REC_KEY_DIM)
    nbytes = (2 * (4 * SEG_REC * w * 2 + 2 * SEG_REC * w * 4 + 2 * SEG_REC * w * 2)
              + 2 * HB_REC * (REC_VAL_DIM * REC_KEY_DIM * 10 + 5 * CHUNK * LANES * 2 + 8 * LANES * 4))
    out = jax.ShapeDtypeStruct((T, D_REC_V), BF16)
    return pl.pallas_call(
        functools.partial(_hgrn_kernel, seq_groups),
        grid=(N_REC_HEADS // HB_REC, T // SEG_REC),
        in_specs=[
            pl.BlockSpec((1, w), lambda hb, g: (0, hb)),
            fspec(A_QR), fspec(A_IR), fspec(0),
            bspec(A_QR), bspec(A_IR), bspec(D_REC_K),
        ],
        out_specs=[pl.BlockSpec((SEG_REC, w), lambda hb, g: (g, hb)),
                   pl.BlockSpec((SEG_REC, w), lambda hb, g: (mirror(g), hb))],
        out_shape=[out, out],
        scratch_shapes=[pltpu.VMEM(state, F32), pltpu.VMEM(state, BF16), pltpu.VMEM(state, F32),
                        pltpu.VMEM((2, HB_REC, 5, CHUNK, LANES), BF16),
                        pltpu.VMEM((2, HB_REC, 8, LANES), F32),
                        pltpu.VMEM((2, HB_REC, CHUNK, CHUNK), F32)],
        compiler_params=pltpu.CompilerParams(
            dimension_semantics=("parallel", "arbitrary"),
            vmem_limit_bytes=_vmem_limit(nbytes)),
        name="hgrn2",
    )(lb_row, slab, slab, gates, slab, slab, gates)


def _outproj_kernel(n_first, a_ref, of_ref, ob_ref, grl_ref, grh_ref, gtl_ref, gth_ref, xa_ref, xb_ref,
                    w_ref, rg_ref, g_ref, h_ref, m_ref):
    half = D_REC_V // 2
    heads_per_chunk = KC_OUT // REC_VAL_DIM
    y = None
    for c in range(D_MODEL // KC_OUT):
        for hh in range(c * heads_per_chunk, (c + 1) * heads_per_chunk):
            cs = slice(hh * REC_VAL_DIM, (hh + 1) * REC_VAL_DIM)
            lo = hh * REC_VAL_DIM < half
            hs = cs if lo else slice(hh * REC_VAL_DIM - half, (hh + 1) * REC_VAL_DIM - half)
            rec = of_ref[:, cs].astype(F32) + ob_ref[:, cs].astype(F32)
            ms = jnp.mean(rec * rec, axis=-1, keepdims=True)
            r = (rec * lax.rsqrt(ms + RMS_EPS)) * rg_ref[:, cs]
            gr = (grl_ref if lo else grh_ref)[:, hs].astype(F32)
            gt = (gtl_ref if lo else gth_ref)[:, hs].astype(F32)
            r = _sigmoid(gt) * (r * (gr * _sigmoid(gr)))
            m_ref[:, cs] = (a_ref[:, cs].astype(F32) + r).astype(BF16)
        ks = slice(c * KC_OUT, (c + 1) * KC_OUT)
        part = jnp.dot(m_ref[:, ks], w_ref[ks, :], preferred_element_type=F32)
        y = part if y is None else y + part
    ms = jnp.mean(y * y, axis=-1, keepdims=True)
    h_ref[...] = (y * lax.rsqrt(ms + RMS_EPS)) * g_ref[...]

    @pl.when(pl.program_id(0) < n_first)
    def _():
        h_ref[...] += xa_ref[...]

    @pl.when(pl.program_id(0) >= n_first)
    def _():
        h_ref[...] += xb_ref[...]


def _out_projection(attn, rec_f, rec_b, slab, xa, xb, w_bf16, rec_gain, gain):
    T = attn.shape[0]
    n_first = xa.shape[0] // TM_OUT
    half = D_REC_V // 2
    assert A_GR % half == 0 and A_GT % half == 0
    tok = lambda i: (i, 0)
    nbytes = (2 * 3 * TM_OUT * D_MODEL * 2 + 2 * 4 * TM_OUT * half * 2 + 2 * 3 * TM_OUT * D_MODEL * 4
              + 2 * D_MODEL * D_MODEL * 2 + TM_OUT * D_MODEL * 2 + 2 * TM_OUT * D_MODEL * 4)
    return pl.pallas_call(
        functools.partial(_outproj_kernel, n_first),
        grid=(T // TM_OUT,),
        in_specs=[
            pl.BlockSpec((TM_OUT, D_MODEL), tok),
            pl.BlockSpec((TM_OUT, D_REC_V), tok),
            pl.BlockSpec((TM_OUT, D_REC_V), tok),
            pl.BlockSpec((TM_OUT, half), lambda i: (i, A_GR // half)),
            pl.BlockSpec((TM_OUT, half), lambda i: (i, A_GR // half + 1)),
            pl.BlockSpec((TM_OUT, half), lambda i: (i, A_GT // half)),
            pl.BlockSpec((TM_OUT, half), lambda i: (i, A_GT // half + 1)),
            *_two_source_specs(TM_OUT, n_first),
            pl.BlockSpec((D_MODEL, D_MODEL), lambda i: (0, 0)),
            pl.BlockSpec((1, D_REC_V), lambda i: (0, 0)),
            pl.BlockSpec((1, D_MODEL), lambda i: (0, 0)),
        ],
        out_specs=pl.BlockSpec((TM_OUT, D_MODEL), tok),
        out_shape=jax.ShapeDtypeStruct((T, D_MODEL), F32),
        scratch_shapes=[pltpu.VMEM((TM_OUT, D_MODEL), BF16)],
        compiler_params=pltpu.CompilerParams(
            dimension_semantics=("parallel",),
            vmem_limit_bytes=_vmem_limit(nbytes)),
        name="out_projection",
    )(attn, rec_f, rec_b, slab, slab, slab, slab, xa, xb, w_bf16, rec_gain, gain)


def _ffn_kernel(n_first, h_ref, gpre_ref, gpost_ref, wg_ref, wu_ref, wd_ref, oa_ref, ob_ref, hn_ref, acc_ref):
    i = pl.program_id(0)
    j = pl.program_id(1)

    @pl.when(j == 0)
    def _():
        def body(rows):
            h = h_ref[rows, :]
            ms = jnp.mean(h * h, axis=-1, keepdims=True)
            hn_ref[rows, :] = ((h * lax.rsqrt(ms + RMS_EPS)) * gpre_ref[...]).astype(BF16)
            acc_ref[rows, :] = jnp.zeros((ROWS_EW, D_MODEL), F32)
        _row_tiles(TM_FFN, body)

    hn = hn_ref[...]
    g = jnp.dot(hn, wg_ref[...], preferred_element_type=F32)
    u = jnp.dot(hn, wu_ref[...], preferred_element_type=F32)
    act = ((g * _sigmoid(g)) * u).astype(BF16)
    acc_ref[...] += jnp.dot(act, wd_ref[...], preferred_element_type=F32)

    def finish(o_ref):
        def body(rows):
            y = acc_ref[rows, :]
            ms = jnp.mean(y * y, axis=-1, keepdims=True)
            o_ref[rows, :] = h_ref[rows, :] + (y * lax.rsqrt(ms + RMS_EPS)) * gpost_ref[...]
        _row_tiles(TM_FFN, body)

    last = j == pl.num_programs(1) - 1

    @pl.when(jnp.logical_and(last, i < n_first))
    def _():
        finish(oa_ref)

    @pl.when(jnp.logical_and(last, i >= n_first))
    def _():
        finish(ob_ref)


def _ffn(h, n_tokens_first, gpre, gpost, wg, wu, wd):
    T = h.shape[0]
    n_first = n_tokens_first // TM_FFN
    nbytes = (2 * 3 * TM_FFN * D_MODEL * 4 + TM_FFN * D_MODEL * 2 + 2 * 3 * D_MODEL * TF_FFN * 2
              + 3 * TM_FFN * TF_FFN * 4 + 2 * TM_FFN * D_MODEL * 4)
    return pl.pallas_call(
        functools.partial(_ffn_kernel, n_first),
        grid=(T // TM_FFN, D_FF // TF_FFN),
        in_specs=[
            pl.BlockSpec((TM_FFN, D_MODEL), lambda i, j: (i, 0)),
            pl.BlockSpec((1, D_MODEL), lambda i, j: (0, 0)),
            pl.BlockSpec((1, D_MODEL), lambda i, j: (0, 0)),
            pl.BlockSpec((D_MODEL, TF_FFN), lambda i, j: (0, j)),
            pl.BlockSpec((D_MODEL, TF_FFN), lambda i, j: (0, j)),
            pl.BlockSpec((TF_FFN, D_MODEL), lambda i, j: (j, 0)),
        ],
        out_specs=list(_two_source_specs(TM_FFN, n_first)),
        out_shape=[jax.ShapeDtypeStruct((n_tokens_first, D_MODEL), F32),
                   jax.ShapeDtypeStruct((T - n_tokens_first, D_MODEL), F32)],
        scratch_shapes=[pltpu.VMEM((TM_FFN, D_MODEL), BF16), pltpu.VMEM((TM_FFN, D_MODEL), F32)],
        compiler_params=pltpu.CompilerParams(
            dimension_semantics=("arbitrary", "arbitrary"),
            vmem_limit_bytes=_vmem_limit(nbytes)),
        name="swiglu_ffn",
    )(h, gpre, gpost, wg, wu, wd)


def _rope_tables(max_len):
    pos = jnp.arange(max_len, dtype=F32)
    inv_freq = ROPE_THETA ** (-jnp.arange(ROPE_HALF, dtype=F32) / ROPE_HALF)
    ang = pos[:, None] * inv_freq[None, :]
    cos, sin = jnp.cos(ang), jnp.sin(ang)
    rest = HEAD_DIM - ROPE_DIM
    cos_t = jnp.concatenate([cos, cos, jnp.ones((max_len, rest), F32)], axis=1)
    sa_t = jnp.concatenate([-sin, jnp.zeros((max_len, HEAD_DIM - ROPE_HALF), F32)], axis=1)
    sb_t = jnp.concatenate([jnp.zeros((max_len, ROPE_HALF), F32), sin, jnp.zeros((max_len, rest), F32)], axis=1)
    return cos_t, sa_t, sb_t


def _seq_groups(seq_shapes, seg):
    groups, first = [], 0
    for B, L in seq_shapes:
        assert L % seg == 0
        groups.append((first, L // seg))
        first += B * L // seg
    return tuple(groups)


def _encoder_layer(xa, xb, seq_shapes, tables, w_in, sink, rec_norm, lb, w_out, norm_mix_pre, norm_mix_post,
                   norm_ffn_pre, norm_ffn_post, w_gate, w_up, w_down):
    row = lambda v: v.astype(F32).reshape(1, -1)
    xn = _stream_norm(xa, xb, row(norm_mix_pre))

    assert all(off % TN_IN == 0 for off in (OFF_K, OFF_QR, OFF_ZF, OFF_IR)) and 2 * D_KV == TN_IN
    gate_tiles = list(range(OFF_ZF // TN_IN, OFF_IR // TN_IN))
    slab_tiles = [t for t in range(D_IN // TN_IN) if t not in gate_tiles]
    kind_of = lambda t: ("rope" if t < OFF_K // TN_IN else "rope_k" if t == OFF_K // TN_IN
                         else "silu" if OFF_QR // TN_IN <= t < OFF_ZF // TN_IN else "plain")
    table_groups = _seq_groups(seq_shapes, TM_IN)
    slab = _in_projection(xn, w_in, tables, table_groups, slab_tiles, tuple(kind_of(t) for t in slab_tiles),
                          BF16, "in_projection")
    gates = _in_projection(xn, w_in, None, None, gate_tiles, ("plain",) * len(gate_tiles),
                           F32, "gate_projection")

    attn = _window_attention(slab, sink.astype(F32), _seq_groups(seq_shapes, TQ_ATT))
    rec_f, rec_b = _hgrn2(slab, gates, row(lb), _seq_groups(seq_shapes, SEG_REC))
    h = _out_projection(attn, rec_f, rec_b, slab, xa, xb, w_out.astype(BF16), row(rec_norm),
                        row(norm_mix_post))
    return _ffn(h, xa.shape[0], row(norm_ffn_pre), row(norm_ffn_post),
                w_gate.astype(BF16), w_up.astype(BF16), w_down.astype(BF16))


def kernel(x_prompt, x_sample, w_in, sink, rec_norm, lb_logits, w_out, norm_mix_pre, norm_mix_post,
           norm_ffn_pre, norm_ffn_post, w_gate, w_up, w_down):
    lb_all = jnp.cumsum(jax.nn.softmax(lb_logits.astype(F32), axis=0), axis=0)
    seq_shapes = (x_prompt.shape[:2], x_sample.shape[:2])
    xa = x_prompt.reshape(-1, D_MODEL)
    xb = x_sample.reshape(-1, D_MODEL)
    tables = _rope_tables(max(L for _, L in seq_shapes))
    for l in range(DEPTH):
        xa, xb = _encoder_layer(xa, xb, seq_shapes, tables, w_in[l], sink[l], rec_norm[l], lb_all[l], w_out[l],
                                norm_mix_pre[l], norm_mix_post[l], norm_ffn_pre[l], norm_ffn_post[l],
                                w_gate[l], w_up[l], w_down[l])
    return (xa.reshape(x_prompt.shape), xb.reshape(x_sample.shape))
```

```python
import functools

import jax
import jax.numpy as jnp
import numpy as np
from jax import lax
from jax.experimental import pallas as pl
from jax.experimental.pallas import tpu as pltpu

F32 = jnp.float32
BF16 = jnp.bfloat16

D_MODEL = 2048
DEPTH = 1
HEAD_DIM = 128
N_Q_HEADS = 16
N_KV_HEADS = 4
GROUP = N_Q_HEADS // N_KV_HEADS
WINDOW = 128
BLOCK = 128
ROPE_DIM = HEAD_DIM // 4
ROPE_HALF = ROPE_DIM // 2
ROPE_THETA = 500000.0
N_REC_HEADS = 16
REC_KEY_DIM = 128
REC_VAL_DIM = 128
CHUNK = 32
D_FF = -(-8 * D_MODEL // (3 * 256)) * 256
RMS_EPS = 1e-6
LOG2E = 1.4426950408889634
Q_SCALE = HEAD_DIM ** -0.5 * LOG2E

D_ATT = N_Q_HEADS * HEAD_DIM
D_KV = N_KV_HEADS * HEAD_DIM
D_REC_K = N_REC_HEADS * REC_KEY_DIM
D_REC_V = N_REC_HEADS * REC_VAL_DIM
SPLIT_SIZES = (D_ATT, D_KV, D_KV, D_REC_K, D_REC_K, D_REC_K, D_REC_V, D_REC_V, D_MODEL, D_MODEL)
D_IN = sum(SPLIT_SIZES)
(OFF_Q, OFF_K, OFF_V, OFF_QR, OFF_ZF, OFF_ZB, OFF_IR, OFF_GR, OFF_GA, OFF_GT) = (
    int(v) for v in np.concatenate([[0], np.cumsum(SPLIT_SIZES)[:-1]]))
D_GATES = OFF_IR - OFF_ZF
D_SLAB = D_IN - D_GATES
A_Q, A_K, A_V, A_QR = OFF_Q, OFF_K, OFF_V, OFF_QR
A_IR, A_GR, A_GA, A_GT = (OFF_IR - D_GATES, OFF_GR - D_GATES, OFF_GA - D_GATES, OFF_GT - D_GATES)

V7X_VMEM_CEILING = 56 * 1024 * 1024
LANES = 128

TM_NORM = 512
TM_IN, TN_IN = 1024, 1024
SUB_IN = 256
ROWS_EW = 256
SLAB_ATT = 32
TQ_ATT = 1024
SEG_REC = 512
HB_REC = 8
TM_OUT = 256
KC_OUT = 512
TM_FFN, TF_FFN = 512, 512


def _vmem_limit(nbytes):
    return int(min(V7X_VMEM_CEILING, nbytes * 1.25 + (4 << 20)))


def _sigmoid(x):
    return 1.0 / (1.0 + jnp.exp(-x))


def _dot_nt(a, b):
    return lax.dot_general(a, b, (((1,), (1,)), ((), ())), preferred_element_type=F32)


def _dot_tn(a, b):
    return lax.dot_general(a, b, (((0,), (0,)), ((), ())), preferred_element_type=F32)


def _rows(start, size):
    return pl.ds(start if isinstance(start, int) else pl.multiple_of(start, size), size)


def _row_tiles(n_rows, body):
    def step(r, c):
        body(_rows(r * ROWS_EW, ROWS_EW))
        return c
    lax.fori_loop(0, n_rows // ROWS_EW, step, 0)


def _segment_position(g, seq_groups):
    first, local, per = None, None, None
    for g0, n in reversed(seq_groups):
        loc = lax.rem(g - g0, n)
        fst = g - loc
        if first is None:
            first, local, per = fst, loc, n
        else:
            here = g < nxt
            first = jnp.where(here, fst, first)
            local = jnp.where(here, loc, local)
            per = jnp.where(here, n, per)
        nxt = g0
    return first, local, per


def _two_source_specs(tm, n_first):
    return (pl.BlockSpec((tm, D_MODEL), lambda i, *_: (jnp.minimum(i, n_first - 1), 0)),
            pl.BlockSpec((tm, D_MODEL), lambda i, *_: (jnp.maximum(i - n_first, 0), 0)))


def _norm_kernel(n_first, xa_ref, xb_ref, g_ref, o_ref):
    def run(x_ref):
        def body(rows):
            x = x_ref[rows, :]
            ms = jnp.mean(x * x, axis=-1, keepdims=True)
            o_ref[rows, :] = ((x * lax.rsqrt(ms + RMS_EPS)) * g_ref[...]).astype(o_ref.dtype)
        _row_tiles(TM_NORM, body)

    @pl.when(pl.program_id(0) < n_first)
    def _():
        run(xa_ref)

    @pl.when(pl.program_id(0) >= n_first)
    def _():
        run(xb_ref)


def _stream_norm(xa, xb, gain):
    T = xa.shape[0] + xb.shape[0]
    n_first = xa.shape[0] // TM_NORM
    nbytes = 2 * 2 * TM_NORM * D_MODEL * 4 + 2 * TM_NORM * D_MODEL * 2
    return pl.pallas_call(
        functools.partial(_norm_kernel, n_first),
        grid=(T // TM_NORM,),
        in_specs=[*_two_source_specs(TM_NORM, n_first), pl.BlockSpec((1, D_MODEL), lambda i: (0, 0))],
        out_specs=pl.BlockSpec((TM_NORM, D_MODEL), lambda i: (i, 0)),
        out_shape=jax.ShapeDtypeStruct((T, D_MODEL), BF16),
        compiler_params=pltpu.CompilerParams(
            dimension_semantics=("parallel",), vmem_limit_bytes=_vmem_limit(nbytes)),
        name="stream_norm",
    )(xa, xb, gain)


def _inproj_kernel(kinds, col_tiles_ref, *refs):
    del col_tiles_ref
    with_epilogue = any(k != "plain" for k in kinds)
    if with_epilogue:
        xn_ref, cos_ref, sa_ref, sb_ref, w_ref, o_ref, wb_ref, y_ref = refs
    else:
        xn_ref, w_ref, o_ref, wb_ref = refs
    j = pl.program_id(0)

    @pl.when(pl.program_id(1) == 0)
    def _():
        def body(rows):
            wb_ref[rows, :] = w_ref[rows, :].astype(BF16)
        _row_tiles(D_MODEL, body)

    def transform(kind, y, rows, lane_tile):
        if kind == "silu":
            return y * _sigmoid(y)
        if kind == "rope_k" and lane_tile >= N_KV_HEADS:
            return y
        y = (y * cos_ref[rows, :] + pltpu.roll(y, HEAD_DIM - ROPE_HALF, 1) * sa_ref[rows, :]
             + pltpu.roll(y, ROPE_HALF, 1) * sb_ref[rows, :])
        return y * Q_SCALE if kind == "rope" else y

    def epilogue(kind, sub):
        for r in range(TM_IN // ROWS_EW):
            rows = slice(r * ROWS_EW, (r + 1) * ROWS_EW)
            for cc in range(SUB_IN // LANES):
                lane_tile = sub * (SUB_IN // LANES) + cc
                y = y_ref[sub, rows, cc * LANES:(cc + 1) * LANES]
                o_ref[rows, lane_tile * LANES:(lane_tile + 1) * LANES] = (
                    transform(kind, y, rows, lane_tile).astype(o_ref.dtype))

    for kind in sorted(set(kinds)):
        tiles = [t for t, k in enumerate(kinds) if k == kind]
        cond = functools.reduce(jnp.logical_or, [j == t for t in tiles])

        @pl.when(cond)
        def _(kind=kind):
            if kind == "plain":
                o_ref[...] = jnp.dot(xn_ref[...], wb_ref[...], preferred_element_type=F32).astype(o_ref.dtype)
            else:
                n_sub = TN_IN // SUB_IN
                for sub in range(n_sub):
                    y_ref[sub] = jnp.dot(xn_ref[...], wb_ref[:, sub * SUB_IN:(sub + 1) * SUB_IN],
                                         preferred_element_type=F32)
                    if sub > 0:
                        epilogue(kind, sub - 1)
                epilogue(kind, n_sub - 1)


def _in_projection(xn, w_in, tables, table_groups, col_tiles, kinds, out_dtype, name):
    T = xn.shape[0]
    n_col = len(col_tiles)
    col_tiles = jnp.asarray(col_tiles, jnp.int32)
    with_epilogue = any(k != "plain" for k in kinds)
    last_table_tile = max([t for t, k in enumerate(kinds) if k.startswith("rope")], default=-1)
    out_bytes = jnp.dtype(out_dtype).itemsize

    def table_map(j, i, ct):
        _, local, _ = _segment_position(i, table_groups)
        return (jnp.where(j <= last_table_tile, local, 0), 0)

    in_specs = [pl.BlockSpec((TM_IN, D_MODEL), lambda j, i, ct: (i, 0))]
    args = [xn]
    scratch = [pltpu.VMEM((D_MODEL, TN_IN), BF16)]
    nbytes = (2 * TM_IN * D_MODEL * 2 + 2 * D_MODEL * TN_IN * 4 + D_MODEL * TN_IN * 2
              + 2 * TM_IN * TN_IN * out_bytes + TM_IN * TN_IN * 4)
    if with_epilogue:
        in_specs += [pl.BlockSpec((TM_IN, LANES), table_map)] * 3
        args += list(tables)
        scratch.append(pltpu.VMEM((TN_IN // SUB_IN, TM_IN, SUB_IN), F32))
        nbytes += 2 * 3 * TM_IN * LANES * 4 + TM_IN * TN_IN * 4
    in_specs.append(pl.BlockSpec((D_MODEL, TN_IN), lambda j, i, ct: (0, ct[j])))
    args.append(w_in)
    return pl.pallas_call(
        functools.partial(_inproj_kernel, kinds),
        grid_spec=pltpu.PrefetchScalarGridSpec(
            num_scalar_prefetch=1,
            grid=(n_col, T // TM_IN),
            in_specs=in_specs,
            out_specs=pl.BlockSpec((TM_IN, TN_IN), lambda j, i, ct: (i, j)),
            scratch_shapes=scratch),
        out_shape=jax.ShapeDtypeStruct((T, n_col * TN_IN), out_dtype),
        compiler_params=pltpu.CompilerParams(
            dimension_semantics=("arbitrary", "arbitrary"),
            vmem_limit_bytes=_vmem_limit(nbytes)),
        name=name,
    )(col_tiles, *args)


def _attn_kernel(seq_groups, sink_ref, q_ref, kc_ref, vc_ref, kp_ref, kn_ref, vp_ref, vn_ref,
                 ga_ref, o_ref, s_all, p_all, r_all):
    t = pl.program_id(0)
    h = pl.program_id(1)
    _, local, per = _segment_position(t, seq_groups)
    has_prev = local != 0
    has_next = local != per - 1

    k_all = jnp.concatenate([kp_ref[...], kc_ref[...], kn_ref[...]], axis=0)
    v_all = jnp.concatenate([vp_ref[...], vc_ref[...], vn_ref[...]], axis=0)
    qi = lax.broadcasted_iota(jnp.int32, (BLOCK, BLOCK), 0)
    kj = lax.broadcasted_iota(jnp.int32, (BLOCK, BLOCK), 1)
    neg = jnp.float32(-jnp.inf)
    bias_prev = jnp.where(kj >= qi, 0.0, neg)
    bias_next = jnp.where(kj <= qi, 0.0, neg)
    bias_prev0 = jnp.where(has_prev, bias_prev, neg)
    bias_next_last = jnp.where(has_next, bias_next, neg)
    n_blocks = TQ_ATT // BLOCK
    for b in range(n_blocks):
        rs = slice(b * BLOCK, (b + 1) * BLOCK)
        kb = k_all[b * BLOCK:(b + 3) * BLOCK]
        vb = v_all[b * BLOCK:(b + 3) * BLOCK]
        bp = bias_prev0 if b == 0 else bias_prev
        bn = bias_next_last if b == n_blocks - 1 else bias_next
        heads = [slice(g * HEAD_DIM, (g + 1) * HEAD_DIM) for g in range(GROUP)]
        s_ref, p_ref, r_ref = s_all.at[b], p_all.at[b], r_all.at[b]
        for g, cs in enumerate(heads):
            s_ref[g] = _dot_nt(q_ref[rs, cs], kb)
        for g, cs in enumerate(heads):
            sink2 = sink_ref[GROUP * h + g] * LOG2E
            for r in range(BLOCK // SLAB_ATT):
                rr = slice(r * SLAB_ATT, (r + 1) * SLAB_ATT)
                s = s_ref[g, rr, :]
                s_p = s[:, :BLOCK] + bp[rr]
                s_c = s[:, BLOCK:2 * BLOCK]
                s_n = s[:, 2 * BLOCK:] + bn[rr]
                m = jnp.maximum(jnp.max(jnp.maximum(jnp.maximum(s_p, s_c), s_n), axis=-1, keepdims=True), sink2)
                p_p, p_c, p_n = jnp.exp2(s_p - m), jnp.exp2(s_c - m), jnp.exp2(s_n - m)
                denom = jnp.sum(p_p + p_c + p_n, axis=-1, keepdims=True) + jnp.exp2(sink2 - m)
                p_ref[g, rr, :] = jnp.concatenate([p_p, p_c, p_n], axis=1).astype(BF16)
                r_ref[g, rr, :] = jnp.broadcast_to(1.0 / denom, (SLAB_ATT, LANES))
        for g, cs in enumerate(heads):
            o = jnp.dot(p_ref[g], vb, preferred_element_type=F32) * r_ref[g]
            o_ref[rs, cs] = (_sigmoid(ga_ref[rs, cs].astype(F32)) * o).astype(o_ref.dtype)


def _window_attention(slab, sink, seq_groups):
    T = slab.shape[0]
    n_tiles = T // TQ_ATT
    bpt = TQ_ATT // BLOCK
    last_blk = T // BLOCK - 1
    gw = GROUP * HEAD_DIM
    kcol, vcol, gcol = A_K // HEAD_DIM, A_V // HEAD_DIM, A_GA // gw
    nbytes = (2 * (3 * TQ_ATT * gw * 2 + 2 * TQ_ATT * HEAD_DIM * 2 + 4 * BLOCK * HEAD_DIM * 2)
              + bpt * GROUP * BLOCK * (3 * BLOCK * 6 + LANES * 4))
    return pl.pallas_call(
        functools.partial(_attn_kernel, seq_groups),
        grid=(n_tiles, N_KV_HEADS),
        in_specs=[
            pl.BlockSpec(memory_space=pltpu.SMEM),
            pl.BlockSpec((TQ_ATT, gw), lambda t, h: (t, h)),
            pl.BlockSpec((TQ_ATT, HEAD_DIM), lambda t, h: (t, kcol + h)),
            pl.BlockSpec((TQ_ATT, HEAD_DIM), lambda t, h: (t, vcol + h)),
            pl.BlockSpec((BLOCK, HEAD_DIM), lambda t, h: (jnp.maximum(t * bpt - 1, 0), kcol + h)),
            pl.BlockSpec((BLOCK, HEAD_DIM), lambda t, h: (jnp.minimum((t + 1) * bpt, last_blk), kcol + h)),
            pl.BlockSpec((BLOCK, HEAD_DIM), lambda t, h: (jnp.maximum(t * bpt - 1, 0), vcol + h)),
            pl.BlockSpec((BLOCK, HEAD_DIM), lambda t, h: (jnp.minimum((t + 1) * bpt, last_blk), vcol + h)),
            pl.BlockSpec((TQ_ATT, gw), lambda t, h: (t, gcol + h)),
        ],
        out_specs=pl.BlockSpec((TQ_ATT, gw), lambda t, h: (t, h)),
        out_shape=jax.ShapeDtypeStruct((T, D_ATT), BF16),
        scratch_shapes=[pltpu.VMEM((bpt, GROUP, BLOCK, 3 * BLOCK), F32),
                        pltpu.VMEM((bpt, GROUP, BLOCK, 3 * BLOCK), BF16),
                        pltpu.VMEM((bpt, GROUP, BLOCK, LANES), F32)],
        compiler_params=pltpu.CompilerParams(
            dimension_semantics=("parallel", "arbitrary"),
            vmem_limit_bytes=_vmem_limit(nbytes)),
        name="window_attention",
    )(sink, slab, slab, slab, slab, slab, slab, slab, slab)


def _hgrn_kernel(seq_groups, lb_ref, qf_ref, if_ref, zf_ref, qb_ref, ib_ref, zb_ref,
                 of_ref, ob_ref, st_ref, stb_ref, ds_ref, ops_ref, cross_ref, dec_ref, a_ref):
    g = pl.program_id(1)
    _, local, _ = _segment_position(g, seq_groups)

    @pl.when(local == 0)
    def _():
        st_ref[...] = jnp.zeros(st_ref.shape, F32)
        stb_ref[...] = jnp.zeros(stb_ref.shape, BF16)

    Q_IN, K_IN, Q_ST, K_ST, I_C = range(5)
    pair = 2 * CHUNK
    n_pairs = SEG_REC // pair
    rin = lax.broadcasted_iota(jnp.int32, (CHUNK, LANES), 0)
    ti = lax.broadcasted_iota(jnp.int32, (CHUNK, CHUNK), 0)
    si = lax.broadcasted_iota(jnp.int32, (CHUNK, CHUNK), 1)
    half = (slice(0, CHUNK), slice(CHUNK, pair))

    def prepare(d, hh, start, q_ref, i_ref, z_ref):
        cs = slice(hh * LANES, (hh + 1) * LANES)
        lb = lb_ref[:, cs]
        one_m_lb = 1.0 - lb
        q_in, k_in, k_st, q_dc, dec = [], [], [], [], []
        for t in range(2):
            rows = _rows(start + t * CHUNK, CHUNK)
            q = q_ref[rows, cs].astype(F32)
            gate = one_m_lb * _sigmoid(z_ref[rows, cs])
            k = one_m_lb - gate
            logf = jnp.log2(lb + gate)
            b = logf
            for s in (1, 2, 4, 8, 16):
                b = b + jnp.where(rin >= s, pltpu.roll(b, s, 0), 0.0)
            tot = b[CHUNK - 1:CHUNK, :]
            if d == 0:
                c = b
                cref = b[CHUNK // 2 - 1:CHUNK // 2, :]
            else:
                c = tot - b + logf
                cref = c[CHUNK // 2:CHUNK // 2 + 1, :]
            q_in.append(q * jnp.exp2(c - cref))
            k_in.append(k * jnp.exp2(cref - c))
            k_st.append(k_in[t] * jnp.exp2(tot - cref))
            q_dc.append(q_in[t] * jnp.exp2(cref))
            dec.append(jnp.exp2(tot))
            ops_ref[d, hh, Q_IN, half[t]] = q_in[t].astype(BF16)
            ops_ref[d, hh, K_IN, half[t]] = k_in[t].astype(BF16)
            ops_ref[d, hh, I_C, half[t]] = i_ref[rows, cs]
        ca, cb = (0, 1) if d == 0 else (1, 0)
        ops_ref[d, hh, Q_ST, half[ca]] = q_dc[ca].astype(BF16)
        ops_ref[d, hh, Q_ST, half[cb]] = (q_dc[cb] * dec[ca]).astype(BF16)
        ops_ref[d, hh, K_ST, half[ca]] = (k_st[ca] * dec[cb]).astype(BF16)
        ops_ref[d, hh, K_ST, half[cb]] = k_st[cb].astype(BF16)
        cross_ref[d, hh, 0] = q_dc[cb].astype(BF16)
        cross_ref[d, hh, 1] = k_st[ca].astype(BF16)
        dec_ref[d, hh] = jnp.broadcast_to(dec[0] * dec[1], (8, LANES))

    def intra(d, hh):
        for t in range(2):
            a_ref[d, hh, t] = _dot_nt(ops_ref[d, hh, Q_IN, half[t]], ops_ref[d, hh, K_IN, half[t]])
        a_ref[d, hh, 2] = _dot_nt(cross_ref[d, hh, 0], cross_ref[d, hh, 1])

    def output(d, hh, start, o_ref):
        cs = slice(hh * LANES, (hh + 1) * LANES)
        tri = (ti >= si) if d == 0 else (ti <= si)
        a_top = jnp.where(tri, a_ref[d, hh, 0], 0.0).astype(BF16)
        a_bot = jnp.where(tri, a_ref[d, hh, 1], 0.0).astype(BF16)
        a_x = a_ref[d, hh, 2].astype(BF16)
        i_top, i_bot = ops_ref[d, hh, I_C, half[0]], ops_ref[d, hh, I_C, half[1]]
        inter = _dot_nt(ops_ref[d, hh, Q_ST], stb_ref[d, hh])
        if d == 0:
            both = jnp.dot(jnp.concatenate([a_top, a_x], axis=0), i_top, preferred_element_type=F32)
            o_top = both[:CHUNK] + inter[:CHUNK]
            o_bot = both[CHUNK:] + jnp.dot(a_bot, i_bot, preferred_element_type=F32) + inter[CHUNK:]
        else:
            both = jnp.dot(jnp.concatenate([a_x, a_bot], axis=0), i_bot, preferred_element_type=F32)
            o_top = both[:CHUNK] + jnp.dot(a_top, i_top, preferred_element_type=F32) + inter[:CHUNK]
            o_bot = both[CHUNK:] + inter[CHUNK:]
        o_ref[_rows(start, CHUNK), cs] = o_top.astype(o_ref.dtype)
        o_ref[_rows(start + CHUNK, CHUNK), cs] = o_bot.astype(o_ref.dtype)
        ds_ref[d, hh] = _dot_tn(ops_ref[d, hh, I_C], ops_ref[d, hh, K_ST])

    def update(d, hh):
        dec = dec_ref[d, hh, 0:1, :]
        for r in range(REC_VAL_DIM // CHUNK):
            rr = slice(r * CHUNK, (r + 1) * CHUNK)
            new = dec * st_ref[d, hh, rr, :] + ds_ref[d, hh, rr, :]
            st_ref[d, hh, rr, :] = new
            stb_ref[d, hh, rr, :] = new.astype(BF16)

    def scan(n, carry):
        start_f = pl.multiple_of(n * pair, pair)
        start_b = pl.multiple_of((n_pairs - 1 - n) * pair, pair)
        for hh in range(HB_REC):
            prepare(0, hh, start_f, qf_ref, if_ref, zf_ref)
            prepare(1, hh, start_b, qb_ref, ib_ref, zb_ref)
        for hh in range(HB_REC):
            intra(0, hh)
            intra(1, hh)
        for hh in range(HB_REC):
            output(0, hh, start_f, of_ref)
            output(1, hh, start_b, ob_ref)
        for hh in range(HB_REC):
            update(0, hh)
            update(1, hh)
        return carry

    lax.fori_loop(0, n_pairs, scan, 0)


def _hgrn2(slab, gates, lb_row, seq_groups):
    T = slab.shape[0]
    w = HB_REC * LANES

    def mirror(g):
        first, local, per = _segment_position(g, seq_groups)
        return first + per - 1 - local

    def fspec(off):
        return pl.BlockSpec((SEG_REC, w), lambda hb, g: (g, off // w + hb))

    def bspec(off):
        return pl.BlockSpec((SEG_REC, w), lambda hb, g: (mirror(g), off // w + hb))

    state = (2, HB_REC, REC_VAL_DIM, REC_KEY_DIM)
    nbytes = (2 * (4 * SEG_REC * w * 2 + 2 * SEG_REC * w * 4 + 2 * SEG_REC * w * 2)
              + 2 * HB_REC * (REC_VAL_DIM * REC_KEY_DIM * 10 + 12 * CHUNK * LANES * 2 + 8 * LANES * 4))
    out = jax.ShapeDtypeStruct((T, D_REC_V), BF16)
    return pl.pallas_call(
        functools.partial(_hgrn_kernel, seq_groups),
        grid=(N_REC_HEADS // HB_REC, T // SEG_REC),
        in_specs=[
            pl.BlockSpec((1, w), lambda hb, g: (0, hb)),
            fspec(A_QR), fspec(A_IR), fspec(0),
            bspec(A_QR), bspec(A_IR), bspec(D_REC_K),
        ],
        out_specs=[pl.BlockSpec((SEG_REC, w), lambda hb, g: (g, hb)),
                   pl.BlockSpec((SEG_REC, w), lambda hb, g: (mirror(g), hb))],
        out_shape=[out, out],
        scratch_shapes=[pltpu.VMEM(state, F32), pltpu.VMEM(state, BF16), pltpu.VMEM(state, F32),
                        pltpu.VMEM((2, HB_REC, 5, 2 * CHUNK, LANES), BF16),
                        pltpu.VMEM((2, HB_REC, 2, CHUNK, LANES), BF16),
                        pltpu.VMEM((2, HB_REC, 8, LANES), F32),
                        pltpu.VMEM((2, HB_REC, 3, CHUNK, CHUNK), F32)],
        compiler_params=pltpu.CompilerParams(
            dimension_semantics=("parallel", "arbitrary"),
            vmem_limit_bytes=_vmem_limit(nbytes)),
        name="hgrn2",
    )(lb_row, slab, slab, gates, slab, slab, gates)


def _outproj_kernel(n_first, a_ref, of_ref, ob_ref, grl_ref, grh_ref, gtl_ref, gth_ref, xa_ref, xb_ref,
                    w_ref, rg_ref, g_ref, h_ref, m_ref):
    half = D_REC_V // 2
    heads_per_chunk = KC_OUT // REC_VAL_DIM
    y = None
    for c in range(D_MODEL // KC_OUT):
        for hh in range(c * heads_per_chunk, (c + 1) * heads_per_chunk):
            cs = slice(hh * REC_VAL_DIM, (hh + 1) * REC_VAL_DIM)
            lo = hh * REC_VAL_DIM < half
            hs = cs if lo else slice(hh * REC_VAL_DIM - half, (hh + 1) * REC_VAL_DIM - half)
            rec = of_ref[:, cs].astype(F32) + ob_ref[:, cs].astype(F32)
            ms = jnp.mean(rec * rec, axis=-1, keepdims=True)
            r = (rec * lax.rsqrt(ms + RMS_EPS)) * rg_ref[:, cs]
            gr = (grl_ref if lo else grh_ref)[:, hs].astype(F32)
            gt = (gtl_ref if lo else gth_ref)[:, hs].astype(F32)
            r = _sigmoid(gt) * (r * (gr * _sigmoid(gr)))
            m_ref[:, cs] = (a_ref[:, cs].astype(F32) + r).astype(BF16)
        ks = slice(c * KC_OUT, (c + 1) * KC_OUT)
        part = jnp.dot(m_ref[:, ks], w_ref[ks, :], preferred_element_type=F32)
        y = part if y is None else y + part
    ms = jnp.mean(y * y, axis=-1, keepdims=True)
    h_ref[...] = (y * lax.rsqrt(ms + RMS_EPS)) * g_ref[...]

    @pl.when(pl.program_id(0) < n_first)
    def _():
        h_ref[...] += xa_ref[...]

    @pl.when(pl.program_id(0) >= n_first)
    def _():
        h_ref[...] += xb_ref[...]


def _out_projection(attn, rec_f, rec_b, slab, xa, xb, w_bf16, rec_gain, gain):
    T = attn.shape[0]
    n_first = xa.shape[0] // TM_OUT
    half = D_REC_V // 2
    assert A_GR % half == 0 and A_GT % half == 0
    tok = lambda i: (i, 0)
    nbytes = (2 * 3 * TM_OUT * D_MODEL * 2 + 2 * 4 * TM_OUT * half * 2 + 2 * 3 * TM_OUT * D_MODEL * 4
              + 2 * D_MODEL * D_MODEL * 2 + TM_OUT * D_MODEL * 2 + 2 * TM_OUT * D_MODEL * 4)
    return pl.pallas_call(
        functools.partial(_outproj_kernel, n_first),
        grid=(T // TM_OUT,),
        in_specs=[
            pl.BlockSpec((TM_OUT, D_MODEL), tok),
            pl.BlockSpec((TM_OUT, D_REC_V), tok),
            pl.BlockSpec((TM_OUT, D_REC_V), tok),
            pl.BlockSpec((TM_OUT, half), lambda i: (i, A_GR // half)),
            pl.BlockSpec((TM_OUT, half), lambda i: (i, A_GR // half + 1)),
            pl.BlockSpec((TM_OUT, half), lambda i: (i, A_GT // half)),
            pl.BlockSpec((TM_OUT, half), lambda i: (i, A_GT // half + 1)),
            *_two_source_specs(TM_OUT, n_first),
            pl.BlockSpec((D_MODEL, D_MODEL), lambda i: (0, 0)),
            pl.BlockSpec((1, D_REC_V), lambda i: (0, 0)),
            pl.BlockSpec((1, D_MODEL), lambda i: (0, 0)),
        ],
        out_specs=pl.BlockSpec((TM_OUT, D_MODEL), tok),
        out_shape=jax.ShapeDtypeStruct((T, D_MODEL), F32),
        scratch_shapes=[pltpu.VMEM((TM_OUT, D_MODEL), BF16)],
        compiler_params=pltpu.CompilerParams(
            dimension_semantics=("parallel",),
            vmem_limit_bytes=_vmem_limit(nbytes)),
        name="out_projection",
    )(attn, rec_f, rec_b, slab, slab, slab, slab, xa, xb, w_bf16, rec_gain, gain)


def _ffn_kernel(n_first, h_ref, gpre_ref, gpost_ref, wg_ref, wu_ref, wd_ref, oa_ref, ob_ref, hn_ref, acc_ref):
    i = pl.program_id(0)
    j = pl.program_id(1)

    @pl.when(j == 0)
    def _():
        def body(rows):
            h = h_ref[rows, :]
            ms = jnp.mean(h * h, axis=-1, keepdims=True)
            hn_ref[rows, :] = ((h * lax.rsqrt(ms + RMS_EPS)) * gpre_ref[...]).astype(BF16)
            acc_ref[rows, :] = jnp.zeros((ROWS_EW, D_MODEL), F32)
        _row_tiles(TM_FFN, body)

    hn = hn_ref[...]
    g = jnp.dot(hn, wg_ref[...], preferred_element_type=F32)
    u = jnp.dot(hn, wu_ref[...], preferred_element_type=F32)
    act = ((g * _sigmoid(g)) * u).astype(BF16)
    acc_ref[...] += jnp.dot(act, wd_ref[...], preferred_element_type=F32)

    def finish(o_ref):
        def body(rows):
            y = acc_ref[rows, :]
            ms = jnp.mean(y * y, axis=-1, keepdims=True)
            o_ref[rows, :] = h_ref[rows, :] + (y * lax.rsqrt(ms + RMS_EPS)) * gpost_ref[...]
        _row_tiles(TM_FFN, body)

    last = j == pl.num_programs(1) - 1

    @pl.when(jnp.logical_and(last, i < n_first))
    def _():
        finish(oa_ref)

    @pl.when(jnp.logical_and(last, i >= n_first))
    def _():
        finish(ob_ref)


def _ffn(h, n_tokens_first, gpre, gpost, wg, wu, wd):
    T = h.shape[0]
    n_first = n_tokens_first // TM_FFN
    nbytes = (2 * 3 * TM_FFN * D_MODEL * 4 + TM_FFN * D_MODEL * 2 + 2 * 3 * D_MODEL * TF_FFN * 2
              + 3 * TM_FFN * TF_FFN * 4 + 2 * TM_FFN * D_MODEL * 4)
    return pl.pallas_call(
        functools.partial(_ffn_kernel, n_first),
        grid=(T // TM_FFN, D_FF // TF_FFN),
        in_specs=[
            pl.BlockSpec((TM_FFN, D_MODEL), lambda i, j: (i, 0)),
            pl.BlockSpec((1, D_MODEL), lambda i, j: (0, 0)),
            pl.BlockSpec((1, D_MODEL), lambda i, j: (0, 0)),
            pl.BlockSpec((D_MODEL, TF_FFN), lambda i, j: (0, j)),
            pl.BlockSpec((D_MODEL, TF_FFN), lambda i, j: (0, j)),
            pl.BlockSpec((TF_FFN, D_MODEL), lambda i, j: (j, 0)),
        ],
        out_specs=list(_two_source_specs(TM_FFN, n_first)),
        out_shape=[jax.ShapeDtypeStruct((n_tokens_first, D_MODEL), F32),
                   jax.ShapeDtypeStruct((T - n_tokens_first, D_MODEL), F32)],
        scratch_shapes=[pltpu.VMEM((TM_FFN, D_MODEL), BF16), pltpu.VMEM((TM_FFN, D_MODEL), F32)],
        compiler_params=pltpu.CompilerParams(
            dimension_semantics=("arbitrary", "arbitrary"),
            vmem_limit_bytes=_vmem_limit(nbytes)),
        name="swiglu_ffn",
    )(h, gpre, gpost, wg, wu, wd)


def _rope_tables(max_len):
    pos = jnp.arange(max_len, dtype=F32)
    inv_freq = ROPE_THETA ** (-jnp.arange(ROPE_HALF, dtype=F32) / ROPE_HALF)
    ang = pos[:, None] * inv_freq[None, :]
    cos, sin = jnp.cos(ang), jnp.sin(ang)
    rest = HEAD_DIM - ROPE_DIM
    cos_t = jnp.concatenate([cos, cos, jnp.ones((max_len, rest), F32)], axis=1)
    sa_t = jnp.concatenate([-sin, jnp.zeros((max_len, HEAD_DIM - ROPE_HALF), F32)], axis=1)
    sb_t = jnp.concatenate([jnp.zeros((max_len, ROPE_HALF), F32), sin, jnp.zeros((max_len, rest), F32)], axis=1)
    return cos_t, sa_t, sb_t


def _seq_groups(seq_shapes, seg):
    groups, first = [], 0
    for B, L in seq_shapes:
        assert L % seg == 0
        groups.append((first, L // seg))
        first += B * L // seg
    return tuple(groups)


def _encoder_layer(xa, xb, seq_shapes, tables, w_in, sink, rec_norm, lb, w_out, norm_mix_pre, norm_mix_post,
                   norm_ffn_pre, norm_ffn_post, w_gate, w_up, w_down):
    row = lambda v: v.astype(F32).reshape(1, -1)
    xn = _stream_norm(xa, xb, row(norm_mix_pre))

    assert all(off % TN_IN == 0 for off in (OFF_K, OFF_QR, OFF_ZF, OFF_IR)) and 2 * D_KV == TN_IN
    gate_tiles = list(range(OFF_ZF // TN_IN, OFF_IR // TN_IN))
    slab_tiles = [t for t in range(D_IN // TN_IN) if t not in gate_tiles]
    kind_of = lambda t: ("rope" if t < OFF_K // TN_IN else "rope_k" if t == OFF_K // TN_IN
                         else "silu" if OFF_QR // TN_IN <= t < OFF_ZF // TN_IN else "plain")
    table_groups = _seq_groups(seq_shapes, TM_IN)
    slab = _in_projection(xn, w_in, tables, table_groups, slab_tiles, tuple(kind_of(t) for t in slab_tiles),
                          BF16, "in_projection")
    gates = _in_projection(xn, w_in, None, None, gate_tiles, ("plain",) * len(gate_tiles),
                           F32, "gate_projection")

    attn = _window_attention(slab, sink.astype(F32), _seq_groups(seq_shapes, TQ_ATT))
    rec_f, rec_b = _hgrn2(slab, gates, row(lb), _seq_groups(seq_shapes, SEG_REC))
    h = _out_projection(attn, rec_f, rec_b, slab, xa, xb, w_out.astype(BF16), row(rec_norm),
                        row(norm_mix_post))
    return _ffn(h, xa.shape[0], row(norm_ffn_pre), row(norm_ffn_post),
                w_gate.astype(BF16), w_up.astype(BF16), w_down.astype(BF16))


def kernel(x_prompt, x_sample, w_in, sink, rec_norm, lb_logits, w_out, norm_mix_pre, norm_mix_post,
           norm_ffn_pre, norm_ffn_post, w_gate, w_up, w_down):
    lb_all = jnp.cumsum(jax.nn.softmax(lb_logits.astype(F32), axis=0), axis=0)
    seq_shapes = (x_prompt.shape[:2], x_sample.shape[:2])
    xa = x_prompt.reshape(-1, D_MODEL)
    xb = x_sample.reshape(-1, D_MODEL)
    tables = _rope_tables(max(L for _, L in seq_shapes))
    for l in range(DEPTH):
        xa, xb = _encoder_layer(xa, xb, seq_shapes, tables, w_in[l], sink[l], rec_norm[l], lb_all[l], w_out[l],
                                norm_mix_pre[l], norm_mix_post[l], norm_ffn_pre[l], norm_ffn_post[l],
                                w_gate[l], w_up[l], w_down[l])
    return (xa.reshape(x_prompt.shape), xb.reshape(x_sample.shape))
```

```python
import functools

import jax
import jax.numpy as jnp
import numpy as np
from jax import lax
from jax.experimental import pallas as pl
from jax.experimental.pallas import tpu as pltpu

F32 = jnp.float32
BF16 = jnp.bfloat16

D_MODEL = 2048
DEPTH = 1
HEAD_DIM = 128
N_Q_HEADS = 16
N_KV_HEADS = 4
GROUP = N_Q_HEADS // N_KV_HEADS
WINDOW = 128
BLOCK = 128
ROPE_DIM = HEAD_DIM // 4
ROPE_HALF = ROPE_DIM // 2
ROPE_THETA = 500000.0
N_REC_HEADS = 16
REC_KEY_DIM = 128
REC_VAL_DIM = 128
CHUNK = 32
D_FF = -(-8 * D_MODEL // (3 * 256)) * 256
RMS_EPS = 1e-6
LOG2E = 1.4426950408889634
Q_SCALE = HEAD_DIM ** -0.5 * LOG2E

D_ATT = N_Q_HEADS * HEAD_DIM
D_KV = N_KV_HEADS * HEAD_DIM
D_REC_K = N_REC_HEADS * REC_KEY_DIM
D_REC_V = N_REC_HEADS * REC_VAL_DIM
SPLIT_SIZES = (D_ATT, D_KV, D_KV, D_REC_K, D_REC_K, D_REC_K, D_REC_V, D_REC_V, D_MODEL, D_MODEL)
D_IN = sum(SPLIT_SIZES)
(OFF_Q, OFF_K, OFF_V, OFF_QR, OFF_ZF, OFF_ZB, OFF_IR, OFF_GR, OFF_GA, OFF_GT) = (
    int(v) for v in np.concatenate([[0], np.cumsum(SPLIT_SIZES)[:-1]]))
D_GATES = OFF_IR - OFF_ZF
D_SLAB = D_IN - D_GATES
A_Q, A_K, A_V, A_QR = OFF_Q, OFF_K, OFF_V, OFF_QR
A_IR, A_GR, A_GA, A_GT = (OFF_IR - D_GATES, OFF_GR - D_GATES, OFF_GA - D_GATES, OFF_GT - D_GATES)

V7X_VMEM_CEILING = 56 * 1024 * 1024
LANES = 128

TM_NORM = 512
TM_IN, TN_IN = 1024, 1024
SUB_IN = 256
ROWS_EW = 256
SLAB_ATT = 32
TQ_ATT = 1024
SEG_REC = 512
HB_REC = 8
TM_OUT = 256
KC_OUT = 512
TM_FFN, TF_FFN = 512, 512


def _vmem_limit(nbytes):
    return int(min(V7X_VMEM_CEILING, nbytes * 1.25 + (4 << 20)))


def _sigmoid(x):
    return 1.0 / (1.0 + jnp.exp(-x))


def _dot_nt(a, b):
    return lax.dot_general(a, b, (((1,), (1,)), ((), ())), preferred_element_type=F32)


def _dot_tn(a, b):
    return lax.dot_general(a, b, (((0,), (0,)), ((), ())), preferred_element_type=F32)


def _rows(start, size):
    return pl.ds(start if isinstance(start, int) else pl.multiple_of(start, size), size)


def _row_tiles(n_rows, body):
    def step(r, c):
        body(_rows(r * ROWS_EW, ROWS_EW))
        return c
    lax.fori_loop(0, n_rows // ROWS_EW, step, 0)


def _segment_position(g, seq_groups):
    first, local, per = None, None, None
    for g0, n in reversed(seq_groups):
        loc = lax.rem(g - g0, n)
        fst = g - loc
        if first is None:
            first, local, per = fst, loc, n
        else:
            here = g < nxt
            first = jnp.where(here, fst, first)
            local = jnp.where(here, loc, local)
            per = jnp.where(here, n, per)
        nxt = g0
    return first, local, per


def _two_source_specs(tm, n_first):
    return (pl.BlockSpec((tm, D_MODEL), lambda i, *_: (jnp.minimum(i, n_first - 1), 0)),
            pl.BlockSpec((tm, D_MODEL), lambda i, *_: (jnp.maximum(i - n_first, 0), 0)))


def _norm_kernel(n_first, xa_ref, xb_ref, g_ref, o_ref):
    def run(x_ref):
        def body(rows):
            x = x_ref[rows, :]
            ms = jnp.mean(x * x, axis=-1, keepdims=True)
            o_ref[rows, :] = ((x * lax.rsqrt(ms + RMS_EPS)) * g_ref[...]).astype(o_ref.dtype)
        _row_tiles(TM_NORM, body)

    @pl.when(pl.program_id(0) < n_first)
    def _():
        run(xa_ref)

    @pl.when(pl.program_id(0) >= n_first)
    def _():
        run(xb_ref)


def _stream_norm(xa, xb, gain):
    T = xa.shape[0] + xb.shape[0]
    n_first = xa.shape[0] // TM_NORM
    nbytes = 2 * 2 * TM_NORM * D_MODEL * 4 + 2 * TM_NORM * D_MODEL * 2
    return pl.pallas_call(
        functools.partial(_norm_kernel, n_first),
        grid=(T // TM_NORM,),
        in_specs=[*_two_source_specs(TM_NORM, n_first), pl.BlockSpec((1, D_MODEL), lambda i: (0, 0))],
        out_specs=pl.BlockSpec((TM_NORM, D_MODEL), lambda i: (i, 0)),
        out_shape=jax.ShapeDtypeStruct((T, D_MODEL), BF16),
        compiler_params=pltpu.CompilerParams(
            dimension_semantics=("parallel",), vmem_limit_bytes=_vmem_limit(nbytes)),
        name="stream_norm",
    )(xa, xb, gain)


def _inproj_kernel(kinds, col_tiles_ref, *refs):
    del col_tiles_ref
    with_epilogue = any(k != "plain" for k in kinds)
    if with_epilogue:
        xn_ref, cos_ref, sa_ref, sb_ref, w_ref, o_ref, wb_ref, y_ref = refs
    else:
        xn_ref, w_ref, o_ref, wb_ref = refs
    j = pl.program_id(0)

    @pl.when(pl.program_id(1) == 0)
    def _():
        def body(rows):
            wb_ref[rows, :] = w_ref[rows, :].astype(BF16)
        _row_tiles(D_MODEL, body)

    def transform(kind, y, rows, lane_tile):
        if kind == "silu":
            return y * _sigmoid(y)
        if kind == "rope_k" and lane_tile >= N_KV_HEADS:
            return y
        y = (y * cos_ref[rows, :] + pltpu.roll(y, HEAD_DIM - ROPE_HALF, 1) * sa_ref[rows, :]
             + pltpu.roll(y, ROPE_HALF, 1) * sb_ref[rows, :])
        return y * Q_SCALE if kind == "rope" else y

    def epilogue(kind, sub):
        for r in range(TM_IN // ROWS_EW):
            rows = slice(r * ROWS_EW, (r + 1) * ROWS_EW)
            for cc in range(SUB_IN // LANES):
                lane_tile = sub * (SUB_IN // LANES) + cc
                y = y_ref[sub, rows, cc * LANES:(cc + 1) * LANES]
                o_ref[rows, lane_tile * LANES:(lane_tile + 1) * LANES] = (
                    transform(kind, y, rows, lane_tile).astype(o_ref.dtype))

    for kind in sorted(set(kinds)):
        tiles = [t for t, k in enumerate(kinds) if k == kind]
        cond = functools.reduce(jnp.logical_or, [j == t for t in tiles])

        @pl.when(cond)
        def _(kind=kind):
            if kind == "plain":
                o_ref[...] = jnp.dot(xn_ref[...], wb_ref[...], preferred_element_type=F32).astype(o_ref.dtype)
            else:
                n_sub = TN_IN // SUB_IN
                for sub in range(n_sub):
                    y_ref[sub] = jnp.dot(xn_ref[...], wb_ref[:, sub * SUB_IN:(sub + 1) * SUB_IN],
                                         preferred_element_type=F32)
                    if sub > 0:
                        epilogue(kind, sub - 1)
                epilogue(kind, n_sub - 1)


def _in_projection(xn, w_in, tables, table_groups, col_tiles, kinds, out_dtype, name):
    T = xn.shape[0]
    n_col = len(col_tiles)
    col_tiles = jnp.asarray(col_tiles, jnp.int32)
    with_epilogue = any(k != "plain" for k in kinds)
    last_table_tile = max([t for t, k in enumerate(kinds) if k.startswith("rope")], default=-1)
    out_bytes = jnp.dtype(out_dtype).itemsize

    def table_map(j, i, ct):
        _, local, _ = _segment_position(i, table_groups)
        return (jnp.where(j <= last_table_tile, local, 0), 0)

    in_specs = [pl.BlockSpec((TM_IN, D_MODEL), lambda j, i, ct: (i, 0))]
    args = [xn]
    scratch = [pltpu.VMEM((D_MODEL, TN_IN), BF16)]
    nbytes = (2 * TM_IN * D_MODEL * 2 + 2 * D_MODEL * TN_IN * 4 + D_MODEL * TN_IN * 2
              + 2 * TM_IN * TN_IN * out_bytes + TM_IN * TN_IN * 4)
    if with_epilogue:
        in_specs += [pl.BlockSpec((TM_IN, LANES), table_map)] * 3
        args += list(tables)
        scratch.append(pltpu.VMEM((TN_IN // SUB_IN, TM_IN, SUB_IN), F32))
        nbytes += 2 * 3 * TM_IN * LANES * 4 + TM_IN * TN_IN * 4
    in_specs.append(pl.BlockSpec((D_MODEL, TN_IN), lambda j, i, ct: (0, ct[j])))
    args.append(w_in)
    return pl.pallas_call(
        functools.partial(_inproj_kernel, kinds),
        grid_spec=pltpu.PrefetchScalarGridSpec(
            num_scalar_prefetch=1,
            grid=(n_col, T // TM_IN),
            in_specs=in_specs,
            out_specs=pl.BlockSpec((TM_IN, TN_IN), lambda j, i, ct: (i, j)),
            scratch_shapes=scratch),
        out_shape=jax.ShapeDtypeStruct((T, n_col * TN_IN), out_dtype),
        compiler_params=pltpu.CompilerParams(
            dimension_semantics=("arbitrary", "arbitrary"),
            vmem_limit_bytes=_vmem_limit(nbytes)),
        name=name,
    )(col_tiles, *args)


def _attn_kernel(seq_groups, sink_ref, q_ref, kc_ref, vc_ref, kp_ref, kn_ref, vp_ref, vn_ref,
                 ga_ref, o_ref, s_all, p_all, r_all):
    t = pl.program_id(0)
    h = pl.program_id(1)
    _, local, per = _segment_position(t, seq_groups)
    has_prev = local != 0
    has_next = local != per - 1

    k_all = jnp.concatenate([kp_ref[...], kc_ref[...], kn_ref[...]], axis=0)
    v_all = jnp.concatenate([vp_ref[...], vc_ref[...], vn_ref[...]], axis=0)
    qi = lax.broadcasted_iota(jnp.int32, (BLOCK, BLOCK), 0)
    kj = lax.broadcasted_iota(jnp.int32, (BLOCK, BLOCK), 1)
    neg = jnp.float32(-jnp.inf)
    bias_prev = jnp.where(kj >= qi, 0.0, neg)
    bias_next = jnp.where(kj <= qi, 0.0, neg)
    bias_prev0 = jnp.where(has_prev, bias_prev, neg)
    bias_next_last = jnp.where(has_next, bias_next, neg)
    n_blocks = TQ_ATT // BLOCK
    for b in range(n_blocks):
        rs = slice(b * BLOCK, (b + 1) * BLOCK)
        kb = k_all[b * BLOCK:(b + 3) * BLOCK]
        vb = v_all[b * BLOCK:(b + 3) * BLOCK]
        bp = bias_prev0 if b == 0 else bias_prev
        bn = bias_next_last if b == n_blocks - 1 else bias_next
        heads = [slice(g * HEAD_DIM, (g + 1) * HEAD_DIM) for g in range(GROUP)]
        s_ref, p_ref, r_ref = s_all.at[b], p_all.at[b], r_all.at[b]
        for g, cs in enumerate(heads):
            s_ref[g] = _dot_nt(q_ref[rs, cs], kb)
        for g, cs in enumerate(heads):
            sink2 = sink_ref[GROUP * h + g] * LOG2E
            for r in range(BLOCK // SLAB_ATT):
                rr = slice(r * SLAB_ATT, (r + 1) * SLAB_ATT)
                s = s_ref[g, rr, :]
                s_p = s[:, :BLOCK] + bp[rr]
                s_c = s[:, BLOCK:2 * BLOCK]
                s_n = s[:, 2 * BLOCK:] + bn[rr]
                m = jnp.maximum(jnp.max(jnp.maximum(jnp.maximum(s_p, s_c), s_n), axis=-1, keepdims=True), sink2)
                p_p, p_c, p_n = jnp.exp2(s_p - m), jnp.exp2(s_c - m), jnp.exp2(s_n - m)
                denom = jnp.sum(p_p + p_c + p_n, axis=-1, keepdims=True) + jnp.exp2(sink2 - m)
                p_ref[g, rr, :] = jnp.concatenate([p_p, p_c, p_n], axis=1).astype(BF16)
                r_ref[g, rr, :] = jnp.broadcast_to(1.0 / denom, (SLAB_ATT, LANES))
        for g, cs in enumerate(heads):
            o = jnp.dot(p_ref[g], vb, preferred_element_type=F32) * r_ref[g]
            o_ref[rs, cs] = (_sigmoid(ga_ref[rs, cs].astype(F32)) * o).astype(o_ref.dtype)


def _window_attention(slab, sink, seq_groups):
    T = slab.shape[0]
    n_tiles = T // TQ_ATT
    bpt = TQ_ATT // BLOCK
    last_blk = T // BLOCK - 1
    gw = GROUP * HEAD_DIM
    kcol, vcol, gcol = A_K // HEAD_DIM, A_V // HEAD_DIM, A_GA // gw
    nbytes = (2 * (3 * TQ_ATT * gw * 2 + 2 * TQ_ATT * HEAD_DIM * 2 + 4 * BLOCK * HEAD_DIM * 2)
              + bpt * GROUP * BLOCK * (3 * BLOCK * 6 + LANES * 4))
    return pl.pallas_call(
        functools.partial(_attn_kernel, seq_groups),
        grid=(n_tiles, N_KV_HEADS),
        in_specs=[
            pl.BlockSpec(memory_space=pltpu.SMEM),
            pl.BlockSpec((TQ_ATT, gw), lambda t, h: (t, h)),
            pl.BlockSpec((TQ_ATT, HEAD_DIM), lambda t, h: (t, kcol + h)),
            pl.BlockSpec((TQ_ATT, HEAD_DIM), lambda t, h: (t, vcol + h)),
            pl.BlockSpec((BLOCK, HEAD_DIM), lambda t, h: (jnp.maximum(t * bpt - 1, 0), kcol + h)),
            pl.BlockSpec((BLOCK, HEAD_DIM), lambda t, h: (jnp.minimum((t + 1) * bpt, last_blk), kcol + h)),
            pl.BlockSpec((BLOCK, HEAD_DIM), lambda t, h: (jnp.maximum(t * bpt - 1, 0), vcol + h)),
            pl.BlockSpec((BLOCK, HEAD_DIM), lambda t, h: (jnp.minimum((t + 1) * bpt, last_blk), vcol + h)),
            pl.BlockSpec((TQ_ATT, gw), lambda t, h: (t, gcol + h)),
        ],
        out_specs=pl.BlockSpec((TQ_ATT, gw), lambda t, h: (t, h)),
        out_shape=jax.ShapeDtypeStruct((T, D_ATT), BF16),
        scratch_shapes=[pltpu.VMEM((bpt, GROUP, BLOCK, 3 * BLOCK), F32),
                        pltpu.VMEM((bpt, GROUP, BLOCK, 3 * BLOCK), BF16),
                        pltpu.VMEM((bpt, GROUP, BLOCK, LANES), F32)],
        compiler_params=pltpu.CompilerParams(
            dimension_semantics=("parallel", "arbitrary"),
            vmem_limit_bytes=_vmem_limit(nbytes)),
        name="window_attention",
    )(sink, slab, slab, slab, slab, slab, slab, slab, slab)


def _hgrn_kernel(seq_groups, lb_ref, qf_ref, if_ref, zf_ref, qb_ref, ib_ref, zb_ref,
                 of_ref, ob_ref, st_ref, stb_ref, ds_ref, ops_ref, cross_ref, dec_ref, a_ref):
    g = pl.program_id(1)
    _, local, _ = _segment_position(g, seq_groups)

    @pl.when(local == 0)
    def _():
        st_ref[...] = jnp.zeros(st_ref.shape, F32)
        stb_ref[...] = jnp.zeros(stb_ref.shape, BF16)

    Q_IN, K_IN, Q_ST, K_ST, I_C = range(5)
    pair = 2 * CHUNK
    n_pairs = SEG_REC // pair
    rin = lax.broadcasted_iota(jnp.int32, (CHUNK, LANES), 0)
    ti = lax.broadcasted_iota(jnp.int32, (CHUNK, CHUNK), 0)
    si = lax.broadcasted_iota(jnp.int32, (CHUNK, CHUNK), 1)
    half = (slice(0, CHUNK), slice(CHUNK, pair))

    def prepare(d, hh, start, q_ref, i_ref, z_ref):
        cs = slice(hh * LANES, (hh + 1) * LANES)
        lb = lb_ref[:, cs]
        one_m_lb = 1.0 - lb
        q_in, k_in, k_st, q_dc, dec = [], [], [], [], []
        for t in range(2):
            rows = _rows(start + t * CHUNK, CHUNK)
            q = q_ref[rows, cs].astype(F32)
            gate = one_m_lb * _sigmoid(z_ref[rows, cs])
            k = one_m_lb - gate
            logf = jnp.log2(lb + gate)
            b = logf
            for s in (1, 2, 4, 8, 16):
                b = b + jnp.where(rin >= s, pltpu.roll(b, s, 0), 0.0)
            tot = b[CHUNK - 1:CHUNK, :]
            if d == 0:
                c = b
                cref = b[CHUNK // 2 - 1:CHUNK // 2, :]
            else:
                c = tot - b + logf
                cref = c[CHUNK // 2:CHUNK // 2 + 1, :]
            q_in.append(q * jnp.exp2(c - cref))
            k_in.append(k * jnp.exp2(cref - c))
            k_st.append(k_in[t] * jnp.exp2(tot - cref))
            q_dc.append(q_in[t] * jnp.exp2(cref))
            dec.append(jnp.exp2(tot))
            ops_ref[d, hh, Q_IN, half[t]] = q_in[t].astype(BF16)
            ops_ref[d, hh, K_IN, half[t]] = k_in[t].astype(BF16)
            ops_ref[d, hh, I_C, half[t]] = i_ref[rows, cs]
        ca, cb = (0, 1) if d == 0 else (1, 0)
        ops_ref[d, hh, Q_ST, half[ca]] = q_dc[ca].astype(BF16)
        ops_ref[d, hh, Q_ST, half[cb]] = (q_dc[cb] * dec[ca]).astype(BF16)
        ops_ref[d, hh, K_ST, half[ca]] = (k_st[ca] * dec[cb]).astype(BF16)
        ops_ref[d, hh, K_ST, half[cb]] = k_st[cb].astype(BF16)
        cross_ref[d, hh, 0] = q_dc[cb].astype(BF16)
        cross_ref[d, hh, 1] = k_st[ca].astype(BF16)
        dec_ref[d, hh] = jnp.broadcast_to(dec[0] * dec[1], (8, LANES))

    def intra(d, hh):
        for t in range(2):
            a_ref[d, hh, t] = _dot_nt(ops_ref[d, hh, Q_IN, half[t]], ops_ref[d, hh, K_IN, half[t]])
        a_ref[d, hh, 2] = _dot_nt(cross_ref[d, hh, 0], cross_ref[d, hh, 1])

    def output(d, hh, start, o_ref):
        cs = slice(hh * LANES, (hh + 1) * LANES)
        tri = (ti >= si) if d == 0 else (ti <= si)
        a_top = jnp.where(tri, a_ref[d, hh, 0], 0.0).astype(BF16)
        a_bot = jnp.where(tri, a_ref[d, hh, 1], 0.0).astype(BF16)
        a_x = a_ref[d, hh, 2].astype(BF16)
        i_top, i_bot = ops_ref[d, hh, I_C, half[0]], ops_ref[d, hh, I_C, half[1]]
        inter = _dot_nt(ops_ref[d, hh, Q_ST], stb_ref[d, hh])
        if d == 0:
            both = jnp.dot(jnp.concatenate([a_top, a_x], axis=0), i_top, preferred_element_type=F32)
            o_top = both[:CHUNK] + inter[:CHUNK]
            o_bot = both[CHUNK:] + jnp.dot(a_bot, i_bot, preferred_element_type=F32) + inter[CHUNK:]
        else:
            both = jnp.dot(jnp.concatenate([a_x, a_bot], axis=0), i_bot, preferred_element_type=F32)
            o_top = both[:CHUNK] + jnp.dot(a_top, i_top, preferred_element_type=F32) + inter[:CHUNK]
            o_bot = both[CHUNK:] + inter[CHUNK:]
        o_ref[_rows(start, CHUNK), cs] = o_top.astype(o_ref.dtype)
        o_ref[_rows(start + CHUNK, CHUNK), cs] = o_bot.astype(o_ref.dtype)
        ds_ref[d, hh] = _dot_tn(ops_ref[d, hh, I_C], ops_ref[d, hh, K_ST])

    def update(d, hh):
        dec = dec_ref[d, hh, 0:1, :]
        for r in range(REC_VAL_DIM // CHUNK):
            rr = slice(r * CHUNK, (r + 1) * CHUNK)
            new = dec * st_ref[d, hh, rr, :] + ds_ref[d, hh, rr, :]
            st_ref[d, hh, rr, :] = new
            stb_ref[d, hh, rr, :] = new.astype(BF16)

    def scan(n, carry):
        start_f = pl.multiple_of(n * pair, pair)
        start_b = pl.multiple_of((n_pairs - 1 - n) * pair, pair)
        for hh in range(HB_REC):
            prepare(0, hh, start_f, qf_ref, if_ref, zf_ref)
            prepare(1, hh, start_b, qb_ref, ib_ref, zb_ref)
        for hh in range(HB_REC):
            intra(0, hh)
            intra(1, hh)
        for hh in range(HB_REC):
            output(0, hh, start_f, of_ref)
            output(1, hh, start_b, ob_ref)
        for hh in range(HB_REC):
            update(0, hh)
            update(1, hh)
        return carry

    lax.fori_loop(0, n_pairs, scan, 0, unroll=2)


def _hgrn2(slab, gates, lb_row, seq_groups):
    T = slab.shape[0]
    w = HB_REC * LANES

    def mirror(g):
        first, local, per = _segment_position(g, seq_groups)
        return first + per - 1 - local

    def fspec(off):
        return pl.BlockSpec((SEG_REC, w), lambda hb, g: (g, off // w + hb))

    def bspec(off):
        return pl.BlockSpec((SEG_REC, w), lambda hb, g: (mirror(g), off // w + hb))

    state = (2, HB_REC, REC_VAL_DIM, REC_KEY_DIM)
    nbytes = (2 * (4 * SEG_REC * w * 2 + 2 * SEG_REC * w * 4 + 2 * SEG_REC * w * 2)
              + 2 * HB_REC * (REC_VAL_DIM * REC_KEY_DIM * 10 + 12 * CHUNK * LANES * 2 + 8 * LANES * 4))
    out = jax.ShapeDtypeStruct((T, D_REC_V), BF16)
    return pl.pallas_call(
        functools.partial(_hgrn_kernel, seq_groups),
        grid=(N_REC_HEADS // HB_REC, T // SEG_REC),
        in_specs=[
            pl.BlockSpec((1, w), lambda hb, g: (0, hb)),
            fspec(A_QR), fspec(A_IR), fspec(0),
            bspec(A_QR), bspec(A_IR), bspec(D_REC_K),
        ],
        out_specs=[pl.BlockSpec((SEG_REC, w), lambda hb, g: (g, hb)),
                   pl.BlockSpec((SEG_REC, w), lambda hb, g: (mirror(g), hb))],
        out_shape=[out, out],
        scratch_shapes=[pltpu.VMEM(state, F32), pltpu.VMEM(state, BF16), pltpu.VMEM(state, F32),
                        pltpu.VMEM((2, HB_REC, 5, 2 * CHUNK, LANES), BF16),
                        pltpu.VMEM((2, HB_REC, 2, CHUNK, LANES), BF16),
                        pltpu.VMEM((2, HB_REC, 8, LANES), F32),
                        pltpu.VMEM((2, HB_REC, 3, CHUNK, CHUNK), F32)],
        compiler_params=pltpu.CompilerParams(
            dimension_semantics=("parallel", "arbitrary"),
            vmem_limit_bytes=_vmem_limit(nbytes)),
        name="hgrn2",
    )(lb_row, slab, slab, gates, slab, slab, gates)


def _outproj_kernel(n_first, a_ref, of_ref, ob_ref, grl_ref, grh_ref, gtl_ref, gth_ref, xa_ref, xb_ref,
                    w_ref, rg_ref, g_ref, h_ref, m_ref):
    half = D_REC_V // 2
    heads_per_chunk = KC_OUT // REC_VAL_DIM
    y = None
    for c in range(D_MODEL // KC_OUT):
        for hh in range(c * heads_per_chunk, (c + 1) * heads_per_chunk):
            cs = slice(hh * REC_VAL_DIM, (hh + 1) * REC_VAL_DIM)
            lo = hh * REC_VAL_DIM < half
            hs = cs if lo else slice(hh * REC_VAL_DIM - half, (hh + 1) * REC_VAL_DIM - half)
            rec = of_ref[:, cs].astype(F32) + ob_ref[:, cs].astype(F32)
            ms = jnp.mean(rec * rec, axis=-1, keepdims=True)
            r = (rec * lax.rsqrt(ms + RMS_EPS)) * rg_ref[:, cs]
            gr = (grl_ref if lo else grh_ref)[:, hs].astype(F32)
            gt = (gtl_ref if lo else gth_ref)[:, hs].astype(F32)
            r = _sigmoid(gt) * (r * (gr * _sigmoid(gr)))
            m_ref[:, cs] = (a_ref[:, cs].astype(F32) + r).astype(BF16)
        ks = slice(c * KC_OUT, (c + 1) * KC_OUT)
        part = jnp.dot(m_ref[:, ks], w_ref[ks, :], preferred_element_type=F32)
        y = part if y is None else y + part
    ms = jnp.mean(y * y, axis=-1, keepdims=True)
    h_ref[...] = (y * lax.rsqrt(ms + RMS_EPS)) * g_ref[...]

    @pl.when(pl.program_id(0) < n_first)
    def _():
        h_ref[...] += xa_ref[...]

    @pl.when(pl.program_id(0) >= n_first)
    def _():
        h_ref[...] += xb_ref[...]


def _out_projection(attn, rec_f, rec_b, slab, xa, xb, w_bf16, rec_gain, gain):
    T = attn.shape[0]
    n_first = xa.shape[0] // TM_OUT
    half = D_REC_V // 2
    assert A_GR % half == 0 and A_GT % half == 0
    tok = lambda i: (i, 0)
    nbytes = (2 * 3 * TM_OUT * D_MODEL * 2 + 2 * 4 * TM_OUT * half * 2 + 2 * 3 * TM_OUT * D_MODEL * 4
              + 2 * D_MODEL * D_MODEL * 2 + TM_OUT * D_MODEL * 2 + 2 * TM_OUT * D_MODEL * 4)
    return pl.pallas_call(
        functools.partial(_outproj_kernel, n_first),
        grid=(T // TM_OUT,),
        in_specs=[
            pl.BlockSpec((TM_OUT, D_MODEL), tok),
            pl.BlockSpec((TM_OUT, D_REC_V), tok),
            pl.BlockSpec((TM_OUT, D_REC_V), tok),
            pl.BlockSpec((TM_OUT, half), lambda i: (i, A_GR // half)),
            pl.BlockSpec((TM_OUT, half), lambda i: (i, A_GR // half + 1)),
            pl.BlockSpec((TM_OUT, half), lambda i: (i, A_GT // half)),
            pl.BlockSpec((TM_OUT, half), lambda i: (i, A_GT // half + 1)),
            *_two_source_specs(TM_OUT, n_first),
            pl.BlockSpec((D_MODEL, D_MODEL), lambda i: (0, 0)),
            pl.BlockSpec((1, D_REC_V), lambda i: (0, 0)),
            pl.BlockSpec((1, D_MODEL), lambda i: (0, 0)),
        ],
        out_specs=pl.BlockSpec((TM_OUT, D_MODEL), tok),
        out_shape=jax.ShapeDtypeStruct((T, D_MODEL), F32),
        scratch_shapes=[pltpu.VMEM((TM_OUT, D_MODEL), BF16)],
        compiler_params=pltpu.CompilerParams(
            dimension_semantics=("parallel",),
            vmem_limit_bytes=_vmem_limit(nbytes)),
        name="out_projection",
    )(attn, rec_f, rec_b, slab, slab, slab, slab, xa, xb, w_bf16, rec_gain, gain)


def _ffn_kernel(n_first, h_ref, gpre_ref, gpost_ref, wg_ref, wu_ref, wd_ref, oa_ref, ob_ref, hn_ref, acc_ref):
    i = pl.program_id(0)
    j = pl.program_id(1)

    @pl.when(j == 0)
    def _():
        def body(rows):
            h = h_ref[rows, :]
            ms = jnp.mean(h * h, axis=-1, keepdims=True)
            hn_ref[rows, :] = ((h * lax.rsqrt(ms + RMS_EPS)) * gpre_ref[...]).astype(BF16)
            acc_ref[rows, :] = jnp.zeros((ROWS_EW, D_MODEL), F32)
        _row_tiles(TM_FFN, body)

    hn = hn_ref[...]
    g = jnp.dot(hn, wg_ref[...], preferred_element_type=F32)
    u = jnp.dot(hn, wu_ref[...], preferred_element_type=F32)
    act = ((g * _sigmoid(g)) * u).astype(BF16)
    acc_ref[...] += jnp.dot(act, wd_ref[...], preferred_element_type=F32)

    def finish(o_ref):
        def body(rows):
            y = acc_ref[rows, :]
            ms = jnp.mean(y * y, axis=-1, keepdims=True)
            o_ref[rows, :] = h_ref[rows, :] + (y * lax.rsqrt(ms + RMS_EPS)) * gpost_ref[...]
        _row_tiles(TM_FFN, body)

    last = j == pl.num_programs(1) - 1

    @pl.when(jnp.logical_and(last, i < n_first))
    def _():
        finish(oa_ref)

    @pl.when(jnp.logical_and(last, i >= n_first))
    def _():
        finish(ob_ref)


def _ffn(h, n_tokens_first, gpre, gpost, wg, wu, wd):
    T = h.shape[0]
    n_first = n_tokens_first // TM_FFN
    nbytes = (2 * 3 * TM_FFN * D_MODEL * 4 + TM_FFN * D_MODEL * 2 + 2 * 3 * D_MODEL * TF_FFN * 2
              + 3 * TM_FFN * TF_FFN * 4 + 2 * TM_FFN * D_MODEL * 4)
    return pl.pallas_call(
        functools.partial(_ffn_kernel, n_first),
        grid=(T // TM_FFN, D_FF // TF_FFN),
        in_specs=[
            pl.BlockSpec((TM_FFN, D_MODEL), lambda i, j: (i, 0)),
            pl.BlockSpec((1, D_MODEL), lambda i, j: (0, 0)),
            pl.BlockSpec((1, D_MODEL), lambda i, j: (0, 0)),
            pl.BlockSpec((D_MODEL, TF_FFN), lambda i, j: (0, j)),
            pl.BlockSpec((D_MODEL, TF_FFN), lambda i, j: (0, j)),
            pl.BlockSpec((TF_FFN, D_MODEL), lambda i, j: (j, 0)),
        ],
        out_specs=list(_two_source_specs(TM_FFN, n_first)),
        out_shape=[jax.ShapeDtypeStruct((n_tokens_first, D_MODEL), F32),
                   jax.ShapeDtypeStruct((T - n_tokens_first, D_MODEL), F32)],
        scratch_shapes=[pltpu.VMEM((TM_FFN, D_MODEL), BF16), pltpu.VMEM((TM_FFN, D_MODEL), F32)],
        compiler_params=pltpu.CompilerParams(
            dimension_semantics=("arbitrary", "arbitrary"),
            vmem_limit_bytes=_vmem_limit(nbytes)),
        name="swiglu_ffn",
    )(h, gpre, gpost, wg, wu, wd)


def _rope_tables(max_len):
    pos = jnp.arange(max_len, dtype=F32)
    inv_freq = ROPE_THETA ** (-jnp.arange(ROPE_HALF, dtype=F32) / ROPE_HALF)
    ang = pos[:, None] * inv_freq[None, :]
    cos, sin = jnp.cos(ang), jnp.sin(ang)
    rest = HEAD_DIM - ROPE_DIM
    cos_t = jnp.concatenate([cos, cos, jnp.ones((max_len, rest), F32)], axis=1)
    sa_t = jnp.concatenate([-sin, jnp.zeros((max_len, HEAD_DIM - ROPE_HALF), F32)], axis=1)
    sb_t = jnp.concatenate([jnp.zeros((max_len, ROPE_HALF), F32), sin, jnp.zeros((max_len, rest), F32)], axis=1)
    return cos_t, sa_t, sb_t


def _seq_groups(seq_shapes, seg):
    groups, first = [], 0
    for B, L in seq_shapes:
        assert L % seg == 0
        groups.append((first, L // seg))
        first += B * L // seg
    return tuple(groups)


def _encoder_layer(xa, xb, seq_shapes, tables, w_in, sink, rec_norm, lb, w_out, norm_mix_pre, norm_mix_post,
                   norm_ffn_pre, norm_ffn_post, w_gate, w_up, w_down):
    row = lambda v: v.astype(F32).reshape(1, -1)
    xn = _stream_norm(xa, xb, row(norm_mix_pre))

    assert all(off % TN_IN == 0 for off in (OFF_K, OFF_QR, OFF_ZF, OFF_IR)) and 2 * D_KV == TN_IN
    gate_tiles = list(range(OFF_ZF // TN_IN, OFF_IR // TN_IN))
    slab_tiles = [t for t in range(D_IN // TN_IN) if t not in gate_tiles]
    kind_of = lambda t: ("rope" if t < OFF_K // TN_IN else "rope_k" if t == OFF_K // TN_IN
                         else "silu" if OFF_QR // TN_IN <= t < OFF_ZF // TN_IN else "plain")
    table_groups = _seq_groups(seq_shapes, TM_IN)
    slab = _in_projection(xn, w_in, tables, table_groups, slab_tiles, tuple(kind_of(t) for t in slab_tiles),
                          BF16, "in_projection")
    gates = _in_projection(xn, w_in, None, None, gate_tiles, ("plain",) * len(gate_tiles),
                           F32, "gate_projection")

    attn = _window_attention(slab, sink.astype(F32), _seq_groups(seq_shapes, TQ_ATT))
    rec_f, rec_b = _hgrn2(slab, gates, row(lb), _seq_groups(seq_shapes, SEG_REC))
    h = _out_projection(attn, rec_f, rec_b, slab, xa, xb, w_out.astype(BF16), row(rec_norm),
                        row(norm_mix_post))
    return _ffn(h, xa.shape[0], row(norm_ffn_pre), row(norm_ffn_post),
                w_gate.astype(BF16), w_up.astype(BF16), w_down.astype(BF16))


def kernel(x_prompt, x_sample, w_in, sink, rec_norm, lb_logits, w_out, norm_mix_pre, norm_mix_post,
           norm_ffn_pre, norm_ffn_post, w_gate, w_up, w_down):
    lb_all = jnp.cumsum(jax.nn.softmax(lb_logits.astype(F32), axis=0), axis=0)
    seq_shapes = (x_prompt.shape[:2], x_sample.shape[:2])
    xa = x_prompt.reshape(-1, D_MODEL)
    xb = x_sample.reshape(-1, D_MODEL)
    tables = _rope_tables(max(L for _, L in seq_shapes))
    for l in range(DEPTH):
        xa, xb = _encoder_layer(xa, xb, seq_shapes, tables, w_in[l], sink[l], rec_norm[l], lb_all[l], w_out[l],
                                norm_mix_pre[l], norm_mix_post[l], norm_ffn_pre[l], norm_ffn_post[l],
                                w_gate[l], w_up[l], w_down[l])
    return (xa.reshape(x_prompt.shape), xb.reshape(x_sample.shape))
```

```python
import functools

import jax
import jax.numpy as jnp
import numpy as np
from jax import lax
from jax.experimental import pallas as pl
from jax.experimental.pallas import tpu as pltpu

F32 = jnp.float32
BF16 = jnp.bfloat16

D_MODEL = 2048
DEPTH = 1
HEAD_DIM = 128
N_Q_HEADS = 16
N_KV_HEADS = 4
GROUP = N_Q_HEADS // N_KV_HEADS
WINDOW = 128
BLOCK = 128
ROPE_DIM = HEAD_DIM // 4
ROPE_HALF = ROPE_DIM // 2
ROPE_THETA = 500000.0
N_REC_HEADS = 16
REC_KEY_DIM = 128
REC_VAL_DIM = 128
CHUNK = 32
D_FF = -(-8 * D_MODEL // (3 * 256)) * 256
RMS_EPS = 1e-6
LOG2E = 1.4426950408889634
Q_SCALE = HEAD_DIM ** -0.5 * LOG2E

D_ATT = N_Q_HEADS * HEAD_DIM
D_KV = N_KV_HEADS * HEAD_DIM
D_REC_K = N_REC_HEADS * REC_KEY_DIM
D_REC_V = N_REC_HEADS * REC_VAL_DIM
SPLIT_SIZES = (D_ATT, D_KV, D_KV, D_REC_K, D_REC_K, D_REC_K, D_REC_V, D_REC_V, D_MODEL, D_MODEL)
D_IN = sum(SPLIT_SIZES)
(OFF_Q, OFF_K, OFF_V, OFF_QR, OFF_ZF, OFF_ZB, OFF_IR, OFF_GR, OFF_GA, OFF_GT) = (
    int(v) for v in np.concatenate([[0], np.cumsum(SPLIT_SIZES)[:-1]]))
D_GATES = OFF_IR - OFF_ZF
D_SLAB = D_IN - D_GATES
A_Q, A_K, A_V, A_QR = OFF_Q, OFF_K, OFF_V, OFF_QR
A_IR, A_GR, A_GA, A_GT = (OFF_IR - D_GATES, OFF_GR - D_GATES, OFF_GA - D_GATES, OFF_GT - D_GATES)

V7X_VMEM_CEILING = 56 * 1024 * 1024
LANES = 128

TM_NORM = 512
TM_IN, TN_IN = 1024, 1024
SUB_IN = 256
ROWS_EW = 256
SLAB_ATT = 32
TQ_ATT = 1024
SEG_REC = 512
HB_REC = 8
TM_OUT = 256
KC_OUT = 512
TM_FFN, TF_FFN = 512, 512


def _vmem_limit(nbytes):
    return int(min(V7X_VMEM_CEILING, nbytes * 1.25 + (4 << 20)))


def _sigmoid(x):
    return 1.0 / (1.0 + jnp.exp(-x))


def _dot_nt(a, b):
    return lax.dot_general(a, b, (((1,), (1,)), ((), ())), preferred_element_type=F32)


def _dot_tn(a, b):
    return lax.dot_general(a, b, (((0,), (0,)), ((), ())), preferred_element_type=F32)


def _rows(start, size):
    return pl.ds(start if isinstance(start, int) else pl.multiple_of(start, size), size)


def _row_tiles(n_rows, body):
    def step(r, c):
        body(_rows(r * ROWS_EW, ROWS_EW))
        return c
    lax.fori_loop(0, n_rows // ROWS_EW, step, 0)


def _segment_position(g, seq_groups):
    first, local, per = None, None, None
    for g0, n in reversed(seq_groups):
        loc = lax.rem(g - g0, n)
        fst = g - loc
        if first is None:
            first, local, per = fst, loc, n
        else:
            here = g < nxt
            first = jnp.where(here, fst, first)
            local = jnp.where(here, loc, local)
            per = jnp.where(here, n, per)
        nxt = g0
    return first, local, per


def _two_source_specs(tm, n_first):
    return (pl.BlockSpec((tm, D_MODEL), lambda i, *_: (jnp.minimum(i, n_first - 1), 0)),
            pl.BlockSpec((tm, D_MODEL), lambda i, *_: (jnp.maximum(i - n_first, 0), 0)))


def _norm_kernel(n_first, xa_ref, xb_ref, g_ref, o_ref):
    def run(x_ref):
        def body(rows):
            x = x_ref[rows, :]
            ms = jnp.mean(x * x, axis=-1, keepdims=True)
            o_ref[rows, :] = ((x * lax.rsqrt(ms + RMS_EPS)) * g_ref[...]).astype(o_ref.dtype)
        _row_tiles(TM_NORM, body)

    @pl.when(pl.program_id(0) < n_first)
    def _():
        run(xa_ref)

    @pl.when(pl.program_id(0) >= n_first)
    def _():
        run(xb_ref)


def _stream_norm(xa, xb, gain):
    T = xa.shape[0] + xb.shape[0]
    n_first = xa.shape[0] // TM_NORM
    nbytes = 2 * 2 * TM_NORM * D_MODEL * 4 + 2 * TM_NORM * D_MODEL * 2
    return pl.pallas_call(
        functools.partial(_norm_kernel, n_first),
        grid=(T // TM_NORM,),
        in_specs=[*_two_source_specs(TM_NORM, n_first), pl.BlockSpec((1, D_MODEL), lambda i: (0, 0))],
        out_specs=pl.BlockSpec((TM_NORM, D_MODEL), lambda i: (i, 0)),
        out_shape=jax.ShapeDtypeStruct((T, D_MODEL), BF16),
        compiler_params=pltpu.CompilerParams(
            dimension_semantics=("parallel",), vmem_limit_bytes=_vmem_limit(nbytes)),
        name="stream_norm",
    )(xa, xb, gain)


def _inproj_kernel(kinds, col_tiles_ref, *refs):
    del col_tiles_ref
    with_epilogue = any(k != "plain" for k in kinds)
    if with_epilogue:
        xn_ref, cos_ref, sa_ref, sb_ref, w_ref, o_ref, wb_ref, y_ref = refs
    else:
        xn_ref, w_ref, o_ref, wb_ref = refs
    j = pl.program_id(0)

    @pl.when(pl.program_id(1) == 0)
    def _():
        def body(rows):
            wb_ref[rows, :] = w_ref[rows, :].astype(BF16)
        _row_tiles(D_MODEL, body)

    def transform(kind, y, rows, lane_tile):
        if kind == "silu":
            return y * _sigmoid(y)
        if kind == "rope_k" and lane_tile >= N_KV_HEADS:
            return y
        y = (y * cos_ref[rows, :] + pltpu.roll(y, HEAD_DIM - ROPE_HALF, 1) * sa_ref[rows, :]
             + pltpu.roll(y, ROPE_HALF, 1) * sb_ref[rows, :])
        return y * Q_SCALE if kind == "rope" else y

    def epilogue(kind, sub):
        for r in range(TM_IN // ROWS_EW):
            rows = slice(r * ROWS_EW, (r + 1) * ROWS_EW)
            for cc in range(SUB_IN // LANES):
                lane_tile = sub * (SUB_IN // LANES) + cc
                y = y_ref[sub, rows, cc * LANES:(cc + 1) * LANES]
                o_ref[rows, lane_tile * LANES:(lane_tile + 1) * LANES] = (
                    transform(kind, y, rows, lane_tile).astype(o_ref.dtype))

    for kind in sorted(set(kinds)):
        tiles = [t for t, k in enumerate(kinds) if k == kind]
        cond = functools.reduce(jnp.logical_or, [j == t for t in tiles])

        @pl.when(cond)
        def _(kind=kind):
            if kind == "plain":
                o_ref[...] = jnp.dot(xn_ref[...], wb_ref[...], preferred_element_type=F32).astype(o_ref.dtype)
            else:
                n_sub = TN_IN // SUB_IN
                for sub in range(n_sub):
                    y_ref[sub] = jnp.dot(xn_ref[...], wb_ref[:, sub * SUB_IN:(sub + 1) * SUB_IN],
                                         preferred_element_type=F32)
                    if sub > 0:
                        epilogue(kind, sub - 1)
                epilogue(kind, n_sub - 1)


def _in_projection(xn, w_in, tables, table_groups, col_tiles, kinds, out_dtype, name):
    T = xn.shape[0]
    n_col = len(col_tiles)
    col_tiles = jnp.asarray(col_tiles, jnp.int32)
    with_epilogue = any(k != "plain" for k in kinds)
    last_table_tile = max([t for t, k in enumerate(kinds) if k.startswith("rope")], default=-1)
    out_bytes = jnp.dtype(out_dtype).itemsize

    def table_map(j, i, ct):
        _, local, _ = _segment_position(i, table_groups)
        return (jnp.where(j <= last_table_tile, local, 0), 0)

    in_specs = [pl.BlockSpec((TM_IN, D_MODEL), lambda j, i, ct: (i, 0))]
    args = [xn]
    scratch = [pltpu.VMEM((D_MODEL, TN_IN), BF16)]
    nbytes = (2 * TM_IN * D_MODEL * 2 + 2 * D_MODEL * TN_IN * 4 + D_MODEL * TN_IN * 2
              + 2 * TM_IN * TN_IN * out_bytes + TM_IN * TN_IN * 4)
    if with_epilogue:
        in_specs += [pl.BlockSpec((TM_IN, LANES), table_map)] * 3
        args += list(tables)
        scratch.append(pltpu.VMEM((TN_IN // SUB_IN, TM_IN, SUB_IN), F32))
        nbytes += 2 * 3 * TM_IN * LANES * 4 + TM_IN * TN_IN * 4
    in_specs.append(pl.BlockSpec((D_MODEL, TN_IN), lambda j, i, ct: (0, ct[j])))
    args.append(w_in)
    return pl.pallas_call(
        functools.partial(_inproj_kernel, kinds),
        grid_spec=pltpu.PrefetchScalarGridSpec(
            num_scalar_prefetch=1,
            grid=(n_col, T // TM_IN),
            in_specs=in_specs,
            out_specs=pl.BlockSpec((TM_IN, TN_IN), lambda j, i, ct: (i, j)),
            scratch_shapes=scratch),
        out_shape=jax.ShapeDtypeStruct((T, n_col * TN_IN), out_dtype),
        compiler_params=pltpu.CompilerParams(
            dimension_semantics=("arbitrary", "arbitrary"),
            vmem_limit_bytes=_vmem_limit(nbytes)),
        name=name,
    )(col_tiles, *args)


def _attn_kernel(seq_groups, sink_ref, q_ref, kc_ref, vc_ref, kp_ref, kn_ref, vp_ref, vn_ref,
                 ga_ref, o_ref, s_all, p_all, r_all):
    t = pl.program_id(0)
    h = pl.program_id(1)
    _, local, per = _segment_position(t, seq_groups)
    has_prev = local != 0
    has_next = local != per - 1

    k_all = jnp.concatenate([kp_ref[...], kc_ref[...], kn_ref[...]], axis=0)
    v_all = jnp.concatenate([vp_ref[...], vc_ref[...], vn_ref[...]], axis=0)
    qi = lax.broadcasted_iota(jnp.int32, (BLOCK, BLOCK), 0)
    kj = lax.broadcasted_iota(jnp.int32, (BLOCK, BLOCK), 1)
    neg = jnp.float32(-jnp.inf)
    bias_prev = jnp.where(kj >= qi, 0.0, neg)
    bias_next = jnp.where(kj <= qi, 0.0, neg)
    bias_prev0 = jnp.where(has_prev, bias_prev, neg)
    bias_next_last = jnp.where(has_next, bias_next, neg)
    n_blocks = TQ_ATT // BLOCK
    for b in range(n_blocks):
        rs = slice(b * BLOCK, (b + 1) * BLOCK)
        kb = k_all[b * BLOCK:(b + 3) * BLOCK]
        vb = v_all[b * BLOCK:(b + 3) * BLOCK]
        bp = bias_prev0 if b == 0 else bias_prev
        bn = bias_next_last if b == n_blocks - 1 else bias_next
        heads = [slice(g * HEAD_DIM, (g + 1) * HEAD_DIM) for g in range(GROUP)]
        s_ref, p_ref, r_ref = s_all.at[b], p_all.at[b], r_all.at[b]
        q4 = jnp.concatenate([q_ref[rs, cs] for cs in heads], axis=0)
        s_ref[...] = _dot_nt(q4, kb).reshape(GROUP, BLOCK, 3 * BLOCK)
        for g, cs in enumerate(heads):
            sink2 = sink_ref[GROUP * h + g] * LOG2E
            for r in range(BLOCK // SLAB_ATT):
                rr = slice(r * SLAB_ATT, (r + 1) * SLAB_ATT)
                s = s_ref[g, rr, :]
                s_p = s[:, :BLOCK] + bp[rr]
                s_c = s[:, BLOCK:2 * BLOCK]
                s_n = s[:, 2 * BLOCK:] + bn[rr]
                m = jnp.maximum(jnp.max(jnp.maximum(jnp.maximum(s_p, s_c), s_n), axis=-1, keepdims=True), sink2)
                p_p, p_c, p_n = jnp.exp2(s_p - m), jnp.exp2(s_c - m), jnp.exp2(s_n - m)
                denom = jnp.sum(p_p + p_c + p_n, axis=-1, keepdims=True) + jnp.exp2(sink2 - m)
                p_ref[g, rr, :] = jnp.concatenate([p_p, p_c, p_n], axis=1).astype(BF16)
                r_ref[g, rr, :] = jnp.broadcast_to(1.0 / denom, (SLAB_ATT, LANES))
        o4 = jnp.dot(p_ref[...].reshape(GROUP * BLOCK, 3 * BLOCK), vb, preferred_element_type=F32)
        for g, cs in enumerate(heads):
            o = o4[g * BLOCK:(g + 1) * BLOCK] * r_ref[g]
            o_ref[rs, cs] = (_sigmoid(ga_ref[rs, cs].astype(F32)) * o).astype(o_ref.dtype)


def _window_attention(slab, sink, seq_groups):
    T = slab.shape[0]
    n_tiles = T // TQ_ATT
    bpt = TQ_ATT // BLOCK
    last_blk = T // BLOCK - 1
    gw = GROUP * HEAD_DIM
    kcol, vcol, gcol = A_K // HEAD_DIM, A_V // HEAD_DIM, A_GA // gw
    nbytes = (2 * (3 * TQ_ATT * gw * 2 + 2 * TQ_ATT * HEAD_DIM * 2 + 4 * BLOCK * HEAD_DIM * 2)
              + bpt * GROUP * BLOCK * (3 * BLOCK * 6 + LANES * 4))
    return pl.pallas_call(
        functools.partial(_attn_kernel, seq_groups),
        grid=(n_tiles, N_KV_HEADS),
        in_specs=[
            pl.BlockSpec(memory_space=pltpu.SMEM),
            pl.BlockSpec((TQ_ATT, gw), lambda t, h: (t, h)),
            pl.BlockSpec((TQ_ATT, HEAD_DIM), lambda t, h: (t, kcol + h)),
            pl.BlockSpec((TQ_ATT, HEAD_DIM), lambda t, h: (t, vcol + h)),
            pl.BlockSpec((BLOCK, HEAD_DIM), lambda t, h: (jnp.maximum(t * bpt - 1, 0), kcol + h)),
            pl.BlockSpec((BLOCK, HEAD_DIM), lambda t, h: (jnp.minimum((t + 1) * bpt, last_blk), kcol + h)),
            pl.BlockSpec((BLOCK, HEAD_DIM), lambda t, h: (jnp.maximum(t * bpt - 1, 0), vcol + h)),
            pl.BlockSpec((BLOCK, HEAD_DIM), lambda t, h: (jnp.minimum((t + 1) * bpt, last_blk), vcol + h)),
            pl.BlockSpec((TQ_ATT, gw), lambda t, h: (t, gcol + h)),
        ],
        out_specs=pl.BlockSpec((TQ_ATT, gw), lambda t, h: (t, h)),
        out_shape=jax.ShapeDtypeStruct((T, D_ATT), BF16),
        scratch_shapes=[pltpu.VMEM((bpt, GROUP, BLOCK, 3 * BLOCK), F32),
                        pltpu.VMEM((bpt, GROUP, BLOCK, 3 * BLOCK), BF16),
                        pltpu.VMEM((bpt, GROUP, BLOCK, LANES), F32)],
        compiler_params=pltpu.CompilerParams(
            dimension_semantics=("parallel", "arbitrary"),
            vmem_limit_bytes=_vmem_limit(nbytes)),
        name="window_attention",
    )(sink, slab, slab, slab, slab, slab, slab, slab, slab)


def _hgrn_kernel(seq_groups, lb_ref, qf_ref, if_ref, zf_ref, qb_ref, ib_ref, zb_ref,
                 of_ref, ob_ref, st_ref, stb_ref, ds_ref, ops_ref, cross_ref, dec_ref, a_ref):
    g = pl.program_id(1)
    _, local, _ = _segment_position(g, seq_groups)

    @pl.when(local == 0)
    def _():
        st_ref[...] = jnp.zeros(st_ref.shape, F32)
        stb_ref[...] = jnp.zeros(stb_ref.shape, BF16)

    Q_IN, K_IN, Q_ST, K_ST, I_C = range(5)
    pair = 2 * CHUNK
    n_pairs = SEG_REC // pair
    rin = lax.broadcasted_iota(jnp.int32, (CHUNK, LANES), 0)
    ti = lax.broadcasted_iota(jnp.int32, (CHUNK, CHUNK), 0)
    si = lax.broadcasted_iota(jnp.int32, (CHUNK, CHUNK), 1)
    half = (slice(0, CHUNK), slice(CHUNK, pair))

    def prepare(d, hh, start, q_ref, i_ref, z_ref):
        cs = slice(hh * LANES, (hh + 1) * LANES)
        lb = lb_ref[:, cs]
        one_m_lb = 1.0 - lb
        q_in, k_in, k_st, q_dc, dec = [], [], [], [], []
        for t in range(2):
            rows = _rows(start + t * CHUNK, CHUNK)
            q = q_ref[rows, cs].astype(F32)
            gate = one_m_lb * _sigmoid(z_ref[rows, cs])
            k = one_m_lb - gate
            logf = jnp.log2(lb + gate)
            b = logf
            for s in (1, 2, 4, 8, 16):
                b = b + jnp.where(rin >= s, pltpu.roll(b, s, 0), 0.0)
            tot = b[CHUNK - 1:CHUNK, :]
            if d == 0:
                c = b
                cref = b[CHUNK // 2 - 1:CHUNK // 2, :]
            else:
                c = tot - b + logf
                cref = c[CHUNK // 2:CHUNK // 2 + 1, :]
            q_in.append(q * jnp.exp2(c - cref))
            k_in.append(k * jnp.exp2(cref - c))
            k_st.append(k_in[t] * jnp.exp2(tot - cref))
            q_dc.append(q_in[t] * jnp.exp2(cref))
            dec.append(jnp.exp2(tot))
            ops_ref[d, hh, Q_IN, half[t]] = q_in[t].astype(BF16)
            ops_ref[d, hh, K_IN, half[t]] = k_in[t].astype(BF16)
            ops_ref[d, hh, I_C, half[t]] = i_ref[rows, cs]
        ca, cb = (0, 1) if d == 0 else (1, 0)
        ops_ref[d, hh, Q_ST, half[ca]] = q_dc[ca].astype(BF16)
        ops_ref[d, hh, Q_ST, half[cb]] = (q_dc[cb] * dec[ca]).astype(BF16)
        ops_ref[d, hh, K_ST, half[ca]] = (k_st[ca] * dec[cb]).astype(BF16)
        ops_ref[d, hh, K_ST, half[cb]] = k_st[cb].astype(BF16)
        cross_ref[d, hh, 0] = q_dc[cb].astype(BF16)
        cross_ref[d, hh, 1] = k_st[ca].astype(BF16)
        dec_ref[d, hh] = jnp.broadcast_to(dec[0] * dec[1], (8, LANES))

    def intra(d, hh):
        for t in range(2):
            a_ref[d, hh, t] = _dot_nt(ops_ref[d, hh, Q_IN, half[t]], ops_ref[d, hh, K_IN, half[t]])
        a_ref[d, hh, 2] = _dot_nt(cross_ref[d, hh, 0], cross_ref[d, hh, 1])

    def output(d, hh, start, o_ref):
        cs = slice(hh * LANES, (hh + 1) * LANES)
        tri = (ti >= si) if d == 0 else (ti <= si)
        a_top = jnp.where(tri, a_ref[d, hh, 0], 0.0).astype(BF16)
        a_bot = jnp.where(tri, a_ref[d, hh, 1], 0.0).astype(BF16)
        a_x = a_ref[d, hh, 2].astype(BF16)
        i_top, i_bot = ops_ref[d, hh, I_C, half[0]], ops_ref[d, hh, I_C, half[1]]
        inter = _dot_nt(ops_ref[d, hh, Q_ST], stb_ref[d, hh])
        if d == 0:
            both = jnp.dot(jnp.concatenate([a_top, a_x], axis=0), i_top, preferred_element_type=F32)
            o_top = both[:CHUNK] + inter[:CHUNK]
            o_bot = both[CHUNK:] + jnp.dot(a_bot, i_bot, preferred_element_type=F32) + inter[CHUNK:]
        else:
            both = jnp.dot(jnp.concatenate([a_x, a_bot], axis=0), i_bot, preferred_element_type=F32)
            o_top = both[:CHUNK] + jnp.dot(a_top, i_top, preferred_element_type=F32) + inter[:CHUNK]
            o_bot = both[CHUNK:] + inter[CHUNK:]
        o_ref[_rows(start, CHUNK), cs] = o_top.astype(o_ref.dtype)
        o_ref[_rows(start + CHUNK, CHUNK), cs] = o_bot.astype(o_ref.dtype)
        ds_ref[d, hh] = _dot_tn(ops_ref[d, hh, I_C], ops_ref[d, hh, K_ST])

    def update(d, hh):
        dec = dec_ref[d, hh, 0:1, :]
        for r in range(REC_VAL_DIM // CHUNK):
            rr = slice(r * CHUNK, (r + 1) * CHUNK)
            new = dec * st_ref[d, hh, rr, :] + ds_ref[d, hh, rr, :]
            st_ref[d, hh, rr, :] = new
            stb_ref[d, hh, rr, :] = new.astype(BF16)

    def scan(n, carry):
        start_f = pl.multiple_of(n * pair, pair)
        start_b = pl.multiple_of((n_pairs - 1 - n) * pair, pair)
        for hh in range(HB_REC):
            prepare(0, hh, start_f, qf_ref, if_ref, zf_ref)
            prepare(1, hh, start_b, qb_ref, ib_ref, zb_ref)
        for hh in range(HB_REC):
            intra(0, hh)
            intra(1, hh)
        for hh in range(HB_REC):
            output(0, hh, start_f, of_ref)
            output(1, hh, start_b, ob_ref)
        for hh in range(HB_REC):
            update(0, hh)
            update(1, hh)
        return carry

    lax.fori_loop(0, n_pairs, scan, 0)


def _hgrn2(slab, gates, lb_row, seq_groups):
    T = slab.shape[0]
    w = HB_REC * LANES

    def mirror(g):
        first, local, per = _segment_position(g, seq_groups)
        return first + per - 1 - local

    def fspec(off):
        return pl.BlockSpec((SEG_REC, w), lambda hb, g: (g, off // w + hb))

    def bspec(off):
        return pl.BlockSpec((SEG_REC, w), lambda hb, g: (mirror(g), off // w + hb))

    state = (2, HB_REC, REC_VAL_DIM, REC_KEY_DIM)
    nbytes = (2 * (4 * SEG_REC * w * 2 + 2 * SEG_REC * w * 4 + 2 * SEG_REC * w * 2)
              + 2 * HB_REC * (REC_VAL_DIM * REC_KEY_DIM * 10 + 12 * CHUNK * LANES * 2 + 8 * LANES * 4))
    out = jax.ShapeDtypeStruct((T, D_REC_V), BF16)
    return pl.pallas_call(
        functools.partial(_hgrn_kernel, seq_groups),
        grid=(N_REC_HEADS // HB_REC, T // SEG_REC),
        in_specs=[
            pl.BlockSpec((1, w), lambda hb, g: (0, hb)),
            fspec(A_QR), fspec(A_IR), fspec(0),
            bspec(A_QR), bspec(A_IR), bspec(D_REC_K),
        ],
        out_specs=[pl.BlockSpec((SEG_REC, w), lambda hb, g: (g, hb)),
                   pl.BlockSpec((SEG_REC, w), lambda hb, g: (mirror(g), hb))],
        out_shape=[out, out],
        scratch_shapes=[pltpu.VMEM(state, F32), pltpu.VMEM(state, BF16), pltpu.VMEM(state, F32),
                        pltpu.VMEM((2, HB_REC, 5, 2 * CHUNK, LANES), BF16),
                        pltpu.VMEM((2, HB_REC, 2, CHUNK, LANES), BF16),
                        pltpu.VMEM((2, HB_REC, 8, LANES), F32),
                        pltpu.VMEM((2, HB_REC, 3, CHUNK, CHUNK), F32)],
        compiler_params=pltpu.CompilerParams(
            dimension_semantics=("parallel", "arbitrary"),
            vmem_limit_bytes=_vmem_limit(nbytes)),
        name="hgrn2",
    )(lb_row, slab, slab, gates, slab, slab, gates)


def _outproj_kernel(n_first, a_ref, of_ref, ob_ref, grl_ref, grh_ref, gtl_ref, gth_ref, xa_ref, xb_ref,
                    w_ref, rg_ref, g_ref, h_ref, m_ref):
    half = D_REC_V // 2
    heads_per_chunk = KC_OUT // REC_VAL_DIM
    y = None
    for c in range(D_MODEL // KC_OUT):
        for hh in range(c * heads_per_chunk, (c + 1) * heads_per_chunk):
            cs = slice(hh * REC_VAL_DIM, (hh + 1) * REC_VAL_DIM)
            lo = hh * REC_VAL_DIM < half
            hs = cs if lo else slice(hh * REC_VAL_DIM - half, (hh + 1) * REC_VAL_DIM - half)
            rec = of_ref[:, cs].astype(F32) + ob_ref[:, cs].astype(F32)
            ms = jnp.mean(rec * rec, axis=-1, keepdims=True)
            r = (rec * lax.rsqrt(ms + RMS_EPS)) * rg_ref[:, cs]
            gr = (grl_ref if lo else grh_ref)[:, hs].astype(F32)
            gt = (gtl_ref if lo else gth_ref)[:, hs].astype(F32)
            r = _sigmoid(gt) * (r * (gr * _sigmoid(gr)))
            m_ref[:, cs] = (a_ref[:, cs].astype(F32) + r).astype(BF16)
        ks = slice(c * KC_OUT, (c + 1) * KC_OUT)
        part = jnp.dot(m_ref[:, ks], w_ref[ks, :], preferred_element_type=F32)
        y = part if y is None else y + part
    ms = jnp.mean(y * y, axis=-1, keepdims=True)
    h_ref[...] = (y * lax.rsqrt(ms + RMS_EPS)) * g_ref[...]

    @pl.when(pl.program_id(0) < n_first)
    def _():
        h_ref[...] += xa_ref[...]

    @pl.when(pl.program_id(0) >= n_first)
    def _():
        h_ref[...] += xb_ref[...]


def _out_projection(attn, rec_f, rec_b, slab, xa, xb, w_bf16, rec_gain, gain):
    T = attn.shape[0]
    n_first = xa.shape[0] // TM_OUT
    half = D_REC_V // 2
    assert A_GR % half == 0 and A_GT % half == 0
    tok = lambda i: (i, 0)
    nbytes = (2 * 3 * TM_OUT * D_MODEL * 2 + 2 * 4 * TM_OUT * half * 2 + 2 * 3 * TM_OUT * D_MODEL * 4
              + 2 * D_MODEL * D_MODEL * 2 + TM_OUT * D_MODEL * 2 + 2 * TM_OUT * D_MODEL * 4)
    return pl.pallas_call(
        functools.partial(_outproj_kernel, n_first),
        grid=(T // TM_OUT,),
        in_specs=[
            pl.BlockSpec((TM_OUT, D_MODEL), tok),
            pl.BlockSpec((TM_OUT, D_REC_V), tok),
            pl.BlockSpec((TM_OUT, D_REC_V), tok),
            pl.BlockSpec((TM_OUT, half), lambda i: (i, A_GR // half)),
            pl.BlockSpec((TM_OUT, half), lambda i: (i, A_GR // half + 1)),
            pl.BlockSpec((TM_OUT, half), lambda i: (i, A_GT // half)),
            pl.BlockSpec((TM_OUT, half), lambda i: (i, A_GT // half + 1)),
            *_two_source_specs(TM_OUT, n_first),
            pl.BlockSpec((D_MODEL, D_MODEL), lambda i: (0, 0)),
            pl.BlockSpec((1, D_REC_V), lambda i: (0, 0)),
            pl.BlockSpec((1, D_MODEL), lambda i: (0, 0)),
        ],
        out_specs=pl.BlockSpec((TM_OUT, D_MODEL), tok),
        out_shape=jax.ShapeDtypeStruct((T, D_MODEL), F32),
        scratch_shapes=[pltpu.VMEM((TM_OUT, D_MODEL), BF16)],
        compiler_params=pltpu.CompilerParams(
            dimension_semantics=("parallel",),
            vmem_limit_bytes=_vmem_limit(nbytes)),
        name="out_projection",
    )(attn, rec_f, rec_b, slab, slab, slab, slab, xa, xb, w_bf16, rec_gain, gain)


def _ffn_kernel(n_first, h_ref, gpre_ref, gpost_ref, wg_ref, wu_ref, wd_ref, oa_ref, ob_ref, hn_ref, acc_ref):
    i = pl.program_id(0)
    j = pl.program_id(1)

    @pl.when(j == 0)
    def _():
        def body(rows):
            h = h_ref[rows, :]
            ms = jnp.mean(h * h, axis=-1, keepdims=True)
            hn_ref[rows, :] = ((h * lax.rsqrt(ms + RMS_EPS)) * gpre_ref[...]).astype(BF16)
            acc_ref[rows, :] = jnp.zeros((ROWS_EW, D_MODEL), F32)
        _row_tiles(TM_FFN, body)

    hn = hn_ref[...]
    g = jnp.dot(hn, wg_ref[...], preferred_element_type=F32)
    u = jnp.dot(hn, wu_ref[...], preferred_element_type=F32)
    act = ((g * _sigmoid(g)) * u).astype(BF16)
    acc_ref[...] += jnp.dot(act, wd_ref[...], preferred_element_type=F32)

    def finish(o_ref):
        def body(rows):
            y = acc_ref[rows, :]
            ms = jnp.mean(y * y, axis=-1, keepdims=True)
            o_ref[rows, :] = h_ref[rows, :] + (y * lax.rsqrt(ms + RMS_EPS)) * gpost_ref[...]
        _row_tiles(TM_FFN, body)

    last = j == pl.num_programs(1) - 1

    @pl.when(jnp.logical_and(last, i < n_first))
    def _():
        finish(oa_ref)

    @pl.when(jnp.logical_and(last, i >= n_first))
    def _():
        finish(ob_ref)


def _ffn(h, n_tokens_first, gpre, gpost, wg, wu, wd):
    T = h.shape[0]
    n_first = n_tokens_first // TM_FFN
    nbytes = (2 * 3 * TM_FFN * D_MODEL * 4 + TM_FFN * D_MODEL * 2 + 2 * 3 * D_MODEL * TF_FFN * 2
              + 3 * TM_FFN * TF_FFN * 4 + 2 * TM_FFN * D_MODEL * 4)
    return pl.pallas_call(
        functools.partial(_ffn_kernel, n_first),
        grid=(T // TM_FFN, D_FF // TF_FFN),
        in_specs=[
            pl.BlockSpec((TM_FFN, D_MODEL), lambda i, j: (i, 0)),
            pl.BlockSpec((1, D_MODEL), lambda i, j: (0, 0)),
            pl.BlockSpec((1, D_MODEL), lambda i, j: (0, 0)),
            pl.BlockSpec((D_MODEL, TF_FFN), lambda i, j: (0, j)),
            pl.BlockSpec((D_MODEL, TF_FFN), lambda i, j: (0, j)),
            pl.BlockSpec((TF_FFN, D_MODEL), lambda i, j: (j, 0)),
        ],
        out_specs=list(_two_source_specs(TM_FFN, n_first)),
        out_shape=[jax.ShapeDtypeStruct((n_tokens_first, D_MODEL), F32),
                   jax.ShapeDtypeStruct((T - n_tokens_first, D_MODEL), F32)],
        scratch_shapes=[pltpu.VMEM((TM_FFN, D_MODEL), BF16), pltpu.VMEM((TM_FFN, D_MODEL), F32)],
        compiler_params=pltpu.CompilerParams(
            dimension_semantics=("arbitrary", "arbitrary"),
            vmem_limit_bytes=_vmem_limit(nbytes)),
        name="swiglu_ffn",
    )(h, gpre, gpost, wg, wu, wd)


def _rope_tables(max_len):
    pos = jnp.arange(max_len, dtype=F32)
    inv_freq = ROPE_THETA ** (-jnp.arange(ROPE_HALF, dtype=F32) / ROPE_HALF)
    ang = pos[:, None] * inv_freq[None, :]
    cos, sin = jnp.cos(ang), jnp.sin(ang)
    rest = HEAD_DIM - ROPE_DIM
    cos_t = jnp.concatenate([cos, cos, jnp.ones((max_len, rest), F32)], axis=1)
    sa_t = jnp.concatenate([-sin, jnp.zeros((max_len, HEAD_DIM - ROPE_HALF), F32)], axis=1)
    sb_t = jnp.concatenate([jnp.zeros((max_len, ROPE_HALF), F32), sin, jnp.zeros((max_len, rest), F32)], axis=1)
    return cos_t, sa_t, sb_t


def _seq_groups(seq_shapes, seg):
    groups, first = [], 0
    for B, L in seq_shapes:
        assert L % seg == 0
        groups.append((first, L // seg))
        first += B * L // seg
    return tuple(groups)


def _encoder_layer(xa, xb, seq_shapes, tables, w_in, sink, rec_norm, lb, w_out, norm_mix_pre, norm_mix_post,
                   norm_ffn_pre, norm_ffn_post, w_gate, w_up, w_down):
    row = lambda v: v.astype(F32).reshape(1, -1)
    xn = _stream_norm(xa, xb, row(norm_mix_pre))

    assert all(off % TN_IN == 0 for off in (OFF_K, OFF_QR, OFF_ZF, OFF_IR)) and 2 * D_KV == TN_IN
    gate_tiles = list(range(OFF_ZF // TN_IN, OFF_IR // TN_IN))
    slab_tiles = [t for t in range(D_IN // TN_IN) if t not in gate_tiles]
    kind_of = lambda t: ("rope" if t < OFF_K // TN_IN else "rope_k" if t == OFF_K // TN_IN
                         else "silu" if OFF_QR // TN_IN <= t < OFF_ZF // TN_IN else "plain")
    table_groups = _seq_groups(seq_shapes, TM_IN)
    slab = _in_projection(xn, w_in, tables, table_groups, slab_tiles, tuple(kind_of(t) for t in slab_tiles),
                          BF16, "in_projection")
    gates = _in_projection(xn, w_in, None, None, gate_tiles, ("plain",) * len(gate_tiles),
                           F32, "gate_projection")

    attn = _window_attention(slab, sink.astype(F32), _seq_groups(seq_shapes, TQ_ATT))
    rec_f, rec_b = _hgrn2(slab, gates, row(lb), _seq_groups(seq_shapes, SEG_REC))
    h = _out_projection(attn, rec_f, rec_b, slab, xa, xb, w_out.astype(BF16), row(rec_norm),
                        row(norm_mix_post))
    return _ffn(h, xa.shape[0], row(norm_ffn_pre), row(norm_ffn_post),
                w_gate.astype(BF16), w_up.astype(BF16), w_down.astype(BF16))


def kernel(x_prompt, x_sample, w_in, sink, rec_norm, lb_logits, w_out, norm_mix_pre, norm_mix_post,
           norm_ffn_pre, norm_ffn_post, w_gate, w_up, w_down):
    lb_all = jnp.cumsum(jax.nn.softmax(lb_logits.astype(F32), axis=0), axis=0)
    seq_shapes = (x_prompt.shape[:2], x_sample.shape[:2])
    xa = x_prompt.reshape(-1, D_MODEL)
    xb = x_sample.reshape(-1, D_MODEL)
    tables = _rope_tables(max(L for _, L in seq_shapes))
    for l in range(DEPTH):
        xa, xb = _encoder_layer(xa, xb, seq_shapes, tables, w_in[l], sink[l], rec_norm[l], lb_all[l], w_out[l],
                                norm_mix_pre[l], norm_mix_post[l], norm_ffn_pre[l], norm_ffn_post[l],
                                w_gate[l], w_up[l], w_down[l])
    return (xa.reshape(x_prompt.shape), xb.reshape(x_sample.shape))
```

```python
import functools

import jax
import jax.numpy as jnp
import numpy as np
from jax import lax
from jax.experimental import pallas as pl
from jax.experimental.pallas import tpu as pltpu

F32 = jnp.float32
BF16 = jnp.bfloat16

D_MODEL = 2048
DEPTH = 1
HEAD_DIM = 128
N_Q_HEADS = 16
N_KV_HEADS = 4
GROUP = N_Q_HEADS // N_KV_HEADS
WINDOW = 128
BLOCK = 128
ROPE_DIM = HEAD_DIM // 4
ROPE_HALF = ROPE_DIM // 2
ROPE_THETA = 500000.0
N_REC_HEADS = 16
REC_KEY_DIM = 128
REC_VAL_DIM = 128
CHUNK = 32
D_FF = -(-8 * D_MODEL // (3 * 256)) * 256
RMS_EPS = 1e-6
LOG2E = 1.4426950408889634
Q_SCALE = HEAD_DIM ** -0.5 * LOG2E

D_ATT = N_Q_HEADS * HEAD_DIM
D_KV = N_KV_HEADS * HEAD_DIM
D_REC_K = N_REC_HEADS * REC_KEY_DIM
D_REC_V = N_REC_HEADS * REC_VAL_DIM
SPLIT_SIZES = (D_ATT, D_KV, D_KV, D_REC_K, D_REC_K, D_REC_K, D_REC_V, D_REC_V, D_MODEL, D_MODEL)
D_IN = sum(SPLIT_SIZES)
(OFF_Q, OFF_K, OFF_V, OFF_QR, OFF_ZF, OFF_ZB, OFF_IR, OFF_GR, OFF_GA, OFF_GT) = (
    int(v) for v in np.concatenate([[0], np.cumsum(SPLIT_SIZES)[:-1]]))
D_GATES = OFF_IR - OFF_ZF
D_SLAB = D_IN - D_GATES
A_Q, A_K, A_V, A_QR = OFF_Q, OFF_K, OFF_V, OFF_QR
A_IR, A_GR, A_GA, A_GT = (OFF_IR - D_GATES, OFF_GR - D_GATES, OFF_GA - D_GATES, OFF_GT - D_GATES)

V7X_VMEM_CEILING = 56 * 1024 * 1024
LANES = 128

TM_NORM = 512
TM_IN, TN_IN = 1024, 1024
SUB_IN = 256
ROWS_EW = 256
SLAB_ATT = 32
TQ_ATT = 1024
SEG_REC = 512
HB_REC = 8
TM_OUT = 256
KC_OUT = 512
TM_FFN, TF_FFN = 1024, 512


def _vmem_limit(nbytes):
    return int(min(V7X_VMEM_CEILING, nbytes * 1.25 + (4 << 20)))


def _sigmoid(x):
    return 1.0 / (1.0 + jnp.exp(-x))


def _dot_nt(a, b):
    return lax.dot_general(a, b, (((1,), (1,)), ((), ())), preferred_element_type=F32)


def _dot_tn(a, b):
    return lax.dot_general(a, b, (((0,), (0,)), ((), ())), preferred_element_type=F32)


def _rows(start, size):
    return pl.ds(start if isinstance(start, int) else pl.multiple_of(start, size), size)


def _row_tiles(n_rows, body):
    def step(r, c):
        body(_rows(r * ROWS_EW, ROWS_EW))
        return c
    lax.fori_loop(0, n_rows // ROWS_EW, step, 0)


def _segment_position(g, seq_groups):
    first, local, per = None, None, None
    for g0, n in reversed(seq_groups):
        loc = lax.rem(g - g0, n)
        fst = g - loc
        if first is None:
            first, local, per = fst, loc, n
        else:
            here = g < nxt
            first = jnp.where(here, fst, first)
            local = jnp.where(here, loc, local)
            per = jnp.where(here, n, per)
        nxt = g0
    return first, local, per


def _two_source_specs(tm, n_first, pipeline_mode=None):
    return (pl.BlockSpec((tm, D_MODEL), lambda i, *_: (jnp.minimum(i, n_first - 1), 0), pipeline_mode=pipeline_mode),
            pl.BlockSpec((tm, D_MODEL), lambda i, *_: (jnp.maximum(i - n_first, 0), 0), pipeline_mode=pipeline_mode))


def _norm_kernel(n_first, xa_ref, xb_ref, g_ref, o_ref):
    def run(x_ref):
        def body(rows):
            x = x_ref[rows, :]
            ms = jnp.mean(x * x, axis=-1, keepdims=True)
            o_ref[rows, :] = ((x * lax.rsqrt(ms + RMS_EPS)) * g_ref[...]).astype(o_ref.dtype)
        _row_tiles(TM_NORM, body)

    @pl.when(pl.program_id(0) < n_first)
    def _():
        run(xa_ref)

    @pl.when(pl.program_id(0) >= n_first)
    def _():
        run(xb_ref)


def _stream_norm(xa, xb, gain):
    T = xa.shape[0] + xb.shape[0]
    n_first = xa.shape[0] // TM_NORM
    nbytes = 2 * 2 * TM_NORM * D_MODEL * 4 + 2 * TM_NORM * D_MODEL * 2
    return pl.pallas_call(
        functools.partial(_norm_kernel, n_first),
        grid=(T // TM_NORM,),
        in_specs=[*_two_source_specs(TM_NORM, n_first), pl.BlockSpec((1, D_MODEL), lambda i: (0, 0))],
        out_specs=pl.BlockSpec((TM_NORM, D_MODEL), lambda i: (i, 0)),
        out_shape=jax.ShapeDtypeStruct((T, D_MODEL), BF16),
        compiler_params=pltpu.CompilerParams(
            dimension_semantics=("parallel",), vmem_limit_bytes=_vmem_limit(nbytes)),
        name="stream_norm",
    )(xa, xb, gain)


def _inproj_kernel(kinds, col_tiles_ref, *refs):
    del col_tiles_ref
    with_epilogue = any(k != "plain" for k in kinds)
    if with_epilogue:
        xn_ref, cos_ref, sa_ref, sb_ref, w_ref, o_ref, wb_ref, y_ref = refs
    else:
        xn_ref, w_ref, o_ref, wb_ref = refs
    j = pl.program_id(0)

    @pl.when(pl.program_id(1) == 0)
    def _():
        def body(rows):
            wb_ref[rows, :] = w_ref[rows, :].astype(BF16)
        _row_tiles(D_MODEL, body)

    def transform(kind, y, rows, lane_tile):
        if kind == "silu":
            return y * _sigmoid(y)
        if kind == "rope_k" and lane_tile >= N_KV_HEADS:
            return y
        y = (y * cos_ref[rows, :] + pltpu.roll(y, HEAD_DIM - ROPE_HALF, 1) * sa_ref[rows, :]
             + pltpu.roll(y, ROPE_HALF, 1) * sb_ref[rows, :])
        return y * Q_SCALE if kind == "rope" else y

    def epilogue(kind, sub):
        for r in range(TM_IN // ROWS_EW):
            rows = slice(r * ROWS_EW, (r + 1) * ROWS_EW)
            for cc in range(SUB_IN // LANES):
                lane_tile = sub * (SUB_IN // LANES) + cc
                y = y_ref[sub, rows, cc * LANES:(cc + 1) * LANES]
                o_ref[rows, lane_tile * LANES:(lane_tile + 1) * LANES] = (
                    transform(kind, y, rows, lane_tile).astype(o_ref.dtype))

    for kind in sorted(set(kinds)):
        tiles = [t for t, k in enumerate(kinds) if k == kind]
        cond = functools.reduce(jnp.logical_or, [j == t for t in tiles])

        @pl.when(cond)
        def _(kind=kind):
            if kind == "plain":
                o_ref[...] = jnp.dot(xn_ref[...], wb_ref[...], preferred_element_type=F32).astype(o_ref.dtype)
            else:
                n_sub = TN_IN // SUB_IN
                for sub in range(n_sub):
                    y_ref[sub] = jnp.dot(xn_ref[...], wb_ref[:, sub * SUB_IN:(sub + 1) * SUB_IN],
                                         preferred_element_type=F32)
                    if sub > 0:
                        epilogue(kind, sub - 1)
                epilogue(kind, n_sub - 1)


def _in_projection(xn, w_in, tables, table_groups, col_tiles, kinds, out_dtype, name):
    T = xn.shape[0]
    n_col = len(col_tiles)
    col_tiles = jnp.asarray(col_tiles, jnp.int32)
    with_epilogue = any(k != "plain" for k in kinds)
    last_table_tile = max([t for t, k in enumerate(kinds) if k.startswith("rope")], default=-1)
    out_bytes = jnp.dtype(out_dtype).itemsize

    def table_map(j, i, ct):
        _, local, _ = _segment_position(i, table_groups)
        return (jnp.where(j <= last_table_tile, local, 0), 0)

    in_specs = [pl.BlockSpec((TM_IN, D_MODEL), lambda j, i, ct: (i, 0))]
    args = [xn]
    scratch = [pltpu.VMEM((D_MODEL, TN_IN), BF16)]
    nbytes = (2 * TM_IN * D_MODEL * 2 + 2 * D_MODEL * TN_IN * 4 + D_MODEL * TN_IN * 2
              + 2 * TM_IN * TN_IN * out_bytes + TM_IN * TN_IN * 4)
    if with_epilogue:
        in_specs += [pl.BlockSpec((TM_IN, LANES), table_map)] * 3
        args += list(tables)
        scratch.append(pltpu.VMEM((TN_IN // SUB_IN, TM_IN, SUB_IN), F32))
        nbytes += 2 * 3 * TM_IN * LANES * 4 + TM_IN * TN_IN * 4
    in_specs.append(pl.BlockSpec((D_MODEL, TN_IN), lambda j, i, ct: (0, ct[j])))
    args.append(w_in)
    return pl.pallas_call(
        functools.partial(_inproj_kernel, kinds),
        grid_spec=pltpu.PrefetchScalarGridSpec(
            num_scalar_prefetch=1,
            grid=(n_col, T // TM_IN),
            in_specs=in_specs,
            out_specs=pl.BlockSpec((TM_IN, TN_IN), lambda j, i, ct: (i, j)),
            scratch_shapes=scratch),
        out_shape=jax.ShapeDtypeStruct((T, n_col * TN_IN), out_dtype),
        compiler_params=pltpu.CompilerParams(
            dimension_semantics=("arbitrary", "arbitrary"),
            vmem_limit_bytes=_vmem_limit(nbytes)),
        name=name,
    )(col_tiles, *args)


def _attn_kernel(seq_groups, sink_ref, q_ref, kc_ref, vc_ref, kp_ref, kn_ref, vp_ref, vn_ref,
                 ga_ref, o_ref, s_all, p_all, r_all):
    t = pl.program_id(0)
    h = pl.program_id(1)
    _, local, per = _segment_position(t, seq_groups)
    has_prev = local != 0
    has_next = local != per - 1

    k_all = jnp.concatenate([kp_ref[...], kc_ref[...], kn_ref[...]], axis=0)
    v_all = jnp.concatenate([vp_ref[...], vc_ref[...], vn_ref[...]], axis=0)
    qi = lax.broadcasted_iota(jnp.int32, (BLOCK, BLOCK), 0)
    kj = lax.broadcasted_iota(jnp.int32, (BLOCK, BLOCK), 1)
    neg = jnp.float32(-jnp.inf)
    bias_prev = jnp.where(kj >= qi, 0.0, neg)
    bias_next = jnp.where(kj <= qi, 0.0, neg)
    bias_prev0 = jnp.where(has_prev, bias_prev, neg)
    bias_next_last = jnp.where(has_next, bias_next, neg)
    n_blocks = TQ_ATT // BLOCK
    for b in range(n_blocks):
        rs = slice(b * BLOCK, (b + 1) * BLOCK)
        kb = k_all[b * BLOCK:(b + 3) * BLOCK]
        vb = v_all[b * BLOCK:(b + 3) * BLOCK]
        bp = bias_prev0 if b == 0 else bias_prev
        bn = bias_next_last if b == n_blocks - 1 else bias_next
        heads = [slice(g * HEAD_DIM, (g + 1) * HEAD_DIM) for g in range(GROUP)]
        s_ref, p_ref, r_ref = s_all.at[b], p_all.at[b], r_all.at[b]
        q4 = jnp.concatenate([q_ref[rs, cs] for cs in heads], axis=0)
        s_ref[...] = _dot_nt(q4, kb).reshape(GROUP, BLOCK, 3 * BLOCK)
        for g, cs in enumerate(heads):
            sink2 = sink_ref[GROUP * h + g] * LOG2E
            for r in range(BLOCK // SLAB_ATT):
                rr = slice(r * SLAB_ATT, (r + 1) * SLAB_ATT)
                s = s_ref[g, rr, :]
                s_p = s[:, :BLOCK] + bp[rr]
                s_c = s[:, BLOCK:2 * BLOCK]
                s_n = s[:, 2 * BLOCK:] + bn[rr]
                m = jnp.maximum(jnp.max(jnp.maximum(jnp.maximum(s_p, s_c), s_n), axis=-1, keepdims=True), sink2)
                p_p, p_c, p_n = jnp.exp2(s_p - m), jnp.exp2(s_c - m), jnp.exp2(s_n - m)
                denom = jnp.sum(p_p + p_c + p_n, axis=-1, keepdims=True) + jnp.exp2(sink2 - m)
                p_ref[g, rr, :] = jnp.concatenate([p_p, p_c, p_n], axis=1).astype(BF16)
                r_ref[g, rr, :] = jnp.broadcast_to(1.0 / denom, (SLAB_ATT, LANES))
        o4 = jnp.dot(p_ref[...].reshape(GROUP * BLOCK, 3 * BLOCK), vb, preferred_element_type=F32)
        for g, cs in enumerate(heads):
            o = o4[g * BLOCK:(g + 1) * BLOCK] * r_ref[g]
            o_ref[rs, cs] = (_sigmoid(ga_ref[rs, cs].astype(F32)) * o).astype(o_ref.dtype)


def _window_attention(slab, sink, seq_groups):
    T = slab.shape[0]
    n_tiles = T // TQ_ATT
    bpt = TQ_ATT // BLOCK
    last_blk = T // BLOCK - 1
    gw = GROUP * HEAD_DIM
    kcol, vcol, gcol = A_K // HEAD_DIM, A_V // HEAD_DIM, A_GA // gw
    nbytes = (2 * (3 * TQ_ATT * gw * 2 + 2 * TQ_ATT * HEAD_DIM * 2 + 4 * BLOCK * HEAD_DIM * 2)
              + bpt * GROUP * BLOCK * (3 * BLOCK * 6 + LANES * 4))
    return pl.pallas_call(
        functools.partial(_attn_kernel, seq_groups),
        grid=(n_tiles, N_KV_HEADS),
        in_specs=[
            pl.BlockSpec(memory_space=pltpu.SMEM),
            pl.BlockSpec((TQ_ATT, gw), lambda t, h: (t, h)),
            pl.BlockSpec((TQ_ATT, HEAD_DIM), lambda t, h: (t, kcol + h)),
            pl.BlockSpec((TQ_ATT, HEAD_DIM), lambda t, h: (t, vcol + h)),
            pl.BlockSpec((BLOCK, HEAD_DIM), lambda t, h: (jnp.maximum(t * bpt - 1, 0), kcol + h)),
            pl.BlockSpec((BLOCK, HEAD_DIM), lambda t, h: (jnp.minimum((t + 1) * bpt, last_blk), kcol + h)),
            pl.BlockSpec((BLOCK, HEAD_DIM), lambda t, h: (jnp.maximum(t * bpt - 1, 0), vcol + h)),
            pl.BlockSpec((BLOCK, HEAD_DIM), lambda t, h: (jnp.minimum((t + 1) * bpt, last_blk), vcol + h)),
            pl.BlockSpec((TQ_ATT, gw), lambda t, h: (t, gcol + h)),
        ],
        out_specs=pl.BlockSpec((TQ_ATT, gw), lambda t, h: (t, h)),
        out_shape=jax.ShapeDtypeStruct((T, D_ATT), BF16),
        scratch_shapes=[pltpu.VMEM((bpt, GROUP, BLOCK, 3 * BLOCK), F32),
                        pltpu.VMEM((bpt, GROUP, BLOCK, 3 * BLOCK), BF16),
                        pltpu.VMEM((bpt, GROUP, BLOCK, LANES), F32)],
        compiler_params=pltpu.CompilerParams(
            dimension_semantics=("parallel", "arbitrary"),
            vmem_limit_bytes=_vmem_limit(nbytes)),
        name="window_attention",
    )(sink, slab, slab, slab, slab, slab, slab, slab, slab)


def _hgrn_kernel(seq_groups, lb_ref, qf_ref, if_ref, zf_ref, qb_ref, ib_ref, zb_ref,
                 of_ref, ob_ref, st_ref, stb_ref, ds_ref, ops_ref, cross_ref, dec_ref, a_ref):
    g = pl.program_id(1)
    _, local, _ = _segment_position(g, seq_groups)

    @pl.when(local == 0)
    def _():
        st_ref[...] = jnp.zeros(st_ref.shape, F32)
        stb_ref[...] = jnp.zeros(stb_ref.shape, BF16)

    Q_IN, K_IN, Q_ST, K_ST, I_C = range(5)
    pair = 2 * CHUNK
    n_pairs = SEG_REC // pair
    rin = lax.broadcasted_iota(jnp.int32, (CHUNK, LANES), 0)
    ti = lax.broadcasted_iota(jnp.int32, (CHUNK, CHUNK), 0)
    si = lax.broadcasted_iota(jnp.int32, (CHUNK, CHUNK), 1)
    half = (slice(0, CHUNK), slice(CHUNK, pair))

    def prepare(d, hh, start, q_ref, i_ref, z_ref):
        cs = slice(hh * LANES, (hh + 1) * LANES)
        lb = lb_ref[:, cs]
        one_m_lb = 1.0 - lb
        q_in, k_in, k_st, q_dc, dec = [], [], [], [], []
        for t in range(2):
            rows = _rows(start + t * CHUNK, CHUNK)
            q = q_ref[rows, cs].astype(F32)
            gate = one_m_lb * _sigmoid(z_ref[rows, cs])
            k = one_m_lb - gate
            logf = jnp.log2(lb + gate)
            b = logf
            for s in (1, 2, 4, 8, 16):
                b = b + jnp.where(rin >= s, pltpu.roll(b, s, 0), 0.0)
            tot = b[CHUNK - 1:CHUNK, :]
            if d == 0:
                c = b
                cref = b[CHUNK // 2 - 1:CHUNK // 2, :]
            else:
                c = tot - b + logf
                cref = c[CHUNK // 2:CHUNK // 2 + 1, :]
            q_in.append(q * jnp.exp2(c - cref))
            k_in.append(k * jnp.exp2(cref - c))
            k_st.append(k_in[t] * jnp.exp2(tot - cref))
            q_dc.append(q_in[t] * jnp.exp2(cref))
            dec.append(jnp.exp2(tot))
            ops_ref[d, hh, Q_IN, half[t]] = q_in[t].astype(BF16)
            ops_ref[d, hh, K_IN, half[t]] = k_in[t].astype(BF16)
            ops_ref[d, hh, I_C, half[t]] = i_ref[rows, cs]
        ca, cb = (0, 1) if d == 0 else (1, 0)
        ops_ref[d, hh, Q_ST, half[ca]] = q_dc[ca].astype(BF16)
        ops_ref[d, hh, Q_ST, half[cb]] = (q_dc[cb] * dec[ca]).astype(BF16)
        ops_ref[d, hh, K_ST, half[ca]] = (k_st[ca] * dec[cb]).astype(BF16)
        ops_ref[d, hh, K_ST, half[cb]] = k_st[cb].astype(BF16)
        cross_ref[d, hh, 0] = q_dc[cb].astype(BF16)
        cross_ref[d, hh, 1] = k_st[ca].astype(BF16)
        dec_ref[d, hh] = jnp.broadcast_to(dec[0] * dec[1], (8, LANES))

    def intra(d, hh):
        for t in range(2):
            a_ref[d, hh, t] = _dot_nt(ops_ref[d, hh, Q_IN, half[t]], ops_ref[d, hh, K_IN, half[t]])
        a_ref[d, hh, 2] = _dot_nt(cross_ref[d, hh, 0], cross_ref[d, hh, 1])

    def output(d, hh, start, o_ref):
        cs = slice(hh * LANES, (hh + 1) * LANES)
        tri = (ti >= si) if d == 0 else (ti <= si)
        a_top = jnp.where(tri, a_ref[d, hh, 0], 0.0).astype(BF16)
        a_bot = jnp.where(tri, a_ref[d, hh, 1], 0.0).astype(BF16)
        a_x = a_ref[d, hh, 2].astype(BF16)
        i_top, i_bot = ops_ref[d, hh, I_C, half[0]], ops_ref[d, hh, I_C, half[1]]
        inter = _dot_nt(ops_ref[d, hh, Q_ST], stb_ref[d, hh])
        if d == 0:
            both = jnp.dot(jnp.concatenate([a_top, a_x], axis=0), i_top, preferred_element_type=F32)
            o_top = both[:CHUNK] + inter[:CHUNK]
            o_bot = both[CHUNK:] + jnp.dot(a_bot, i_bot, preferred_element_type=F32) + inter[CHUNK:]
        else:
            both = jnp.dot(jnp.concatenate([a_x, a_bot], axis=0), i_bot, preferred_element_type=F32)
            o_top = both[:CHUNK] + jnp.dot(a_top, i_top, preferred_element_type=F32) + inter[:CHUNK]
            o_bot = both[CHUNK:] + inter[CHUNK:]
        o_ref[_rows(start, CHUNK), cs] = o_top.astype(o_ref.dtype)
        o_ref[_rows(start + CHUNK, CHUNK), cs] = o_bot.astype(o_ref.dtype)
        ds_ref[d, hh] = _dot_tn(ops_ref[d, hh, I_C], ops_ref[d, hh, K_ST])

    def update(d, hh):
        dec = dec_ref[d, hh, 0:1, :]
        for r in range(REC_VAL_DIM // CHUNK):
            rr = slice(r * CHUNK, (r + 1) * CHUNK)
            new = dec * st_ref[d, hh, rr, :] + ds_ref[d, hh, rr, :]
            st_ref[d, hh, rr, :] = new
            stb_ref[d, hh, rr, :] = new.astype(BF16)

    def scan(n, carry):
        start_f = pl.multiple_of(n * pair, pair)
        start_b = pl.multiple_of((n_pairs - 1 - n) * pair, pair)
        for hh in range(HB_REC):
            prepare(0, hh, start_f, qf_ref, if_ref, zf_ref)
            prepare(1, hh, start_b, qb_ref, ib_ref, zb_ref)
        for hh in range(HB_REC):
            intra(0, hh)
            intra(1, hh)
        for hh in range(HB_REC):
            output(0, hh, start_f, of_ref)
            output(1, hh, start_b, ob_ref)
        for hh in range(HB_REC):
            update(0, hh)
            update(1, hh)
        return carry

    lax.fori_loop(0, n_pairs, scan, 0)


def _hgrn2(slab, gates, lb_row, seq_groups):
    T = slab.shape[0]
    w = HB_REC * LANES

    def mirror(g):
        first, local, per = _segment_position(g, seq_groups)
        return first + per - 1 - local

    def fspec(off):
        return pl.BlockSpec((SEG_REC, w), lambda hb, g: (g, off // w + hb))

    def bspec(off):
        return pl.BlockSpec((SEG_REC, w), lambda hb, g: (mirror(g), off // w + hb))

    state = (2, HB_REC, REC_VAL_DIM, REC_KEY_DIM)
    nbytes = (2 * (4 * SEG_REC * w * 2 + 2 * SEG_REC * w * 4 + 2 * SEG_REC * w * 2)
              + 2 * HB_REC * (REC_VAL_DIM * REC_KEY_DIM * 10 + 12 * CHUNK * LANES * 2 + 8 * LANES * 4))
    out = jax.ShapeDtypeStruct((T, D_REC_V), BF16)
    return pl.pallas_call(
        functools.partial(_hgrn_kernel, seq_groups),
        grid=(N_REC_HEADS // HB_REC, T // SEG_REC),
        in_specs=[
            pl.BlockSpec((1, w), lambda hb, g: (0, hb)),
            fspec(A_QR), fspec(A_IR), fspec(0),
            bspec(A_QR), bspec(A_IR), bspec(D_REC_K),
        ],
        out_specs=[pl.BlockSpec((SEG_REC, w), lambda hb, g: (g, hb)),
                   pl.BlockSpec((SEG_REC, w), lambda hb, g: (mirror(g), hb))],
        out_shape=[out, out],
        scratch_shapes=[pltpu.VMEM(state, F32), pltpu.VMEM(state, BF16), pltpu.VMEM(state, F32),
                        pltpu.VMEM((2, HB_REC, 5, 2 * CHUNK, LANES), BF16),
                        pltpu.VMEM((2, HB_REC, 2, CHUNK, LANES), BF16),
                        pltpu.VMEM((2, HB_REC, 8, LANES), F32),
                        pltpu.VMEM((2, HB_REC, 3, CHUNK, CHUNK), F32)],
        compiler_params=pltpu.CompilerParams(
            dimension_semantics=("parallel", "arbitrary"),
            vmem_limit_bytes=_vmem_limit(nbytes)),
        name="hgrn2",
    )(lb_row, slab, slab, gates, slab, slab, gates)


def _outproj_kernel(n_first, a_ref, of_ref, ob_ref, grl_ref, grh_ref, gtl_ref, gth_ref, xa_ref, xb_ref,
                    w_ref, rg_ref, g_ref, h_ref, m_ref):
    half = D_REC_V // 2
    heads_per_chunk = KC_OUT // REC_VAL_DIM
    y = None
    for c in range(D_MODEL // KC_OUT):
        for hh in range(c * heads_per_chunk, (c + 1) * heads_per_chunk):
            cs = slice(hh * REC_VAL_DIM, (hh + 1) * REC_VAL_DIM)
            lo = hh * REC_VAL_DIM < half
            hs = cs if lo else slice(hh * REC_VAL_DIM - half, (hh + 1) * REC_VAL_DIM - half)
            rec = of_ref[:, cs].astype(F32) + ob_ref[:, cs].astype(F32)
            ms = jnp.mean(rec * rec, axis=-1, keepdims=True)
            r = (rec * lax.rsqrt(ms + RMS_EPS)) * rg_ref[:, cs]
            gr = (grl_ref if lo else grh_ref)[:, hs].astype(F32)
            gt = (gtl_ref if lo else gth_ref)[:, hs].astype(F32)
            r = _sigmoid(gt) * (r * (gr * _sigmoid(gr)))
            m_ref[:, cs] = (a_ref[:, cs].astype(F32) + r).astype(BF16)
        ks = slice(c * KC_OUT, (c + 1) * KC_OUT)
        part = jnp.dot(m_ref[:, ks], w_ref[ks, :], preferred_element_type=F32)
        y = part if y is None else y + part
    ms = jnp.mean(y * y, axis=-1, keepdims=True)
    h_ref[...] = (y * lax.rsqrt(ms + RMS_EPS)) * g_ref[...]

    @pl.when(pl.program_id(0) < n_first)
    def _():
        h_ref[...] += xa_ref[...]

    @pl.when(pl.program_id(0) >= n_first)
    def _():
        h_ref[...] += xb_ref[...]


def _out_projection(attn, rec_f, rec_b, slab, xa, xb, w_bf16, rec_gain, gain):
    T = attn.shape[0]
    n_first = xa.shape[0] // TM_OUT
    half = D_REC_V // 2
    assert A_GR % half == 0 and A_GT % half == 0
    tok = lambda i: (i, 0)
    nbytes = (2 * 3 * TM_OUT * D_MODEL * 2 + 2 * 4 * TM_OUT * half * 2 + 2 * 3 * TM_OUT * D_MODEL * 4
              + 2 * D_MODEL * D_MODEL * 2 + TM_OUT * D_MODEL * 2 + 2 * TM_OUT * D_MODEL * 4)
    return pl.pallas_call(
        functools.partial(_outproj_kernel, n_first),
        grid=(T // TM_OUT,),
        in_specs=[
            pl.BlockSpec((TM_OUT, D_MODEL), tok),
            pl.BlockSpec((TM_OUT, D_REC_V), tok),
            pl.BlockSpec((TM_OUT, D_REC_V), tok),
            pl.BlockSpec((TM_OUT, half), lambda i: (i, A_GR // half)),
            pl.BlockSpec((TM_OUT, half), lambda i: (i, A_GR // half + 1)),
            pl.BlockSpec((TM_OUT, half), lambda i: (i, A_GT // half)),
            pl.BlockSpec((TM_OUT, half), lambda i: (i, A_GT // half + 1)),
            *_two_source_specs(TM_OUT, n_first),
            pl.BlockSpec((D_MODEL, D_MODEL), lambda i: (0, 0)),
            pl.BlockSpec((1, D_REC_V), lambda i: (0, 0)),
            pl.BlockSpec((1, D_MODEL), lambda i: (0, 0)),
        ],
        out_specs=pl.BlockSpec((TM_OUT, D_MODEL), tok),
        out_shape=jax.ShapeDtypeStruct((T, D_MODEL), F32),
        scratch_shapes=[pltpu.VMEM((TM_OUT, D_MODEL), BF16)],
        compiler_params=pltpu.CompilerParams(
            dimension_semantics=("parallel",),
            vmem_limit_bytes=_vmem_limit(nbytes)),
        name="out_projection",
    )(attn, rec_f, rec_b, slab, slab, slab, slab, xa, xb, w_bf16, rec_gain, gain)


def _ffn_kernel(n_first, h_ref, gpre_ref, gpost_ref, wg_ref, wu_ref, wd_ref, oa_ref, ob_ref, hn_ref, acc_ref):
    i = pl.program_id(0)
    j = pl.program_id(1)

    @pl.when(j == 0)
    def _():
        def body(rows):
            h = h_ref[rows, :]
            ms = jnp.mean(h * h, axis=-1, keepdims=True)
            hn_ref[rows, :] = ((h * lax.rsqrt(ms + RMS_EPS)) * gpre_ref[...]).astype(BF16)
            acc_ref[rows, :] = jnp.zeros((ROWS_EW, D_MODEL), F32)
        _row_tiles(TM_FFN, body)

    hn = hn_ref[...]
    g = jnp.dot(hn, wg_ref[...], preferred_element_type=F32)
    u = jnp.dot(hn, wu_ref[...], preferred_element_type=F32)
    act = ((g * _sigmoid(g)) * u).astype(BF16)
    acc_ref[...] += jnp.dot(act, wd_ref[...], preferred_element_type=F32)

    def finish(o_ref):
        def body(rows):
            y = acc_ref[rows, :]
            ms = jnp.mean(y * y, axis=-1, keepdims=True)
            o_ref[rows, :] = h_ref[rows, :] + (y * lax.rsqrt(ms + RMS_EPS)) * gpost_ref[...]
        _row_tiles(TM_FFN, body)

    last = j == pl.num_programs(1) - 1

    @pl.when(jnp.logical_and(last, i < n_first))
    def _():
        finish(oa_ref)

    @pl.when(jnp.logical_and(last, i >= n_first))
    def _():
        finish(ob_ref)


def _ffn(h, n_tokens_first, gpre, gpost, wg, wu, wd):
    T = h.shape[0]
    n_first = n_tokens_first // TM_FFN
    once = pl.Buffered(1)
    nbytes = (3 * TM_FFN * D_MODEL * 4 + TM_FFN * D_MODEL * 2 + 2 * 3 * D_MODEL * TF_FFN * 2
              + 3 * TM_FFN * TF_FFN * 4 + 2 * TM_FFN * D_MODEL * 4)
    return pl.pallas_call(
        functools.partial(_ffn_kernel, n_first),
        grid=(T // TM_FFN, D_FF // TF_FFN),
        in_specs=[
            pl.BlockSpec((TM_FFN, D_MODEL), lambda i, j: (i, 0), pipeline_mode=once),
            pl.BlockSpec((1, D_MODEL), lambda i, j: (0, 0)),
            pl.BlockSpec((1, D_MODEL), lambda i, j: (0, 0)),
            pl.BlockSpec((D_MODEL, TF_FFN), lambda i, j: (0, j)),
            pl.BlockSpec((D_MODEL, TF_FFN), lambda i, j: (0, j)),
            pl.BlockSpec((TF_FFN, D_MODEL), lambda i, j: (j, 0)),
        ],
        out_specs=list(_two_source_specs(TM_FFN, n_first, once)),
        out_shape=[jax.ShapeDtypeStruct((n_tokens_first, D_MODEL), F32),
                   jax.ShapeDtypeStruct((T - n_tokens_first, D_MODEL), F32)],
        scratch_shapes=[pltpu.VMEM((TM_FFN, D_MODEL), BF16), pltpu.VMEM((TM_FFN, D_MODEL), F32)],
        compiler_params=pltpu.CompilerParams(
            dimension_semantics=("arbitrary", "arbitrary"),
            vmem_limit_bytes=_vmem_limit(nbytes)),
        name="swiglu_ffn",
    )(h, gpre, gpost, wg, wu, wd)


def _rope_tables(max_len):
    pos = jnp.arange(max_len, dtype=F32)
    inv_freq = ROPE_THETA ** (-jnp.arange(ROPE_HALF, dtype=F32) / ROPE_HALF)
    ang = pos[:, None] * inv_freq[None, :]
    cos, sin = jnp.cos(ang), jnp.sin(ang)
    rest = HEAD_DIM - ROPE_DIM
    cos_t = jnp.concatenate([cos, cos, jnp.ones((max_len, rest), F32)], axis=1)
    sa_t = jnp.concatenate([-sin, jnp.zeros((max_len, HEAD_DIM - ROPE_HALF), F32)], axis=1)
    sb_t = jnp.concatenate([jnp.zeros((max_len, ROPE_HALF), F32), sin, jnp.zeros((max_len, rest), F32)], axis=1)
    return cos_t, sa_t, sb_t


def _seq_groups(seq_shapes, seg):
    groups, first = [], 0
    for B, L in seq_shapes:
        assert L % seg == 0
        groups.append((first, L // seg))
        first += B * L // seg
    return tuple(groups)


def _encoder_layer(xa, xb, seq_shapes, tables, w_in, sink, rec_norm, lb, w_out, norm_mix_pre, norm_mix_post,
                   norm_ffn_pre, norm_ffn_post, w_gate, w_up, w_down):
    row = lambda v: v.astype(F32).reshape(1, -1)
    xn = _stream_norm(xa, xb, row(norm_mix_pre))

    assert all(off % TN_IN == 0 for off in (OFF_K, OFF_QR, OFF_ZF, OFF_IR)) and 2 * D_KV == TN_IN
    gate_tiles = list(range(OFF_ZF // TN_IN, OFF_IR // TN_IN))
    slab_tiles = [t for t in range(D_IN // TN_IN) if t not in gate_tiles]
    kind_of = lambda t: ("rope" if t < OFF_K // TN_IN else "rope_k" if t == OFF_K // TN_IN
                         else "silu" if OFF_QR // TN_IN <= t < OFF_ZF // TN_IN else "plain")
    table_groups = _seq_groups(seq_shapes, TM_IN)
    slab = _in_projection(xn, w_in, tables, table_groups, slab_tiles, tuple(kind_of(t) for t in slab_tiles),
                          BF16, "in_projection")
    gates = _in_projection(xn, w_in, None, None, gate_tiles, ("plain",) * len(gate_tiles),
                           F32, "gate_projection")

    attn = _window_attention(slab, sink.astype(F32), _seq_groups(seq_shapes, TQ_ATT))
    rec_f, rec_b = _hgrn2(slab, gates, row(lb), _seq_groups(seq_shapes, SEG_REC))
    h = _out_projection(attn, rec_f, rec_b, slab, xa, xb, w_out.astype(BF16), row(rec_norm),
                        row(norm_mix_post))
    return _ffn(h, xa.shape[0], row(norm_ffn_pre), row(norm_ffn_post),
                w_gate.astype(BF16), w_up.astype(BF16), w_down.astype(BF16))


def kernel(x_prompt, x_sample, w_in, sink, rec_norm, lb_logits, w_out, norm_mix_pre, norm_mix_post,
           norm_ffn_pre, norm_ffn_post, w_gate, w_up, w_down):
    lb_all = jnp.cumsum(jax.nn.softmax(lb_logits.astype(F32), axis=0), axis=0)
    seq_shapes = (x_prompt.shape[:2], x_sample.shape[:2])
    xa = x_prompt.reshape(-1, D_MODEL)
    xb = x_sample.reshape(-1, D_MODEL)
    tables = _rope_tables(max(L for _, L in seq_shapes))
    for l in range(DEPTH):
        xa, xb = _encoder_layer(xa, xb, seq_shapes, tables, w_in[l], sink[l], rec_norm[l], lb_all[l], w_out[l],
                                norm_mix_pre[l], norm_mix_post[l], norm_ffn_pre[l], norm_ffn_post[l],
                                w_gate[l], w_up[l], w_down[l])
    return (xa.reshape(x_prompt.shape), xb.reshape(x_sample.shape))
```

```python
import functools

import jax
import jax.numpy as jnp
import numpy as np
from jax import lax
from jax.experimental import pallas as pl
from jax.experimental.pallas import tpu as pltpu

F32 = jnp.float32
BF16 = jnp.bfloat16

D_MODEL = 2048
DEPTH = 1
HEAD_DIM = 128
N_Q_HEADS = 16
N_KV_HEADS = 4
GROUP = N_Q_HEADS // N_KV_HEADS
WINDOW = 128
BLOCK = 128
ROPE_DIM = HEAD_DIM // 4
ROPE_HALF = ROPE_DIM // 2
ROPE_THETA = 500000.0
N_REC_HEADS = 16
REC_KEY_DIM = 128
REC_VAL_DIM = 128
CHUNK = 32
D_FF = -(-8 * D_MODEL // (3 * 256)) * 256
RMS_EPS = 1e-6
LOG2E = 1.4426950408889634
Q_SCALE = HEAD_DIM ** -0.5 * LOG2E

D_ATT = N_Q_HEADS * HEAD_DIM
D_KV = N_KV_HEADS * HEAD_DIM
D_REC_K = N_REC_HEADS * REC_KEY_DIM
D_REC_V = N_REC_HEADS * REC_VAL_DIM
SPLIT_SIZES = (D_ATT, D_KV, D_KV, D_REC_K, D_REC_K, D_REC_K, D_REC_V, D_REC_V, D_MODEL, D_MODEL)
D_IN = sum(SPLIT_SIZES)
(OFF_Q, OFF_K, OFF_V, OFF_QR, OFF_ZF, OFF_ZB, OFF_IR, OFF_GR, OFF_GA, OFF_GT) = (
    int(v) for v in np.concatenate([[0], np.cumsum(SPLIT_SIZES)[:-1]]))
D_GATES = OFF_IR - OFF_ZF
D_SLAB = D_IN - D_GATES
A_Q, A_K, A_V, A_QR = OFF_Q, OFF_K, OFF_V, OFF_QR
A_IR, A_GR, A_GA, A_GT = (OFF_IR - D_GATES, OFF_GR - D_GATES, OFF_GA - D_GATES, OFF_GT - D_GATES)

V7X_VMEM_CEILING = 56 * 1024 * 1024
LANES = 128

TM_NORM = 512
TM_IN, TN_IN = 1024, 1024
SUB_IN = 256
ROWS_EW = 256
SLAB_ATT = 32
TQ_ATT = 1024
SEG_REC = 512
HB_REC = 8
TM_OUT = 256
KC_OUT = 512
TM_FFN, TF_FFN = 512, 512


def _vmem_limit(nbytes):
    return int(min(V7X_VMEM_CEILING, nbytes * 1.25 + (4 << 20)))


def _sigmoid(x):
    return 1.0 / (1.0 + jnp.exp(-x))


def _dot_nt(a, b):
    return lax.dot_general(a, b, (((1,), (1,)), ((), ())), preferred_element_type=F32)


def _dot_tn(a, b):
    return lax.dot_general(a, b, (((0,), (0,)), ((), ())), preferred_element_type=F32)


def _rows(start, size):
    return pl.ds(start if isinstance(start, int) else pl.multiple_of(start, size), size)


def _row_tiles(n_rows, body):
    def step(r, c):
        body(_rows(r * ROWS_EW, ROWS_EW))
        return c
    lax.fori_loop(0, n_rows // ROWS_EW, step, 0)


def _segment_position(g, seq_groups):
    first, local, per = None, None, None
    for g0, n in reversed(seq_groups):
        loc = lax.rem(g - g0, n)
        fst = g - loc
        if first is None:
            first, local, per = fst, loc, n
        else:
            here = g < nxt
            first = jnp.where(here, fst, first)
            local = jnp.where(here, loc, local)
            per = jnp.where(here, n, per)
        nxt = g0
    return first, local, per


def _two_source_specs(tm, n_first):
    return (pl.BlockSpec((tm, D_MODEL), lambda i, *_: (jnp.minimum(i, n_first - 1), 0)),
            pl.BlockSpec((tm, D_MODEL), lambda i, *_: (jnp.maximum(i - n_first, 0), 0)))


def _norm_kernel(n_first, xa_ref, xb_ref, g_ref, o_ref):
    def run(x_ref):
        def body(rows):
            x = x_ref[rows, :]
            ms = jnp.mean(x * x, axis=-1, keepdims=True)
            o_ref[rows, :] = ((x * lax.rsqrt(ms + RMS_EPS)) * g_ref[...]).astype(o_ref.dtype)
        _row_tiles(TM_NORM, body)

    @pl.when(pl.program_id(0) < n_first)
    def _():
        run(xa_ref)

    @pl.when(pl.program_id(0) >= n_first)
    def _():
        run(xb_ref)


def _stream_norm(xa, xb, gain):
    T = xa.shape[0] + xb.shape[0]
    n_first = xa.shape[0] // TM_NORM
    nbytes = 2 * 2 * TM_NORM * D_MODEL * 4 + 2 * TM_NORM * D_MODEL * 2
    return pl.pallas_call(
        functools.partial(_norm_kernel, n_first),
        grid=(T // TM_NORM,),
        in_specs=[*_two_source_specs(TM_NORM, n_first), pl.BlockSpec((1, D_MODEL), lambda i: (0, 0))],
        out_specs=pl.BlockSpec((TM_NORM, D_MODEL), lambda i: (i, 0)),
        out_shape=jax.ShapeDtypeStruct((T, D_MODEL), BF16),
        compiler_params=pltpu.CompilerParams(
            dimension_semantics=("parallel",), vmem_limit_bytes=_vmem_limit(nbytes)),
        name="stream_norm",
    )(xa, xb, gain)


def _inproj_kernel(kinds, col_tiles_ref, *refs):
    del col_tiles_ref
    with_epilogue = any(k != "plain" for k in kinds)
    if with_epilogue:
        xn_ref, cos_ref, sa_ref, sb_ref, w_ref, o_ref, wb_ref, y_ref = refs
    else:
        xn_ref, w_ref, o_ref, wb_ref = refs
    j = pl.program_id(0)

    @pl.when(pl.program_id(1) == 0)
    def _():
        def body(rows):
            wb_ref[rows, :] = w_ref[rows, :].astype(BF16)
        _row_tiles(D_MODEL, body)

    def transform(kind, y, rows, lane_tile):
        if kind == "silu":
            return y * _sigmoid(y)
        if kind == "rope_k" and lane_tile >= N_KV_HEADS:
            return y
        y = (y * cos_ref[rows, :] + pltpu.roll(y, HEAD_DIM - ROPE_HALF, 1) * sa_ref[rows, :]
             + pltpu.roll(y, ROPE_HALF, 1) * sb_ref[rows, :])
        return y * Q_SCALE if kind == "rope" else y

    def epilogue(kind, sub):
        for r in range(TM_IN // ROWS_EW):
            rows = slice(r * ROWS_EW, (r + 1) * ROWS_EW)
            for cc in range(SUB_IN // LANES):
                lane_tile = sub * (SUB_IN // LANES) + cc
                y = y_ref[sub, rows, cc * LANES:(cc + 1) * LANES]
                o_ref[rows, lane_tile * LANES:(lane_tile + 1) * LANES] = (
                    transform(kind, y, rows, lane_tile).astype(o_ref.dtype))

    for kind in sorted(set(kinds)):
        tiles = [t for t, k in enumerate(kinds) if k == kind]
        cond = functools.reduce(jnp.logical_or, [j == t for t in tiles])

        @pl.when(cond)
        def _(kind=kind):
            if kind == "plain":
                o_ref[...] = jnp.dot(xn_ref[...], wb_ref[...], preferred_element_type=F32).astype(o_ref.dtype)
            else:
                n_sub = TN_IN // SUB_IN
                for sub in range(n_sub):
                    y_ref[sub] = jnp.dot(xn_ref[...], wb_ref[:, sub * SUB_IN:(sub + 1) * SUB_IN],
                                         preferred_element_type=F32)
                    if sub > 0:
                        epilogue(kind, sub - 1)
                epilogue(kind, n_sub - 1)


def _in_projection(xn, w_in, tables, table_groups, col_tiles, kinds, out_dtype, name):
    T = xn.shape[0]
    n_col = len(col_tiles)
    col_tiles = jnp.asarray(col_tiles, jnp.int32)
    with_epilogue = any(k != "plain" for k in kinds)
    last_table_tile = max([t for t, k in enumerate(kinds) if k.startswith("rope")], default=-1)
    out_bytes = jnp.dtype(out_dtype).itemsize

    def table_map(j, i, ct):
        _, local, _ = _segment_position(i, table_groups)
        return (jnp.where(j <= last_table_tile, local, 0), 0)

    in_specs = [pl.BlockSpec((TM_IN, D_MODEL), lambda j, i, ct: (i, 0))]
    args = [xn]
    scratch = [pltpu.VMEM((D_MODEL, TN_IN), BF16)]
    nbytes = (2 * TM_IN * D_MODEL * 2 + 2 * D_MODEL * TN_IN * 4 + D_MODEL * TN_IN * 2
              + 2 * TM_IN * TN_IN * out_bytes + TM_IN * TN_IN * 4)
    if with_epilogue:
        in_specs += [pl.BlockSpec((TM_IN, LANES), table_map)] * 3
        args += list(tables)
        scratch.append(pltpu.VMEM((TN_IN // SUB_IN, TM_IN, SUB_IN), F32))
        nbytes += 2 * 3 * TM_IN * LANES * 4 + TM_IN * TN_IN * 4
    in_specs.append(pl.BlockSpec((D_MODEL, TN_IN), lambda j, i, ct: (0, ct[j])))
    args.append(w_in)
    return pl.pallas_call(
        functools.partial(_inproj_kernel, kinds),
        grid_spec=pltpu.PrefetchScalarGridSpec(
            num_scalar_prefetch=1,
            grid=(n_col, T // TM_IN),
            in_specs=in_specs,
            out_specs=pl.BlockSpec((TM_IN, TN_IN), lambda j, i, ct: (i, j)),
            scratch_shapes=scratch),
        out_shape=jax.ShapeDtypeStruct((T, n_col * TN_IN), out_dtype),
        compiler_params=pltpu.CompilerParams(
            dimension_semantics=("arbitrary", "arbitrary"),
            vmem_limit_bytes=_vmem_limit(nbytes)),
        name=name,
    )(col_tiles, *args)


def _attn_kernel(seq_groups, sink_ref, q_ref, kc_ref, vc_ref, kp_ref, kn_ref, vp_ref, vn_ref,
                 ga_ref, o_ref, s_all, p_all, r_all):
    t = pl.program_id(0)
    h = pl.program_id(1)
    _, local, per = _segment_position(t, seq_groups)
    has_prev = local != 0
    has_next = local != per - 1

    k_all = jnp.concatenate([kp_ref[...], kc_ref[...], kn_ref[...]], axis=0)
    v_all = jnp.concatenate([vp_ref[...], vc_ref[...], vn_ref[...]], axis=0)
    qi = lax.broadcasted_iota(jnp.int32, (BLOCK, BLOCK), 0)
    kj = lax.broadcasted_iota(jnp.int32, (BLOCK, BLOCK), 1)
    neg = jnp.float32(-jnp.inf)
    bias_prev = jnp.where(kj >= qi, 0.0, neg)
    bias_next = jnp.where(kj <= qi, 0.0, neg)
    bias_prev0 = jnp.where(has_prev, bias_prev, neg)
    bias_next_last = jnp.where(has_next, bias_next, neg)
    n_blocks = TQ_ATT // BLOCK
    for b in range(n_blocks):
        rs = slice(b * BLOCK, (b + 1) * BLOCK)
        kb = k_all[b * BLOCK:(b + 3) * BLOCK]
        vb = v_all[b * BLOCK:(b + 3) * BLOCK]
        bp = bias_prev0 if b == 0 else bias_prev
        bn = bias_next_last if b == n_blocks - 1 else bias_next
        heads = [slice(g * HEAD_DIM, (g + 1) * HEAD_DIM) for g in range(GROUP)]
        s_ref, p_ref, r_ref = s_all.at[b], p_all.at[b], r_all.at[b]
        q4 = jnp.concatenate([q_ref[rs, cs] for cs in heads], axis=0)
        s_ref[...] = _dot_nt(q4, kb).reshape(GROUP, BLOCK, 3 * BLOCK)
        for g, cs in enumerate(heads):
            sink2 = sink_ref[GROUP * h + g] * LOG2E
            for r in range(BLOCK // SLAB_ATT):
                rr = slice(r * SLAB_ATT, (r + 1) * SLAB_ATT)
                s = s_ref[g, rr, :]
                s_p = s[:, :BLOCK] + bp[rr]
                s_c = s[:, BLOCK:2 * BLOCK]
                s_n = s[:, 2 * BLOCK:] + bn[rr]
                m = jnp.maximum(jnp.max(jnp.maximum(jnp.maximum(s_p, s_c), s_n), axis=-1, keepdims=True), sink2)
                p_p, p_c, p_n = jnp.exp2(s_p - m), jnp.exp2(s_c - m), jnp.exp2(s_n - m)
                denom = jnp.sum(p_p + p_c + p_n, axis=-1, keepdims=True) + jnp.exp2(sink2 - m)
                p_ref[g, rr, :] = jnp.concatenate([p_p, p_c, p_n], axis=1).astype(BF16)
                r_ref[g, rr, :] = jnp.broadcast_to(1.0 / denom, (SLAB_ATT, LANES))
        o4 = jnp.dot(p_ref[...].reshape(GROUP * BLOCK, 3 * BLOCK), vb, preferred_element_type=F32)
        for g, cs in enumerate(heads):
            o = o4[g * BLOCK:(g + 1) * BLOCK] * r_ref[g]
            o_ref[rs, cs] = (_sigmoid(ga_ref[rs, cs].astype(F32)) * o).astype(o_ref.dtype)


def _window_attention(slab, sink, seq_groups):
    T = slab.shape[0]
    n_tiles = T // TQ_ATT
    bpt = TQ_ATT // BLOCK
    last_blk = T // BLOCK - 1
    gw = GROUP * HEAD_DIM
    kcol, vcol, gcol = A_K // HEAD_DIM, A_V // HEAD_DIM, A_GA // gw
    nbytes = (2 * (3 * TQ_ATT * gw * 2 + 2 * TQ_ATT * HEAD_DIM * 2 + 4 * BLOCK * HEAD_DIM * 2)
              + bpt * GROUP * BLOCK * (3 * BLOCK * 6 + LANES * 4))
    return pl.pallas_call(
        functools.partial(_attn_kernel, seq_groups),
        grid=(n_tiles, N_KV_HEADS),
        in_specs=[
            pl.BlockSpec(memory_space=pltpu.SMEM),
            pl.BlockSpec((TQ_ATT, gw), lambda t, h: (t, h)),
            pl.BlockSpec((TQ_ATT, HEAD_DIM), lambda t, h: (t, kcol + h)),
            pl.BlockSpec((TQ_ATT, HEAD_DIM), lambda t, h: (t, vcol + h)),
            pl.BlockSpec((BLOCK, HEAD_DIM), lambda t, h: (jnp.maximum(t * bpt - 1, 0), kcol + h)),
            pl.BlockSpec((BLOCK, HEAD_DIM), lambda t, h: (jnp.minimum((t + 1) * bpt, last_blk), kcol + h)),
            pl.BlockSpec((BLOCK, HEAD_DIM), lambda t, h: (jnp.maximum(t * bpt - 1, 0), vcol + h)),
            pl.BlockSpec((BLOCK, HEAD_DIM), lambda t, h: (jnp.minimum((t + 1) * bpt, last_blk), vcol + h)),
            pl.BlockSpec((TQ_ATT, gw), lambda t, h: (t, gcol + h)),
        ],
        out_specs=pl.BlockSpec((TQ_ATT, gw), lambda t, h: (t, h)),
        out_shape=jax.ShapeDtypeStruct((T, D_ATT), BF16),
        scratch_shapes=[pltpu.VMEM((bpt, GROUP, BLOCK, 3 * BLOCK), F32),
                        pltpu.VMEM((bpt, GROUP, BLOCK, 3 * BLOCK), BF16),
                        pltpu.VMEM((bpt, GROUP, BLOCK, LANES), F32)],
        compiler_params=pltpu.CompilerParams(
            dimension_semantics=("parallel", "arbitrary"),
            vmem_limit_bytes=_vmem_limit(nbytes)),
        name="window_attention",
    )(sink, slab, slab, slab, slab, slab, slab, slab, slab)


def _hgrn_kernel(seq_groups, lb_ref, qf_ref, if_ref, zf_ref, qb_ref, ib_ref, zb_ref,
                 of_ref, ob_ref, st_ref, stb_ref, ds_ref, ops_ref, aq_ref, ak_ref, dec_ref, a_ref):
    g = pl.program_id(1)
    _, local, _ = _segment_position(g, seq_groups)

    @pl.when(local == 0)
    def _():
        st_ref[...] = jnp.zeros(st_ref.shape, F32)
        stb_ref[...] = jnp.zeros(stb_ref.shape, BF16)
        aq_ref[...] = jnp.zeros(aq_ref.shape, BF16)
        ak_ref[...] = jnp.zeros(ak_ref.shape, BF16)

    Q_ST, K_ST, I_C = range(3)
    pair = 2 * CHUNK
    n_pairs = SEG_REC // pair
    rin = lax.broadcasted_iota(jnp.int32, (CHUNK, LANES), 0)
    ti = lax.broadcasted_iota(jnp.int32, (pair, pair), 0)
    si = lax.broadcasted_iota(jnp.int32, (pair, pair), 1)
    half = (slice(0, CHUNK), slice(CHUNK, pair))
    slot = tuple(slice(s * LANES, (s + 1) * LANES) for s in range(3))

    def prepare(d, hh, start, q_ref, i_ref, z_ref):
        cs = slice(hh * LANES, (hh + 1) * LANES)
        lb = lb_ref[:, cs]
        one_m_lb = 1.0 - lb
        q_in, k_in, k_st, q_dc, dec = [], [], [], [], []
        for t in range(2):
            rows = _rows(start + t * CHUNK, CHUNK)
            q = q_ref[rows, cs].astype(F32)
            gate = one_m_lb * _sigmoid(z_ref[rows, cs])
            k = one_m_lb - gate
            logf = jnp.log2(lb + gate)
            b = logf
            for s in (1, 2, 4, 8, 16):
                b = b + jnp.where(rin >= s, pltpu.roll(b, s, 0), 0.0)
            tot = b[CHUNK - 1:CHUNK, :]
            if d == 0:
                c = b
                cref = b[CHUNK // 2 - 1:CHUNK // 2, :]
            else:
                c = tot - b + logf
                cref = c[CHUNK // 2:CHUNK // 2 + 1, :]
            q_in.append(q * jnp.exp2(c - cref))
            k_in.append(k * jnp.exp2(cref - c))
            k_st.append(k_in[t] * jnp.exp2(tot - cref))
            q_dc.append(q_in[t] * jnp.exp2(cref))
            dec.append(jnp.exp2(tot))
            aq_ref[d, hh, half[t], slot[t]] = q_in[t].astype(BF16)
            ak_ref[d, hh, half[t], slot[t]] = k_in[t].astype(BF16)
            ops_ref[d, hh, I_C, half[t]] = i_ref[rows, cs]
        ca, cb = (0, 1) if d == 0 else (1, 0)
        ops_ref[d, hh, Q_ST, half[ca]] = q_dc[ca].astype(BF16)
        ops_ref[d, hh, Q_ST, half[cb]] = (q_dc[cb] * dec[ca]).astype(BF16)
        ops_ref[d, hh, K_ST, half[ca]] = (k_st[ca] * dec[cb]).astype(BF16)
        ops_ref[d, hh, K_ST, half[cb]] = k_st[cb].astype(BF16)
        aq_ref[d, hh, half[cb], slot[2]] = q_dc[cb].astype(BF16)
        ak_ref[d, hh, half[ca], slot[2]] = k_st[ca].astype(BF16)
        dec_ref[d, hh] = jnp.broadcast_to(dec[0] * dec[1], (8, LANES))

    def intra(d, hh):
        a_ref[d, hh] = _dot_nt(aq_ref[d, hh], ak_ref[d, hh])

    def output(d, hh, start, o_ref):
        cs = slice(hh * LANES, (hh + 1) * LANES)
        a = jnp.where((ti >= si) if d == 0 else (ti <= si), a_ref[d, hh], 0.0)
        o = (jnp.dot(a.astype(BF16), ops_ref[d, hh, I_C], preferred_element_type=F32)
             + _dot_nt(ops_ref[d, hh, Q_ST], stb_ref[d, hh]))
        o_ref[_rows(start, pair), cs] = o.astype(o_ref.dtype)
        ds_ref[d, hh] = _dot_tn(ops_ref[d, hh, I_C], ops_ref[d, hh, K_ST])

    def update(d, hh):
        dec = dec_ref[d, hh, 0:1, :]
        for r in range(REC_VAL_DIM // CHUNK):
            rr = slice(r * CHUNK, (r + 1) * CHUNK)
            new = dec * st_ref[d, hh, rr, :] + ds_ref[d, hh, rr, :]
            st_ref[d, hh, rr, :] = new
            stb_ref[d, hh, rr, :] = new.astype(BF16)

    def scan(n, carry):
        start_f = pl.multiple_of(n * pair, pair)
        start_b = pl.multiple_of((n_pairs - 1 - n) * pair, pair)
        for hh in range(HB_REC):
            prepare(0, hh, start_f, qf_ref, if_ref, zf_ref)
            prepare(1, hh, start_b, qb_ref, ib_ref, zb_ref)
        for hh in range(HB_REC):
            intra(0, hh)
            intra(1, hh)
        for hh in range(HB_REC):
            output(0, hh, start_f, of_ref)
            output(1, hh, start_b, ob_ref)
        for hh in range(HB_REC):
            update(0, hh)
            update(1, hh)
        return carry

    lax.fori_loop(0, n_pairs, scan, 0)


def _hgrn2(slab, gates, lb_row, seq_groups):
    T = slab.shape[0]
    w = HB_REC * LANES

    def mirror(g):
        first, local, per = _segment_position(g, seq_groups)
        return first + per - 1 - local

    def fspec(off):
        return pl.BlockSpec((SEG_REC, w), lambda hb, g: (g, off // w + hb))

    def bspec(off):
        return pl.BlockSpec((SEG_REC, w), lambda hb, g: (mirror(g), off // w + hb))

    state = (2, HB_REC, REC_VAL_DIM, REC_KEY_DIM)
    nbytes = (2 * (4 * SEG_REC * w * 2 + 2 * SEG_REC * w * 4 + 2 * SEG_REC * w * 2)
              + 2 * HB_REC * (REC_VAL_DIM * REC_KEY_DIM * 10 + 12 * CHUNK * LANES * 2 + 8 * LANES * 4))
    out = jax.ShapeDtypeStruct((T, D_REC_V), BF16)
    return pl.pallas_call(
        functools.partial(_hgrn_kernel, seq_groups),
        grid=(N_REC_HEADS // HB_REC, T // SEG_REC),
        in_specs=[
            pl.BlockSpec((1, w), lambda hb, g: (0, hb)),
            fspec(A_QR), fspec(A_IR), fspec(0),
            bspec(A_QR), bspec(A_IR), bspec(D_REC_K),
        ],
        out_specs=[pl.BlockSpec((SEG_REC, w), lambda hb, g: (g, hb)),
                   pl.BlockSpec((SEG_REC, w), lambda hb, g: (mirror(g), hb))],
        out_shape=[out, out],
        scratch_shapes=[pltpu.VMEM(state, F32), pltpu.VMEM(state, BF16), pltpu.VMEM(state, F32),
                        pltpu.VMEM((2, HB_REC, 3, 2 * CHUNK, LANES), BF16),
                        pltpu.VMEM((2, HB_REC, 2 * CHUNK, 3 * LANES), BF16),
                        pltpu.VMEM((2, HB_REC, 2 * CHUNK, 3 * LANES), BF16),
                        pltpu.VMEM((2, HB_REC, 8, LANES), F32),
                        pltpu.VMEM((2, HB_REC, 2 * CHUNK, 2 * CHUNK), F32)],
        compiler_params=pltpu.CompilerParams(
            dimension_semantics=("parallel", "arbitrary"),
            vmem_limit_bytes=_vmem_limit(nbytes)),
        name="hgrn2",
    )(lb_row, slab, slab, gates, slab, slab, gates)


def _outproj_kernel(n_first, a_ref, of_ref, ob_ref, grl_ref, grh_ref, gtl_ref, gth_ref, xa_ref, xb_ref,
                    w_ref, rg_ref, g_ref, h_ref, m_ref):
    half = D_REC_V // 2
    heads_per_chunk = KC_OUT // REC_VAL_DIM
    y = None
    for c in range(D_MODEL // KC_OUT):
        for hh in range(c * heads_per_chunk, (c + 1) * heads_per_chunk):
            cs = slice(hh * REC_VAL_DIM, (hh + 1) * REC_VAL_DIM)
            lo = hh * REC_VAL_DIM < half
            hs = cs if lo else slice(hh * REC_VAL_DIM - half, (hh + 1) * REC_VAL_DIM - half)
            rec = of_ref[:, cs].astype(F32) + ob_ref[:, cs].astype(F32)
            ms = jnp.mean(rec * rec, axis=-1, keepdims=True)
            r = (rec * lax.rsqrt(ms + RMS_EPS)) * rg_ref[:, cs]
            gr = (grl_ref if lo else grh_ref)[:, hs].astype(F32)
            gt = (gtl_ref if lo else gth_ref)[:, hs].astype(F32)
            r = _sigmoid(gt) * (r * (gr * _sigmoid(gr)))
            m_ref[:, cs] = (a_ref[:, cs].astype(F32) + r).astype(BF16)
        ks = slice(c * KC_OUT, (c + 1) * KC_OUT)
        part = jnp.dot(m_ref[:, ks], w_ref[ks, :], preferred_element_type=F32)
        y = part if y is None else y + part
    ms = jnp.mean(y * y, axis=-1, keepdims=True)
    h_ref[...] = (y * lax.rsqrt(ms + RMS_EPS)) * g_ref[...]

    @pl.when(pl.program_id(0) < n_first)
    def _():
        h_ref[...] += xa_ref[...]

    @pl.when(pl.program_id(0) >= n_first)
    def _():
        h_ref[...] += xb_ref[...]


def _out_projection(attn, rec_f, rec_b, slab, xa, xb, w_bf16, rec_gain, gain):
    T = attn.shape[0]
    n_first = xa.shape[0] // TM_OUT
    half = D_REC_V // 2
    assert A_GR % half == 0 and A_GT % half == 0
    tok = lambda i: (i, 0)
    nbytes = (2 * 3 * TM_OUT * D_MODEL * 2 + 2 * 4 * TM_OUT * half * 2 + 2 * 3 * TM_OUT * D_MODEL * 4
              + 2 * D_MODEL * D_MODEL * 2 + TM_OUT * D_MODEL * 2 + 2 * TM_OUT * D_MODEL * 4)
    return pl.pallas_call(
        functools.partial(_outproj_kernel, n_first),
        grid=(T // TM_OUT,),
        in_specs=[
            pl.BlockSpec((TM_OUT, D_MODEL), tok),
            pl.BlockSpec((TM_OUT, D_REC_V), tok),
            pl.BlockSpec((TM_OUT, D_REC_V), tok),
            pl.BlockSpec((TM_OUT, half), lambda i: (i, A_GR // half)),
            pl.BlockSpec((TM_OUT, half), lambda i: (i, A_GR // half + 1)),
            pl.BlockSpec((TM_OUT, half), lambda i: (i, A_GT // half)),
            pl.BlockSpec((TM_OUT, half), lambda i: (i, A_GT // half + 1)),
            *_two_source_specs(TM_OUT, n_first),
            pl.BlockSpec((D_MODEL, D_MODEL), lambda i: (0, 0)),
            pl.BlockSpec((1, D_REC_V), lambda i: (0, 0)),
            pl.BlockSpec((1, D_MODEL), lambda i: (0, 0)),
        ],
        out_specs=pl.BlockSpec((TM_OUT, D_MODEL), tok),
        out_shape=jax.ShapeDtypeStruct((T, D_MODEL), F32),
        scratch_shapes=[pltpu.VMEM((TM_OUT, D_MODEL), BF16)],
        compiler_params=pltpu.CompilerParams(
            dimension_semantics=("parallel",),
            vmem_limit_bytes=_vmem_limit(nbytes)),
        name="out_projection",
    )(attn, rec_f, rec_b, slab, slab, slab, slab, xa, xb, w_bf16, rec_gain, gain)


def _ffn_kernel(n_first, h_ref, gpre_ref, gpost_ref, wg_ref, wu_ref, wd_ref, oa_ref, ob_ref, hn_ref, acc_ref):
    i = pl.program_id(0)
    j = pl.program_id(1)

    @pl.when(j == 0)
    def _():
        def body(rows):
            h = h_ref[rows, :]
            ms = jnp.mean(h * h, axis=-1, keepdims=True)
            hn_ref[rows, :] = ((h * lax.rsqrt(ms + RMS_EPS)) * gpre_ref[...]).astype(BF16)
            acc_ref[rows, :] = jnp.zeros((ROWS_EW, D_MODEL), F32)
        _row_tiles(TM_FFN, body)

    hn = hn_ref[...]
    g = jnp.dot(hn, wg_ref[...], preferred_element_type=F32)
    u = jnp.dot(hn, wu_ref[...], preferred_element_type=F32)
    act = ((g * _sigmoid(g)) * u).astype(BF16)
    acc_ref[...] += jnp.dot(act, wd_ref[...], preferred_element_type=F32)

    def finish(o_ref):
        def body(rows):
            y = acc_ref[rows, :]
            ms = jnp.mean(y * y, axis=-1, keepdims=True)
            o_ref[rows, :] = h_ref[rows, :] + (y * lax.rsqrt(ms + RMS_EPS)) * gpost_ref[...]
        _row_tiles(TM_FFN, body)

    last = j == pl.num_programs(1) - 1

    @pl.when(jnp.logical_and(last, i < n_first))
    def _():
        finish(oa_ref)

    @pl.when(jnp.logical_and(last, i >= n_first))
    def _():
        finish(ob_ref)


def _ffn(h, n_tokens_first, gpre, gpost, wg, wu, wd):
    T = h.shape[0]
    n_first = n_tokens_first // TM_FFN
    nbytes = (2 * 3 * TM_FFN * D_MODEL * 4 + TM_FFN * D_MODEL * 2 + 2 * 3 * D_MODEL * TF_FFN * 2
              + 3 * TM_FFN * TF_FFN * 4 + 2 * TM_FFN * D_MODEL * 4)
    return pl.pallas_call(
        functools.partial(_ffn_kernel, n_first),
        grid=(T // TM_FFN, D_FF // TF_FFN),
        in_specs=[
            pl.BlockSpec((TM_FFN, D_MODEL), lambda i, j: (i, 0)),
            pl.BlockSpec((1, D_MODEL), lambda i, j: (0, 0)),
            pl.BlockSpec((1, D_MODEL), lambda i, j: (0, 0)),
            pl.BlockSpec((D_MODEL, TF_FFN), lambda i, j: (0, j)),
            pl.BlockSpec((D_MODEL, TF_FFN), lambda i, j: (0, j)),
            pl.BlockSpec((TF_FFN, D_MODEL), lambda i, j: (j, 0)),
        ],
        out_specs=list(_two_source_specs(TM_FFN, n_first)),
        out_shape=[jax.ShapeDtypeStruct((n_tokens_first, D_MODEL), F32),
                   jax.ShapeDtypeStruct((T - n_tokens_first, D_MODEL), F32)],
        scratch_shapes=[pltpu.VMEM((TM_FFN, D_MODEL), BF16), pltpu.VMEM((TM_FFN, D_MODEL), F32)],
        compiler_params=pltpu.CompilerParams(
            dimension_semantics=("arbitrary", "arbitrary"),
            vmem_limit_bytes=_vmem_limit(nbytes)),
        name="swiglu_ffn",
    )(h, gpre, gpost, wg, wu, wd)


def _rope_tables(max_len):
    pos = jnp.arange(max_len, dtype=F32)
    inv_freq = ROPE_THETA ** (-jnp.arange(ROPE_HALF, dtype=F32) / ROPE_HALF)
    ang = pos[:, None] * inv_freq[None, :]
    cos, sin = jnp.cos(ang), jnp.sin(ang)
    rest = HEAD_DIM - ROPE_DIM
    cos_t = jnp.concatenate([cos, cos, jnp.ones((max_len, rest), F32)], axis=1)
    sa_t = jnp.concatenate([-sin, jnp.zeros((max_len, HEAD_DIM - ROPE_HALF), F32)], axis=1)
    sb_t = jnp.concatenate([jnp.zeros((max_len, ROPE_HALF), F32), sin, jnp.zeros((max_len, rest), F32)], axis=1)
    return cos_t, sa_t, sb_t


def _seq_groups(seq_shapes, seg):
    groups, first = [], 0
    for B, L in seq_shapes:
        assert L % seg == 0
        groups.append((first, L // seg))
        first += B * L // seg
    return tuple(groups)


def _encoder_layer(xa, xb, seq_shapes, tables, w_in, sink, rec_norm, lb, w_out, norm_mix_pre, norm_mix_post,
                   norm_ffn_pre, norm_ffn_post, w_gate, w_up, w_down):
    row = lambda v: v.astype(F32).reshape(1, -1)
    xn = _stream_norm(xa, xb, row(norm_mix_pre))

    assert all(off % TN_IN == 0 for off in (OFF_K, OFF_QR, OFF_ZF, OFF_IR)) and 2 * D_KV == TN_IN
    gate_tiles = list(range(OFF_ZF // TN_IN, OFF_IR // TN_IN))
    slab_tiles = [t for t in range(D_IN // TN_IN) if t not in gate_tiles]
    kind_of = lambda t: ("rope" if t < OFF_K // TN_IN else "rope_k" if t == OFF_K // TN_IN
                         else "silu" if OFF_QR // TN_IN <= t < OFF_ZF // TN_IN else "plain")
    table_groups = _seq_groups(seq_shapes, TM_IN)
    slab = _in_projection(xn, w_in, tables, table_groups, slab_tiles, tuple(kind_of(t) for t in slab_tiles),
                          BF16, "in_projection")
    gates = _in_projection(xn, w_in, None, None, gate_tiles, ("plain",) * len(gate_tiles),
                           F32, "gate_projection")

    attn = _window_attention(slab, sink.astype(F32), _seq_groups(seq_shapes, TQ_ATT))
    rec_f, rec_b = _hgrn2(slab, gates, row(lb), _seq_groups(seq_shapes, SEG_REC))
    h = _out_projection(attn, rec_f, rec_b, slab, xa, xb, w_out.astype(BF16), row(rec_norm),
                        row(norm_mix_post))
    return _ffn(h, xa.shape[0], row(norm_ffn_pre), row(norm_ffn_post),
                w_gate.astype(BF16), w_up.astype(BF16), w_down.astype(BF16))


def kernel(x_prompt, x_sample, w_in, sink, rec_norm, lb_logits, w_out, norm_mix_pre, norm_mix_post,
           norm_ffn_pre, norm_ffn_post, w_gate, w_up, w_down):
    lb_all = jnp.cumsum(jax.nn.softmax(lb_logits.astype(F32), axis=0), axis=0)
    seq_shapes = (x_prompt.shape[:2], x_sample.shape[:2])
    xa = x_prompt.reshape(-1, D_MODEL)
    xb = x_sample.reshape(-1, D_MODEL)
    tables = _rope_tables(max(L for _, L in seq_shapes))
    for l in range(DEPTH):
        xa, xb = _encoder_layer(xa, xb, seq_shapes, tables, w_in[l], sink[l], rec_norm[l], lb_all[l], w_out[l],
                                norm_mix_pre[l], norm_mix_post[l], norm_ffn_pre[l], norm_ffn_post[l],
                                w_gate[l], w_up[l], w_down[l])
    return (xa.reshape(x_prompt.shape), xb.reshape(x_sample.shape))
```

```python
import functools

import jax
import jax.numpy as jnp
import numpy as np
from jax import lax
from jax.experimental import pallas as pl
from jax.experimental.pallas import tpu as pltpu

F32 = jnp.float32
BF16 = jnp.bfloat16

D_MODEL = 2048
DEPTH = 1
HEAD_DIM = 128
N_Q_HEADS = 16
N_KV_HEADS = 4
GROUP = N_Q_HEADS // N_KV_HEADS
WINDOW = 128
BLOCK = 128
ROPE_DIM = HEAD_DIM // 4
ROPE_HALF = ROPE_DIM // 2
ROPE_THETA = 500000.0
N_REC_HEADS = 16
REC_KEY_DIM = 128
REC_VAL_DIM = 128
CHUNK = 32
D_FF = -(-8 * D_MODEL // (3 * 256)) * 256
RMS_EPS = 1e-6
LOG2E = 1.4426950408889634
Q_SCALE = HEAD_DIM ** -0.5 * LOG2E

D_ATT = N_Q_HEADS * HEAD_DIM
D_KV = N_KV_HEADS * HEAD_DIM
D_REC_K = N_REC_HEADS * REC_KEY_DIM
D_REC_V = N_REC_HEADS * REC_VAL_DIM
SPLIT_SIZES = (D_ATT, D_KV, D_KV, D_REC_K, D_REC_K, D_REC_K, D_REC_V, D_REC_V, D_MODEL, D_MODEL)
D_IN = sum(SPLIT_SIZES)
(OFF_Q, OFF_K, OFF_V, OFF_QR, OFF_ZF, OFF_ZB, OFF_IR, OFF_GR, OFF_GA, OFF_GT) = (
    int(v) for v in np.concatenate([[0], np.cumsum(SPLIT_SIZES)[:-1]]))
D_GATES = OFF_IR - OFF_ZF
D_SLAB = D_IN - D_GATES
A_Q, A_K, A_V, A_QR = OFF_Q, OFF_K, OFF_V, OFF_QR
A_IR, A_GR, A_GA, A_GT = (OFF_IR - D_GATES, OFF_GR - D_GATES, OFF_GA - D_GATES, OFF_GT - D_GATES)

V7X_VMEM_CEILING = 56 * 1024 * 1024
LANES = 128

TM_NORM = 512
TM_IN, TN_IN = 1024, 1024
SUB_IN = 256
ROWS_EW = 256
SLAB_ATT = 32
TQ_ATT = 1024
SEG_REC = 512
HB_REC = 8
TM_OUT = 256
KC_OUT = 512
TM_FFN, TF_FFN = 512, 512


def _vmem_limit(nbytes):
    return int(min(V7X_VMEM_CEILING, nbytes * 1.25 + (4 << 20)))


def _sigmoid(x):
    return 1.0 / (1.0 + jnp.exp(-x))


def _dot_nt(a, b):
    return lax.dot_general(a, b, (((1,), (1,)), ((), ())), preferred_element_type=F32)


def _dot_tn(a, b):
    return lax.dot_general(a, b, (((0,), (0,)), ((), ())), preferred_element_type=F32)


def _rows(start, size):
    return pl.ds(start if isinstance(start, int) else pl.multiple_of(start, size), size)


def _row_tiles(n_rows, body):
    def step(r, c):
        body(_rows(r * ROWS_EW, ROWS_EW))
        return c
    lax.fori_loop(0, n_rows // ROWS_EW, step, 0)


def _segment_position(g, seq_groups):
    first, local, per = None, None, None
    for g0, n in reversed(seq_groups):
        loc = lax.rem(g - g0, n)
        fst = g - loc
        if first is None:
            first, local, per = fst, loc, n
        else:
            here = g < nxt
            first = jnp.where(here, fst, first)
            local = jnp.where(here, loc, local)
            per = jnp.where(here, n, per)
        nxt = g0
    return first, local, per


def _two_source_specs(tm, n_first):
    return (pl.BlockSpec((tm, D_MODEL), lambda i, *_: (jnp.minimum(i, n_first - 1), 0)),
            pl.BlockSpec((tm, D_MODEL), lambda i, *_: (jnp.maximum(i - n_first, 0), 0)))


def _norm_kernel(n_first, xa_ref, xb_ref, g_ref, o_ref):
    def run(x_ref):
        def body(rows):
            x = x_ref[rows, :]
            ms = jnp.mean(x * x, axis=-1, keepdims=True)
            o_ref[rows, :] = ((x * lax.rsqrt(ms + RMS_EPS)) * g_ref[...]).astype(o_ref.dtype)
        _row_tiles(TM_NORM, body)

    @pl.when(pl.program_id(0) < n_first)
    def _():
        run(xa_ref)

    @pl.when(pl.program_id(0) >= n_first)
    def _():
        run(xb_ref)


def _stream_norm(xa, xb, gain):
    T = xa.shape[0] + xb.shape[0]
    n_first = xa.shape[0] // TM_NORM
    nbytes = 2 * 2 * TM_NORM * D_MODEL * 4 + 2 * TM_NORM * D_MODEL * 2
    return pl.pallas_call(
        functools.partial(_norm_kernel, n_first),
        grid=(T // TM_NORM,),
        in_specs=[*_two_source_specs(TM_NORM, n_first), pl.BlockSpec((1, D_MODEL), lambda i: (0, 0))],
        out_specs=pl.BlockSpec((TM_NORM, D_MODEL), lambda i: (i, 0)),
        out_shape=jax.ShapeDtypeStruct((T, D_MODEL), BF16),
        compiler_params=pltpu.CompilerParams(
            dimension_semantics=("parallel",), vmem_limit_bytes=_vmem_limit(nbytes)),
        name="stream_norm",
    )(xa, xb, gain)


def _inproj_kernel(kinds, col_tiles_ref, *refs):
    del col_tiles_ref
    with_epilogue = any(k != "plain" for k in kinds)
    if with_epilogue:
        xn_ref, cos_ref, sa_ref, sb_ref, w_ref, o_ref, wb_ref, y_ref = refs
    else:
        xn_ref, w_ref, o_ref, wb_ref = refs
    j = pl.program_id(0)

    @pl.when(pl.program_id(1) == 0)
    def _():
        def body(rows):
            wb_ref[rows, :] = w_ref[rows, :].astype(BF16)
        _row_tiles(D_MODEL, body)

    def transform(kind, y, rows, lane_tile):
        if kind == "silu":
            return y * _sigmoid(y)
        if kind == "rope_k" and lane_tile >= N_KV_HEADS:
            return y
        y = (y * cos_ref[rows, :] + pltpu.roll(y, HEAD_DIM - ROPE_HALF, 1) * sa_ref[rows, :]
             + pltpu.roll(y, ROPE_HALF, 1) * sb_ref[rows, :])
        return y * Q_SCALE if kind == "rope" else y

    def epilogue(kind, sub):
        for r in range(TM_IN // ROWS_EW):
            rows = slice(r * ROWS_EW, (r + 1) * ROWS_EW)
            for cc in range(SUB_IN // LANES):
                lane_tile = sub * (SUB_IN // LANES) + cc
                y = y_ref[sub, rows, cc * LANES:(cc + 1) * LANES]
                o_ref[rows, lane_tile * LANES:(lane_tile + 1) * LANES] = (
                    transform(kind, y, rows, lane_tile).astype(o_ref.dtype))

    for kind in sorted(set(kinds)):
        tiles = [t for t, k in enumerate(kinds) if k == kind]
        cond = functools.reduce(jnp.logical_or, [j == t for t in tiles])

        @pl.when(cond)
        def _(kind=kind):
            if kind == "plain":
                o_ref[...] = jnp.dot(xn_ref[...], wb_ref[...], preferred_element_type=F32).astype(o_ref.dtype)
            else:
                n_sub = TN_IN // SUB_IN
                for sub in range(n_sub):
                    y_ref[sub] = jnp.dot(xn_ref[...], wb_ref[:, sub * SUB_IN:(sub + 1) * SUB_IN],
                                         preferred_element_type=F32)
                    if sub > 0:
                        epilogue(kind, sub - 1)
                epilogue(kind, n_sub - 1)


def _in_projection(xn, w_in, tables, table_groups, col_tiles, kinds, out_dtype, name):
    T = xn.shape[0]
    n_col = len(col_tiles)
    col_tiles = jnp.asarray(col_tiles, jnp.int32)
    with_epilogue = any(k != "plain" for k in kinds)
    last_table_tile = max([t for t, k in enumerate(kinds) if k.startswith("rope")], default=-1)
    out_bytes = jnp.dtype(out_dtype).itemsize

    def table_map(j, i, ct):
        _, local, _ = _segment_position(i, table_groups)
        return (jnp.where(j <= last_table_tile, local, 0), 0)

    in_specs = [pl.BlockSpec((TM_IN, D_MODEL), lambda j, i, ct: (i, 0))]
    args = [xn]
    scratch = [pltpu.VMEM((D_MODEL, TN_IN), BF16)]
    nbytes = (2 * TM_IN * D_MODEL * 2 + 2 * D_MODEL * TN_IN * 4 + D_MODEL * TN_IN * 2
              + 2 * TM_IN * TN_IN * out_bytes + TM_IN * TN_IN * 4)
    if with_epilogue:
        in_specs += [pl.BlockSpec((TM_IN, LANES), table_map)] * 3
        args += list(tables)
        scratch.append(pltpu.VMEM((TN_IN // SUB_IN, TM_IN, SUB_IN), F32))
        nbytes += 2 * 3 * TM_IN * LANES * 4 + TM_IN * TN_IN * 4
    in_specs.append(pl.BlockSpec((D_MODEL, TN_IN), lambda j, i, ct: (0, ct[j])))
    args.append(w_in)
    return pl.pallas_call(
        functools.partial(_inproj_kernel, kinds),
        grid_spec=pltpu.PrefetchScalarGridSpec(
            num_scalar_prefetch=1,
            grid=(n_col, T // TM_IN),
            in_specs=in_specs,
            out_specs=pl.BlockSpec((TM_IN, TN_IN), lambda j, i, ct: (i, j)),
            scratch_shapes=scratch),
        out_shape=jax.ShapeDtypeStruct((T, n_col * TN_IN), out_dtype),
        compiler_params=pltpu.CompilerParams(
            dimension_semantics=("arbitrary", "arbitrary"),
            vmem_limit_bytes=_vmem_limit(nbytes)),
        name=name,
    )(col_tiles, *args)


def _attn_kernel(seq_groups, sink_ref, q_ref, kc_ref, vc_ref, kp_ref, kn_ref, vp_ref, vn_ref,
                 ga_ref, o_ref, s_all, p_all, r_all):
    t = pl.program_id(0)
    h = pl.program_id(1)
    _, local, per = _segment_position(t, seq_groups)
    has_prev = local != 0
    has_next = local != per - 1

    k_all = jnp.concatenate([kp_ref[...], kc_ref[...], kn_ref[...]], axis=0)
    v_all = jnp.concatenate([vp_ref[...], vc_ref[...], vn_ref[...]], axis=0)
    qi = lax.broadcasted_iota(jnp.int32, (BLOCK, BLOCK), 0)
    kj = lax.broadcasted_iota(jnp.int32, (BLOCK, BLOCK), 1)
    neg = jnp.float32(-jnp.inf)
    bias_prev = jnp.where(kj >= qi, 0.0, neg)
    bias_next = jnp.where(kj <= qi, 0.0, neg)
    bias_prev0 = jnp.where(has_prev, bias_prev, neg)
    bias_next_last = jnp.where(has_next, bias_next, neg)
    n_blocks = TQ_ATT // BLOCK
    for b in range(n_blocks):
        rs = slice(b * BLOCK, (b + 1) * BLOCK)
        kb = k_all[b * BLOCK:(b + 3) * BLOCK]
        vb = v_all[b * BLOCK:(b + 3) * BLOCK]
        bp = bias_prev0 if b == 0 else bias_prev
        bn = bias_next_last if b == n_blocks - 1 else bias_next
        heads = [slice(g * HEAD_DIM, (g + 1) * HEAD_DIM) for g in range(GROUP)]
        s_ref, p_ref, r_ref = s_all.at[b], p_all.at[b], r_all.at[b]
        q4 = jnp.concatenate([q_ref[rs, cs] for cs in heads], axis=0)
        s_ref[...] = _dot_nt(q4, kb).reshape(GROUP, BLOCK, 3 * BLOCK)
        for g, cs in enumerate(heads):
            sink2 = sink_ref[GROUP * h + g] * LOG2E
            for r in range(BLOCK // SLAB_ATT):
                rr = slice(r * SLAB_ATT, (r + 1) * SLAB_ATT)
                s = s_ref[g, rr, :]
                s_p = s[:, :BLOCK] + bp[rr]
                s_c = s[:, BLOCK:2 * BLOCK]
                s_n = s[:, 2 * BLOCK:] + bn[rr]
                m = jnp.maximum(jnp.max(jnp.maximum(jnp.maximum(s_p, s_c), s_n), axis=-1, keepdims=True), sink2)
                p_p, p_c, p_n = jnp.exp2(s_p - m), jnp.exp2(s_c - m), jnp.exp2(s_n - m)
                denom = jnp.sum(p_p + p_c + p_n, axis=-1, keepdims=True) + jnp.exp2(sink2 - m)
                p_ref[g, rr, :] = jnp.concatenate([p_p, p_c, p_n], axis=1).astype(BF16)
                r_ref[g, rr, :] = jnp.broadcast_to(1.0 / denom, (SLAB_ATT, LANES))
        o4 = jnp.dot(p_ref[...].reshape(GROUP * BLOCK, 3 * BLOCK), vb, preferred_element_type=F32)
        for g, cs in enumerate(heads):
            o = o4[g * BLOCK:(g + 1) * BLOCK] * r_ref[g]
            o_ref[rs, cs] = (_sigmoid(ga_ref[rs, cs].astype(F32)) * o).astype(o_ref.dtype)


def _window_attention(slab, sink, seq_groups):
    T = slab.shape[0]
    n_tiles = T // TQ_ATT
    bpt = TQ_ATT // BLOCK
    last_blk = T // BLOCK - 1
    gw = GROUP * HEAD_DIM
    kcol, vcol, gcol = A_K // HEAD_DIM, A_V // HEAD_DIM, A_GA // gw
    nbytes = (2 * (3 * TQ_ATT * gw * 2 + 2 * TQ_ATT * HEAD_DIM * 2 + 4 * BLOCK * HEAD_DIM * 2)
              + bpt * GROUP * BLOCK * (3 * BLOCK * 6 + LANES * 4))
    return pl.pallas_call(
        functools.partial(_attn_kernel, seq_groups),
        grid=(n_tiles, N_KV_HEADS),
        in_specs=[
            pl.BlockSpec(memory_space=pltpu.SMEM),
            pl.BlockSpec((TQ_ATT, gw), lambda t, h: (t, h)),
            pl.BlockSpec((TQ_ATT, HEAD_DIM), lambda t, h: (t, kcol + h)),
            pl.BlockSpec((TQ_ATT, HEAD_DIM), lambda t, h: (t, vcol + h)),
            pl.BlockSpec((BLOCK, HEAD_DIM), lambda t, h: (jnp.maximum(t * bpt - 1, 0), kcol + h)),
            pl.BlockSpec((BLOCK, HEAD_DIM), lambda t, h: (jnp.minimum((t + 1) * bpt, last_blk), kcol + h)),
            pl.BlockSpec((BLOCK, HEAD_DIM), lambda t, h: (jnp.maximum(t * bpt - 1, 0), vcol + h)),
            pl.BlockSpec((BLOCK, HEAD_DIM), lambda t, h: (jnp.minimum((t + 1) * bpt, last_blk), vcol + h)),
            pl.BlockSpec((TQ_ATT, gw), lambda t, h: (t, gcol + h)),
        ],
        out_specs=pl.BlockSpec((TQ_ATT, gw), lambda t, h: (t, h)),
        out_shape=jax.ShapeDtypeStruct((T, D_ATT), BF16),
        scratch_shapes=[pltpu.VMEM((bpt, GROUP, BLOCK, 3 * BLOCK), F32),
                        pltpu.VMEM((bpt, GROUP, BLOCK, 3 * BLOCK), BF16),
                        pltpu.VMEM((bpt, GROUP, BLOCK, LANES), F32)],
        compiler_params=pltpu.CompilerParams(
            dimension_semantics=("parallel", "arbitrary"),
            vmem_limit_bytes=_vmem_limit(nbytes)),
        name="window_attention",
    )(sink, slab, slab, slab, slab, slab, slab, slab, slab)


def _hgrn_kernel(seq_groups, lb_ref, qf_ref, if_ref, zf_ref, qb_ref, ib_ref, zb_ref,
                 of_ref, ob_ref, st_ref, stb_ref, ds_ref, ops_ref, cross_ref, dec_ref, a_ref):
    g = pl.program_id(1)
    _, local, _ = _segment_position(g, seq_groups)

    @pl.when(local == 0)
    def _():
        st_ref[...] = jnp.zeros(st_ref.shape, F32)
        stb_ref[...] = jnp.zeros(stb_ref.shape, BF16)

    Q_IN, K_IN, Q_ST, K_ST, I_C = range(5)
    pair = 2 * CHUNK
    n_pairs = SEG_REC // pair
    rin = lax.broadcasted_iota(jnp.int32, (CHUNK, LANES), 0)
    ti = lax.broadcasted_iota(jnp.int32, (CHUNK, CHUNK), 0)
    si = lax.broadcasted_iota(jnp.int32, (CHUNK, CHUNK), 1)
    half = (slice(0, CHUNK), slice(CHUNK, pair))

    def prepare(d, hh, start, q_ref, i_ref, z_ref):
        cs = slice(hh * LANES, (hh + 1) * LANES)
        lb = lb_ref[:, cs]
        one_m_lb = 1.0 - lb
        q_in, k_in, k_st, q_dc, dec = [], [], [], [], []
        for t in range(2):
            rows = _rows(start + t * CHUNK, CHUNK)
            q = q_ref[rows, cs].astype(F32)
            gate = one_m_lb * _sigmoid(z_ref[rows, cs])
            k = one_m_lb - gate
            logf = jnp.log2(lb + gate)
            b = logf
            for s in (1, 2, 4, 8, 16):
                b = b + jnp.where(rin >= s, pltpu.roll(b, s, 0), 0.0)
            tot = b[CHUNK - 1:CHUNK, :]
            if d == 0:
                c = b
                cref = b[CHUNK // 2 - 1:CHUNK // 2, :]
            else:
                c = tot - b + logf
                cref = c[CHUNK // 2:CHUNK // 2 + 1, :]
            q_in.append(q * jnp.exp2(c - cref))
            k_in.append(k * jnp.exp2(cref - c))
            k_st.append(k_in[t] * jnp.exp2(tot - cref))
            q_dc.append(q_in[t] * jnp.exp2(cref))
            dec.append(jnp.exp2(tot))
            ops_ref[d, hh, Q_IN, half[t]] = q_in[t].astype(BF16)
            ops_ref[d, hh, K_IN, half[t]] = k_in[t].astype(BF16)
            ops_ref[d, hh, I_C, half[t]] = i_ref[rows, cs]
        ca, cb = (0, 1) if d == 0 else (1, 0)
        ops_ref[d, hh, Q_ST, half[ca]] = q_dc[ca].astype(BF16)
        ops_ref[d, hh, Q_ST, half[cb]] = (q_dc[cb] * dec[ca]).astype(BF16)
        ops_ref[d, hh, K_ST, half[ca]] = (k_st[ca] * dec[cb]).astype(BF16)
        ops_ref[d, hh, K_ST, half[cb]] = k_st[cb].astype(BF16)
        cross_ref[d, hh, 0] = q_dc[cb].astype(BF16)
        cross_ref[d, hh, 1] = k_st[ca].astype(BF16)
        dec_ref[d, hh] = jnp.broadcast_to(dec[0] * dec[1], (8, LANES))

    def intra(d, hh):
        for t in range(2):
            a_ref[d, hh, t] = _dot_nt(ops_ref[d, hh, Q_IN, half[t]], ops_ref[d, hh, K_IN, half[t]])
        a_ref[d, hh, 2] = _dot_nt(cross_ref[d, hh, 0], cross_ref[d, hh, 1])

    def output(d, hh, start, o_ref):
        cs = slice(hh * LANES, (hh + 1) * LANES)
        tri = (ti >= si) if d == 0 else (ti <= si)
        a_top = jnp.where(tri, a_ref[d, hh, 0], 0.0).astype(BF16)
        a_bot = jnp.where(tri, a_ref[d, hh, 1], 0.0).astype(BF16)
        a_x = a_ref[d, hh, 2].astype(BF16)
        i_top, i_bot = ops_ref[d, hh, I_C, half[0]], ops_ref[d, hh, I_C, half[1]]
        inter = jnp.dot(ops_ref[d, hh, Q_ST], stb_ref[d, hh], preferred_element_type=F32)
        if d == 0:
            both = jnp.dot(jnp.concatenate([a_top, a_x], axis=0), i_top, preferred_element_type=F32)
            o_top = both[:CHUNK] + inter[:CHUNK]
            o_bot = both[CHUNK:] + jnp.dot(a_bot, i_bot, preferred_element_type=F32) + inter[CHUNK:]
        else:
            both = jnp.dot(jnp.concatenate([a_x, a_bot], axis=0), i_bot, preferred_element_type=F32)
            o_top = both[:CHUNK] + jnp.dot(a_top, i_top, preferred_element_type=F32) + inter[:CHUNK]
            o_bot = both[CHUNK:] + inter[CHUNK:]
        o_ref[_rows(start, CHUNK), cs] = o_top.astype(o_ref.dtype)
        o_ref[_rows(start + CHUNK, CHUNK), cs] = o_bot.astype(o_ref.dtype)
        ds_ref[d, hh] = _dot_tn(ops_ref[d, hh, I_C], ops_ref[d, hh, K_ST])

    def update(d, hh):
        dec = dec_ref[d, hh, 0:1, :]
        for r in range(REC_VAL_DIM // CHUNK):
            rr = slice(r * CHUNK, (r + 1) * CHUNK)
            new = dec * st_ref[d, hh, rr, :] + ds_ref[d, hh, rr, :]
            st_ref[d, hh, rr, :] = new
        stb_ref[d, hh] = st_ref[d, hh].T.astype(BF16)

    def scan(n, carry):
        start_f = pl.multiple_of(n * pair, pair)
        start_b = pl.multiple_of((n_pairs - 1 - n) * pair, pair)
        for hh in range(HB_REC):
            prepare(0, hh, start_f, qf_ref, if_ref, zf_ref)
            prepare(1, hh, start_b, qb_ref, ib_ref, zb_ref)
        for hh in range(HB_REC):
            intra(0, hh)
            intra(1, hh)
        for hh in range(HB_REC):
            output(0, hh, start_f, of_ref)
            output(1, hh, start_b, ob_ref)
        for hh in range(HB_REC):
            update(0, hh)
            update(1, hh)
        return carry

    lax.fori_loop(0, n_pairs, scan, 0)


def _hgrn2(slab, gates, lb_row, seq_groups):
    T = slab.shape[0]
    w = HB_REC * LANES

    def mirror(g):
        first, local, per = _segment_position(g, seq_groups)
        return first + per - 1 - local

    def fspec(off):
        return pl.BlockSpec((SEG_REC, w), lambda hb, g: (g, off // w + hb))

    def bspec(off):
        return pl.BlockSpec((SEG_REC, w), lambda hb, g: (mirror(g), off // w + hb))

    state = (2, HB_REC, REC_VAL_DIM, REC_KEY_DIM)
    nbytes = (2 * (4 * SEG_REC * w * 2 + 2 * SEG_REC * w * 4 + 2 * SEG_REC * w * 2)
              + 2 * HB_REC * (REC_VAL_DIM * REC_KEY_DIM * 10 + 12 * CHUNK * LANES * 2 + 8 * LANES * 4))
    out = jax.ShapeDtypeStruct((T, D_REC_V), BF16)
    return pl.pallas_call(
        functools.partial(_hgrn_kernel, seq_groups),
        grid=(N_REC_HEADS // HB_REC, T // SEG_REC),
        in_specs=[
            pl.BlockSpec((1, w), lambda hb, g: (0, hb)),
            fspec(A_QR), fspec(A_IR), fspec(0),
            bspec(A_QR), bspec(A_IR), bspec(D_REC_K),
        ],
        out_specs=[pl.BlockSpec((SEG_REC, w), lambda hb, g: (g, hb)),
                   pl.BlockSpec((SEG_REC, w), lambda hb, g: (mirror(g), hb))],
        out_shape=[out, out],
        scratch_shapes=[pltpu.VMEM(state, F32), pltpu.VMEM(state, BF16), pltpu.VMEM(state, F32),
                        pltpu.VMEM((2, HB_REC, 5, 2 * CHUNK, LANES), BF16),
                        pltpu.VMEM((2, HB_REC, 2, CHUNK, LANES), BF16),
                        pltpu.VMEM((2, HB_REC, 8, LANES), F32),
                        pltpu.VMEM((2, HB_REC, 3, CHUNK, CHUNK), F32)],
        compiler_params=pltpu.CompilerParams(
            dimension_semantics=("parallel", "arbitrary"),
            vmem_limit_bytes=_vmem_limit(nbytes)),
        name="hgrn2",
    )(lb_row, slab, slab, gates, slab, slab, gates)


def _outproj_kernel(n_first, a_ref, of_ref, ob_ref, grl_ref, grh_ref, gtl_ref, gth_ref, xa_ref, xb_ref,
                    w_ref, rg_ref, g_ref, h_ref, m_ref):
    half = D_REC_V // 2
    heads_per_chunk = KC_OUT // REC_VAL_DIM
    y = None
    for c in range(D_MODEL // KC_OUT):
        for hh in range(c * heads_per_chunk, (c + 1) * heads_per_chunk):
            cs = slice(hh * REC_VAL_DIM, (hh + 1) * REC_VAL_DIM)
            lo = hh * REC_VAL_DIM < half
            hs = cs if lo else slice(hh * REC_VAL_DIM - half, (hh + 1) * REC_VAL_DIM - half)
            rec = of_ref[:, cs].astype(F32) + ob_ref[:, cs].astype(F32)
            ms = jnp.mean(rec * rec, axis=-1, keepdims=True)
            r = (rec * lax.rsqrt(ms + RMS_EPS)) * rg_ref[:, cs]
            gr = (grl_ref if lo else grh_ref)[:, hs].astype(F32)
            gt = (gtl_ref if lo else gth_ref)[:, hs].astype(F32)
            r = _sigmoid(gt) * (r * (gr * _sigmoid(gr)))
            m_ref[:, cs] = (a_ref[:, cs].astype(F32) + r).astype(BF16)
        ks = slice(c * KC_OUT, (c + 1) * KC_OUT)
        part = jnp.dot(m_ref[:, ks], w_ref[ks, :], preferred_element_type=F32)
        y = part if y is None else y + part
    ms = jnp.mean(y * y, axis=-1, keepdims=True)
    h_ref[...] = (y * lax.rsqrt(ms + RMS_EPS)) * g_ref[...]

    @pl.when(pl.program_id(0) < n_first)
    def _():
        h_ref[...] += xa_ref[...]

    @pl.when(pl.program_id(0) >= n_first)
    def _():
        h_ref[...] += xb_ref[...]


def _out_projection(attn, rec_f, rec_b, slab, xa, xb, w_bf16, rec_gain, gain):
    T = attn.shape[0]
    n_first = xa.shape[0] // TM_OUT
    half = D_REC_V // 2
    assert A_GR % half == 0 and A_GT % half == 0
    tok = lambda i: (i, 0)
    nbytes = (2 * 3 * TM_OUT * D_MODEL * 2 + 2 * 4 * TM_OUT * half * 2 + 2 * 3 * TM_OUT * D_MODEL * 4
              + 2 * D_MODEL * D_MODEL * 2 + TM_OUT * D_MODEL * 2 + 2 * TM_OUT * D_MODEL * 4)
    return pl.pallas_call(
        functools.partial(_outproj_kernel, n_first),
        grid=(T // TM_OUT,),
        in_specs=[
            pl.BlockSpec((TM_OUT, D_MODEL), tok),
            pl.BlockSpec((TM_OUT, D_REC_V), tok),
            pl.BlockSpec((TM_OUT, D_REC_V), tok),
            pl.BlockSpec((TM_OUT, half), lambda i: (i, A_GR // half)),
            pl.BlockSpec((TM_OUT, half), lambda i: (i, A_GR // half + 1)),
            pl.BlockSpec((TM_OUT, half), lambda i: (i, A_GT // half)),
            pl.BlockSpec((TM_OUT, half), lambda i: (i, A_GT // half + 1)),
            *_two_source_specs(TM_OUT, n_first),
            pl.BlockSpec((D_MODEL, D_MODEL), lambda i: (0, 0)),
            pl.BlockSpec((1, D_REC_V), lambda i: (0, 0)),
            pl.BlockSpec((1, D_MODEL), lambda i: (0, 0)),
        ],
        out_specs=pl.BlockSpec((TM_OUT, D_MODEL), tok),
        out_shape=jax.ShapeDtypeStruct((T, D_MODEL), F32),
        scratch_shapes=[pltpu.VMEM((TM_OUT, D_MODEL), BF16)],
        compiler_params=pltpu.CompilerParams(
            dimension_semantics=("parallel",),
            vmem_limit_bytes=_vmem_limit(nbytes)),
        name="out_projection",
    )(attn, rec_f, rec_b, slab, slab, slab, slab, xa, xb, w_bf16, rec_gain, gain)


def _ffn_kernel(n_first, h_ref, gpre_ref, gpost_ref, wg_ref, wu_ref, wd_ref, oa_ref, ob_ref, hn_ref, acc_ref):
    i = pl.program_id(0)
    j = pl.program_id(1)

    @pl.when(j == 0)
    def _():
        def body(rows):
            h = h_ref[rows, :]
            ms = jnp.mean(h * h, axis=-1, keepdims=True)
            hn_ref[rows, :] = ((h * lax.rsqrt(ms + RMS_EPS)) * gpre_ref[...]).astype(BF16)
            acc_ref[rows, :] = jnp.zeros((ROWS_EW, D_MODEL), F32)
        _row_tiles(TM_FFN, body)

    hn = hn_ref[...]
    g = jnp.dot(hn, wg_ref[...], preferred_element_type=F32)
    u = jnp.dot(hn, wu_ref[...], preferred_element_type=F32)
    act = ((g * _sigmoid(g)) * u).astype(BF16)
    acc_ref[...] += jnp.dot(act, wd_ref[...], preferred_element_type=F32)

    def finish(o_ref):
        def body(rows):
            y = acc_ref[rows, :]
            ms = jnp.mean(y * y, axis=-1, keepdims=True)
            o_ref[rows, :] = h_ref[rows, :] + (y * lax.rsqrt(ms + RMS_EPS)) * gpost_ref[...]
        _row_tiles(TM_FFN, body)

    last = j == pl.num_programs(1) - 1

    @pl.when(jnp.logical_and(last, i < n_first))
    def _():
        finish(oa_ref)

    @pl.when(jnp.logical_and(last, i >= n_first))
    def _():
        finish(ob_ref)


def _ffn(h, n_tokens_first, gpre, gpost, wg, wu, wd):
    T = h.shape[0]
    n_first = n_tokens_first // TM_FFN
    nbytes = (2 * 3 * TM_FFN * D_MODEL * 4 + TM_FFN * D_MODEL * 2 + 2 * 3 * D_MODEL * TF_FFN * 2
              + 3 * TM_FFN * TF_FFN * 4 + 2 * TM_FFN * D_MODEL * 4)
    return pl.pallas_call(
        functools.partial(_ffn_kernel, n_first),
        grid=(T // TM_FFN, D_FF // TF_FFN),
        in_specs=[
            pl.BlockSpec((TM_FFN, D_MODEL), lambda i, j: (i, 0)),
            pl.BlockSpec((1, D_MODEL), lambda i, j: (0, 0)),
            pl.BlockSpec((1, D_MODEL), lambda i, j: (0, 0)),
            pl.BlockSpec((D_MODEL, TF_FFN), lambda i, j: (0, j)),
            pl.BlockSpec((D_MODEL, TF_FFN), lambda i, j: (0, j)),
            pl.BlockSpec((TF_FFN, D_MODEL), lambda i, j: (j, 0)),
        ],
        out_specs=list(_two_source_specs(TM_FFN, n_first)),
        out_shape=[jax.ShapeDtypeStruct((n_tokens_first, D_MODEL), F32),
                   jax.ShapeDtypeStruct((T - n_tokens_first, D_MODEL), F32)],
        scratch_shapes=[pltpu.VMEM((TM_FFN, D_MODEL), BF16), pltpu.VMEM((TM_FFN, D_MODEL), F32)],
        compiler_params=pltpu.CompilerParams(
            dimension_semantics=("arbitrary", "arbitrary"),
            vmem_limit_bytes=_vmem_limit(nbytes)),
        name="swiglu_ffn",
    )(h, gpre, gpost, wg, wu, wd)


def _rope_tables(max_len):
    pos = jnp.arange(max_len, dtype=F32)
    inv_freq = ROPE_THETA ** (-jnp.arange(ROPE_HALF, dtype=F32) / ROPE_HALF)
    ang = pos[:, None] * inv_freq[None, :]
    cos, sin = jnp.cos(ang), jnp.sin(ang)
    rest = HEAD_DIM - ROPE_DIM
    cos_t = jnp.concatenate([cos, cos, jnp.ones((max_len, rest), F32)], axis=1)
    sa_t = jnp.concatenate([-sin, jnp.zeros((max_len, HEAD_DIM - ROPE_HALF), F32)], axis=1)
    sb_t = jnp.concatenate([jnp.zeros((max_len, ROPE_HALF), F32), sin, jnp.zeros((max_len, rest), F32)], axis=1)
    return cos_t, sa_t, sb_t


def _seq_groups(seq_shapes, seg):
    groups, first = [], 0
    for B, L in seq_shapes:
        assert L % seg == 0
        groups.append((first, L // seg))
        first += B * L // seg
    return tuple(groups)


def _encoder_layer(xa, xb, seq_shapes, tables, w_in, sink, rec_norm, lb, w_out, norm_mix_pre, norm_mix_post,
                   norm_ffn_pre, norm_ffn_post, w_gate, w_up, w_down):
    row = lambda v: v.astype(F32).reshape(1, -1)
    xn = _stream_norm(xa, xb, row(norm_mix_pre))

    assert all(off % TN_IN == 0 for off in (OFF_K, OFF_QR, OFF_ZF, OFF_IR)) and 2 * D_KV == TN_IN
    gate_tiles = list(range(OFF_ZF // TN_IN, OFF_IR // TN_IN))
    slab_tiles = [t for t in range(D_IN // TN_IN) if t not in gate_tiles]
    kind_of = lambda t: ("rope" if t < OFF_K // TN_IN else "rope_k" if t == OFF_K // TN_IN
                         else "silu" if OFF_QR // TN_IN <= t < OFF_ZF // TN_IN else "plain")
    table_groups = _seq_groups(seq_shapes, TM_IN)
    slab = _in_projection(xn, w_in, tables, table_groups, slab_tiles, tuple(kind_of(t) for t in slab_tiles),
                          BF16, "in_projection")
    gates = _in_projection(xn, w_in, None, None, gate_tiles, ("plain",) * len(gate_tiles),
                           F32, "gate_projection")

    attn = _window_attention(slab, sink.astype(F32), _seq_groups(seq_shapes, TQ_ATT))
    rec_f, rec_b = _hgrn2(slab, gates, row(lb), _seq_groups(seq_shapes, SEG_REC))
    h = _out_projection(attn, rec_f, rec_b, slab, xa, xb, w_out.astype(BF16), row(rec_norm),
                        row(norm_mix_post))
    return _ffn(h, xa.shape[0], row(norm_ffn_pre), row(norm_ffn_post),
                w_gate.astype(BF16), w_up.astype(BF16), w_down.astype(BF16))


def kernel(x_prompt, x_sample, w_in, sink, rec_norm, lb_logits, w_out, norm_mix_pre, norm_mix_post,
           norm_ffn_pre, norm_ffn_post, w_gate, w_up, w_down):
    lb_all = jnp.cumsum(jax.nn.softmax(lb_logits.astype(F32), axis=0), axis=0)
    seq_shapes = (x_prompt.shape[:2], x_sample.shape[:2])
    xa = x_prompt.reshape(-1, D_MODEL)
    xb = x_sample.reshape(-1, D_MODEL)
    tables = _rope_tables(max(L for _, L in seq_shapes))
    for l in range(DEPTH):
        xa, xb = _encoder_layer(xa, xb, seq_shapes, tables, w_in[l], sink[l], rec_norm[l], lb_all[l], w_out[l],
                                norm_mix_pre[l], norm_mix_post[l], norm_ffn_pre[l], norm_ffn_post[l],
                                w_gate[l], w_up[l], w_down[l])
    return (xa.reshape(x_prompt.shape), xb.reshape(x_sample.shape))
```

```python
import functools

import jax
import jax.numpy as jnp
import numpy as np
from jax import lax
from jax.experimental import pallas as pl
from jax.experimental.pallas import tpu as pltpu

F32 = jnp.float32
BF16 = jnp.bfloat16

D_MODEL = 2048
DEPTH = 1
HEAD_DIM = 128
N_Q_HEADS = 16
N_KV_HEADS = 4
GROUP = N_Q_HEADS // N_KV_HEADS
WINDOW = 128
BLOCK = 128
ROPE_DIM = HEAD_DIM // 4
ROPE_HALF = ROPE_DIM // 2
ROPE_THETA = 500000.0
N_REC_HEADS = 16
REC_KEY_DIM = 128
REC_VAL_DIM = 128
CHUNK = 32
D_FF = -(-8 * D_MODEL // (3 * 256)) * 256
RMS_EPS = 1e-6
LOG2E = 1.4426950408889634
Q_SCALE = HEAD_DIM ** -0.5 * LOG2E

D_ATT = N_Q_HEADS * HEAD_DIM
D_KV = N_KV_HEADS * HEAD_DIM
D_REC_K = N_REC_HEADS * REC_KEY_DIM
D_REC_V = N_REC_HEADS * REC_VAL_DIM
SPLIT_SIZES = (D_ATT, D_KV, D_KV, D_REC_K, D_REC_K, D_REC_K, D_REC_V, D_REC_V, D_MODEL, D_MODEL)
D_IN = sum(SPLIT_SIZES)
(OFF_Q, OFF_K, OFF_V, OFF_QR, OFF_ZF, OFF_ZB, OFF_IR, OFF_GR, OFF_GA, OFF_GT) = (
    int(v) for v in np.concatenate([[0], np.cumsum(SPLIT_SIZES)[:-1]]))
SLAB_GROUPS = ((OFF_Q, D_ATT), (OFF_QR, D_REC_K), (OFF_IR, D_REC_V), (OFF_GR, D_REC_V), (OFF_GA, D_MODEL),
               (OFF_GT, D_MODEL), (OFF_K, 2 * D_KV))
A_Q, A_QR, A_IR, A_GR, A_GA, A_GT, A_K = (int(v) for v in np.cumsum([0] + [w for _, w in SLAB_GROUPS])[:-1])
A_V = A_K + D_KV

V7X_VMEM_CEILING = 56 * 1024 * 1024
LANES = 128

TM_NORM = 512
TM_IN, TN_IN = 1024, 1024
SUB_IN = 256
ROWS_EW = 256
SLAB_ATT = 32
TQ_ATT = 1024
SEG_REC = 512
HB_REC = 16
TM_OUT = 256
KC_OUT = 512
TM_FFN, TF_FFN = 512, 512


def _vmem_limit(nbytes):
    return int(min(V7X_VMEM_CEILING, nbytes * 1.25 + (4 << 20)))


def _sigmoid(x):
    return 1.0 / (1.0 + jnp.exp(-x))


def _dot_nt(a, b):
    return lax.dot_general(a, b, (((1,), (1,)), ((), ())), preferred_element_type=F32)


def _dot_tn(a, b):
    return lax.dot_general(a, b, (((0,), (0,)), ((), ())), preferred_element_type=F32)


def _rows(start, size):
    return pl.ds(start if isinstance(start, int) else pl.multiple_of(start, size), size)


def _row_tiles(n_rows, body):
    def step(r, c):
        body(_rows(r * ROWS_EW, ROWS_EW))
        return c
    lax.fori_loop(0, n_rows // ROWS_EW, step, 0)


def _segment_position(g, seq_groups):
    first, local, per = None, None, None
    for g0, n in reversed(seq_groups):
        loc = lax.rem(g - g0, n)
        fst = g - loc
        if first is None:
            first, local, per = fst, loc, n
        else:
            here = g < nxt
            first = jnp.where(here, fst, first)
            local = jnp.where(here, loc, local)
            per = jnp.where(here, n, per)
        nxt = g0
    return first, local, per


def _two_source_specs(tm, n_first):
    return (pl.BlockSpec((tm, D_MODEL), lambda i, *_: (jnp.minimum(i, n_first - 1), 0)),
            pl.BlockSpec((tm, D_MODEL), lambda i, *_: (jnp.maximum(i - n_first, 0), 0)))


def _norm_kernel(n_first, xa_ref, xb_ref, g_ref, o_ref):
    def run(x_ref):
        def body(rows):
            x = x_ref[rows, :]
            ms = jnp.mean(x * x, axis=-1, keepdims=True)
            o_ref[rows, :] = ((x * lax.rsqrt(ms + RMS_EPS)) * g_ref[...]).astype(o_ref.dtype)
        _row_tiles(TM_NORM, body)

    @pl.when(pl.program_id(0) < n_first)
    def _():
        run(xa_ref)

    @pl.when(pl.program_id(0) >= n_first)
    def _():
        run(xb_ref)


def _stream_norm(xa, xb, gain):
    T = xa.shape[0] + xb.shape[0]
    n_first = xa.shape[0] // TM_NORM
    nbytes = 2 * 2 * TM_NORM * D_MODEL * 4 + 2 * TM_NORM * D_MODEL * 2
    return pl.pallas_call(
        functools.partial(_norm_kernel, n_first),
        grid=(T // TM_NORM,),
        in_specs=[*_two_source_specs(TM_NORM, n_first), pl.BlockSpec((1, D_MODEL), lambda i: (0, 0))],
        out_specs=pl.BlockSpec((TM_NORM, D_MODEL), lambda i: (i, 0)),
        out_shape=jax.ShapeDtypeStruct((T, D_MODEL), BF16),
        compiler_params=pltpu.CompilerParams(
            dimension_semantics=("parallel",), vmem_limit_bytes=_vmem_limit(nbytes)),
        name="stream_norm",
    )(xa, xb, gain)


def _inproj_kernel(kinds, col_tiles_ref, *refs):
    del col_tiles_ref
    with_epilogue = any(k != "plain" for k in kinds)
    if with_epilogue:
        xn_ref, cos_ref, sa_ref, sb_ref, w_ref, o_ref, wb_ref, y_ref = refs
    else:
        xn_ref, w_ref, o_ref, wb_ref = refs
    j = pl.program_id(0)

    @pl.when(pl.program_id(1) == 0)
    def _():
        def body(rows):
            wb_ref[rows, :] = w_ref[rows, :].astype(BF16)
        _row_tiles(D_MODEL, body)

    def transform(kind, y, rows, lane_tile):
        if kind == "silu":
            return y * _sigmoid(y)
        if kind == "rope_k" and lane_tile >= N_KV_HEADS:
            return y
        y = (y * cos_ref[rows, :] + pltpu.roll(y, HEAD_DIM - ROPE_HALF, 1) * sa_ref[rows, :]
             + pltpu.roll(y, ROPE_HALF, 1) * sb_ref[rows, :])
        return y * Q_SCALE if kind == "rope" else y

    def epilogue(kind, sub):
        for r in range(TM_IN // ROWS_EW):
            rows = slice(r * ROWS_EW, (r + 1) * ROWS_EW)
            for cc in range(SUB_IN // LANES):
                lane_tile = sub * (SUB_IN // LANES) + cc
                y = y_ref[sub, rows, cc * LANES:(cc + 1) * LANES]
                o_ref[rows, lane_tile * LANES:(lane_tile + 1) * LANES] = (
                    transform(kind, y, rows, lane_tile).astype(o_ref.dtype))

    for kind in sorted(set(kinds)):
        tiles = [t for t, k in enumerate(kinds) if k == kind]
        cond = functools.reduce(jnp.logical_or, [j == t for t in tiles])

        @pl.when(cond)
        def _(kind=kind):
            if kind == "plain":
                o_ref[...] = jnp.dot(xn_ref[...], wb_ref[...], preferred_element_type=F32).astype(o_ref.dtype)
            else:
                n_sub = TN_IN // SUB_IN
                for sub in range(n_sub):
                    y_ref[sub] = jnp.dot(xn_ref[...], wb_ref[:, sub * SUB_IN:(sub + 1) * SUB_IN],
                                         preferred_element_type=F32)
                    if sub > 0:
                        epilogue(kind, sub - 1)
                epilogue(kind, n_sub - 1)


def _in_projection(xn, w_in, tables, table_groups, col_tiles, kinds, out_dtype, name):
    T = xn.shape[0]
    n_col = len(col_tiles)
    col_tiles = jnp.asarray(col_tiles, jnp.int32)
    with_epilogue = any(k != "plain" for k in kinds)
    rope_tiles = [t for t, k in enumerate(kinds) if k.startswith("rope")]
    out_bytes = jnp.dtype(out_dtype).itemsize

    def table_map(j, i, ct):
        _, local, _ = _segment_position(i, table_groups)
        uses_tables = functools.reduce(jnp.logical_or, [j == t for t in rope_tiles])
        return (jnp.where(uses_tables, local, 0), 0)

    in_specs = [pl.BlockSpec((TM_IN, D_MODEL), lambda j, i, ct: (i, 0))]
    args = [xn]
    scratch = [pltpu.VMEM((D_MODEL, TN_IN), BF16)]
    nbytes = (2 * TM_IN * D_MODEL * 2 + 2 * D_MODEL * TN_IN * 4 + D_MODEL * TN_IN * 2
              + 2 * TM_IN * TN_IN * out_bytes + TM_IN * TN_IN * 4)
    if with_epilogue:
        in_specs += [pl.BlockSpec((TM_IN, LANES), table_map)] * 3
        args += list(tables)
        scratch.append(pltpu.VMEM((TN_IN // SUB_IN, TM_IN, SUB_IN), F32))
        nbytes += 2 * 3 * TM_IN * LANES * 4 + TM_IN * TN_IN * 4
    in_specs.append(pl.BlockSpec((D_MODEL, TN_IN), lambda j, i, ct: (0, ct[j])))
    args.append(w_in)
    return pl.pallas_call(
        functools.partial(_inproj_kernel, kinds),
        grid_spec=pltpu.PrefetchScalarGridSpec(
            num_scalar_prefetch=1,
            grid=(n_col, T // TM_IN),
            in_specs=in_specs,
            out_specs=pl.BlockSpec((TM_IN, TN_IN), lambda j, i, ct: (i, j)),
            scratch_shapes=scratch),
        out_shape=jax.ShapeDtypeStruct((T, n_col * TN_IN), out_dtype),
        compiler_params=pltpu.CompilerParams(
            dimension_semantics=("arbitrary", "arbitrary"),
            vmem_limit_bytes=_vmem_limit(nbytes)),
        name=name,
    )(col_tiles, *args)


def _attn_kernel(seq_groups, sink_ref, q_ref, kc_ref, vc_ref, kp_ref, kn_ref, vp_ref, vn_ref,
                 ga_ref, o_ref, s_all, p_all, r_all):
    t = pl.program_id(0)
    h = pl.program_id(1)
    _, local, per = _segment_position(t, seq_groups)
    has_prev = local != 0
    has_next = local != per - 1

    k_all = jnp.concatenate([kp_ref[...], kc_ref[...], kn_ref[...]], axis=0)
    v_all = jnp.concatenate([vp_ref[...], vc_ref[...], vn_ref[...]], axis=0)
    qi = lax.broadcasted_iota(jnp.int32, (BLOCK, BLOCK), 0)
    kj = lax.broadcasted_iota(jnp.int32, (BLOCK, BLOCK), 1)
    neg = jnp.float32(-jnp.inf)
    bias_prev = jnp.where(kj >= qi, 0.0, neg)
    bias_next = jnp.where(kj <= qi, 0.0, neg)
    bias_prev0 = jnp.where(has_prev, bias_prev, neg)
    bias_next_last = jnp.where(has_next, bias_next, neg)
    n_blocks = TQ_ATT // BLOCK
    for b in range(n_blocks):
        rs = slice(b * BLOCK, (b + 1) * BLOCK)
        kb = k_all[b * BLOCK:(b + 3) * BLOCK]
        vb = v_all[b * BLOCK:(b + 3) * BLOCK]
        bp = bias_prev0 if b == 0 else bias_prev
        bn = bias_next_last if b == n_blocks - 1 else bias_next
        heads = [slice(g * HEAD_DIM, (g + 1) * HEAD_DIM) for g in range(GROUP)]
        s_ref, p_ref, r_ref = s_all.at[b], p_all.at[b], r_all.at[b]
        q4 = jnp.concatenate([q_ref[rs, cs] for cs in heads], axis=0)
        s_ref[...] = _dot_nt(q4, kb).reshape(GROUP, BLOCK, 3 * BLOCK)
        for g, cs in enumerate(heads):
            sink2 = sink_ref[GROUP * h + g] * LOG2E
            for r in range(BLOCK // SLAB_ATT):
                rr = slice(r * SLAB_ATT, (r + 1) * SLAB_ATT)
                s = s_ref[g, rr, :]
                s_p = s[:, :BLOCK] + bp[rr]
                s_c = s[:, BLOCK:2 * BLOCK]
                s_n = s[:, 2 * BLOCK:] + bn[rr]
                m = jnp.maximum(jnp.max(jnp.maximum(jnp.maximum(s_p, s_c), s_n), axis=-1, keepdims=True), sink2)
                p_p, p_c, p_n = jnp.exp2(s_p - m), jnp.exp2(s_c - m), jnp.exp2(s_n - m)
                denom = jnp.sum(p_p + p_c + p_n, axis=-1, keepdims=True) + jnp.exp2(sink2 - m)
                p_ref[g, rr, :] = jnp.concatenate([p_p, p_c, p_n], axis=1).astype(BF16)
                r_ref[g, rr, :] = jnp.broadcast_to(1.0 / denom, (SLAB_ATT, LANES))
        o4 = jnp.dot(p_ref[...].reshape(GROUP * BLOCK, 3 * BLOCK), vb, preferred_element_type=F32)
        for g, cs in enumerate(heads):
            o = o4[g * BLOCK:(g + 1) * BLOCK] * r_ref[g]
            o_ref[rs, cs] = (_sigmoid(ga_ref[rs, cs].astype(F32)) * o).astype(o_ref.dtype)


def _window_attention(slab, sink, seq_groups):
    T = slab.shape[0]
    n_tiles = T // TQ_ATT
    bpt = TQ_ATT // BLOCK
    last_blk = T // BLOCK - 1
    gw = GROUP * HEAD_DIM
    assert A_K % HEAD_DIM == 0 and A_V % HEAD_DIM == 0 and A_GA % gw == 0
    kcol, vcol, gcol = A_K // HEAD_DIM, A_V // HEAD_DIM, A_GA // gw
    nbytes = (2 * (3 * TQ_ATT * gw * 2 + 2 * TQ_ATT * HEAD_DIM * 2 + 4 * BLOCK * HEAD_DIM * 2)
              + bpt * GROUP * BLOCK * (3 * BLOCK * 6 + LANES * 4))
    return pl.pallas_call(
        functools.partial(_attn_kernel, seq_groups),
        grid=(n_tiles, N_KV_HEADS),
        in_specs=[
            pl.BlockSpec(memory_space=pltpu.SMEM),
            pl.BlockSpec((TQ_ATT, gw), lambda t, h: (t, h)),
            pl.BlockSpec((TQ_ATT, HEAD_DIM), lambda t, h: (t, kcol + h)),
            pl.BlockSpec((TQ_ATT, HEAD_DIM), lambda t, h: (t, vcol + h)),
            pl.BlockSpec((BLOCK, HEAD_DIM), lambda t, h: (jnp.maximum(t * bpt - 1, 0), kcol + h)),
            pl.BlockSpec((BLOCK, HEAD_DIM), lambda t, h: (jnp.minimum((t + 1) * bpt, last_blk), kcol + h)),
            pl.BlockSpec((BLOCK, HEAD_DIM), lambda t, h: (jnp.maximum(t * bpt - 1, 0), vcol + h)),
            pl.BlockSpec((BLOCK, HEAD_DIM), lambda t, h: (jnp.minimum((t + 1) * bpt, last_blk), vcol + h)),
            pl.BlockSpec((TQ_ATT, gw), lambda t, h: (t, gcol + h)),
        ],
        out_specs=pl.BlockSpec((TQ_ATT, gw), lambda t, h: (t, h)),
        out_shape=jax.ShapeDtypeStruct((T, D_ATT), BF16),
        scratch_shapes=[pltpu.VMEM((bpt, GROUP, BLOCK, 3 * BLOCK), F32),
                        pltpu.VMEM((bpt, GROUP, BLOCK, 3 * BLOCK), BF16),
                        pltpu.VMEM((bpt, GROUP, BLOCK, LANES), F32)],
        compiler_params=pltpu.CompilerParams(
            dimension_semantics=("parallel", "arbitrary"),
            vmem_limit_bytes=_vmem_limit(nbytes)),
        name="window_attention",
    )(sink, slab, slab, slab, slab, slab, slab, slab, slab)


def _hgrn_kernel(seq_groups, lb_ref, qf_ref, if_ref, zf_ref, qb_ref, ib_ref, zb_ref,
                 of_ref, ob_ref, st_ref, stb_ref, ds_ref, ops_ref, cross_ref, dec_ref, a_ref):
    g = pl.program_id(1)
    _, local, _ = _segment_position(g, seq_groups)

    @pl.when(local == 0)
    def _():
        st_ref[...] = jnp.zeros(st_ref.shape, F32)
        stb_ref[...] = jnp.zeros(stb_ref.shape, BF16)

    Q_IN, K_IN, Q_ST, K_ST, I_C = range(5)
    pair = 2 * CHUNK
    n_pairs = SEG_REC // pair
    rin = lax.broadcasted_iota(jnp.int32, (CHUNK, LANES), 0)
    ti = lax.broadcasted_iota(jnp.int32, (CHUNK, CHUNK), 0)
    si = lax.broadcasted_iota(jnp.int32, (CHUNK, CHUNK), 1)
    half = (slice(0, CHUNK), slice(CHUNK, pair))

    def prepare(d, hh, start, q_ref, i_ref, z_ref):
        cs = slice(hh * LANES, (hh + 1) * LANES)
        lb = lb_ref[:, cs]
        one_m_lb = 1.0 - lb
        q_in, k_in, k_st, q_dc, dec = [], [], [], [], []
        for t in range(2):
            rows = _rows(start + t * CHUNK, CHUNK)
            q = q_ref[rows, cs].astype(F32)
            gate = one_m_lb * _sigmoid(z_ref[rows, cs])
            k = one_m_lb - gate
            logf = jnp.log2(lb + gate)
            b = logf
            for s in (1, 2, 4, 8, 16):
                b = b + jnp.where(rin >= s, pltpu.roll(b, s, 0), 0.0)
            tot = b[CHUNK - 1:CHUNK, :]
            if d == 0:
                c = b
                cref = b[CHUNK // 2 - 1:CHUNK // 2, :]
            else:
                c = tot - b + logf
                cref = c[CHUNK // 2:CHUNK // 2 + 1, :]
            q_in.append(q * jnp.exp2(c - cref))
            k_in.append(k * jnp.exp2(cref - c))
            k_st.append(k_in[t] * jnp.exp2(tot - cref))
            q_dc.append(q_in[t] * jnp.exp2(cref))
            dec.append(jnp.exp2(tot))
            ops_ref[d, hh, Q_IN, half[t]] = q_in[t].astype(BF16)
            ops_ref[d, hh, K_IN, half[t]] = k_in[t].astype(BF16)
            ops_ref[d, hh, I_C, half[t]] = i_ref[rows, cs]
        ca, cb = (0, 1) if d == 0 else (1, 0)
        ops_ref[d, hh, Q_ST, half[ca]] = q_dc[ca].astype(BF16)
        ops_ref[d, hh, Q_ST, half[cb]] = (q_dc[cb] * dec[ca]).astype(BF16)
        ops_ref[d, hh, K_ST, half[ca]] = (k_st[ca] * dec[cb]).astype(BF16)
        ops_ref[d, hh, K_ST, half[cb]] = k_st[cb].astype(BF16)
        cross_ref[d, hh, 0] = q_dc[cb].astype(BF16)
        cross_ref[d, hh, 1] = k_st[ca].astype(BF16)
        dec_ref[d, hh] = jnp.broadcast_to(dec[0] * dec[1], (8, LANES))

    def intra(d, hh):
        for t in range(2):
            a_ref[d, hh, t] = _dot_nt(ops_ref[d, hh, Q_IN, half[t]], ops_ref[d, hh, K_IN, half[t]])
        a_ref[d, hh, 2] = _dot_nt(cross_ref[d, hh, 0], cross_ref[d, hh, 1])

    def output(d, hh, start, o_ref):
        cs = slice(hh * LANES, (hh + 1) * LANES)
        tri = (ti >= si) if d == 0 else (ti <= si)
        a_top = jnp.where(tri, a_ref[d, hh, 0], 0.0).astype(BF16)
        a_bot = jnp.where(tri, a_ref[d, hh, 1], 0.0).astype(BF16)
        a_x = a_ref[d, hh, 2].astype(BF16)
        i_top, i_bot = ops_ref[d, hh, I_C, half[0]], ops_ref[d, hh, I_C, half[1]]
        inter = jnp.dot(ops_ref[d, hh, Q_ST], stb_ref[d, hh], preferred_element_type=F32)
        if d == 0:
            both = jnp.dot(jnp.concatenate([a_top, a_x], axis=0), i_top, preferred_element_type=F32)
            o_top = both[:CHUNK] + inter[:CHUNK]
            o_bot = both[CHUNK:] + jnp.dot(a_bot, i_bot, preferred_element_type=F32) + inter[CHUNK:]
        else:
            both = jnp.dot(jnp.concatenate([a_x, a_bot], axis=0), i_bot, preferred_element_type=F32)
            o_top = both[:CHUNK] + jnp.dot(a_top, i_top, preferred_element_type=F32) + inter[:CHUNK]
            o_bot = both[CHUNK:] + inter[CHUNK:]
        o_ref[_rows(start, CHUNK), cs] = o_top.astype(o_ref.dtype)
        o_ref[_rows(start + CHUNK, CHUNK), cs] = o_bot.astype(o_ref.dtype)
        ds_ref[d, hh] = _dot_tn(ops_ref[d, hh, I_C], ops_ref[d, hh, K_ST])

    def update(d, hh):
        dec = dec_ref[d, hh, 0:1, :]
        for r in range(REC_VAL_DIM // CHUNK):
            rr = slice(r * CHUNK, (r + 1) * CHUNK)
            new = dec * st_ref[d, hh, rr, :] + ds_ref[d, hh, rr, :]
            st_ref[d, hh, rr, :] = new
        stb_ref[d, hh] = st_ref[d, hh].T.astype(BF16)

    def scan(n, carry):
        start_f = pl.multiple_of(n * pair, pair)
        start_b = pl.multiple_of((n_pairs - 1 - n) * pair, pair)
        for hh in range(HB_REC):
            prepare(0, hh, start_f, qf_ref, if_ref, zf_ref)
            prepare(1, hh, start_b, qb_ref, ib_ref, zb_ref)
        for hh in range(HB_REC):
            intra(0, hh)
            intra(1, hh)
        for hh in range(HB_REC):
            output(0, hh, start_f, of_ref)
            output(1, hh, start_b, ob_ref)
        for hh in range(HB_REC):
            update(0, hh)
            update(1, hh)
        return carry

    lax.fori_loop(0, n_pairs, scan, 0)


def _hgrn2(slab, gates, lb_row, seq_groups):
    T = slab.shape[0]
    w = HB_REC * LANES
    assert A_QR % w == 0 and A_IR % w == 0 and D_REC_K % w == 0

    def mirror(g):
        first, local, per = _segment_position(g, seq_groups)
        return first + per - 1 - local

    def fspec(off):
        return pl.BlockSpec((SEG_REC, w), lambda hb, g: (g, off // w + hb))

    def bspec(off):
        return pl.BlockSpec((SEG_REC, w), lambda hb, g: (mirror(g), off // w + hb))

    state = (2, HB_REC, REC_VAL_DIM, REC_KEY_DIM)
    nbytes = (2 * (4 * SEG_REC * w * 2 + 2 * SEG_REC * w * 4 + 2 * SEG_REC * w * 2)
              + 2 * HB_REC * (REC_VAL_DIM * REC_KEY_DIM * 10 + 12 * CHUNK * LANES * 2 + 8 * LANES * 4))
    out = jax.ShapeDtypeStruct((T, D_REC_V), BF16)
    return pl.pallas_call(
        functools.partial(_hgrn_kernel, seq_groups),
        grid=(N_REC_HEADS // HB_REC, T // SEG_REC),
        in_specs=[
            pl.BlockSpec((1, w), lambda hb, g: (0, hb)),
            fspec(A_QR), fspec(A_IR), fspec(0),
            bspec(A_QR), bspec(A_IR), bspec(D_REC_K),
        ],
        out_specs=[pl.BlockSpec((SEG_REC, w), lambda hb, g: (g, hb)),
                   pl.BlockSpec((SEG_REC, w), lambda hb, g: (mirror(g), hb))],
        out_shape=[out, out],
        scratch_shapes=[pltpu.VMEM(state, F32), pltpu.VMEM(state, BF16), pltpu.VMEM(state, F32),
                        pltpu.VMEM((2, HB_REC, 5, 2 * CHUNK, LANES), BF16),
                        pltpu.VMEM((2, HB_REC, 2, CHUNK, LANES), BF16),
                        pltpu.VMEM((2, HB_REC, 8, LANES), F32),
                        pltpu.VMEM((2, HB_REC, 3, CHUNK, CHUNK), F32)],
        compiler_params=pltpu.CompilerParams(
            dimension_semantics=("parallel", "arbitrary"),
            vmem_limit_bytes=_vmem_limit(nbytes)),
        name="hgrn2",
    )(lb_row, slab, slab, gates, slab, slab, gates)


def _outproj_kernel(n_first, a_ref, of_ref, ob_ref, grl_ref, grh_ref, gtl_ref, gth_ref, xa_ref, xb_ref,
                    w_ref, rg_ref, g_ref, h_ref, m_ref):
    half = D_REC_V // 2
    heads_per_chunk = KC_OUT // REC_VAL_DIM
    y = None
    for c in range(D_MODEL // KC_OUT):
        for hh in range(c * heads_per_chunk, (c + 1) * heads_per_chunk):
            cs = slice(hh * REC_VAL_DIM, (hh + 1) * REC_VAL_DIM)
            lo = hh * REC_VAL_DIM < half
            hs = cs if lo else slice(hh * REC_VAL_DIM - half, (hh + 1) * REC_VAL_DIM - half)
            rec = of_ref[:, cs].astype(F32) + ob_ref[:, cs].astype(F32)
            ms = jnp.mean(rec * rec, axis=-1, keepdims=True)
            r = (rec * lax.rsqrt(ms + RMS_EPS)) * rg_ref[:, cs]
            gr = (grl_ref if lo else grh_ref)[:, hs].astype(F32)
            gt = (gtl_ref if lo else gth_ref)[:, hs].astype(F32)
            r = _sigmoid(gt) * (r * (gr * _sigmoid(gr)))
            m_ref[:, cs] = (a_ref[:, cs].astype(F32) + r).astype(BF16)
        ks = slice(c * KC_OUT, (c + 1) * KC_OUT)
        part = jnp.dot(m_ref[:, ks], w_ref[ks, :], preferred_element_type=F32)
        y = part if y is None else y + part
    ms = jnp.mean(y * y, axis=-1, keepdims=True)
    h_ref[...] = (y * lax.rsqrt(ms + RMS_EPS)) * g_ref[...]

    @pl.when(pl.program_id(0) < n_first)
    def _():
        h_ref[...] += xa_ref[...]

    @pl.when(pl.program_id(0) >= n_first)
    def _():
        h_ref[...] += xb_ref[...]


def _out_projection(attn, rec_f, rec_b, slab, xa, xb, w_bf16, rec_gain, gain):
    T = attn.shape[0]
    n_first = xa.shape[0] // TM_OUT
    half = D_REC_V // 2
    assert A_GR % half == 0 and A_GT % half == 0
    tok = lambda i: (i, 0)
    nbytes = (2 * 3 * TM_OUT * D_MODEL * 2 + 2 * 4 * TM_OUT * half * 2 + 2 * 3 * TM_OUT * D_MODEL * 4
              + 2 * D_MODEL * D_MODEL * 2 + TM_OUT * D_MODEL * 2 + 2 * TM_OUT * D_MODEL * 4)
    return pl.pallas_call(
        functools.partial(_outproj_kernel, n_first),
        grid=(T // TM_OUT,),
        in_specs=[
            pl.BlockSpec((TM_OUT, D_MODEL), tok),
            pl.BlockSpec((TM_OUT, D_REC_V), tok),
            pl.BlockSpec((TM_OUT, D_REC_V), tok),
            pl.BlockSpec((TM_OUT, half), lambda i: (i, A_GR // half)),
            pl.BlockSpec((TM_OUT, half), lambda i: (i, A_GR // half + 1)),
            pl.BlockSpec((TM_OUT, half), lambda i: (i, A_GT // half)),
            pl.BlockSpec((TM_OUT, half), lambda i: (i, A_GT // half + 1)),
            *_two_source_specs(TM_OUT, n_first),
            pl.BlockSpec((D_MODEL, D_MODEL), lambda i: (0, 0)),
            pl.BlockSpec((1, D_REC_V), lambda i: (0, 0)),
            pl.BlockSpec((1, D_MODEL), lambda i: (0, 0)),
        ],
        out_specs=pl.BlockSpec((TM_OUT, D_MODEL), tok),
        out_shape=jax.ShapeDtypeStruct((T, D_MODEL), F32),
        scratch_shapes=[pltpu.VMEM((TM_OUT, D_MODEL), BF16)],
        compiler_params=pltpu.CompilerParams(
            dimension_semantics=("parallel",),
            vmem_limit_bytes=_vmem_limit(nbytes)),
        name="out_projection",
    )(attn, rec_f, rec_b, slab, slab, slab, slab, xa, xb, w_bf16, rec_gain, gain)


def _ffn_kernel(n_first, h_ref, gpre_ref, gpost_ref, wg_ref, wu_ref, wd_ref, oa_ref, ob_ref, hn_ref, acc_ref):
    i = pl.program_id(0)
    j = pl.program_id(1)

    @pl.when(j == 0)
    def _():
        def body(rows):
            h = h_ref[rows, :]
            ms = jnp.mean(h * h, axis=-1, keepdims=True)
            hn_ref[rows, :] = ((h * lax.rsqrt(ms + RMS_EPS)) * gpre_ref[...]).astype(BF16)
            acc_ref[rows, :] = jnp.zeros((ROWS_EW, D_MODEL), F32)
        _row_tiles(TM_FFN, body)

    hn = hn_ref[...]
    g = jnp.dot(hn, wg_ref[...], preferred_element_type=F32)
    u = jnp.dot(hn, wu_ref[...], preferred_element_type=F32)
    act = ((g * _sigmoid(g)) * u).astype(BF16)
    acc_ref[...] += jnp.dot(act, wd_ref[...], preferred_element_type=F32)

    def finish(o_ref):
        def body(rows):
            y = acc_ref[rows, :]
            ms = jnp.mean(y * y, axis=-1, keepdims=True)
            o_ref[rows, :] = h_ref[rows, :] + (y * lax.rsqrt(ms + RMS_EPS)) * gpost_ref[...]
        _row_tiles(TM_FFN, body)

    last = j == pl.num_programs(1) - 1

    @pl.when(jnp.logical_and(last, i < n_first))
    def _():
        finish(oa_ref)

    @pl.when(jnp.logical_and(last, i >= n_first))
    def _():
        finish(ob_ref)


def _ffn(h, n_tokens_first, gpre, gpost, wg, wu, wd):
    T = h.shape[0]
    n_first = n_tokens_first // TM_FFN
    nbytes = (2 * 3 * TM_FFN * D_MODEL * 4 + TM_FFN * D_MODEL * 2 + 2 * 3 * D_MODEL * TF_FFN * 2
              + 3 * TM_FFN * TF_FFN * 4 + 2 * TM_FFN * D_MODEL * 4)
    return pl.pallas_call(
        functools.partial(_ffn_kernel, n_first),
        grid=(T // TM_FFN, D_FF // TF_FFN),
        in_specs=[
            pl.BlockSpec((TM_FFN, D_MODEL), lambda i, j: (i, 0)),
            pl.BlockSpec((1, D_MODEL), lambda i, j: (0, 0)),
            pl.BlockSpec((1, D_MODEL), lambda i, j: (0, 0)),
            pl.BlockSpec((D_MODEL, TF_FFN), lambda i, j: (0, j)),
            pl.BlockSpec((D_MODEL, TF_FFN), lambda i, j: (0, j)),
            pl.BlockSpec((TF_FFN, D_MODEL), lambda i, j: (j, 0)),
        ],
        out_specs=list(_two_source_specs(TM_FFN, n_first)),
        out_shape=[jax.ShapeDtypeStruct((n_tokens_first, D_MODEL), F32),
                   jax.ShapeDtypeStruct((T - n_tokens_first, D_MODEL), F32)],
        scratch_shapes=[pltpu.VMEM((TM_FFN, D_MODEL), BF16), pltpu.VMEM((TM_FFN, D_MODEL), F32)],
        compiler_params=pltpu.CompilerParams(
            dimension_semantics=("arbitrary", "arbitrary"),
            vmem_limit_bytes=_vmem_limit(nbytes)),
        name="swiglu_ffn",
    )(h, gpre, gpost, wg, wu, wd)


def _rope_tables(max_len):
    pos = jnp.arange(max_len, dtype=F32)
    inv_freq = ROPE_THETA ** (-jnp.arange(ROPE_HALF, dtype=F32) / ROPE_HALF)
    ang = pos[:, None] * inv_freq[None, :]
    cos, sin = jnp.cos(ang), jnp.sin(ang)
    rest = HEAD_DIM - ROPE_DIM
    cos_t = jnp.concatenate([cos, cos, jnp.ones((max_len, rest), F32)], axis=1)
    sa_t = jnp.concatenate([-sin, jnp.zeros((max_len, HEAD_DIM - ROPE_HALF), F32)], axis=1)
    sb_t = jnp.concatenate([jnp.zeros((max_len, ROPE_HALF), F32), sin, jnp.zeros((max_len, rest), F32)], axis=1)
    return cos_t, sa_t, sb_t


def _seq_groups(seq_shapes, seg):
    groups, first = [], 0
    for B, L in seq_shapes:
        assert L % seg == 0
        groups.append((first, L // seg))
        first += B * L // seg
    return tuple(groups)


def _encoder_layer(xa, xb, seq_shapes, tables, w_in, sink, rec_norm, lb, w_out, norm_mix_pre, norm_mix_post,
                   norm_ffn_pre, norm_ffn_post, w_gate, w_up, w_down):
    row = lambda v: v.astype(F32).reshape(1, -1)
    xn = _stream_norm(xa, xb, row(norm_mix_pre))

    assert all(off % TN_IN == 0 and w % TN_IN == 0 for off, w in SLAB_GROUPS) and OFF_ZF % TN_IN == 0
    gate_tiles = list(range(OFF_ZF // TN_IN, OFF_IR // TN_IN))
    slab_tiles = [t for off, w in SLAB_GROUPS for t in range(off // TN_IN, (off + w) // TN_IN)]
    kind_of = lambda t: ("rope" if t < OFF_K // TN_IN else "rope_k" if t == OFF_K // TN_IN
                         else "silu" if OFF_QR // TN_IN <= t < OFF_ZF // TN_IN else "plain")
    table_groups = _seq_groups(seq_shapes, TM_IN)
    slab = _in_projection(xn, w_in, tables, table_groups, slab_tiles, tuple(kind_of(t) for t in slab_tiles),
                          BF16, "in_projection")
    gates = _in_projection(xn, w_in, None, None, gate_tiles, ("plain",) * len(gate_tiles),
                           F32, "gate_projection")

    attn = _window_attention(slab, sink.astype(F32), _seq_groups(seq_shapes, TQ_ATT))
    rec_f, rec_b = _hgrn2(slab, gates, row(lb), _seq_groups(seq_shapes, SEG_REC))
    h = _out_projection(attn, rec_f, rec_b, slab, xa, xb, w_out.astype(BF16), row(rec_norm),
                        row(norm_mix_post))
    return _ffn(h, xa.shape[0], row(norm_ffn_pre), row(norm_ffn_post),
                w_gate.astype(BF16), w_up.astype(BF16), w_down.astype(BF16))


def kernel(x_prompt, x_sample, w_in, sink, rec_norm, lb_logits, w_out, norm_mix_pre, norm_mix_post,
           norm_ffn_pre, norm_ffn_post, w_gate, w_up, w_down):
    lb_all = jnp.cumsum(jax.nn.softmax(lb_logits.astype(F32), axis=0), axis=0)
    seq_shapes = (x_prompt.shape[:2], x_sample.shape[:2])
    xa = x_prompt.reshape(-1, D_MODEL)
    xb = x_sample.reshape(-1, D_MODEL)
    tables = _rope_tables(max(L for _, L in seq_shapes))
    for l in range(DEPTH):
        xa, xb = _encoder_layer(xa, xb, seq_shapes, tables, w_in[l], sink[l], rec_norm[l], lb_all[l], w_out[l],
                                norm_mix_pre[l], norm_mix_post[l], norm_ffn_pre[l], norm_ffn_post[l],
                                w_gate[l], w_up[l], w_down[l])
    return (xa.reshape(x_prompt.shape), xb.reshape(x_sample.shape))
```

```python
import functools

import jax
import jax.numpy as jnp
import numpy as np
from jax import lax
from jax.experimental import pallas as pl
from jax.experimental.pallas import tpu as pltpu

F32 = jnp.float32
BF16 = jnp.bfloat16

D_MODEL = 2048
DEPTH = 1
HEAD_DIM = 128
N_Q_HEADS = 16
N_KV_HEADS = 4
GROUP = N_Q_HEADS // N_KV_HEADS
WINDOW = 128
BLOCK = 128
ROPE_DIM = HEAD_DIM // 4
ROPE_HALF = ROPE_DIM // 2
ROPE_THETA = 500000.0
N_REC_HEADS = 16
REC_KEY_DIM = 128
REC_VAL_DIM = 128
CHUNK = 32
D_FF = -(-8 * D_MODEL // (3 * 256)) * 256
RMS_EPS = 1e-6
LOG2E = 1.4426950408889634
Q_SCALE = HEAD_DIM ** -0.5 * LOG2E

D_ATT = N_Q_HEADS * HEAD_DIM
D_KV = N_KV_HEADS * HEAD_DIM
D_REC_K = N_REC_HEADS * REC_KEY_DIM
D_REC_V = N_REC_HEADS * REC_VAL_DIM
SPLIT_SIZES = (D_ATT, D_KV, D_KV, D_REC_K, D_REC_K, D_REC_K, D_REC_V, D_REC_V, D_MODEL, D_MODEL)
D_IN = sum(SPLIT_SIZES)
(OFF_Q, OFF_K, OFF_V, OFF_QR, OFF_ZF, OFF_ZB, OFF_IR, OFF_GR, OFF_GA, OFF_GT) = (
    int(v) for v in np.concatenate([[0], np.cumsum(SPLIT_SIZES)[:-1]]))
SLAB_GROUPS = ((OFF_Q, D_ATT), (OFF_QR, D_REC_K), (OFF_IR, D_REC_V), (OFF_GR, D_REC_V), (OFF_GA, D_MODEL),
               (OFF_GT, D_MODEL), (OFF_K, 2 * D_KV))
A_Q, A_QR, A_IR, A_GR, A_GA, A_GT, A_K = (int(v) for v in np.cumsum([0] + [w for _, w in SLAB_GROUPS])[:-1])
A_V = A_K + D_KV

V7X_VMEM_CEILING = 56 * 1024 * 1024
LANES = 128

TM_NORM = 512
TM_IN, TN_IN = 1024, 1024
SUB_IN = 256
ROWS_EW = 256
SLAB_ATT = 32
TQ_ATT = 1024
SEG_REC = 512
HB_REC = 16
TM_OUT = 256
KC_OUT = 256
TM_FFN, TF_FFN = 512, 512


def _vmem_limit(nbytes):
    return int(min(V7X_VMEM_CEILING, nbytes * 1.25 + (4 << 20)))


def _sigmoid(x):
    return 1.0 / (1.0 + jnp.exp(-x))


def _dot_nt(a, b):
    return lax.dot_general(a, b, (((1,), (1,)), ((), ())), preferred_element_type=F32)


def _dot_tn(a, b):
    return lax.dot_general(a, b, (((0,), (0,)), ((), ())), preferred_element_type=F32)


def _rows(start, size):
    return pl.ds(start if isinstance(start, int) else pl.multiple_of(start, size), size)


def _row_tiles(n_rows, body):
    def step(r, c):
        body(_rows(r * ROWS_EW, ROWS_EW))
        return c
    lax.fori_loop(0, n_rows // ROWS_EW, step, 0)


def _segment_position(g, seq_groups):
    first, local, per = None, None, None
    for g0, n in reversed(seq_groups):
        loc = lax.rem(g - g0, n)
        fst = g - loc
        if first is None:
            first, local, per = fst, loc, n
        else:
            here = g < nxt
            first = jnp.where(here, fst, first)
            local = jnp.where(here, loc, local)
            per = jnp.where(here, n, per)
        nxt = g0
    return first, local, per


def _two_source_specs(tm, n_first):
    return (pl.BlockSpec((tm, D_MODEL), lambda i, *_: (jnp.minimum(i, n_first - 1), 0)),
            pl.BlockSpec((tm, D_MODEL), lambda i, *_: (jnp.maximum(i - n_first, 0), 0)))


def _norm_kernel(n_first, xa_ref, xb_ref, g_ref, o_ref):
    def run(x_ref):
        def body(rows):
            x = x_ref[rows, :]
            ms = jnp.mean(x * x, axis=-1, keepdims=True)
            o_ref[rows, :] = ((x * lax.rsqrt(ms + RMS_EPS)) * g_ref[...]).astype(o_ref.dtype)
        _row_tiles(TM_NORM, body)

    @pl.when(pl.program_id(0) < n_first)
    def _():
        run(xa_ref)

    @pl.when(pl.program_id(0) >= n_first)
    def _():
        run(xb_ref)


def _stream_norm(xa, xb, gain):
    T = xa.shape[0] + xb.shape[0]
    n_first = xa.shape[0] // TM_NORM
    nbytes = 2 * 2 * TM_NORM * D_MODEL * 4 + 2 * TM_NORM * D_MODEL * 2
    return pl.pallas_call(
        functools.partial(_norm_kernel, n_first),
        grid=(T // TM_NORM,),
        in_specs=[*_two_source_specs(TM_NORM, n_first), pl.BlockSpec((1, D_MODEL), lambda i: (0, 0))],
        out_specs=pl.BlockSpec((TM_NORM, D_MODEL), lambda i: (i, 0)),
        out_shape=jax.ShapeDtypeStruct((T, D_MODEL), BF16),
        compiler_params=pltpu.CompilerParams(
            dimension_semantics=("parallel",), vmem_limit_bytes=_vmem_limit(nbytes)),
        name="stream_norm",
    )(xa, xb, gain)


def _inproj_kernel(kinds, col_tiles_ref, *refs):
    del col_tiles_ref
    with_epilogue = any(k != "plain" for k in kinds)
    if with_epilogue:
        xn_ref, cos_ref, sa_ref, sb_ref, w_ref, o_ref, wb_ref, y_ref = refs
    else:
        xn_ref, w_ref, o_ref, wb_ref = refs
    j = pl.program_id(0)

    @pl.when(pl.program_id(1) == 0)
    def _():
        def body(rows):
            wb_ref[rows, :] = w_ref[rows, :].astype(BF16)
        _row_tiles(D_MODEL, body)

    def transform(kind, y, rows, lane_tile):
        if kind == "silu":
            return y * _sigmoid(y)
        if kind == "rope_k" and lane_tile >= N_KV_HEADS:
            return y
        y = (y * cos_ref[rows, :] + pltpu.roll(y, HEAD_DIM - ROPE_HALF, 1) * sa_ref[rows, :]
             + pltpu.roll(y, ROPE_HALF, 1) * sb_ref[rows, :])
        return y * Q_SCALE if kind == "rope" else y

    def epilogue(kind, sub):
        for r in range(TM_IN // ROWS_EW):
            rows = slice(r * ROWS_EW, (r + 1) * ROWS_EW)
            for cc in range(SUB_IN // LANES):
                lane_tile = sub * (SUB_IN // LANES) + cc
                y = y_ref[sub, rows, cc * LANES:(cc + 1) * LANES]
                o_ref[rows, lane_tile * LANES:(lane_tile + 1) * LANES] = (
                    transform(kind, y, rows, lane_tile).astype(o_ref.dtype))

    for kind in sorted(set(kinds)):
        tiles = [t for t, k in enumerate(kinds) if k == kind]
        cond = functools.reduce(jnp.logical_or, [j == t for t in tiles])

        @pl.when(cond)
        def _(kind=kind):
            if kind == "plain":
                o_ref[...] = jnp.dot(xn_ref[...], wb_ref[...], preferred_element_type=F32).astype(o_ref.dtype)
            else:
                n_sub = TN_IN // SUB_IN
                for sub in range(n_sub):
                    y_ref[sub] = jnp.dot(xn_ref[...], wb_ref[:, sub * SUB_IN:(sub + 1) * SUB_IN],
                                         preferred_element_type=F32)
                    if sub > 0:
                        epilogue(kind, sub - 1)
                epilogue(kind, n_sub - 1)


def _in_projection(xn, w_in, tables, table_groups, col_tiles, kinds, out_dtype, name):
    T = xn.shape[0]
    n_col = len(col_tiles)
    col_tiles = jnp.asarray(col_tiles, jnp.int32)
    with_epilogue = any(k != "plain" for k in kinds)
    rope_tiles = [t for t, k in enumerate(kinds) if k.startswith("rope")]
    out_bytes = jnp.dtype(out_dtype).itemsize

    def table_map(j, i, ct):
        _, local, _ = _segment_position(i, table_groups)
        uses_tables = functools.reduce(jnp.logical_or, [j == t for t in rope_tiles])
        return (jnp.where(uses_tables, local, 0), 0)

    in_specs = [pl.BlockSpec((TM_IN, D_MODEL), lambda j, i, ct: (i, 0))]
    args = [xn]
    scratch = [pltpu.VMEM((D_MODEL, TN_IN), BF16)]
    nbytes = (2 * TM_IN * D_MODEL * 2 + 2 * D_MODEL * TN_IN * 4 + D_MODEL * TN_IN * 2
              + 2 * TM_IN * TN_IN * out_bytes + TM_IN * TN_IN * 4)
    if with_epilogue:
        in_specs += [pl.BlockSpec((TM_IN, LANES), table_map)] * 3
        args += list(tables)
        scratch.append(pltpu.VMEM((TN_IN // SUB_IN, TM_IN, SUB_IN), F32))
        nbytes += 2 * 3 * TM_IN * LANES * 4 + TM_IN * TN_IN * 4
    in_specs.append(pl.BlockSpec((D_MODEL, TN_IN), lambda j, i, ct: (0, ct[j])))
    args.append(w_in)
    return pl.pallas_call(
        functools.partial(_inproj_kernel, kinds),
        grid_spec=pltpu.PrefetchScalarGridSpec(
            num_scalar_prefetch=1,
            grid=(n_col, T // TM_IN),
            in_specs=in_specs,
            out_specs=pl.BlockSpec((TM_IN, TN_IN), lambda j, i, ct: (i, j)),
            scratch_shapes=scratch),
        out_shape=jax.ShapeDtypeStruct((T, n_col * TN_IN), out_dtype),
        compiler_params=pltpu.CompilerParams(
            dimension_semantics=("arbitrary", "arbitrary"),
            vmem_limit_bytes=_vmem_limit(nbytes)),
        name=name,
    )(col_tiles, *args)


def _attn_kernel(seq_groups, sink_ref, q_ref, kc_ref, vc_ref, kp_ref, kn_ref, vp_ref, vn_ref,
                 ga_ref, o_ref, s_all, p_all, r_all):
    t = pl.program_id(0)
    h = pl.program_id(1)
    _, local, per = _segment_position(t, seq_groups)
    has_prev = local != 0
    has_next = local != per - 1

    k_all = jnp.concatenate([kp_ref[...], kc_ref[...], kn_ref[...]], axis=0)
    v_all = jnp.concatenate([vp_ref[...], vc_ref[...], vn_ref[...]], axis=0)
    qi = lax.broadcasted_iota(jnp.int32, (BLOCK, BLOCK), 0)
    kj = lax.broadcasted_iota(jnp.int32, (BLOCK, BLOCK), 1)
    neg = jnp.float32(-jnp.inf)
    bias_prev = jnp.where(kj >= qi, 0.0, neg)
    bias_next = jnp.where(kj <= qi, 0.0, neg)
    bias_prev0 = jnp.where(has_prev, bias_prev, neg)
    bias_next_last = jnp.where(has_next, bias_next, neg)
    n_blocks = TQ_ATT // BLOCK
    for b in range(n_blocks):
        rs = slice(b * BLOCK, (b + 1) * BLOCK)
        kb = k_all[b * BLOCK:(b + 3) * BLOCK]
        vb = v_all[b * BLOCK:(b + 3) * BLOCK]
        bp = bias_prev0 if b == 0 else bias_prev
        bn = bias_next_last if b == n_blocks - 1 else bias_next
        heads = [slice(g * HEAD_DIM, (g + 1) * HEAD_DIM) for g in range(GROUP)]
        s_ref, p_ref, r_ref = s_all.at[b], p_all.at[b], r_all.at[b]
        q4 = jnp.concatenate([q_ref[rs, cs] for cs in heads], axis=0)
        s_ref[...] = _dot_nt(q4, kb).reshape(GROUP, BLOCK, 3 * BLOCK)
        for g, cs in enumerate(heads):
            sink2 = sink_ref[GROUP * h + g] * LOG2E
            for r in range(BLOCK // SLAB_ATT):
                rr = slice(r * SLAB_ATT, (r + 1) * SLAB_ATT)
                s = s_ref[g, rr, :]
                s_p = s[:, :BLOCK] + bp[rr]
                s_c = s[:, BLOCK:2 * BLOCK]
                s_n = s[:, 2 * BLOCK:] + bn[rr]
                m = jnp.maximum(jnp.max(jnp.maximum(jnp.maximum(s_p, s_c), s_n), axis=-1, keepdims=True), sink2)
                p_p, p_c, p_n = jnp.exp2(s_p - m), jnp.exp2(s_c - m), jnp.exp2(s_n - m)
                denom = jnp.sum(p_p + p_c + p_n, axis=-1, keepdims=True) + jnp.exp2(sink2 - m)
                p_ref[g, rr, :] = jnp.concatenate([p_p, p_c, p_n], axis=1).astype(BF16)
                r_ref[g, rr, :] = jnp.broadcast_to(1.0 / denom, (SLAB_ATT, LANES))
        o4 = jnp.dot(p_ref[...].reshape(GROUP * BLOCK, 3 * BLOCK), vb, preferred_element_type=F32)
        for g, cs in enumerate(heads):
            o = o4[g * BLOCK:(g + 1) * BLOCK] * r_ref[g]
            o_ref[rs, cs] = (_sigmoid(ga_ref[rs, cs].astype(F32)) * o).astype(o_ref.dtype)


def _window_attention(slab, sink, seq_groups):
    T = slab.shape[0]
    n_tiles = T // TQ_ATT
    bpt = TQ_ATT // BLOCK
    last_blk = T // BLOCK - 1
    gw = GROUP * HEAD_DIM
    assert A_K % HEAD_DIM == 0 and A_V % HEAD_DIM == 0 and A_GA % gw == 0
    kcol, vcol, gcol = A_K // HEAD_DIM, A_V // HEAD_DIM, A_GA // gw
    nbytes = (2 * (3 * TQ_ATT * gw * 2 + 2 * TQ_ATT * HEAD_DIM * 2 + 4 * BLOCK * HEAD_DIM * 2)
              + bpt * GROUP * BLOCK * (3 * BLOCK * 6 + LANES * 4))
    return pl.pallas_call(
        functools.partial(_attn_kernel, seq_groups),
        grid=(n_tiles, N_KV_HEADS),
        in_specs=[
            pl.BlockSpec(memory_space=pltpu.SMEM),
            pl.BlockSpec((TQ_ATT, gw), lambda t, h: (t, h)),
            pl.BlockSpec((TQ_ATT, HEAD_DIM), lambda t, h: (t, kcol + h)),
            pl.BlockSpec((TQ_ATT, HEAD_DIM), lambda t, h: (t, vcol + h)),
            pl.BlockSpec((BLOCK, HEAD_DIM), lambda t, h: (jnp.maximum(t * bpt - 1, 0), kcol + h)),
            pl.BlockSpec((BLOCK, HEAD_DIM), lambda t, h: (jnp.minimum((t + 1) * bpt, last_blk), kcol + h)),
            pl.BlockSpec((BLOCK, HEAD_DIM), lambda t, h: (jnp.maximum(t * bpt - 1, 0), vcol + h)),
            pl.BlockSpec((BLOCK, HEAD_DIM), lambda t, h: (jnp.minimum((t + 1) * bpt, last_blk), vcol + h)),
            pl.BlockSpec((TQ_ATT, gw), lambda t, h: (t, gcol + h)),
        ],
        out_specs=pl.BlockSpec((TQ_ATT, gw), lambda t, h: (t, h)),
        out_shape=jax.ShapeDtypeStruct((T, D_ATT), BF16),
        scratch_shapes=[pltpu.VMEM((bpt, GROUP, BLOCK, 3 * BLOCK), F32),
                        pltpu.VMEM((bpt, GROUP, BLOCK, 3 * BLOCK), BF16),
                        pltpu.VMEM((bpt, GROUP, BLOCK, LANES), F32)],
        compiler_params=pltpu.CompilerParams(
            dimension_semantics=("parallel", "arbitrary"),
            vmem_limit_bytes=_vmem_limit(nbytes)),
        name="window_attention",
    )(sink, slab, slab, slab, slab, slab, slab, slab, slab)


def _hgrn_kernel(seq_groups, lb_ref, qf_ref, if_ref, zf_ref, qb_ref, ib_ref, zb_ref,
                 of_ref, ob_ref, st_ref, stb_ref, ds_ref, ops_ref, cross_ref, dec_ref, a_ref):
    g = pl.program_id(1)
    _, local, _ = _segment_position(g, seq_groups)

    @pl.when(local == 0)
    def _():
        st_ref[...] = jnp.zeros(st_ref.shape, F32)
        stb_ref[...] = jnp.zeros(stb_ref.shape, BF16)

    Q_IN, K_IN, Q_ST, K_ST, I_C = range(5)
    pair = 2 * CHUNK
    n_pairs = SEG_REC // pair
    rin = lax.broadcasted_iota(jnp.int32, (CHUNK, LANES), 0)
    ti = lax.broadcasted_iota(jnp.int32, (CHUNK, CHUNK), 0)
    si = lax.broadcasted_iota(jnp.int32, (CHUNK, CHUNK), 1)
    half = (slice(0, CHUNK), slice(CHUNK, pair))

    def prepare(d, hh, start, q_ref, i_ref, z_ref):
        cs = slice(hh * LANES, (hh + 1) * LANES)
        lb = lb_ref[:, cs]
        one_m_lb = 1.0 - lb
        q_in, k_in, k_st, q_dc, dec = [], [], [], [], []
        for t in range(2):
            rows = _rows(start + t * CHUNK, CHUNK)
            q = q_ref[rows, cs].astype(F32)
            gate = one_m_lb * _sigmoid(z_ref[rows, cs])
            k = one_m_lb - gate
            logf = jnp.log2(lb + gate)
            b = logf
            for s in (1, 2, 4, 8, 16):
                b = b + jnp.where(rin >= s, pltpu.roll(b, s, 0), 0.0)
            tot = b[CHUNK - 1:CHUNK, :]
            if d == 0:
                c = b
                cref = b[CHUNK // 2 - 1:CHUNK // 2, :]
            else:
                c = tot - b + logf
                cref = c[CHUNK // 2:CHUNK // 2 + 1, :]
            q_in.append(q * jnp.exp2(c - cref))
            k_in.append(k * jnp.exp2(cref - c))
            k_st.append(k_in[t] * jnp.exp2(tot - cref))
            q_dc.append(q_in[t] * jnp.exp2(cref))
            dec.append(jnp.exp2(tot))
            ops_ref[d, hh, Q_IN, half[t]] = q_in[t].astype(BF16)
            ops_ref[d, hh, K_IN, half[t]] = k_in[t].astype(BF16)
            ops_ref[d, hh, I_C, half[t]] = i_ref[rows, cs]
        ca, cb = (0, 1) if d == 0 else (1, 0)
        ops_ref[d, hh, Q_ST, half[ca]] = q_dc[ca].astype(BF16)
        ops_ref[d, hh, Q_ST, half[cb]] = (q_dc[cb] * dec[ca]).astype(BF16)
        ops_ref[d, hh, K_ST, half[ca]] = (k_st[ca] * dec[cb]).astype(BF16)
        ops_ref[d, hh, K_ST, half[cb]] = k_st[cb].astype(BF16)
        cross_ref[d, hh, 0] = q_dc[cb].astype(BF16)
        cross_ref[d, hh, 1] = k_st[ca].astype(BF16)
        dec_ref[d, hh] = jnp.broadcast_to(dec[0] * dec[1], (8, LANES))

    def intra(d, hh):
        for t in range(2):
            a_ref[d, hh, t] = _dot_nt(ops_ref[d, hh, Q_IN, half[t]], ops_ref[d, hh, K_IN, half[t]])
        a_ref[d, hh, 2] = _dot_nt(cross_ref[d, hh, 0], cross_ref[d, hh, 1])

    def output(d, hh, start, o_ref):
        cs = slice(hh * LANES, (hh + 1) * LANES)
        tri = (ti >= si) if d == 0 else (ti <= si)
        a_top = jnp.where(tri, a_ref[d, hh, 0], 0.0).astype(BF16)
        a_bot = jnp.where(tri, a_ref[d, hh, 1], 0.0).astype(BF16)
        a_x = a_ref[d, hh, 2].astype(BF16)
        i_top, i_bot = ops_ref[d, hh, I_C, half[0]], ops_ref[d, hh, I_C, half[1]]
        inter = jnp.dot(ops_ref[d, hh, Q_ST], stb_ref[d, hh], preferred_element_type=F32)
        if d == 0:
            both = jnp.dot(jnp.concatenate([a_top, a_x], axis=0), i_top, preferred_element_type=F32)
            o_top = both[:CHUNK] + inter[:CHUNK]
            o_bot = both[CHUNK:] + jnp.dot(a_bot, i_bot, preferred_element_type=F32) + inter[CHUNK:]
        else:
            both = jnp.dot(jnp.concatenate([a_x, a_bot], axis=0), i_bot, preferred_element_type=F32)
            o_top = both[:CHUNK] + jnp.dot(a_top, i_top, preferred_element_type=F32) + inter[:CHUNK]
            o_bot = both[CHUNK:] + inter[CHUNK:]
        o_ref[_rows(start, CHUNK), cs] = o_top.astype(o_ref.dtype)
        o_ref[_rows(start + CHUNK, CHUNK), cs] = o_bot.astype(o_ref.dtype)
        ds_ref[d, hh] = _dot_tn(ops_ref[d, hh, I_C], ops_ref[d, hh, K_ST])

    def update(d, hh):
        dec = dec_ref[d, hh, 0:1, :]
        for r in range(REC_VAL_DIM // CHUNK):
            rr = slice(r * CHUNK, (r + 1) * CHUNK)
            new = dec * st_ref[d, hh, rr, :] + ds_ref[d, hh, rr, :]
            st_ref[d, hh, rr, :] = new
        stb_ref[d, hh] = st_ref[d, hh].T.astype(BF16)

    def scan(n, carry):
        start_f = pl.multiple_of(n * pair, pair)
        start_b = pl.multiple_of((n_pairs - 1 - n) * pair, pair)
        for hh in range(HB_REC):
            prepare(0, hh, start_f, qf_ref, if_ref, zf_ref)
            prepare(1, hh, start_b, qb_ref, ib_ref, zb_ref)
        for hh in range(HB_REC):
            intra(0, hh)
            intra(1, hh)
        for hh in range(HB_REC):
            output(0, hh, start_f, of_ref)
            output(1, hh, start_b, ob_ref)
        for hh in range(HB_REC):
            update(0, hh)
            update(1, hh)
        return carry

    lax.fori_loop(0, n_pairs, scan, 0)


def _hgrn2(slab, gates, lb_row, seq_groups):
    T = slab.shape[0]
    w = HB_REC * LANES
    assert A_QR % w == 0 and A_IR % w == 0 and D_REC_K % w == 0

    def mirror(g):
        first, local, per = _segment_position(g, seq_groups)
        return first + per - 1 - local

    def fspec(off):
        return pl.BlockSpec((SEG_REC, w), lambda hb, g: (g, off // w + hb))

    def bspec(off):
        return pl.BlockSpec((SEG_REC, w), lambda hb, g: (mirror(g), off // w + hb))

    state = (2, HB_REC, REC_VAL_DIM, REC_KEY_DIM)
    nbytes = (2 * (4 * SEG_REC * w * 2 + 2 * SEG_REC * w * 4 + 2 * SEG_REC * w * 2)
              + 2 * HB_REC * (REC_VAL_DIM * REC_KEY_DIM * 10 + 12 * CHUNK * LANES * 2 + 8 * LANES * 4))
    out = jax.ShapeDtypeStruct((T, D_REC_V), BF16)
    return pl.pallas_call(
        functools.partial(_hgrn_kernel, seq_groups),
        grid=(N_REC_HEADS // HB_REC, T // SEG_REC),
        in_specs=[
            pl.BlockSpec((1, w), lambda hb, g: (0, hb)),
            fspec(A_QR), fspec(A_IR), fspec(0),
            bspec(A_QR), bspec(A_IR), bspec(D_REC_K),
        ],
        out_specs=[pl.BlockSpec((SEG_REC, w), lambda hb, g: (g, hb)),
                   pl.BlockSpec((SEG_REC, w), lambda hb, g: (mirror(g), hb))],
        out_shape=[out, out],
        scratch_shapes=[pltpu.VMEM(state, F32), pltpu.VMEM(state, BF16), pltpu.VMEM(state, F32),
                        pltpu.VMEM((2, HB_REC, 5, 2 * CHUNK, LANES), BF16),
                        pltpu.VMEM((2, HB_REC, 2, CHUNK, LANES), BF16),
                        pltpu.VMEM((2, HB_REC, 8, LANES), F32),
                        pltpu.VMEM((2, HB_REC, 3, CHUNK, CHUNK), F32)],
        compiler_params=pltpu.CompilerParams(
            dimension_semantics=("parallel", "arbitrary"),
            vmem_limit_bytes=_vmem_limit(nbytes)),
        name="hgrn2",
    )(lb_row, slab, slab, gates, slab, slab, gates)


def _outproj_kernel(n_first, a_ref, of_ref, ob_ref, grl_ref, grh_ref, gtl_ref, gth_ref, xa_ref, xb_ref,
                    w_ref, rg_ref, g_ref, h_ref, m_ref):
    half = D_REC_V // 2
    heads_per_chunk = KC_OUT // REC_VAL_DIM
    y = None
    for c in range(D_MODEL // KC_OUT):
        for hh in range(c * heads_per_chunk, (c + 1) * heads_per_chunk):
            cs = slice(hh * REC_VAL_DIM, (hh + 1) * REC_VAL_DIM)
            lo = hh * REC_VAL_DIM < half
            hs = cs if lo else slice(hh * REC_VAL_DIM - half, (hh + 1) * REC_VAL_DIM - half)
            rec = of_ref[:, cs].astype(F32) + ob_ref[:, cs].astype(F32)
            ms = jnp.mean(rec * rec, axis=-1, keepdims=True)
            r = (rec * lax.rsqrt(ms + RMS_EPS)) * rg_ref[:, cs]
            gr = (grl_ref if lo else grh_ref)[:, hs].astype(F32)
            gt = (gtl_ref if lo else gth_ref)[:, hs].astype(F32)
            r = _sigmoid(gt) * (r * (gr * _sigmoid(gr)))
            m_ref[:, cs] = (a_ref[:, cs].astype(F32) + r).astype(BF16)
        ks = slice(c * KC_OUT, (c + 1) * KC_OUT)
        part = jnp.dot(m_ref[:, ks], w_ref[ks, :], preferred_element_type=F32)
        y = part if y is None else y + part
    ms = jnp.mean(y * y, axis=-1, keepdims=True)
    h_ref[...] = (y * lax.rsqrt(ms + RMS_EPS)) * g_ref[...]

    @pl.when(pl.program_id(0) < n_first)
    def _():
        h_ref[...] += xa_ref[...]

    @pl.when(pl.program_id(0) >= n_first)
    def _():
        h_ref[...] += xb_ref[...]


def _out_projection(attn, rec_f, rec_b, slab, xa, xb, w_bf16, rec_gain, gain):
    T = attn.shape[0]
    n_first = xa.shape[0] // TM_OUT
    half = D_REC_V // 2
    assert A_GR % half == 0 and A_GT % half == 0
    tok = lambda i: (i, 0)
    nbytes = (2 * 3 * TM_OUT * D_MODEL * 2 + 2 * 4 * TM_OUT * half * 2 + 2 * 3 * TM_OUT * D_MODEL * 4
              + 2 * D_MODEL * D_MODEL * 2 + TM_OUT * D_MODEL * 2 + 2 * TM_OUT * D_MODEL * 4)
    return pl.pallas_call(
        functools.partial(_outproj_kernel, n_first),
        grid=(T // TM_OUT,),
        in_specs=[
            pl.BlockSpec((TM_OUT, D_MODEL), tok),
            pl.BlockSpec((TM_OUT, D_REC_V), tok),
            pl.BlockSpec((TM_OUT, D_REC_V), tok),
            pl.BlockSpec((TM_OUT, half), lambda i: (i, A_GR // half)),
            pl.BlockSpec((TM_OUT, half), lambda i: (i, A_GR // half + 1)),
            pl.BlockSpec((TM_OUT, half), lambda i: (i, A_GT // half)),
            pl.BlockSpec((TM_OUT, half), lambda i: (i, A_GT // half + 1)),
            *_two_source_specs(TM_OUT, n_first),
            pl.BlockSpec((D_MODEL, D_MODEL), lambda i: (0, 0)),
            pl.BlockSpec((1, D_REC_V), lambda i: (0, 0)),
            pl.BlockSpec((1, D_MODEL), lambda i: (0, 0)),
        ],
        out_specs=pl.BlockSpec((TM_OUT, D_MODEL), tok),
        out_shape=jax.ShapeDtypeStruct((T, D_MODEL), F32),
        scratch_shapes=[pltpu.VMEM((TM_OUT, D_MODEL), BF16)],
        compiler_params=pltpu.CompilerParams(
            dimension_semantics=("parallel",),
            vmem_limit_bytes=_vmem_limit(nbytes)),
        name="out_projection",
    )(attn, rec_f, rec_b, slab, slab, slab, slab, xa, xb, w_bf16, rec_gain, gain)


def _ffn_kernel(n_first, h_ref, gpre_ref, gpost_ref, wg_ref, wu_ref, wd_ref, oa_ref, ob_ref, hn_ref, acc_ref):
    i = pl.program_id(0)
    j = pl.program_id(1)

    @pl.when(j == 0)
    def _():
        def body(rows):
            h = h_ref[rows, :]
            ms = jnp.mean(h * h, axis=-1, keepdims=True)
            hn_ref[rows, :] = ((h * lax.rsqrt(ms + RMS_EPS)) * gpre_ref[...]).astype(BF16)
            acc_ref[rows, :] = jnp.zeros((ROWS_EW, D_MODEL), F32)
        _row_tiles(TM_FFN, body)

    hn = hn_ref[...]
    g = jnp.dot(hn, wg_ref[...], preferred_element_type=F32)
    u = jnp.dot(hn, wu_ref[...], preferred_element_type=F32)
    act = ((g * _sigmoid(g)) * u).astype(BF16)
    acc_ref[...] += jnp.dot(act, wd_ref[...], preferred_element_type=F32)

    def finish(o_ref):
        def body(rows):
            y = acc_ref[rows, :]
            ms = jnp.mean(y * y, axis=-1, keepdims=True)
            o_ref[rows, :] = h_ref[rows, :] + (y * lax.rsqrt(ms + RMS_EPS)) * gpost_ref[...]
        _row_tiles(TM_FFN, body)

    last = j == pl.num_programs(1) - 1

    @pl.when(jnp.logical_and(last, i < n_first))
    def _():
        finish(oa_ref)

    @pl.when(jnp.logical_and(last, i >= n_first))
    def _():
        finish(ob_ref)


def _ffn(h, n_tokens_first, gpre, gpost, wg, wu, wd):
    T = h.shape[0]
    n_first = n_tokens_first // TM_FFN
    nbytes = (2 * 3 * TM_FFN * D_MODEL * 4 + TM_FFN * D_MODEL * 2 + 2 * 3 * D_MODEL * TF_FFN * 2
              + 3 * TM_FFN * TF_FFN * 4 + 2 * TM_FFN * D_MODEL * 4)
    return pl.pallas_call(
        functools.partial(_ffn_kernel, n_first),
        grid=(T // TM_FFN, D_FF // TF_FFN),
        in_specs=[
            pl.BlockSpec((TM_FFN, D_MODEL), lambda i, j: (i, 0)),
            pl.BlockSpec((1, D_MODEL), lambda i, j: (0, 0)),
            pl.BlockSpec((1, D_MODEL), lambda i, j: (0, 0)),
            pl.BlockSpec((D_MODEL, TF_FFN), lambda i, j: (0, j)),
            pl.BlockSpec((D_MODEL, TF_FFN), lambda i, j: (0, j)),
            pl.BlockSpec((TF_FFN, D_MODEL), lambda i, j: (j, 0)),
        ],
        out_specs=list(_two_source_specs(TM_FFN, n_first)),
        out_shape=[jax.ShapeDtypeStruct((n_tokens_first, D_MODEL), F32),
                   jax.ShapeDtypeStruct((T - n_tokens_first, D_MODEL), F32)],
        scratch_shapes=[pltpu.VMEM((TM_FFN, D_MODEL), BF16), pltpu.VMEM((TM_FFN, D_MODEL), F32)],
        compiler_params=pltpu.CompilerParams(
            dimension_semantics=("arbitrary", "arbitrary"),
            vmem_limit_bytes=_vmem_limit(nbytes)),
        name="swiglu_ffn",
    )(h, gpre, gpost, wg, wu, wd)


def _rope_tables(max_len):
    pos = jnp.arange(max_len, dtype=F32)
    inv_freq = ROPE_THETA ** (-jnp.arange(ROPE_HALF, dtype=F32) / ROPE_HALF)
    ang = pos[:, None] * inv_freq[None, :]
    cos, sin = jnp.cos(ang), jnp.sin(ang)
    rest = HEAD_DIM - ROPE_DIM
    cos_t = jnp.concatenate([cos, cos, jnp.ones((max_len, rest), F32)], axis=1)
    sa_t = jnp.concatenate([-sin, jnp.zeros((max_len, HEAD_DIM - ROPE_HALF), F32)], axis=1)
    sb_t = jnp.concatenate([jnp.zeros((max_len, ROPE_HALF), F32), sin, jnp.zeros((max_len, rest), F32)], axis=1)
    return cos_t, sa_t, sb_t


def _seq_groups(seq_shapes, seg):
    groups, first = [], 0
    for B, L in seq_shapes:
        assert L % seg == 0
        groups.append((first, L // seg))
        first += B * L // seg
    return tuple(groups)


def _encoder_layer(xa, xb, seq_shapes, tables, w_in, sink, rec_norm, lb, w_out, norm_mix_pre, norm_mix_post,
                   norm_ffn_pre, norm_ffn_post, w_gate, w_up, w_down):
    row = lambda v: v.astype(F32).reshape(1, -1)
    xn = _stream_norm(xa, xb, row(norm_mix_pre))

    assert all(off % TN_IN == 0 and w % TN_IN == 0 for off, w in SLAB_GROUPS) and OFF_ZF % TN_IN == 0
    gate_tiles = list(range(OFF_ZF // TN_IN, OFF_IR // TN_IN))
    slab_tiles = [t for off, w in SLAB_GROUPS for t in range(off // TN_IN, (off + w) // TN_IN)]
    kind_of = lambda t: ("rope" if t < OFF_K // TN_IN else "rope_k" if t == OFF_K // TN_IN
                         else "silu" if OFF_QR // TN_IN <= t < OFF_ZF // TN_IN else "plain")
    table_groups = _seq_groups(seq_shapes, TM_IN)
    slab = _in_projection(xn, w_in, tables, table_groups, slab_tiles, tuple(kind_of(t) for t in slab_tiles),
                          BF16, "in_projection")
    gates = _in_projection(xn, w_in, None, None, gate_tiles, ("plain",) * len(gate_tiles),
                           F32, "gate_projection")

    attn = _window_attention(slab, sink.astype(F32), _seq_groups(seq_shapes, TQ_ATT))
    rec_f, rec_b = _hgrn2(slab, gates, row(lb), _seq_groups(seq_shapes, SEG_REC))
    h = _out_projection(attn, rec_f, rec_b, slab, xa, xb, w_out.astype(BF16), row(rec_norm),
                        row(norm_mix_post))
    return _ffn(h, xa.shape[0], row(norm_ffn_pre), row(norm_ffn_post),
                w_gate.astype(BF16), w_up.astype(BF16), w_down.astype(BF16))


def kernel(x_prompt, x_sample, w_in, sink, rec_norm, lb_logits, w_out, norm_mix_pre, norm_mix_post,
           norm_ffn_pre, norm_ffn_post, w_gate, w_up, w_down):
    lb_all = jnp.cumsum(jax.nn.softmax(lb_logits.astype(F32), axis=0), axis=0)
    seq_shapes = (x_prompt.shape[:2], x_sample.shape[:2])
    xa = x_prompt.reshape(-1, D_MODEL)
    xb = x_sample.reshape(-1, D_MODEL)
    tables = _rope_tables(max(L for _, L in seq_shapes))
    for l in range(DEPTH):
        xa, xb = _encoder_layer(xa, xb, seq_shapes, tables, w_in[l], sink[l], rec_norm[l], lb_all[l], w_out[l],
                                norm_mix_pre[l], norm_mix_post[l], norm_ffn_pre[l], norm_ffn_post[l],
                                w_gate[l], w_up[l], w_down[l])
    return (xa.reshape(x_prompt.shape), xb.reshape(x_sample.shape))
```

```python
import functools

import jax
import jax.numpy as jnp
import numpy as np
from jax import lax
from jax.experimental import pallas as pl
from jax.experimental.pallas import tpu as pltpu

F32 = jnp.float32
BF16 = jnp.bfloat16

D_MODEL = 2048
DEPTH = 1
HEAD_DIM = 128
N_Q_HEADS = 16
N_KV_HEADS = 4
GROUP = N_Q_HEADS // N_KV_HEADS
WINDOW = 128
BLOCK = 128
ROPE_DIM = HEAD_DIM // 4
ROPE_HALF = ROPE_DIM // 2
ROPE_THETA = 500000.0
N_REC_HEADS = 16
REC_KEY_DIM = 128
REC_VAL_DIM = 128
CHUNK = 32
D_FF = -(-8 * D_MODEL // (3 * 256)) * 256
RMS_EPS = 1e-6
LOG2E = 1.4426950408889634
Q_SCALE = HEAD_DIM ** -0.5 * LOG2E

D_ATT = N_Q_HEADS * HEAD_DIM
D_KV = N_KV_HEADS * HEAD_DIM
D_REC_K = N_REC_HEADS * REC_KEY_DIM
D_REC_V = N_REC_HEADS * REC_VAL_DIM
SPLIT_SIZES = (D_ATT, D_KV, D_KV, D_REC_K, D_REC_K, D_REC_K, D_REC_V, D_REC_V, D_MODEL, D_MODEL)
D_IN = sum(SPLIT_SIZES)
(OFF_Q, OFF_K, OFF_V, OFF_QR, OFF_ZF, OFF_ZB, OFF_IR, OFF_GR, OFF_GA, OFF_GT) = (
    int(v) for v in np.concatenate([[0], np.cumsum(SPLIT_SIZES)[:-1]]))
SLAB_GROUPS = ((OFF_Q, D_ATT), (OFF_QR, D_REC_K), (OFF_IR, D_REC_V), (OFF_GR, D_REC_V), (OFF_GA, D_MODEL),
               (OFF_GT, D_MODEL), (OFF_K, 2 * D_KV))
A_Q, A_QR, A_IR, A_GR, A_GA, A_GT, A_K = (int(v) for v in np.cumsum([0] + [w for _, w in SLAB_GROUPS])[:-1])
A_V = A_K + D_KV

V7X_VMEM_CEILING = 56 * 1024 * 1024
LANES = 128

TM_NORM = 512
TM_IN, TN_IN = 1024, 1024
SUB_IN = 256
ROWS_EW = 256
SLAB_ATT = 32
TQ_ATT = 1024
SEG_REC = 512
HB_REC = 16
TM_OUT = 256
KC_OUT = 512
TM_FFN, TF_FFN = 512, 512


def _vmem_limit(nbytes):
    return int(min(V7X_VMEM_CEILING, nbytes * 1.25 + (4 << 20)))


def _sigmoid(x):
    return 1.0 / (1.0 + jnp.exp(-x))


def _dot_nt(a, b):
    return lax.dot_general(a, b, (((1,), (1,)), ((), ())), preferred_element_type=F32)


def _dot_tn(a, b):
    return lax.dot_general(a, b, (((0,), (0,)), ((), ())), preferred_element_type=F32)


def _rows(start, size):
    return pl.ds(start if isinstance(start, int) else pl.multiple_of(start, size), size)


def _row_tiles(n_rows, body):
    def step(r, c):
        body(_rows(r * ROWS_EW, ROWS_EW))
        return c
    lax.fori_loop(0, n_rows // ROWS_EW, step, 0)


def _segment_position(g, seq_groups):
    first, local, per = None, None, None
    for g0, n in reversed(seq_groups):
        loc = lax.rem(g - g0, n)
        fst = g - loc
        if first is None:
            first, local, per = fst, loc, n
        else:
            here = g < nxt
            first = jnp.where(here, fst, first)
            local = jnp.where(here, loc, local)
            per = jnp.where(here, n, per)
        nxt = g0
    return first, local, per


def _two_source_specs(tm, n_first):
    return (pl.BlockSpec((tm, D_MODEL), lambda i, *_: (jnp.minimum(i, n_first - 1), 0)),
            pl.BlockSpec((tm, D_MODEL), lambda i, *_: (jnp.maximum(i - n_first, 0), 0)))


def _norm_kernel(n_first, xa_ref, xb_ref, g_ref, o_ref):
    def run(x_ref):
        def body(rows):
            x = x_ref[rows, :]
            ms = jnp.mean(x * x, axis=-1, keepdims=True)
            o_ref[rows, :] = ((x * lax.rsqrt(ms + RMS_EPS)) * g_ref[...]).astype(o_ref.dtype)
        _row_tiles(TM_NORM, body)

    @pl.when(pl.program_id(0) < n_first)
    def _():
        run(xa_ref)

    @pl.when(pl.program_id(0) >= n_first)
    def _():
        run(xb_ref)


def _stream_norm(xa, xb, gain):
    T = xa.shape[0] + xb.shape[0]
    n_first = xa.shape[0] // TM_NORM
    nbytes = 2 * 2 * TM_NORM * D_MODEL * 4 + 2 * TM_NORM * D_MODEL * 2
    return pl.pallas_call(
        functools.partial(_norm_kernel, n_first),
        grid=(T // TM_NORM,),
        in_specs=[*_two_source_specs(TM_NORM, n_first), pl.BlockSpec((1, D_MODEL), lambda i: (0, 0))],
        out_specs=pl.BlockSpec((TM_NORM, D_MODEL), lambda i: (i, 0)),
        out_shape=jax.ShapeDtypeStruct((T, D_MODEL), BF16),
        compiler_params=pltpu.CompilerParams(
            dimension_semantics=("parallel",), vmem_limit_bytes=_vmem_limit(nbytes)),
        name="stream_norm",
    )(xa, xb, gain)


def _inproj_kernel(kinds, col_tiles_ref, *refs):
    del col_tiles_ref
    with_epilogue = any(k != "plain" for k in kinds)
    if with_epilogue:
        xn_ref, cos_ref, sa_ref, sb_ref, w_ref, o_ref, wb_ref, y_ref = refs
    else:
        xn_ref, w_ref, o_ref, wb_ref = refs
    j = pl.program_id(0)

    @pl.when(pl.program_id(1) == 0)
    def _():
        def body(rows):
            wb_ref[rows, :] = w_ref[rows, :].astype(BF16)
        _row_tiles(D_MODEL, body)

    def transform(kind, y, rows, lane_tile):
        if kind == "silu":
            return y * _sigmoid(y)
        if kind == "rope_k" and lane_tile >= N_KV_HEADS:
            return y
        y = (y * cos_ref[rows, :] + pltpu.roll(y, HEAD_DIM - ROPE_HALF, 1) * sa_ref[rows, :]
             + pltpu.roll(y, ROPE_HALF, 1) * sb_ref[rows, :])
        return y * Q_SCALE if kind == "rope" else y

    def epilogue(kind, sub):
        for r in range(TM_IN // ROWS_EW):
            rows = slice(r * ROWS_EW, (r + 1) * ROWS_EW)
            for cc in range(SUB_IN // LANES):
                lane_tile = sub * (SUB_IN // LANES) + cc
                y = y_ref[sub, rows, cc * LANES:(cc + 1) * LANES]
                o_ref[rows, lane_tile * LANES:(lane_tile + 1) * LANES] = (
                    transform(kind, y, rows, lane_tile).astype(o_ref.dtype))

    for kind in sorted(set(kinds)):
        tiles = [t for t, k in enumerate(kinds) if k == kind]
        cond = functools.reduce(jnp.logical_or, [j == t for t in tiles])

        @pl.when(cond)
        def _(kind=kind):
            if kind == "plain":
                o_ref[...] = jnp.dot(xn_ref[...], wb_ref[...], preferred_element_type=F32).astype(o_ref.dtype)
            else:
                n_sub = TN_IN // SUB_IN
                for sub in range(n_sub):
                    y_ref[sub] = jnp.dot(xn_ref[...], wb_ref[:, sub * SUB_IN:(sub + 1) * SUB_IN],
                                         preferred_element_type=F32)
                    if sub > 0:
                        epilogue(kind, sub - 1)
                epilogue(kind, n_sub - 1)


def _in_projection(xn, w_in, tables, table_groups, col_tiles, kinds, out_dtype, name):
    T = xn.shape[0]
    n_col = len(col_tiles)
    col_tiles = jnp.asarray(col_tiles, jnp.int32)
    with_epilogue = any(k != "plain" for k in kinds)
    rope_tiles = [t for t, k in enumerate(kinds) if k.startswith("rope")]
    out_bytes = jnp.dtype(out_dtype).itemsize

    def table_map(j, i, ct):
        _, local, _ = _segment_position(i, table_groups)
        uses_tables = functools.reduce(jnp.logical_or, [j == t for t in rope_tiles])
        return (jnp.where(uses_tables, local, 0), 0)

    in_specs = [pl.BlockSpec((TM_IN, D_MODEL), lambda j, i, ct: (i, 0))]
    args = [xn]
    scratch = [pltpu.VMEM((D_MODEL, TN_IN), BF16)]
    nbytes = (2 * TM_IN * D_MODEL * 2 + 2 * D_MODEL * TN_IN * 4 + D_MODEL * TN_IN * 2
              + 2 * TM_IN * TN_IN * out_bytes + TM_IN * TN_IN * 4)
    if with_epilogue:
        in_specs += [pl.BlockSpec((TM_IN, LANES), table_map)] * 3
        args += list(tables)
        scratch.append(pltpu.VMEM((TN_IN // SUB_IN, TM_IN, SUB_IN), F32))
        nbytes += 2 * 3 * TM_IN * LANES * 4 + TM_IN * TN_IN * 4
    in_specs.append(pl.BlockSpec((D_MODEL, TN_IN), lambda j, i, ct: (0, ct[j])))
    args.append(w_in)
    return pl.pallas_call(
        functools.partial(_inproj_kernel, kinds),
        grid_spec=pltpu.PrefetchScalarGridSpec(
            num_scalar_prefetch=1,
            grid=(n_col, T // TM_IN),
            in_specs=in_specs,
            out_specs=pl.BlockSpec((TM_IN, TN_IN), lambda j, i, ct: (i, j)),
            scratch_shapes=scratch),
        out_shape=jax.ShapeDtypeStruct((T, n_col * TN_IN), out_dtype),
        compiler_params=pltpu.CompilerParams(
            dimension_semantics=("arbitrary", "arbitrary"),
            vmem_limit_bytes=_vmem_limit(nbytes)),
        name=name,
    )(col_tiles, *args)


def _attn_kernel(seq_groups, sink_ref, q_ref, kc_ref, vc_ref, kp_ref, kn_ref, vp_ref, vn_ref,
                 ga_ref, o_ref, s_all, p_all, r_all):
    t = pl.program_id(0)
    h = pl.program_id(1)
    _, local, per = _segment_position(t, seq_groups)
    has_prev = local != 0
    has_next = local != per - 1

    k_all = jnp.concatenate([kp_ref[...], kc_ref[...], kn_ref[...]], axis=0)
    v_all = jnp.concatenate([vp_ref[...], vc_ref[...], vn_ref[...]], axis=0)
    qi = lax.broadcasted_iota(jnp.int32, (BLOCK, BLOCK), 0)
    kj = lax.broadcasted_iota(jnp.int32, (BLOCK, BLOCK), 1)
    neg = jnp.float32(-jnp.inf)
    bias_prev = jnp.where(kj >= qi, 0.0, neg)
    bias_next = jnp.where(kj <= qi, 0.0, neg)
    bias_prev0 = jnp.where(has_prev, bias_prev, neg)
    bias_next_last = jnp.where(has_next, bias_next, neg)
    n_blocks = TQ_ATT // BLOCK
    for b in range(n_blocks):
        rs = slice(b * BLOCK, (b + 1) * BLOCK)
        kb = k_all[b * BLOCK:(b + 3) * BLOCK]
        vb = v_all[b * BLOCK:(b + 3) * BLOCK]
        bp = bias_prev0 if b == 0 else bias_prev
        bn = bias_next_last if b == n_blocks - 1 else bias_next
        heads = [slice(g * HEAD_DIM, (g + 1) * HEAD_DIM) for g in range(GROUP)]
        s_ref, p_ref, r_ref = s_all.at[b], p_all.at[b], r_all.at[b]
        q4 = jnp.concatenate([q_ref[rs, cs] for cs in heads], axis=0)
        s_ref[...] = _dot_nt(q4, kb).reshape(GROUP, BLOCK, 3 * BLOCK)
        for g, cs in enumerate(heads):
            sink2 = sink_ref[GROUP * h + g] * LOG2E
            for r in range(BLOCK // SLAB_ATT):
                rr = slice(r * SLAB_ATT, (r + 1) * SLAB_ATT)
                s = s_ref[g, rr, :]
                s_p = s[:, :BLOCK] + bp[rr]
                s_c = s[:, BLOCK:2 * BLOCK]
                s_n = s[:, 2 * BLOCK:] + bn[rr]
                m = jnp.maximum(jnp.max(jnp.maximum(jnp.maximum(s_p, s_c), s_n), axis=-1, keepdims=True), sink2)
                p_p, p_c, p_n = jnp.exp2(s_p - m), jnp.exp2(s_c - m), jnp.exp2(s_n - m)
                denom = jnp.sum(p_p + p_c + p_n, axis=-1, keepdims=True) + jnp.exp2(sink2 - m)
                p_ref[g, rr, :] = jnp.concatenate([p_p, p_c, p_n], axis=1).astype(BF16)
                r_ref[g, rr, :] = jnp.broadcast_to(1.0 / denom, (SLAB_ATT, LANES))
        o4 = jnp.dot(p_ref[...].reshape(GROUP * BLOCK, 3 * BLOCK), vb, preferred_element_type=F32)
        for g, cs in enumerate(heads):
            o = o4[g * BLOCK:(g + 1) * BLOCK] * r_ref[g]
            o_ref[rs, cs] = (_sigmoid(ga_ref[rs, cs].astype(F32)) * o).astype(o_ref.dtype)


def _window_attention(slab, sink, seq_groups):
    T = slab.shape[0]
    n_tiles = T // TQ_ATT
    bpt = TQ_ATT // BLOCK
    last_blk = T // BLOCK - 1
    gw = GROUP * HEAD_DIM
    assert A_K % HEAD_DIM == 0 and A_V % HEAD_DIM == 0 and A_GA % gw == 0
    kcol, vcol, gcol = A_K // HEAD_DIM, A_V // HEAD_DIM, A_GA // gw
    nbytes = (2 * (3 * TQ_ATT * gw * 2 + 2 * TQ_ATT * HEAD_DIM * 2 + 4 * BLOCK * HEAD_DIM * 2)
              + bpt * GROUP * BLOCK * (3 * BLOCK * 6 + LANES * 4))
    return pl.pallas_call(
        functools.partial(_attn_kernel, seq_groups),
        grid=(n_tiles, N_KV_HEADS),
        in_specs=[
            pl.BlockSpec(memory_space=pltpu.SMEM),
            pl.BlockSpec((TQ_ATT, gw), lambda t, h: (t, h)),
            pl.BlockSpec((TQ_ATT, HEAD_DIM), lambda t, h: (t, kcol + h)),
            pl.BlockSpec((TQ_ATT, HEAD_DIM), lambda t, h: (t, vcol + h)),
            pl.BlockSpec((BLOCK, HEAD_DIM), lambda t, h: (jnp.maximum(t * bpt - 1, 0), kcol + h)),
            pl.BlockSpec((BLOCK, HEAD_DIM), lambda t, h: (jnp.minimum((t + 1) * bpt, last_blk), kcol + h)),
            pl.BlockSpec((BLOCK, HEAD_DIM), lambda t, h: (jnp.maximum(t * bpt - 1, 0), vcol + h)),
            pl.BlockSpec((BLOCK, HEAD_DIM), lambda t, h: (jnp.minimum((t + 1) * bpt, last_blk), vcol + h)),
            pl.BlockSpec((TQ_ATT, gw), lambda t, h: (t, gcol + h)),
        ],
        out_specs=pl.BlockSpec((TQ_ATT, gw), lambda t, h: (t, h)),
        out_shape=jax.ShapeDtypeStruct((T, D_ATT), BF16),
        scratch_shapes=[pltpu.VMEM((bpt, GROUP, BLOCK, 3 * BLOCK), F32),
                        pltpu.VMEM((bpt, GROUP, BLOCK, 3 * BLOCK), BF16),
                        pltpu.VMEM((bpt, GROUP, BLOCK, LANES), F32)],
        compiler_params=pltpu.CompilerParams(
            dimension_semantics=("parallel", "arbitrary"),
            vmem_limit_bytes=_vmem_limit(nbytes)),
        name="window_attention",
    )(sink, slab, slab, slab, slab, slab, slab, slab, slab)


def _hgrn_kernel(seq_groups, lb_ref, qf_ref, if_ref, zf_ref, qb_ref, ib_ref, zb_ref,
                 of_ref, ob_ref, st_ref, stb_ref, ds_ref, ops_ref, cross_ref, dec_ref, a_ref):
    g = pl.program_id(1)
    _, local, _ = _segment_position(g, seq_groups)

    @pl.when(local == 0)
    def _():
        st_ref[...] = jnp.zeros(st_ref.shape, F32)
        stb_ref[...] = jnp.zeros(stb_ref.shape, BF16)

    Q_IN, K_IN, Q_ST, K_ST, I_C = range(5)
    pair = 2 * CHUNK
    n_pairs = SEG_REC // pair
    rin = lax.broadcasted_iota(jnp.int32, (CHUNK, LANES), 0)
    ti = lax.broadcasted_iota(jnp.int32, (CHUNK, CHUNK), 0)
    si = lax.broadcasted_iota(jnp.int32, (CHUNK, CHUNK), 1)
    half = (slice(0, CHUNK), slice(CHUNK, pair))

    def prepare(d, hh, start, q_ref, i_ref, z_ref):
        cs = slice(hh * LANES, (hh + 1) * LANES)
        lb = lb_ref[:, cs]
        one_m_lb = 1.0 - lb
        q_in, k_in, k_st, q_dc, dec = [], [], [], [], []
        for t in range(2):
            rows = _rows(start + t * CHUNK, CHUNK)
            q = q_ref[rows, cs].astype(F32)
            gate = one_m_lb * _sigmoid(z_ref[rows, cs])
            k = one_m_lb - gate
            logf = jnp.log2(lb + gate)
            c = logf
            for s in (1, 2, 4, 8, 16):
                if d == 0:
                    c = c + jnp.where(rin >= s, pltpu.roll(c, s, 0), 0.0)
                else:
                    c = c + jnp.where(rin < CHUNK - s, pltpu.roll(c, CHUNK - s, 0), 0.0)
            last, mid = (CHUNK - 1, CHUNK // 2 - 1) if d == 0 else (0, CHUNK // 2)
            tot = c[last:last + 1, :]
            cref = c[mid:mid + 1, :]
            q_in.append(q * jnp.exp2(c - cref))
            k_in.append(k * jnp.exp2(cref - c))
            k_st.append(k_in[t] * jnp.exp2(tot - cref))
            q_dc.append(q_in[t] * jnp.exp2(cref))
            dec.append(jnp.exp2(tot))
            ops_ref[d, hh, Q_IN, half[t]] = q_in[t].astype(BF16)
            ops_ref[d, hh, K_IN, half[t]] = k_in[t].astype(BF16)
            ops_ref[d, hh, I_C, half[t]] = i_ref[rows, cs]
        ca, cb = (0, 1) if d == 0 else (1, 0)
        ops_ref[d, hh, Q_ST, half[ca]] = q_dc[ca].astype(BF16)
        ops_ref[d, hh, Q_ST, half[cb]] = (q_dc[cb] * dec[ca]).astype(BF16)
        ops_ref[d, hh, K_ST, half[ca]] = (k_st[ca] * dec[cb]).astype(BF16)
        ops_ref[d, hh, K_ST, half[cb]] = k_st[cb].astype(BF16)
        cross_ref[d, hh, 0] = q_dc[cb].astype(BF16)
        cross_ref[d, hh, 1] = k_st[ca].astype(BF16)
        dec_ref[d, hh] = jnp.broadcast_to(dec[0] * dec[1], (8, LANES))

    def intra(d, hh):
        for t in range(2):
            a_ref[d, hh, t] = _dot_nt(ops_ref[d, hh, Q_IN, half[t]], ops_ref[d, hh, K_IN, half[t]])
        a_ref[d, hh, 2] = _dot_nt(cross_ref[d, hh, 0], cross_ref[d, hh, 1])

    def output(d, hh, start, o_ref):
        cs = slice(hh * LANES, (hh + 1) * LANES)
        tri = (ti >= si) if d == 0 else (ti <= si)
        a_top = jnp.where(tri, a_ref[d, hh, 0], 0.0).astype(BF16)
        a_bot = jnp.where(tri, a_ref[d, hh, 1], 0.0).astype(BF16)
        a_x = a_ref[d, hh, 2].astype(BF16)
        i_top, i_bot = ops_ref[d, hh, I_C, half[0]], ops_ref[d, hh, I_C, half[1]]
        inter = jnp.dot(ops_ref[d, hh, Q_ST], stb_ref[d, hh], preferred_element_type=F32)
        if d == 0:
            both = jnp.dot(jnp.concatenate([a_top, a_x], axis=0), i_top, preferred_element_type=F32)
            o_top = both[:CHUNK] + inter[:CHUNK]
            o_bot = both[CHUNK:] + jnp.dot(a_bot, i_bot, preferred_element_type=F32) + inter[CHUNK:]
        else:
            both = jnp.dot(jnp.concatenate([a_x, a_bot], axis=0), i_bot, preferred_element_type=F32)
            o_top = both[:CHUNK] + jnp.dot(a_top, i_top, preferred_element_type=F32) + inter[:CHUNK]
            o_bot = both[CHUNK:] + inter[CHUNK:]
        o_ref[_rows(start, CHUNK), cs] = o_top.astype(o_ref.dtype)
        o_ref[_rows(start + CHUNK, CHUNK), cs] = o_bot.astype(o_ref.dtype)
        ds_ref[d, hh] = _dot_tn(ops_ref[d, hh, I_C], ops_ref[d, hh, K_ST])

    def update(d, hh):
        dec = dec_ref[d, hh, 0:1, :]
        for r in range(REC_VAL_DIM // CHUNK):
            rr = slice(r * CHUNK, (r + 1) * CHUNK)
            new = dec * st_ref[d, hh, rr, :] + ds_ref[d, hh, rr, :]
            st_ref[d, hh, rr, :] = new
        stb_ref[d, hh] = st_ref[d, hh].T.astype(BF16)

    def scan(n, carry):
        start_f = pl.multiple_of(n * pair, pair)
        start_b = pl.multiple_of((n_pairs - 1 - n) * pair, pair)
        for hh in range(HB_REC):
            prepare(0, hh, start_f, qf_ref, if_ref, zf_ref)
            prepare(1, hh, start_b, qb_ref, ib_ref, zb_ref)
        for hh in range(HB_REC):
            intra(0, hh)
            intra(1, hh)
        for hh in range(HB_REC):
            output(0, hh, start_f, of_ref)
            output(1, hh, start_b, ob_ref)
        for hh in range(HB_REC):
            update(0, hh)
            update(1, hh)
        return carry

    lax.fori_loop(0, n_pairs, scan, 0)


def _hgrn2(slab, gates, lb_row, seq_groups):
    T = slab.shape[0]
    w = HB_REC * LANES
    assert A_QR % w == 0 and A_IR % w == 0 and D_REC_K % w == 0

    def mirror(g):
        first, local, per = _segment_position(g, seq_groups)
        return first + per - 1 - local

    def fspec(off):
        return pl.BlockSpec((SEG_REC, w), lambda hb, g: (g, off // w + hb))

    def bspec(off):
        return pl.BlockSpec((SEG_REC, w), lambda hb, g: (mirror(g), off // w + hb))

    state = (2, HB_REC, REC_VAL_DIM, REC_KEY_DIM)
    nbytes = (2 * (4 * SEG_REC * w * 2 + 2 * SEG_REC * w * 4 + 2 * SEG_REC * w * 2)
              + 2 * HB_REC * (REC_VAL_DIM * REC_KEY_DIM * 10 + 12 * CHUNK * LANES * 2 + 8 * LANES * 4))
    out = jax.ShapeDtypeStruct((T, D_REC_V), BF16)
    return pl.pallas_call(
        functools.partial(_hgrn_kernel, seq_groups),
        grid=(N_REC_HEADS // HB_REC, T // SEG_REC),
        in_specs=[
            pl.BlockSpec((1, w), lambda hb, g: (0, hb)),
            fspec(A_QR), fspec(A_IR), fspec(0),
            bspec(A_QR), bspec(A_IR), bspec(D_REC_K),
        ],
        out_specs=[pl.BlockSpec((SEG_REC, w), lambda hb, g: (g, hb)),
                   pl.BlockSpec((SEG_REC, w), lambda hb, g: (mirror(g), hb))],
        out_shape=[out, out],
        scratch_shapes=[pltpu.VMEM(state, F32), pltpu.VMEM(state, BF16), pltpu.VMEM(state, F32),
                        pltpu.VMEM((2, HB_REC, 5, 2 * CHUNK, LANES), BF16),
                        pltpu.VMEM((2, HB_REC, 2, CHUNK, LANES), BF16),
                        pltpu.VMEM((2, HB_REC, 8, LANES), F32),
                        pltpu.VMEM((2, HB_REC, 3, CHUNK, CHUNK), F32)],
        compiler_params=pltpu.CompilerParams(
            dimension_semantics=("parallel", "arbitrary"),
            vmem_limit_bytes=_vmem_limit(nbytes)),
        name="hgrn2",
    )(lb_row, slab, slab, gates, slab, slab, gates)


def _outproj_kernel(n_first, a_ref, of_ref, ob_ref, grl_ref, grh_ref, gtl_ref, gth_ref, xa_ref, xb_ref,
                    w_ref, rg_ref, g_ref, h_ref, m_ref):
    half = D_REC_V // 2
    heads_per_chunk = KC_OUT // REC_VAL_DIM
    y = None
    for c in range(D_MODEL // KC_OUT):
        for hh in range(c * heads_per_chunk, (c + 1) * heads_per_chunk):
            cs = slice(hh * REC_VAL_DIM, (hh + 1) * REC_VAL_DIM)
            lo = hh * REC_VAL_DIM < half
            hs = cs if lo else slice(hh * REC_VAL_DIM - half, (hh + 1) * REC_VAL_DIM - half)
            rec = of_ref[:, cs].astype(F32) + ob_ref[:, cs].astype(F32)
            ms = jnp.mean(rec * rec, axis=-1, keepdims=True)
            r = (rec * lax.rsqrt(ms + RMS_EPS)) * rg_ref[:, cs]
            gr = (grl_ref if lo else grh_ref)[:, hs].astype(F32)
            gt = (gtl_ref if lo else gth_ref)[:, hs].astype(F32)
            r = _sigmoid(gt) * (r * (gr * _sigmoid(gr)))
            m_ref[:, cs] = (a_ref[:, cs].astype(F32) + r).astype(BF16)
        ks = slice(c * KC_OUT, (c + 1) * KC_OUT)
        part = jnp.dot(m_ref[:, ks], w_ref[ks, :], preferred_element_type=F32)
        y = part if y is None else y + part
    ms = jnp.mean(y * y, axis=-1, keepdims=True)
    h_ref[...] = (y * lax.rsqrt(ms + RMS_EPS)) * g_ref[...]

    @pl.when(pl.program_id(0) < n_first)
    def _():
        h_ref[...] += xa_ref[...]

    @pl.when(pl.program_id(0) >= n_first)
    def _():
        h_ref[...] += xb_ref[...]


def _out_projection(attn, rec_f, rec_b, slab, xa, xb, w_bf16, rec_gain, gain):
    T = attn.shape[0]
    n_first = xa.shape[0] // TM_OUT
    half = D_REC_V // 2
    assert A_GR % half == 0 and A_GT % half == 0
    tok = lambda i: (i, 0)
    nbytes = (2 * 3 * TM_OUT * D_MODEL * 2 + 2 * 4 * TM_OUT * half * 2 + 2 * 3 * TM_OUT * D_MODEL * 4
              + 2 * D_MODEL * D_MODEL * 2 + TM_OUT * D_MODEL * 2 + 2 * TM_OUT * D_MODEL * 4)
    return pl.pallas_call(
        functools.partial(_outproj_kernel, n_first),
        grid=(T // TM_OUT,),
        in_specs=[
            pl.BlockSpec((TM_OUT, D_MODEL), tok),
            pl.BlockSpec((TM_OUT, D_REC_V), tok),
            pl.BlockSpec((TM_OUT, D_REC_V), tok),
            pl.BlockSpec((TM_OUT, half), lambda i: (i, A_GR // half)),
            pl.BlockSpec((TM_OUT, half), lambda i: (i, A_GR // half + 1)),
            pl.BlockSpec((TM_OUT, half), lambda i: (i, A_GT // half)),
            pl.BlockSpec((TM_OUT, half), lambda i: (i, A_GT // half + 1)),
            *_two_source_specs(TM_OUT, n_first),
            pl.BlockSpec((D_MODEL, D_MODEL), lambda i: (0, 0)),
            pl.BlockSpec((1, D_REC_V), lambda i: (0, 0)),
            pl.BlockSpec((1, D_MODEL), lambda i: (0, 0)),
        ],
        out_specs=pl.BlockSpec((TM_OUT, D_MODEL), tok),
        out_shape=jax.ShapeDtypeStruct((T, D_MODEL), F32),
        scratch_shapes=[pltpu.VMEM((TM_OUT, D_MODEL), BF16)],
        compiler_params=pltpu.CompilerParams(
            dimension_semantics=("parallel",),
            vmem_limit_bytes=_vmem_limit(nbytes)),
        name="out_projection",
    )(attn, rec_f, rec_b, slab, slab, slab, slab, xa, xb, w_bf16, rec_gain, gain)


def _ffn_kernel(n_first, h_ref, gpre_ref, gpost_ref, wg_ref, wu_ref, wd_ref, oa_ref, ob_ref, hn_ref, acc_ref):
    i = pl.program_id(0)
    j = pl.program_id(1)

    @pl.when(j == 0)
    def _():
        def body(rows):
            h = h_ref[rows, :]
            ms = jnp.mean(h * h, axis=-1, keepdims=True)
            hn_ref[rows, :] = ((h * lax.rsqrt(ms + RMS_EPS)) * gpre_ref[...]).astype(BF16)
            acc_ref[rows, :] = jnp.zeros((ROWS_EW, D_MODEL), F32)
        _row_tiles(TM_FFN, body)

    hn = hn_ref[...]
    g = jnp.dot(hn, wg_ref[...], preferred_element_type=F32)
    u = jnp.dot(hn, wu_ref[...], preferred_element_type=F32)
    act = ((g * _sigmoid(g)) * u).astype(BF16)
    acc_ref[...] += jnp.dot(act, wd_ref[...], preferred_element_type=F32)

    def finish(o_ref):
        def body(rows):
            y = acc_ref[rows, :]
            ms = jnp.mean(y * y, axis=-1, keepdims=True)
            o_ref[rows, :] = h_ref[rows, :] + (y * lax.rsqrt(ms + RMS_EPS)) * gpost_ref[...]
        _row_tiles(TM_FFN, body)

    last = j == pl.num_programs(1) - 1

    @pl.when(jnp.logical_and(last, i < n_first))
    def _():
        finish(oa_ref)

    @pl.when(jnp.logical_and(last, i >= n_first))
    def _():
        finish(ob_ref)


def _ffn(h, n_tokens_first, gpre, gpost, wg, wu, wd):
    T = h.shape[0]
    n_first = n_tokens_first // TM_FFN
    nbytes = (2 * 3 * TM_FFN * D_MODEL * 4 + TM_FFN * D_MODEL * 2 + 2 * 3 * D_MODEL * TF_FFN * 2
              + 3 * TM_FFN * TF_FFN * 4 + 2 * TM_FFN * D_MODEL * 4)
    return pl.pallas_call(
        functools.partial(_ffn_kernel, n_first),
        grid=(T // TM_FFN, D_FF // TF_FFN),
        in_specs=[
            pl.BlockSpec((TM_FFN, D_MODEL), lambda i, j: (i, 0)),
            pl.BlockSpec((1, D_MODEL), lambda i, j: (0, 0)),
            pl.BlockSpec((1, D_MODEL), lambda i, j: (0, 0)),
            pl.BlockSpec((D_MODEL, TF_FFN), lambda i, j: (0, j)),
            pl.BlockSpec((D_MODEL, TF_FFN), lambda i, j: (0, j)),
            pl.BlockSpec((TF_FFN, D_MODEL), lambda i, j: (j, 0)),
        ],
        out_specs=list(_two_source_specs(TM_FFN, n_first)),
        out_shape=[jax.ShapeDtypeStruct((n_tokens_first, D_MODEL), F32),
                   jax.ShapeDtypeStruct((T - n_tokens_first, D_MODEL), F32)],
        scratch_shapes=[pltpu.VMEM((TM_FFN, D_MODEL), BF16), pltpu.VMEM((TM_FFN, D_MODEL), F32)],
        compiler_params=pltpu.CompilerParams(
            dimension_semantics=("arbitrary", "arbitrary"),
            vmem_limit_bytes=_vmem_limit(nbytes)),
        name="swiglu_ffn",
    )(h, gpre, gpost, wg, wu, wd)


def _rope_tables(max_len):
    pos = jnp.arange(max_len, dtype=F32)
    inv_freq = ROPE_THETA ** (-jnp.arange(ROPE_HALF, dtype=F32) / ROPE_HALF)
    ang = pos[:, None] * inv_freq[None, :]
    cos, sin = jnp.cos(ang), jnp.sin(ang)
    rest = HEAD_DIM - ROPE_DIM
    cos_t = jnp.concatenate([cos, cos, jnp.ones((max_len, rest), F32)], axis=1)
    sa_t = jnp.concatenate([-sin, jnp.zeros((max_len, HEAD_DIM - ROPE_HALF), F32)], axis=1)
    sb_t = jnp.concatenate([jnp.zeros((max_len, ROPE_HALF), F32), sin, jnp.zeros((max_len, rest), F32)], axis=1)
    return cos_t, sa_t, sb_t


def _seq_groups(seq_shapes, seg):
    groups, first = [], 0
    for B, L in seq_shapes:
        assert L % seg == 0
        groups.append((first, L // seg))
        first += B * L // seg
    return tuple(groups)


def _encoder_layer(xa, xb, seq_shapes, tables, w_in, sink, rec_norm, lb, w_out, norm_mix_pre, norm_mix_post,
                   norm_ffn_pre, norm_ffn_post, w_gate, w_up, w_down):
    row = lambda v: v.astype(F32).reshape(1, -1)
    xn = _stream_norm(xa, xb, row(norm_mix_pre))

    assert all(off % TN_IN == 0 and w % TN_IN == 0 for off, w in SLAB_GROUPS) and OFF_ZF % TN_IN == 0
    gate_tiles = list(range(OFF_ZF // TN_IN, OFF_IR // TN_IN))
    slab_tiles = [t for off, w in SLAB_GROUPS for t in range(off // TN_IN, (off + w) // TN_IN)]
    kind_of = lambda t: ("rope" if t < OFF_K // TN_IN else "rope_k" if t == OFF_K // TN_IN
                         else "silu" if OFF_QR // TN_IN <= t < OFF_ZF // TN_IN else "plain")
    table_groups = _seq_groups(seq_shapes, TM_IN)
    slab = _in_projection(xn, w_in, tables, table_groups, slab_tiles, tuple(kind_of(t) for t in slab_tiles),
                          BF16, "in_projection")
    gates = _in_projection(xn, w_in, None, None, gate_tiles, ("plain",) * len(gate_tiles),
                           F32, "gate_projection")

    attn = _window_attention(slab, sink.astype(F32), _seq_groups(seq_shapes, TQ_ATT))
    rec_f, rec_b = _hgrn2(slab, gates, row(lb), _seq_groups(seq_shapes, SEG_REC))
    h = _out_projection(attn, rec_f, rec_b, slab, xa, xb, w_out.astype(BF16), row(rec_norm),
                        row(norm_mix_post))
    return _ffn(h, xa.shape[0], row(norm_ffn_pre), row(norm_ffn_post),
                w_gate.astype(BF16), w_up.astype(BF16), w_down.astype(BF16))


def kernel(x_prompt, x_sample, w_in, sink, rec_norm, lb_logits, w_out, norm_mix_pre, norm_mix_post,
           norm_ffn_pre, norm_ffn_post, w_gate, w_up, w_down):
    lb_all = jnp.cumsum(jax.nn.softmax(lb_logits.astype(F32), axis=0), axis=0)
    seq_shapes = (x_prompt.shape[:2], x_sample.shape[:2])
    xa = x_prompt.reshape(-1, D_MODEL)
    xb = x_sample.reshape(-1, D_MODEL)
    tables = _rope_tables(max(L for _, L in seq_shapes))
    for l in range(DEPTH):
        xa, xb = _encoder_layer(xa, xb, seq_shapes, tables, w_in[l], sink[l], rec_norm[l], lb_all[l], w_out[l],
                                norm_mix_pre[l], norm_mix_post[l], norm_ffn_pre[l], norm_ffn_post[l],
                                w_gate[l], w_up[l], w_down[l])
    return (xa.reshape(x_prompt.shape), xb.reshape(x_sample.shape))
```

```python
import functools

import jax
import jax.numpy as jnp
import numpy as np
from jax import lax
from jax.experimental import pallas as pl
from jax.experimental.pallas import tpu as pltpu

F32 = jnp.float32
BF16 = jnp.bfloat16

D_MODEL = 2048
DEPTH = 1
HEAD_DIM = 128
N_Q_HEADS = 16
N_KV_HEADS = 4
GROUP = N_Q_HEADS // N_KV_HEADS
WINDOW = 128
BLOCK = 128
ROPE_DIM = HEAD_DIM // 4
ROPE_HALF = ROPE_DIM // 2
ROPE_THETA = 500000.0
N_REC_HEADS = 16
REC_KEY_DIM = 128
REC_VAL_DIM = 128
CHUNK = 32
D_FF = -(-8 * D_MODEL // (3 * 256)) * 256
RMS_EPS = 1e-6
LOG2E = 1.4426950408889634
Q_SCALE = HEAD_DIM ** -0.5 * LOG2E

D_ATT = N_Q_HEADS * HEAD_DIM
D_KV = N_KV_HEADS * HEAD_DIM
D_REC_K = N_REC_HEADS * REC_KEY_DIM
D_REC_V = N_REC_HEADS * REC_VAL_DIM
SPLIT_SIZES = (D_ATT, D_KV, D_KV, D_REC_K, D_REC_K, D_REC_K, D_REC_V, D_REC_V, D_MODEL, D_MODEL)
D_IN = sum(SPLIT_SIZES)
(OFF_Q, OFF_K, OFF_V, OFF_QR, OFF_ZF, OFF_ZB, OFF_IR, OFF_GR, OFF_GA, OFF_GT) = (
    int(v) for v in np.concatenate([[0], np.cumsum(SPLIT_SIZES)[:-1]]))
SLAB_GROUPS = ((OFF_Q, D_ATT), (OFF_QR, D_REC_K), (OFF_IR, D_REC_V), (OFF_GR, D_REC_V), (OFF_GA, D_MODEL),
               (OFF_GT, D_MODEL), (OFF_K, 2 * D_KV))
A_Q, A_QR, A_IR, A_GR, A_GA, A_GT, A_K = (int(v) for v in np.cumsum([0] + [w for _, w in SLAB_GROUPS])[:-1])
A_V = A_K + D_KV

V7X_VMEM_CEILING = 56 * 1024 * 1024
LANES = 128

TM_NORM = 512
TM_IN, TN_IN = 1024, 1024
SUB_IN = 256
ROWS_EW = 256
SLAB_ATT = 32
TQ_ATT = 1024
SEG_REC = 512
HB_REC = 16
TM_OUT = 256
KC_OUT = 512
TM_FFN, TF_FFN = 512, 512


def _vmem_limit(nbytes):
    return int(min(V7X_VMEM_CEILING, nbytes * 1.25 + (4 << 20)))


def _sigmoid(x):
    return 1.0 / (1.0 + jnp.exp(-x))


def _dot_nt(a, b):
    return lax.dot_general(a, b, (((1,), (1,)), ((), ())), preferred_element_type=F32)


def _dot_tn(a, b):
    return lax.dot_general(a, b, (((0,), (0,)), ((), ())), preferred_element_type=F32)


def _rows(start, size):
    return pl.ds(start if isinstance(start, int) else pl.multiple_of(start, size), size)


def _row_tiles(n_rows, body):
    def step(r, c):
        body(_rows(r * ROWS_EW, ROWS_EW))
        return c
    lax.fori_loop(0, n_rows // ROWS_EW, step, 0)


def _segment_position(g, seq_groups):
    first, local, per = None, None, None
    for g0, n in reversed(seq_groups):
        loc = lax.rem(g - g0, n)
        fst = g - loc
        if first is None:
            first, local, per = fst, loc, n
        else:
            here = g < nxt
            first = jnp.where(here, fst, first)
            local = jnp.where(here, loc, local)
            per = jnp.where(here, n, per)
        nxt = g0
    return first, local, per


def _two_source_specs(tm, n_first):
    return (pl.BlockSpec((tm, D_MODEL), lambda i, *_: (jnp.minimum(i, n_first - 1), 0)),
            pl.BlockSpec((tm, D_MODEL), lambda i, *_: (jnp.maximum(i - n_first, 0), 0)))


def _norm_kernel(n_first, xa_ref, xb_ref, g_ref, o_ref):
    def run(x_ref):
        def body(rows):
            x = x_ref[rows, :]
            ms = jnp.mean(x * x, axis=-1, keepdims=True)
            o_ref[rows, :] = ((x * lax.rsqrt(ms + RMS_EPS)) * g_ref[...]).astype(o_ref.dtype)
        _row_tiles(TM_NORM, body)

    @pl.when(pl.program_id(0) < n_first)
    def _():
        run(xa_ref)

    @pl.when(pl.program_id(0) >= n_first)
    def _():
        run(xb_ref)


def _stream_norm(xa, xb, gain):
    T = xa.shape[0] + xb.shape[0]
    n_first = xa.shape[0] // TM_NORM
    nbytes = 2 * 2 * TM_NORM * D_MODEL * 4 + 2 * TM_NORM * D_MODEL * 2
    return pl.pallas_call(
        functools.partial(_norm_kernel, n_first),
        grid=(T // TM_NORM,),
        in_specs=[*_two_source_specs(TM_NORM, n_first), pl.BlockSpec((1, D_MODEL), lambda i: (0, 0))],
        out_specs=pl.BlockSpec((TM_NORM, D_MODEL), lambda i: (i, 0)),
        out_shape=jax.ShapeDtypeStruct((T, D_MODEL), BF16),
        compiler_params=pltpu.CompilerParams(
            dimension_semantics=("parallel",), vmem_limit_bytes=_vmem_limit(nbytes)),
        name="stream_norm",
    )(xa, xb, gain)


def _inproj_kernel(kinds, w_scale, col_tiles_ref, *refs):
    del col_tiles_ref
    with_epilogue = any(k != "plain" for k in kinds)
    if with_epilogue:
        xn_ref, cos_ref, sa_ref, sb_ref, w_ref, o_ref, wb_ref, y_ref = refs
    else:
        xn_ref, w_ref, o_ref, wb_ref = refs
    j = pl.program_id(0)

    @pl.when(pl.program_id(1) == 0)
    def _():
        def body(rows):
            w = w_ref[rows, :]
            wb_ref[rows, :] = (w if w_scale == 1.0 else w * w_scale).astype(BF16)
        _row_tiles(D_MODEL, body)

    def transform(kind, y, rows, lane_tile):
        if kind == "silu":
            return y * _sigmoid(y)
        if kind == "rope_k" and lane_tile >= N_KV_HEADS:
            return y
        y = (y * cos_ref[rows, :] + pltpu.roll(y, HEAD_DIM - ROPE_HALF, 1) * sa_ref[rows, :]
             + pltpu.roll(y, ROPE_HALF, 1) * sb_ref[rows, :])
        return y * Q_SCALE if kind == "rope" else y

    def epilogue(kind, sub):
        for r in range(TM_IN // ROWS_EW):
            rows = slice(r * ROWS_EW, (r + 1) * ROWS_EW)
            for cc in range(SUB_IN // LANES):
                lane_tile = sub * (SUB_IN // LANES) + cc
                y = y_ref[sub, rows, cc * LANES:(cc + 1) * LANES]
                o_ref[rows, lane_tile * LANES:(lane_tile + 1) * LANES] = (
                    transform(kind, y, rows, lane_tile).astype(o_ref.dtype))

    for kind in sorted(set(kinds)):
        tiles = [t for t, k in enumerate(kinds) if k == kind]
        cond = functools.reduce(jnp.logical_or, [j == t for t in tiles])

        @pl.when(cond)
        def _(kind=kind):
            if kind == "plain":
                o_ref[...] = jnp.dot(xn_ref[...], wb_ref[...], preferred_element_type=F32).astype(o_ref.dtype)
            else:
                n_sub = TN_IN // SUB_IN
                for sub in range(n_sub):
                    y_ref[sub] = jnp.dot(xn_ref[...], wb_ref[:, sub * SUB_IN:(sub + 1) * SUB_IN],
                                         preferred_element_type=F32)
                    if sub > 0:
                        epilogue(kind, sub - 1)
                epilogue(kind, n_sub - 1)


def _in_projection(xn, w_in, tables, table_groups, col_tiles, kinds, out_dtype, name, w_scale=1.0):
    T = xn.shape[0]
    n_col = len(col_tiles)
    col_tiles = jnp.asarray(col_tiles, jnp.int32)
    with_epilogue = any(k != "plain" for k in kinds)
    rope_tiles = [t for t, k in enumerate(kinds) if k.startswith("rope")]
    out_bytes = jnp.dtype(out_dtype).itemsize

    def table_map(j, i, ct):
        _, local, _ = _segment_position(i, table_groups)
        uses_tables = functools.reduce(jnp.logical_or, [j == t for t in rope_tiles])
        return (jnp.where(uses_tables, local, 0), 0)

    in_specs = [pl.BlockSpec((TM_IN, D_MODEL), lambda j, i, ct: (i, 0))]
    args = [xn]
    scratch = [pltpu.VMEM((D_MODEL, TN_IN), BF16)]
    nbytes = (2 * TM_IN * D_MODEL * 2 + 2 * D_MODEL * TN_IN * 4 + D_MODEL * TN_IN * 2
              + 2 * TM_IN * TN_IN * out_bytes + TM_IN * TN_IN * 4)
    if with_epilogue:
        in_specs += [pl.BlockSpec((TM_IN, LANES), table_map)] * 3
        args += list(tables)
        scratch.append(pltpu.VMEM((TN_IN // SUB_IN, TM_IN, SUB_IN), F32))
        nbytes += 2 * 3 * TM_IN * LANES * 4 + TM_IN * TN_IN * 4
    in_specs.append(pl.BlockSpec((D_MODEL, TN_IN), lambda j, i, ct: (0, ct[j])))
    args.append(w_in)
    return pl.pallas_call(
        functools.partial(_inproj_kernel, kinds, w_scale),
        grid_spec=pltpu.PrefetchScalarGridSpec(
            num_scalar_prefetch=1,
            grid=(n_col, T // TM_IN),
            in_specs=in_specs,
            out_specs=pl.BlockSpec((TM_IN, TN_IN), lambda j, i, ct: (i, j)),
            scratch_shapes=scratch),
        out_shape=jax.ShapeDtypeStruct((T, n_col * TN_IN), out_dtype),
        compiler_params=pltpu.CompilerParams(
            dimension_semantics=("arbitrary", "arbitrary"),
            vmem_limit_bytes=_vmem_limit(nbytes)),
        name=name,
    )(col_tiles, *args)


def _attn_kernel(seq_groups, sink_ref, q_ref, kc_ref, vc_ref, kp_ref, kn_ref, vp_ref, vn_ref,
                 ga_ref, o_ref, s_all, p_all, r_all):
    t = pl.program_id(0)
    h = pl.program_id(1)
    _, local, per = _segment_position(t, seq_groups)
    has_prev = local != 0
    has_next = local != per - 1

    k_all = jnp.concatenate([kp_ref[...], kc_ref[...], kn_ref[...]], axis=0)
    v_all = jnp.concatenate([vp_ref[...], vc_ref[...], vn_ref[...]], axis=0)
    qi = lax.broadcasted_iota(jnp.int32, (BLOCK, BLOCK), 0)
    kj = lax.broadcasted_iota(jnp.int32, (BLOCK, BLOCK), 1)
    neg = jnp.float32(-jnp.inf)
    bias_prev = jnp.where(kj >= qi, 0.0, neg)
    bias_next = jnp.where(kj <= qi, 0.0, neg)
    bias_prev0 = jnp.where(has_prev, bias_prev, neg)
    bias_next_last = jnp.where(has_next, bias_next, neg)
    n_blocks = TQ_ATT // BLOCK
    for b in range(n_blocks):
        rs = slice(b * BLOCK, (b + 1) * BLOCK)
        kb = k_all[b * BLOCK:(b + 3) * BLOCK]
        vb = v_all[b * BLOCK:(b + 3) * BLOCK]
        bp = bias_prev0 if b == 0 else bias_prev
        bn = bias_next_last if b == n_blocks - 1 else bias_next
        heads = [slice(g * HEAD_DIM, (g + 1) * HEAD_DIM) for g in range(GROUP)]
        s_ref, p_ref, r_ref = s_all.at[b], p_all.at[b], r_all.at[b]
        q4 = jnp.concatenate([q_ref[rs, cs] for cs in heads], axis=0)
        s_ref[...] = _dot_nt(q4, kb).reshape(GROUP, BLOCK, 3 * BLOCK)
        for g, cs in enumerate(heads):
            sink2 = sink_ref[GROUP * h + g] * LOG2E
            for r in range(BLOCK // SLAB_ATT):
                rr = slice(r * SLAB_ATT, (r + 1) * SLAB_ATT)
                s = s_ref[g, rr, :]
                s_p = s[:, :BLOCK] + bp[rr]
                s_c = s[:, BLOCK:2 * BLOCK]
                s_n = s[:, 2 * BLOCK:] + bn[rr]
                m = jnp.maximum(jnp.max(jnp.maximum(jnp.maximum(s_p, s_c), s_n), axis=-1, keepdims=True), sink2)
                p_p, p_c, p_n = jnp.exp2(s_p - m), jnp.exp2(s_c - m), jnp.exp2(s_n - m)
                denom = jnp.sum(p_p + p_c + p_n, axis=-1, keepdims=True) + jnp.exp2(sink2 - m)
                p_ref[g, rr, :] = jnp.concatenate([p_p, p_c, p_n], axis=1).astype(BF16)
                r_ref[g, rr, :] = jnp.broadcast_to(1.0 / denom, (SLAB_ATT, LANES))
        o4 = jnp.dot(p_ref[...].reshape(GROUP * BLOCK, 3 * BLOCK), vb, preferred_element_type=F32)
        for g, cs in enumerate(heads):
            o = o4[g * BLOCK:(g + 1) * BLOCK] * r_ref[g]
            o_ref[rs, cs] = (_sigmoid(ga_ref[rs, cs].astype(F32)) * o).astype(o_ref.dtype)


def _window_attention(slab, sink, seq_groups):
    T = slab.shape[0]
    n_tiles = T // TQ_ATT
    bpt = TQ_ATT // BLOCK
    last_blk = T // BLOCK - 1
    gw = GROUP * HEAD_DIM
    assert A_K % HEAD_DIM == 0 and A_V % HEAD_DIM == 0 and A_GA % gw == 0
    kcol, vcol, gcol = A_K // HEAD_DIM, A_V // HEAD_DIM, A_GA // gw
    nbytes = (2 * (3 * TQ_ATT * gw * 2 + 2 * TQ_ATT * HEAD_DIM * 2 + 4 * BLOCK * HEAD_DIM * 2)
              + bpt * GROUP * BLOCK * (3 * BLOCK * 6 + LANES * 4))
    return pl.pallas_call(
        functools.partial(_attn_kernel, seq_groups),
        grid=(n_tiles, N_KV_HEADS),
        in_specs=[
            pl.BlockSpec(memory_space=pltpu.SMEM),
            pl.BlockSpec((TQ_ATT, gw), lambda t, h: (t, h)),
            pl.BlockSpec((TQ_ATT, HEAD_DIM), lambda t, h: (t, kcol + h)),
            pl.BlockSpec((TQ_ATT, HEAD_DIM), lambda t, h: (t, vcol + h)),
            pl.BlockSpec((BLOCK, HEAD_DIM), lambda t, h: (jnp.maximum(t * bpt - 1, 0), kcol + h)),
            pl.BlockSpec((BLOCK, HEAD_DIM), lambda t, h: (jnp.minimum((t + 1) * bpt, last_blk), kcol + h)),
            pl.BlockSpec((BLOCK, HEAD_DIM), lambda t, h: (jnp.maximum(t * bpt - 1, 0), vcol + h)),
            pl.BlockSpec((BLOCK, HEAD_DIM), lambda t, h: (jnp.minimum((t + 1) * bpt, last_blk), vcol + h)),
            pl.BlockSpec((TQ_ATT, gw), lambda t, h: (t, gcol + h)),
        ],
        out_specs=pl.BlockSpec((TQ_ATT, gw), lambda t, h: (t, h)),
        out_shape=jax.ShapeDtypeStruct((T, D_ATT), BF16),
        scratch_shapes=[pltpu.VMEM((bpt, GROUP, BLOCK, 3 * BLOCK), F32),
                        pltpu.VMEM((bpt, GROUP, BLOCK, 3 * BLOCK), BF16),
                        pltpu.VMEM((bpt, GROUP, BLOCK, LANES), F32)],
        compiler_params=pltpu.CompilerParams(
            dimension_semantics=("parallel", "arbitrary"),
            vmem_limit_bytes=_vmem_limit(nbytes)),
        name="window_attention",
    )(sink, slab, slab, slab, slab, slab, slab, slab, slab)


def _hgrn_kernel(seq_groups, lb_ref, qf_ref, if_ref, zf_ref, qb_ref, ib_ref, zb_ref,
                 of_ref, ob_ref, st_ref, stb_ref, ds_ref, ops_ref, cross_ref, dec_ref, a_ref):
    g = pl.program_id(1)
    _, local, _ = _segment_position(g, seq_groups)

    @pl.when(local == 0)
    def _():
        st_ref[...] = jnp.zeros(st_ref.shape, F32)
        stb_ref[...] = jnp.zeros(stb_ref.shape, BF16)

    Q_IN, K_IN, Q_ST, K_ST, I_C = range(5)
    pair = 2 * CHUNK
    n_pairs = SEG_REC // pair
    rin = lax.broadcasted_iota(jnp.int32, (CHUNK, LANES), 0)
    ti = lax.broadcasted_iota(jnp.int32, (CHUNK, CHUNK), 0)
    si = lax.broadcasted_iota(jnp.int32, (CHUNK, CHUNK), 1)
    half = (slice(0, CHUNK), slice(CHUNK, pair))

    def prepare(d, hh, start, q_ref, i_ref, z_ref):
        cs = slice(hh * LANES, (hh + 1) * LANES)
        lb = lb_ref[:, cs]
        one_m_lb = 1.0 - lb
        q_in, k_in, k_st, q_dc, dec = [], [], [], [], []
        for t in range(2):
            rows = _rows(start + t * CHUNK, CHUNK)
            q = q_ref[rows, cs].astype(F32)
            gate = one_m_lb * (1.0 / (1.0 + jnp.exp2(z_ref[rows, cs])))
            k = one_m_lb - gate
            logf = jnp.log2(lb + gate)
            c = logf
            for s in (1, 2, 4, 8, 16):
                if d == 0:
                    c = c + jnp.where(rin >= s, pltpu.roll(c, s, 0), 0.0)
                else:
                    c = c + jnp.where(rin < CHUNK - s, pltpu.roll(c, CHUNK - s, 0), 0.0)
            last, mid = (CHUNK - 1, CHUNK // 2 - 1) if d == 0 else (0, CHUNK // 2)
            tot = c[last:last + 1, :]
            cref = c[mid:mid + 1, :]
            q_in.append(q * jnp.exp2(c - cref))
            k_in.append(k * jnp.exp2(cref - c))
            k_st.append(k_in[t] * jnp.exp2(tot - cref))
            q_dc.append(q_in[t] * jnp.exp2(cref))
            dec.append(jnp.exp2(tot))
            ops_ref[d, hh, Q_IN, half[t]] = q_in[t].astype(BF16)
            ops_ref[d, hh, K_IN, half[t]] = k_in[t].astype(BF16)
            ops_ref[d, hh, I_C, half[t]] = i_ref[rows, cs]
        ca, cb = (0, 1) if d == 0 else (1, 0)
        ops_ref[d, hh, Q_ST, half[ca]] = q_dc[ca].astype(BF16)
        ops_ref[d, hh, Q_ST, half[cb]] = (q_dc[cb] * dec[ca]).astype(BF16)
        ops_ref[d, hh, K_ST, half[ca]] = (k_st[ca] * dec[cb]).astype(BF16)
        ops_ref[d, hh, K_ST, half[cb]] = k_st[cb].astype(BF16)
        cross_ref[d, hh, 0] = q_dc[cb].astype(BF16)
        cross_ref[d, hh, 1] = k_st[ca].astype(BF16)
        dec_ref[d, hh] = jnp.broadcast_to(dec[0] * dec[1], (8, LANES))

    def intra(d, hh):
        for t in range(2):
            a_ref[d, hh, t] = _dot_nt(ops_ref[d, hh, Q_IN, half[t]], ops_ref[d, hh, K_IN, half[t]])
        a_ref[d, hh, 2] = _dot_nt(cross_ref[d, hh, 0], cross_ref[d, hh, 1])

    def output(d, hh, start, o_ref):
        cs = slice(hh * LANES, (hh + 1) * LANES)
        tri = (ti >= si) if d == 0 else (ti <= si)
        a_top = jnp.where(tri, a_ref[d, hh, 0], 0.0).astype(BF16)
        a_bot = jnp.where(tri, a_ref[d, hh, 1], 0.0).astype(BF16)
        a_x = a_ref[d, hh, 2].astype(BF16)
        i_top, i_bot = ops_ref[d, hh, I_C, half[0]], ops_ref[d, hh, I_C, half[1]]
        inter = jnp.dot(ops_ref[d, hh, Q_ST], stb_ref[d, hh], preferred_element_type=F32)
        if d == 0:
            both = jnp.dot(jnp.concatenate([a_top, a_x], axis=0), i_top, preferred_element_type=F32)
            o_top = both[:CHUNK] + inter[:CHUNK]
            o_bot = both[CHUNK:] + jnp.dot(a_bot, i_bot, preferred_element_type=F32) + inter[CHUNK:]
        else:
            both = jnp.dot(jnp.concatenate([a_x, a_bot], axis=0), i_bot, preferred_element_type=F32)
            o_top = both[:CHUNK] + jnp.dot(a_top, i_top, preferred_element_type=F32) + inter[:CHUNK]
            o_bot = both[CHUNK:] + inter[CHUNK:]
        o_ref[_rows(start, CHUNK), cs] = o_top.astype(o_ref.dtype)
        o_ref[_rows(start + CHUNK, CHUNK), cs] = o_bot.astype(o_ref.dtype)
        ds_ref[d, hh] = _dot_tn(ops_ref[d, hh, I_C], ops_ref[d, hh, K_ST])

    def update(d, hh):
        dec = dec_ref[d, hh, 0:1, :]
        for r in range(REC_VAL_DIM // CHUNK):
            rr = slice(r * CHUNK, (r + 1) * CHUNK)
            new = dec * st_ref[d, hh, rr, :] + ds_ref[d, hh, rr, :]
            st_ref[d, hh, rr, :] = new
        stb_ref[d, hh] = st_ref[d, hh].T.astype(BF16)

    def scan(n, carry):
        start_f = pl.multiple_of(n * pair, pair)
        start_b = pl.multiple_of((n_pairs - 1 - n) * pair, pair)
        for hh in range(HB_REC):
            prepare(0, hh, start_f, qf_ref, if_ref, zf_ref)
            prepare(1, hh, start_b, qb_ref, ib_ref, zb_ref)
        for hh in range(HB_REC):
            intra(0, hh)
            intra(1, hh)
        for hh in range(HB_REC):
            output(0, hh, start_f, of_ref)
            output(1, hh, start_b, ob_ref)
        for hh in range(HB_REC):
            update(0, hh)
            update(1, hh)
        return carry

    lax.fori_loop(0, n_pairs, scan, 0)


def _hgrn2(slab, gates, lb_row, seq_groups):
    T = slab.shape[0]
    w = HB_REC * LANES
    assert A_QR % w == 0 and A_IR % w == 0 and D_REC_K % w == 0

    def mirror(g):
        first, local, per = _segment_position(g, seq_groups)
        return first + per - 1 - local

    def fspec(off):
        return pl.BlockSpec((SEG_REC, w), lambda hb, g: (g, off // w + hb))

    def bspec(off):
        return pl.BlockSpec((SEG_REC, w), lambda hb, g: (mirror(g), off // w + hb))

    state = (2, HB_REC, REC_VAL_DIM, REC_KEY_DIM)
    nbytes = (2 * (4 * SEG_REC * w * 2 + 2 * SEG_REC * w * 4 + 2 * SEG_REC * w * 2)
              + 2 * HB_REC * (REC_VAL_DIM * REC_KEY_DIM * 10 + 12 * CHUNK * LANES * 2 + 8 * LANES * 4))
    out = jax.ShapeDtypeStruct((T, D_REC_V), BF16)
    return pl.pallas_call(
        functools.partial(_hgrn_kernel, seq_groups),
        grid=(N_REC_HEADS // HB_REC, T // SEG_REC),
        in_specs=[
            pl.BlockSpec((1, w), lambda hb, g: (0, hb)),
            fspec(A_QR), fspec(A_IR), fspec(0),
            bspec(A_QR), bspec(A_IR), bspec(D_REC_K),
        ],
        out_specs=[pl.BlockSpec((SEG_REC, w), lambda hb, g: (g, hb)),
                   pl.BlockSpec((SEG_REC, w), lambda hb, g: (mirror(g), hb))],
        out_shape=[out, out],
        scratch_shapes=[pltpu.VMEM(state, F32), pltpu.VMEM(state, BF16), pltpu.VMEM(state, F32),
                        pltpu.VMEM((2, HB_REC, 5, 2 * CHUNK, LANES), BF16),
                        pltpu.VMEM((2, HB_REC, 2, CHUNK, LANES), BF16),
                        pltpu.VMEM((2, HB_REC, 8, LANES), F32),
                        pltpu.VMEM((2, HB_REC, 3, CHUNK, CHUNK), F32)],
        compiler_params=pltpu.CompilerParams(
            dimension_semantics=("parallel", "arbitrary"),
            vmem_limit_bytes=_vmem_limit(nbytes)),
        name="hgrn2",
    )(lb_row, slab, slab, gates, slab, slab, gates)


def _outproj_kernel(n_first, a_ref, of_ref, ob_ref, grl_ref, grh_ref, gtl_ref, gth_ref, xa_ref, xb_ref,
                    w_ref, rg_ref, g_ref, h_ref, m_ref):
    half = D_REC_V // 2
    heads_per_chunk = KC_OUT // REC_VAL_DIM
    y = None
    for c in range(D_MODEL // KC_OUT):
        for hh in range(c * heads_per_chunk, (c + 1) * heads_per_chunk):
            cs = slice(hh * REC_VAL_DIM, (hh + 1) * REC_VAL_DIM)
            lo = hh * REC_VAL_DIM < half
            hs = cs if lo else slice(hh * REC_VAL_DIM - half, (hh + 1) * REC_VAL_DIM - half)
            rec = of_ref[:, cs].astype(F32) + ob_ref[:, cs].astype(F32)
            ms = jnp.mean(rec * rec, axis=-1, keepdims=True)
            r = (rec * lax.rsqrt(ms + RMS_EPS)) * rg_ref[:, cs]
            gr = (grl_ref if lo else grh_ref)[:, hs].astype(F32)
            gt = (gtl_ref if lo else gth_ref)[:, hs].astype(F32)
            r = _sigmoid(gt) * (r * (gr * _sigmoid(gr)))
            m_ref[:, cs] = (a_ref[:, cs].astype(F32) + r).astype(BF16)
        ks = slice(c * KC_OUT, (c + 1) * KC_OUT)
        part = jnp.dot(m_ref[:, ks], w_ref[ks, :], preferred_element_type=F32)
        y = part if y is None else y + part
    ms = jnp.mean(y * y, axis=-1, keepdims=True)
    h_ref[...] = (y * lax.rsqrt(ms + RMS_EPS)) * g_ref[...]

    @pl.when(pl.program_id(0) < n_first)
    def _():
        h_ref[...] += xa_ref[...]

    @pl.when(pl.program_id(0) >= n_first)
    def _():
        h_ref[...] += xb_ref[...]


def _out_projection(attn, rec_f, rec_b, slab, xa, xb, w_bf16, rec_gain, gain):
    T = attn.shape[0]
    n_first = xa.shape[0] // TM_OUT
    half = D_REC_V // 2
    assert A_GR % half == 0 and A_GT % half == 0
    tok = lambda i: (i, 0)
    nbytes = (2 * 3 * TM_OUT * D_MODEL * 2 + 2 * 4 * TM_OUT * half * 2 + 2 * 3 * TM_OUT * D_MODEL * 4
              + 2 * D_MODEL * D_MODEL * 2 + TM_OUT * D_MODEL * 2 + 2 * TM_OUT * D_MODEL * 4)
    return pl.pallas_call(
        functools.partial(_outproj_kernel, n_first),
        grid=(T // TM_OUT,),
        in_specs=[
            pl.BlockSpec((TM_OUT, D_MODEL), tok),
            pl.BlockSpec((TM_OUT, D_REC_V), tok),
            pl.BlockSpec((TM_OUT, D_REC_V), tok),
            pl.BlockSpec((TM_OUT, half), lambda i: (i, A_GR // half)),
            pl.BlockSpec((TM_OUT, half), lambda i: (i, A_GR // half + 1)),
            pl.BlockSpec((TM_OUT, half), lambda i: (i, A_GT // half)),
            pl.BlockSpec((TM_OUT, half), lambda i: (i, A_GT // half + 1)),
            *_two_source_specs(TM_OUT, n_first),
            pl.BlockSpec((D_MODEL, D_MODEL), lambda i: (0, 0)),
            pl.BlockSpec((1, D_REC_V), lambda i: (0, 0)),
            pl.BlockSpec((1, D_MODEL), lambda i: (0, 0)),
        ],
        out_specs=pl.BlockSpec((TM_OUT, D_MODEL), tok),
        out_shape=jax.ShapeDtypeStruct((T, D_MODEL), F32),
        scratch_shapes=[pltpu.VMEM((TM_OUT, D_MODEL), BF16)],
        compiler_params=pltpu.CompilerParams(
            dimension_semantics=("parallel",),
            vmem_limit_bytes=_vmem_limit(nbytes)),
        name="out_projection",
    )(attn, rec_f, rec_b, slab, slab, slab, slab, xa, xb, w_bf16, rec_gain, gain)


def _ffn_kernel(n_first, h_ref, gpre_ref, gpost_ref, wg_ref, wu_ref, wd_ref, oa_ref, ob_ref, hn_ref, acc_ref):
    i = pl.program_id(0)
    j = pl.program_id(1)

    @pl.when(j == 0)
    def _():
        def body(rows):
            h = h_ref[rows, :]
            ms = jnp.mean(h * h, axis=-1, keepdims=True)
            hn_ref[rows, :] = ((h * lax.rsqrt(ms + RMS_EPS)) * gpre_ref[...]).astype(BF16)
            acc_ref[rows, :] = jnp.zeros((ROWS_EW, D_MODEL), F32)
        _row_tiles(TM_FFN, body)

    hn = hn_ref[...]
    g = jnp.dot(hn, wg_ref[...], preferred_element_type=F32)
    u = jnp.dot(hn, wu_ref[...], preferred_element_type=F32)
    act = ((g * _sigmoid(g)) * u).astype(BF16)
    acc_ref[...] += jnp.dot(act, wd_ref[...].astype(BF16), preferred_element_type=F32)

    def finish(o_ref):
        def body(rows):
            y = acc_ref[rows, :]
            ms = jnp.mean(y * y, axis=-1, keepdims=True)
            o_ref[rows, :] = h_ref[rows, :] + (y * lax.rsqrt(ms + RMS_EPS)) * gpost_ref[...]
        _row_tiles(TM_FFN, body)

    last = j == pl.num_programs(1) - 1

    @pl.when(jnp.logical_and(last, i < n_first))
    def _():
        finish(oa_ref)

    @pl.when(jnp.logical_and(last, i >= n_first))
    def _():
        finish(ob_ref)


def _ffn(h, n_tokens_first, gpre, gpost, wg, wu, wd):
    T = h.shape[0]
    n_first = n_tokens_first // TM_FFN
    nbytes = (2 * 3 * TM_FFN * D_MODEL * 4 + TM_FFN * D_MODEL * 2 + 2 * 4 * D_MODEL * TF_FFN * 2
              + 3 * TM_FFN * TF_FFN * 4 + 2 * TM_FFN * D_MODEL * 4)
    return pl.pallas_call(
        functools.partial(_ffn_kernel, n_first),
        grid=(T // TM_FFN, D_FF // TF_FFN),
        in_specs=[
            pl.BlockSpec((TM_FFN, D_MODEL), lambda i, j: (i, 0)),
            pl.BlockSpec((1, D_MODEL), lambda i, j: (0, 0)),
            pl.BlockSpec((1, D_MODEL), lambda i, j: (0, 0)),
            pl.BlockSpec((D_MODEL, TF_FFN), lambda i, j: (0, j)),
            pl.BlockSpec((D_MODEL, TF_FFN), lambda i, j: (0, j)),
            pl.BlockSpec((TF_FFN, D_MODEL), lambda i, j: (j, 0)),
        ],
        out_specs=list(_two_source_specs(TM_FFN, n_first)),
        out_shape=[jax.ShapeDtypeStruct((n_tokens_first, D_MODEL), F32),
                   jax.ShapeDtypeStruct((T - n_tokens_first, D_MODEL), F32)],
        scratch_shapes=[pltpu.VMEM((TM_FFN, D_MODEL), BF16), pltpu.VMEM((TM_FFN, D_MODEL), F32)],
        compiler_params=pltpu.CompilerParams(
            dimension_semantics=("arbitrary", "arbitrary"),
            vmem_limit_bytes=_vmem_limit(nbytes)),
        name="swiglu_ffn",
    )(h, gpre, gpost, wg, wu, wd)


def _rope_tables(max_len):
    pos = jnp.arange(max_len, dtype=F32)
    inv_freq = ROPE_THETA ** (-jnp.arange(ROPE_HALF, dtype=F32) / ROPE_HALF)
    ang = pos[:, None] * inv_freq[None, :]
    cos, sin = jnp.cos(ang), jnp.sin(ang)
    rest = HEAD_DIM - ROPE_DIM
    cos_t = jnp.concatenate([cos, cos, jnp.ones((max_len, rest), F32)], axis=1)
    sa_t = jnp.concatenate([-sin, jnp.zeros((max_len, HEAD_DIM - ROPE_HALF), F32)], axis=1)
    sb_t = jnp.concatenate([jnp.zeros((max_len, ROPE_HALF), F32), sin, jnp.zeros((max_len, rest), F32)], axis=1)
    return cos_t, sa_t, sb_t


def _seq_groups(seq_shapes, seg):
    groups, first = [], 0
    for B, L in seq_shapes:
        assert L % seg == 0
        groups.append((first, L // seg))
        first += B * L // seg
    return tuple(groups)


def _encoder_layer(xa, xb, seq_shapes, tables, w_in, sink, rec_norm, lb, w_out, norm_mix_pre, norm_mix_post,
                   norm_ffn_pre, norm_ffn_post, w_gate, w_up, w_down):
    row = lambda v: v.astype(F32).reshape(1, -1)
    xn = _stream_norm(xa, xb, row(norm_mix_pre))

    assert all(off % TN_IN == 0 and w % TN_IN == 0 for off, w in SLAB_GROUPS) and OFF_ZF % TN_IN == 0
    gate_tiles = list(range(OFF_ZF // TN_IN, OFF_IR // TN_IN))
    slab_tiles = [t for off, w in SLAB_GROUPS for t in range(off // TN_IN, (off + w) // TN_IN)]
    kind_of = lambda t: ("rope" if t < OFF_K // TN_IN else "rope_k" if t == OFF_K // TN_IN
                         else "silu" if OFF_QR // TN_IN <= t < OFF_ZF // TN_IN else "plain")
    table_groups = _seq_groups(seq_shapes, TM_IN)
    slab = _in_projection(xn, w_in, tables, table_groups, slab_tiles, tuple(kind_of(t) for t in slab_tiles),
                          BF16, "in_projection")
    gates = _in_projection(xn, w_in, None, None, gate_tiles, ("plain",) * len(gate_tiles),
                           F32, "gate_projection", w_scale=-LOG2E)

    attn = _window_attention(slab, sink.astype(F32), _seq_groups(seq_shapes, TQ_ATT))
    rec_f, rec_b = _hgrn2(slab, gates, row(lb), _seq_groups(seq_shapes, SEG_REC))
    h = _out_projection(attn, rec_f, rec_b, slab, xa, xb, w_out.astype(BF16), row(rec_norm),
                        row(norm_mix_post))
    return _ffn(h, xa.shape[0], row(norm_ffn_pre), row(norm_ffn_post),
                w_gate.astype(BF16), w_up.astype(BF16), w_down)


def kernel(x_prompt, x_sample, w_in, sink, rec_norm, lb_logits, w_out, norm_mix_pre, norm_mix_post,
           norm_ffn_pre, norm_ffn_post, w_gate, w_up, w_down):
    lb_all = jnp.cumsum(jax.nn.softmax(lb_logits.astype(F32), axis=0), axis=0)
    seq_shapes = (x_prompt.shape[:2], x_sample.shape[:2])
    xa = x_prompt.reshape(-1, D_MODEL)
    xb = x_sample.reshape(-1, D_MODEL)
    tables = _rope_tables(max(L for _, L in seq_shapes))
    for l in range(DEPTH):
        xa, xb = _encoder_layer(xa, xb, seq_shapes, tables, w_in[l], sink[l], rec_norm[l], lb_all[l], w_out[l],
                                norm_mix_pre[l], norm_mix_post[l], norm_ffn_pre[l], norm_ffn_post[l],
                                w_gate[l], w_up[l], w_down[l])
    return (xa.reshape(x_prompt.shape), xb.reshape(x_sample.shape))
```

```python
import functools

import jax
import jax.numpy as jnp
import numpy as np
from jax import lax
from jax.experimental import pallas as pl
from jax.experimental.pallas import tpu as pltpu

F32 = jnp.float32
BF16 = jnp.bfloat16

D_MODEL = 2048
DEPTH = 1
HEAD_DIM = 128
N_Q_HEADS = 16
N_KV_HEADS = 4
GROUP = N_Q_HEADS // N_KV_HEADS
WINDOW = 128
BLOCK = 128
ROPE_DIM = HEAD_DIM // 4
ROPE_HALF = ROPE_DIM // 2
ROPE_THETA = 500000.0
N_REC_HEADS = 16
REC_KEY_DIM = 128
REC_VAL_DIM = 128
CHUNK = 32
D_FF = -(-8 * D_MODEL // (3 * 256)) * 256
RMS_EPS = 1e-6
LOG2E = 1.4426950408889634
Q_SCALE = HEAD_DIM ** -0.5 * LOG2E

D_ATT = N_Q_HEADS * HEAD_DIM
D_KV = N_KV_HEADS * HEAD_DIM
D_REC_K = N_REC_HEADS * REC_KEY_DIM
D_REC_V = N_REC_HEADS * REC_VAL_DIM
SPLIT_SIZES = (D_ATT, D_KV, D_KV, D_REC_K, D_REC_K, D_REC_K, D_REC_V, D_REC_V, D_MODEL, D_MODEL)
D_IN = sum(SPLIT_SIZES)
(OFF_Q, OFF_K, OFF_V, OFF_QR, OFF_ZF, OFF_ZB, OFF_IR, OFF_GR, OFF_GA, OFF_GT) = (
    int(v) for v in np.concatenate([[0], np.cumsum(SPLIT_SIZES)[:-1]]))
SLAB_GROUPS = ((OFF_Q, D_ATT), (OFF_QR, D_REC_K), (OFF_IR, D_REC_V), (OFF_GR, D_REC_V), (OFF_GA, D_MODEL),
               (OFF_GT, D_MODEL), (OFF_K, 2 * D_KV))
A_Q, A_QR, A_IR, A_GR, A_GA, A_GT, A_K = (int(v) for v in np.cumsum([0] + [w for _, w in SLAB_GROUPS])[:-1])
A_V = A_K + D_KV

V7X_VMEM_CEILING = 56 * 1024 * 1024
LANES = 128

TM_NORM = 512
TM_IN, TN_IN = 1024, 1024
SUB_IN = 256
ROWS_EW = 256
SLAB_ATT = 32
TQ_ATT = 1024
SEG_REC = 512
HB_REC = 16
TM_OUT = 256
KC_OUT = 512
TM_FFN, TF_FFN = 512, 512


def _vmem_limit(nbytes):
    return int(min(V7X_VMEM_CEILING, nbytes * 1.25 + (4 << 20)))


def _sigmoid(x):
    return 1.0 / (1.0 + jnp.exp(-x))


def _dot_nt(a, b):
    return lax.dot_general(a, b, (((1,), (1,)), ((), ())), preferred_element_type=F32)


def _dot_tn(a, b):
    return lax.dot_general(a, b, (((0,), (0,)), ((), ())), preferred_element_type=F32)


def _rows(start, size):
    return pl.ds(start if isinstance(start, int) else pl.multiple_of(start, size), size)


def _row_tiles(n_rows, body):
    def step(r, c):
        body(_rows(r * ROWS_EW, ROWS_EW))
        return c
    lax.fori_loop(0, n_rows // ROWS_EW, step, 0)


def _segment_position(g, seq_groups):
    first, local, per = None, None, None
    for g0, n in reversed(seq_groups):
        loc = lax.rem(g - g0, n)
        fst = g - loc
        if first is None:
            first, local, per = fst, loc, n
        else:
            here = g < nxt
            first = jnp.where(here, fst, first)
            local = jnp.where(here, loc, local)
            per = jnp.where(here, n, per)
        nxt = g0
    return first, local, per


def _two_source_specs(tm, n_first):
    return (pl.BlockSpec((tm, D_MODEL), lambda i, *_: (jnp.minimum(i, n_first - 1), 0)),
            pl.BlockSpec((tm, D_MODEL), lambda i, *_: (jnp.maximum(i - n_first, 0), 0)))


def _norm_kernel(n_first, xa_ref, xb_ref, g_ref, o_ref):
    def run(x_ref):
        def body(rows):
            x = x_ref[rows, :]
            ms = jnp.mean(x * x, axis=-1, keepdims=True)
            o_ref[rows, :] = ((x * lax.rsqrt(ms + RMS_EPS)) * g_ref[...]).astype(o_ref.dtype)
        _row_tiles(TM_NORM, body)

    @pl.when(pl.program_id(0) < n_first)
    def _():
        run(xa_ref)

    @pl.when(pl.program_id(0) >= n_first)
    def _():
        run(xb_ref)


def _stream_norm(xa, xb, gain):
    T = xa.shape[0] + xb.shape[0]
    n_first = xa.shape[0] // TM_NORM
    nbytes = 2 * 2 * TM_NORM * D_MODEL * 4 + 2 * TM_NORM * D_MODEL * 2
    return pl.pallas_call(
        functools.partial(_norm_kernel, n_first),
        grid=(T // TM_NORM,),
        in_specs=[*_two_source_specs(TM_NORM, n_first), pl.BlockSpec((1, D_MODEL), lambda i: (0, 0))],
        out_specs=pl.BlockSpec((TM_NORM, D_MODEL), lambda i: (i, 0)),
        out_shape=jax.ShapeDtypeStruct((T, D_MODEL), BF16),
        compiler_params=pltpu.CompilerParams(
            dimension_semantics=("parallel",), vmem_limit_bytes=_vmem_limit(nbytes)),
        name="stream_norm",
    )(xa, xb, gain)


def _inproj_kernel(kinds, w_scale, col_tiles_ref, *refs):
    del col_tiles_ref
    with_epilogue = any(k != "plain" for k in kinds)
    if with_epilogue:
        xn_ref, cos_ref, sa_ref, sb_ref, w_ref, o_ref, wb_ref, y_ref = refs
    else:
        xn_ref, w_ref, o_ref, wb_ref = refs
    j = pl.program_id(0)

    @pl.when(pl.program_id(1) == 0)
    def _():
        def body(rows):
            w = w_ref[rows, :]
            wb_ref[rows, :] = (w if w_scale == 1.0 else w * w_scale).astype(BF16)
        _row_tiles(D_MODEL, body)

    def transform(kind, y, rows, lane_tile):
        if kind == "silu":
            return y * _sigmoid(y)
        if kind == "rope_k" and lane_tile >= N_KV_HEADS:
            return y
        y = (y * cos_ref[rows, :] + pltpu.roll(y, HEAD_DIM - ROPE_HALF, 1) * sa_ref[rows, :]
             + pltpu.roll(y, ROPE_HALF, 1) * sb_ref[rows, :])
        return y * Q_SCALE if kind == "rope" else y

    def epilogue(kind, sub):
        for r in range(TM_IN // ROWS_EW):
            rows = slice(r * ROWS_EW, (r + 1) * ROWS_EW)
            for cc in range(SUB_IN // LANES):
                lane_tile = sub * (SUB_IN // LANES) + cc
                y = y_ref[sub, rows, cc * LANES:(cc + 1) * LANES]
                o_ref[rows, lane_tile * LANES:(lane_tile + 1) * LANES] = (
                    transform(kind, y, rows, lane_tile).astype(o_ref.dtype))

    for kind in sorted(set(kinds)):
        tiles = [t for t, k in enumerate(kinds) if k == kind]
        cond = functools.reduce(jnp.logical_or, [j == t for t in tiles])

        @pl.when(cond)
        def _(kind=kind):
            if kind == "plain":
                o_ref[...] = jnp.dot(xn_ref[...], wb_ref[...], preferred_element_type=F32).astype(o_ref.dtype)
            else:
                n_sub = TN_IN // SUB_IN
                for sub in range(n_sub):
                    y_ref[sub] = jnp.dot(xn_ref[...], wb_ref[:, sub * SUB_IN:(sub + 1) * SUB_IN],
                                         preferred_element_type=F32)
                    if sub > 0:
                        epilogue(kind, sub - 1)
                epilogue(kind, n_sub - 1)


def _in_projection(xn, w_in, tables, table_groups, col_tiles, kinds, out_dtype, name, w_scale=1.0):
    T = xn.shape[0]
    n_col = len(col_tiles)
    col_tiles = jnp.asarray(col_tiles, jnp.int32)
    with_epilogue = any(k != "plain" for k in kinds)
    rope_tiles = [t for t, k in enumerate(kinds) if k.startswith("rope")]
    out_bytes = jnp.dtype(out_dtype).itemsize

    def table_map(j, i, ct):
        _, local, _ = _segment_position(i, table_groups)
        uses_tables = functools.reduce(jnp.logical_or, [j == t for t in rope_tiles])
        return (jnp.where(uses_tables, local, 0), 0)

    in_specs = [pl.BlockSpec((TM_IN, D_MODEL), lambda j, i, ct: (i, 0))]
    args = [xn]
    scratch = [pltpu.VMEM((D_MODEL, TN_IN), BF16)]
    nbytes = (2 * TM_IN * D_MODEL * 2 + 2 * D_MODEL * TN_IN * 4 + D_MODEL * TN_IN * 2
              + 2 * TM_IN * TN_IN * out_bytes + TM_IN * TN_IN * 4)
    if with_epilogue:
        in_specs += [pl.BlockSpec((TM_IN, LANES), table_map)] * 3
        args += list(tables)
        scratch.append(pltpu.VMEM((TN_IN // SUB_IN, TM_IN, SUB_IN), F32))
        nbytes += 2 * 3 * TM_IN * LANES * 4 + TM_IN * TN_IN * 4
    in_specs.append(pl.BlockSpec((D_MODEL, TN_IN), lambda j, i, ct: (0, ct[j])))
    args.append(w_in)
    return pl.pallas_call(
        functools.partial(_inproj_kernel, kinds, w_scale),
        grid_spec=pltpu.PrefetchScalarGridSpec(
            num_scalar_prefetch=1,
            grid=(n_col, T // TM_IN),
            in_specs=in_specs,
            out_specs=pl.BlockSpec((TM_IN, TN_IN), lambda j, i, ct: (i, j)),
            scratch_shapes=scratch),
        out_shape=jax.ShapeDtypeStruct((T, n_col * TN_IN), out_dtype),
        compiler_params=pltpu.CompilerParams(
            dimension_semantics=("arbitrary", "arbitrary"),
            vmem_limit_bytes=_vmem_limit(nbytes)),
        name=name,
    )(col_tiles, *args)


def _attn_kernel(seq_groups, sink_ref, q_ref, kc_ref, vc_ref, kp_ref, kn_ref, vp_ref, vn_ref,
                 ga_ref, o_ref, s_all, p_all, r_all):
    t = pl.program_id(0)
    h = pl.program_id(1)
    _, local, per = _segment_position(t, seq_groups)
    has_prev = local != 0
    has_next = local != per - 1

    k_all = jnp.concatenate([kp_ref[...], kc_ref[...], kn_ref[...]], axis=0)
    v_all = jnp.concatenate([vp_ref[...], vc_ref[...], vn_ref[...]], axis=0)
    qi = lax.broadcasted_iota(jnp.int32, (BLOCK, BLOCK), 0)
    kj = lax.broadcasted_iota(jnp.int32, (BLOCK, BLOCK), 1)
    neg = jnp.float32(-jnp.inf)
    bias_prev = jnp.where(kj >= qi, 0.0, neg)
    bias_next = jnp.where(kj <= qi, 0.0, neg)
    bias_prev0 = jnp.where(has_prev, bias_prev, neg)
    bias_next_last = jnp.where(has_next, bias_next, neg)
    n_blocks = TQ_ATT // BLOCK
    for b in range(n_blocks):
        rs = slice(b * BLOCK, (b + 1) * BLOCK)
        kb = k_all[b * BLOCK:(b + 3) * BLOCK]
        vb = v_all[b * BLOCK:(b + 3) * BLOCK]
        bp = bias_prev0 if b == 0 else bias_prev
        bn = bias_next_last if b == n_blocks - 1 else bias_next
        heads = [slice(g * HEAD_DIM, (g + 1) * HEAD_DIM) for g in range(GROUP)]
        s_ref, p_ref, r_ref = s_all.at[b], p_all.at[b], r_all.at[b]
        q4 = jnp.concatenate([q_ref[rs, cs] for cs in heads], axis=0)
        s_ref[...] = _dot_nt(q4, kb).reshape(GROUP, BLOCK, 3 * BLOCK)
        for g, cs in enumerate(heads):
            sink2 = sink_ref[GROUP * h + g] * LOG2E
            for r in range(BLOCK // SLAB_ATT):
                rr = slice(r * SLAB_ATT, (r + 1) * SLAB_ATT)
                s = s_ref[g, rr, :]
                s_p = s[:, :BLOCK] + bp[rr]
                s_c = s[:, BLOCK:2 * BLOCK]
                s_n = s[:, 2 * BLOCK:] + bn[rr]
                m = jnp.maximum(jnp.max(jnp.maximum(jnp.maximum(s_p, s_c), s_n), axis=-1, keepdims=True), sink2)
                p_p, p_c, p_n = jnp.exp2(s_p - m), jnp.exp2(s_c - m), jnp.exp2(s_n - m)
                denom = jnp.sum(p_p + p_c + p_n, axis=-1, keepdims=True) + jnp.exp2(sink2 - m)
                p_ref[g, rr, :] = jnp.concatenate([p_p, p_c, p_n], axis=1).astype(BF16)
                r_ref[g, rr, :] = jnp.broadcast_to(1.0 / denom, (SLAB_ATT, LANES))
        o4 = jnp.dot(p_ref[...].reshape(GROUP * BLOCK, 3 * BLOCK), vb, preferred_element_type=F32)
        for g, cs in enumerate(heads):
            o = o4[g * BLOCK:(g + 1) * BLOCK] * r_ref[g]
            o_ref[rs, cs] = (_sigmoid(ga_ref[rs, cs].astype(F32)) * o).astype(o_ref.dtype)


def _window_attention(slab, sink, seq_groups):
    T = slab.shape[0]
    n_tiles = T // TQ_ATT
    bpt = TQ_ATT // BLOCK
    last_blk = T // BLOCK - 1
    gw = GROUP * HEAD_DIM
    assert A_K % HEAD_DIM == 0 and A_V % HEAD_DIM == 0 and A_GA % gw == 0
    kcol, vcol, gcol = A_K // HEAD_DIM, A_V // HEAD_DIM, A_GA // gw
    nbytes = (2 * (3 * TQ_ATT * gw * 2 + 2 * TQ_ATT * HEAD_DIM * 2 + 4 * BLOCK * HEAD_DIM * 2)
              + bpt * GROUP * BLOCK * (3 * BLOCK * 6 + LANES * 4))
    return pl.pallas_call(
        functools.partial(_attn_kernel, seq_groups),
        grid=(n_tiles, N_KV_HEADS),
        in_specs=[
            pl.BlockSpec(memory_space=pltpu.SMEM),
            pl.BlockSpec((TQ_ATT, gw), lambda t, h: (t, h)),
            pl.BlockSpec((TQ_ATT, HEAD_DIM), lambda t, h: (t, kcol + h)),
            pl.BlockSpec((TQ_ATT, HEAD_DIM), lambda t, h: (t, vcol + h)),
            pl.BlockSpec((BLOCK, HEAD_DIM), lambda t, h: (jnp.maximum(t * bpt - 1, 0), kcol + h)),
            pl.BlockSpec((BLOCK, HEAD_DIM), lambda t, h: (jnp.minimum((t + 1) * bpt, last_blk), kcol + h)),
            pl.BlockSpec((BLOCK, HEAD_DIM), lambda t, h: (jnp.maximum(t * bpt - 1, 0), vcol + h)),
            pl.BlockSpec((BLOCK, HEAD_DIM), lambda t, h: (jnp.minimum((t + 1) * bpt, last_blk), vcol + h)),
            pl.BlockSpec((TQ_ATT, gw), lambda t, h: (t, gcol + h)),
        ],
        out_specs=pl.BlockSpec((TQ_ATT, gw), lambda t, h: (t, h)),
        out_shape=jax.ShapeDtypeStruct((T, D_ATT), BF16),
        scratch_shapes=[pltpu.VMEM((bpt, GROUP, BLOCK, 3 * BLOCK), F32),
                        pltpu.VMEM((bpt, GROUP, BLOCK, 3 * BLOCK), BF16),
                        pltpu.VMEM((bpt, GROUP, BLOCK, LANES), F32)],
        compiler_params=pltpu.CompilerParams(
            dimension_semantics=("parallel", "arbitrary"),
            vmem_limit_bytes=_vmem_limit(nbytes)),
        name="window_attention",
    )(sink, slab, slab, slab, slab, slab, slab, slab, slab)


def _hgrn_kernel(seq_groups, lb_ref, qf_ref, if_ref, zf_ref, qb_ref, ib_ref, zb_ref,
                 of_ref, ob_ref, st_ref, stb_ref, ds_ref, ops_ref, cross_ref, dec_ref, a_ref):
    g = pl.program_id(1)
    _, local, _ = _segment_position(g, seq_groups)

    @pl.when(local == 0)
    def _():
        st_ref[...] = jnp.zeros(st_ref.shape, F32)
        stb_ref[...] = jnp.zeros(stb_ref.shape, BF16)

    Q_IN, K_IN, Q_ST, K_ST, I_C = range(5)
    pair = 2 * CHUNK
    n_pairs = SEG_REC // pair
    rin = lax.broadcasted_iota(jnp.int32, (CHUNK, LANES), 0)
    ti = lax.broadcasted_iota(jnp.int32, (CHUNK, CHUNK), 0)
    si = lax.broadcasted_iota(jnp.int32, (CHUNK, CHUNK), 1)
    half = (slice(0, CHUNK), slice(CHUNK, pair))

    def prepare(d, hh, start, q_ref, i_ref, z_ref):
        cs = slice(hh * LANES, (hh + 1) * LANES)
        lb = lb_ref[:, cs]
        one_m_lb = 1.0 - lb
        q_in, k_in, k_st, q_dc, dec = [], [], [], [], []
        for t in range(2):
            rows = _rows(start + t * CHUNK, CHUNK)
            q = q_ref[rows, cs].astype(F32)
            gate = one_m_lb * (1.0 / (1.0 + jnp.exp2(z_ref[rows, cs])))
            k = one_m_lb - gate
            logf = jnp.log2(lb + gate)
            c = logf
            for s in (1, 2, 4, 8, 16):
                if d == 0:
                    c = c + jnp.where(rin >= s, pltpu.roll(c, s, 0), 0.0)
                else:
                    c = c + jnp.where(rin < CHUNK - s, pltpu.roll(c, CHUNK - s, 0), 0.0)
            last, mid = (CHUNK - 1, CHUNK // 2 - 1) if d == 0 else (0, CHUNK // 2)
            tot = c[last:last + 1, :]
            cref = c[mid:mid + 1, :]
            q_in.append(q * jnp.exp2(c - cref))
            k_in.append(k * jnp.exp2(cref - c))
            k_st.append(k_in[t] * jnp.exp2(tot - cref))
            q_dc.append(q_in[t] * jnp.exp2(cref))
            dec.append(jnp.exp2(tot))
            ops_ref[d, hh, Q_IN, half[t]] = q_in[t].astype(BF16)
            ops_ref[d, hh, K_IN, half[t]] = k_in[t].astype(BF16)
            ops_ref[d, hh, I_C, half[t]] = i_ref[rows, cs]
        ca, cb = (0, 1) if d == 0 else (1, 0)
        ops_ref[d, hh, Q_ST, half[ca]] = q_dc[ca].astype(BF16)
        ops_ref[d, hh, Q_ST, half[cb]] = (q_dc[cb] * dec[ca]).astype(BF16)
        ops_ref[d, hh, K_ST, half[ca]] = (k_st[ca] * dec[cb]).astype(BF16)
        ops_ref[d, hh, K_ST, half[cb]] = k_st[cb].astype(BF16)
        cross_ref[d, hh, 0] = q_dc[cb].astype(BF16)
        cross_ref[d, hh, 1] = k_st[ca].astype(BF16)
        dec_ref[d, hh] = jnp.broadcast_to(dec[0] * dec[1], (8, LANES))

    def intra(d, hh):
        for t in range(2):
            a_ref[d, hh, t] = _dot_nt(ops_ref[d, hh, Q_IN, half[t]], ops_ref[d, hh, K_IN, half[t]])
        a_ref[d, hh, 2] = _dot_nt(cross_ref[d, hh, 0], cross_ref[d, hh, 1])

    def output(d, hh, start, o_ref):
        cs = slice(hh * LANES, (hh + 1) * LANES)
        tri = (ti >= si) if d == 0 else (ti <= si)
        a_top = jnp.where(tri, a_ref[d, hh, 0], 0.0).astype(BF16)
        a_bot = jnp.where(tri, a_ref[d, hh, 1], 0.0).astype(BF16)
        a_x = a_ref[d, hh, 2].astype(BF16)
        i_top, i_bot = ops_ref[d, hh, I_C, half[0]], ops_ref[d, hh, I_C, half[1]]
        inter = jnp.dot(ops_ref[d, hh, Q_ST], stb_ref[d, hh], preferred_element_type=F32)
        if d == 0:
            both = jnp.dot(jnp.concatenate([a_top, a_x], axis=0), i_top, preferred_element_type=F32)
            o_top = both[:CHUNK] + inter[:CHUNK]
            o_bot = both[CHUNK:] + jnp.dot(a_bot, i_bot, preferred_element_type=F32) + inter[CHUNK:]
        else:
            both = jnp.dot(jnp.concatenate([a_x, a_bot], axis=0), i_bot, preferred_element_type=F32)
            o_top = both[:CHUNK] + jnp.dot(a_top, i_top, preferred_element_type=F32) + inter[:CHUNK]
            o_bot = both[CHUNK:] + inter[CHUNK:]
        o_ref[_rows(start, CHUNK), cs] = o_top.astype(o_ref.dtype)
        o_ref[_rows(start + CHUNK, CHUNK), cs] = o_bot.astype(o_ref.dtype)
        ds_ref[d, hh] = _dot_tn(ops_ref[d, hh, I_C], ops_ref[d, hh, K_ST])

    def update(d, hh):
        dec = dec_ref[d, hh, 0:1, :]
        for r in range(REC_VAL_DIM // CHUNK):
            rr = slice(r * CHUNK, (r + 1) * CHUNK)
            new = dec * st_ref[d, hh, rr, :] + ds_ref[d, hh, rr, :]
            st_ref[d, hh, rr, :] = new
        stb_ref[d, hh] = st_ref[d, hh].T.astype(BF16)

    def scan(n, carry):
        start_f = pl.multiple_of(n * pair, pair)
        start_b = pl.multiple_of((n_pairs - 1 - n) * pair, pair)
        for hh in range(HB_REC):
            prepare(0, hh, start_f, qf_ref, if_ref, zf_ref)
            prepare(1, hh, start_b, qb_ref, ib_ref, zb_ref)
        for hh in range(HB_REC):
            intra(0, hh)
            intra(1, hh)
        for hh in range(HB_REC):
            output(0, hh, start_f, of_ref)
            output(1, hh, start_b, ob_ref)
        for hh in range(HB_REC):
            update(0, hh)
            update(1, hh)
        return carry

    lax.fori_loop(0, n_pairs, scan, 0)


def _hgrn2(slab, gates, lb_row, seq_groups):
    T = slab.shape[0]
    w = HB_REC * LANES
    assert A_QR % w == 0 and A_IR % w == 0 and D_REC_K % w == 0

    def mirror(g):
        first, local, per = _segment_position(g, seq_groups)
        return first + per - 1 - local

    def fspec(off):
        return pl.BlockSpec((SEG_REC, w), lambda hb, g: (g, off // w + hb))

    def bspec(off):
        return pl.BlockSpec((SEG_REC, w), lambda hb, g: (mirror(g), off // w + hb))

    state = (2, HB_REC, REC_VAL_DIM, REC_KEY_DIM)
    nbytes = (2 * (4 * SEG_REC * w * 2 + 2 * SEG_REC * w * 4 + 2 * SEG_REC * w * 2)
              + 2 * HB_REC * (REC_VAL_DIM * REC_KEY_DIM * 10 + 12 * CHUNK * LANES * 2 + 8 * LANES * 4))
    out = jax.ShapeDtypeStruct((T, D_REC_V), BF16)
    return pl.pallas_call(
        functools.partial(_hgrn_kernel, seq_groups),
        grid=(N_REC_HEADS // HB_REC, T // SEG_REC),
        in_specs=[
            pl.BlockSpec((1, w), lambda hb, g: (0, hb)),
            fspec(A_QR), fspec(A_IR), fspec(0),
            bspec(A_QR), bspec(A_IR), bspec(D_REC_K),
        ],
        out_specs=[pl.BlockSpec((SEG_REC, w), lambda hb, g: (g, hb)),
                   pl.BlockSpec((SEG_REC, w), lambda hb, g: (mirror(g), hb))],
        out_shape=[out, out],
        scratch_shapes=[pltpu.VMEM(state, F32), pltpu.VMEM(state, BF16), pltpu.VMEM(state, F32),
                        pltpu.VMEM((2, HB_REC, 5, 2 * CHUNK, LANES), BF16),
                        pltpu.VMEM((2, HB_REC, 2, CHUNK, LANES), BF16),
                        pltpu.VMEM((2, HB_REC, 8, LANES), F32),
                        pltpu.VMEM((2, HB_REC, 3, CHUNK, CHUNK), F32)],
        compiler_params=pltpu.CompilerParams(
            dimension_semantics=("parallel", "arbitrary"),
            vmem_limit_bytes=_vmem_limit(nbytes)),
        name="hgrn2",
    )(lb_row, slab, slab, gates, slab, slab, gates)


def _outproj_kernel(n_first, a_ref, of_ref, ob_ref, grl_ref, grh_ref, gtl_ref, gth_ref, xa_ref, xb_ref,
                    w_ref, rg_ref, g_ref, h_ref, m_ref):
    half = D_REC_V // 2
    heads_per_chunk = KC_OUT // REC_VAL_DIM
    y = None
    for c in range(D_MODEL // KC_OUT):
        for hh in range(c * heads_per_chunk, (c + 1) * heads_per_chunk):
            cs = slice(hh * REC_VAL_DIM, (hh + 1) * REC_VAL_DIM)
            lo = hh * REC_VAL_DIM < half
            hs = cs if lo else slice(hh * REC_VAL_DIM - half, (hh + 1) * REC_VAL_DIM - half)
            rec = of_ref[:, cs].astype(F32) + ob_ref[:, cs].astype(F32)
            ms = jnp.mean(rec * rec, axis=-1, keepdims=True)
            r = (rec * lax.rsqrt(ms + RMS_EPS)) * rg_ref[:, cs]
            gr = (grl_ref if lo else grh_ref)[:, hs].astype(F32)
            gt = (gtl_ref if lo else gth_ref)[:, hs].astype(F32)
            r = _sigmoid(gt) * (r * (gr * _sigmoid(gr)))
            m_ref[:, cs] = (a_ref[:, cs].astype(F32) + r).astype(BF16)
        ks = slice(c * KC_OUT, (c + 1) * KC_OUT)
        part = jnp.dot(m_ref[:, ks], w_ref[ks, :], preferred_element_type=F32)
        y = part if y is None else y + part
    ms = jnp.mean(y * y, axis=-1, keepdims=True)
    h_ref[...] = (y * lax.rsqrt(ms + RMS_EPS)) * g_ref[...]

    @pl.when(pl.program_id(0) < n_first)
    def _():
        h_ref[...] += xa_ref[...]

    @pl.when(pl.program_id(0) >= n_first)
    def _():
        h_ref[...] += xb_ref[...]


def _out_projection(attn, rec_f, rec_b, slab, xa, xb, w_bf16, rec_gain, gain):
    T = attn.shape[0]
    n_first = xa.shape[0] // TM_OUT
    half = D_REC_V // 2
    assert A_GR % half == 0 and A_GT % half == 0
    tok = lambda i: (i, 0)
    nbytes = (2 * 3 * TM_OUT * D_MODEL * 2 + 2 * 4 * TM_OUT * half * 2 + 2 * 3 * TM_OUT * D_MODEL * 4
              + 2 * D_MODEL * D_MODEL * 2 + TM_OUT * D_MODEL * 2 + 2 * TM_OUT * D_MODEL * 4)
    return pl.pallas_call(
        functools.partial(_outproj_kernel, n_first),
        grid=(T // TM_OUT,),
        in_specs=[
            pl.BlockSpec((TM_OUT, D_MODEL), tok),
            pl.BlockSpec((TM_OUT, D_REC_V), tok),
            pl.BlockSpec((TM_OUT, D_REC_V), tok),
            pl.BlockSpec((TM_OUT, half), lambda i: (i, A_GR // half)),
            pl.BlockSpec((TM_OUT, half), lambda i: (i, A_GR // half + 1)),
            pl.BlockSpec((TM_OUT, half), lambda i: (i, A_GT // half)),
            pl.BlockSpec((TM_OUT, half), lambda i: (i, A_GT // half + 1)),
            *_two_source_specs(TM_OUT, n_first),
            pl.BlockSpec((D_MODEL, D_MODEL), lambda i: (0, 0)),
            pl.BlockSpec((1, D_REC_V), lambda i: (0, 0)),
            pl.BlockSpec((1, D_MODEL), lambda i: (0, 0)),
        ],
        out_specs=pl.BlockSpec((TM_OUT, D_MODEL), tok),
        out_shape=jax.ShapeDtypeStruct((T, D_MODEL), F32),
        scratch_shapes=[pltpu.VMEM((TM_OUT, D_MODEL), BF16)],
        compiler_params=pltpu.CompilerParams(
            dimension_semantics=("parallel",),
            vmem_limit_bytes=_vmem_limit(nbytes)),
        name="out_projection",
    )(attn, rec_f, rec_b, slab, slab, slab, slab, xa, xb, w_bf16, rec_gain, gain)


def _ffn_kernel(n_first, h_ref, gpre_ref, gpost_ref, wg_ref, wu_ref, wd_ref, oa_ref, ob_ref, hn_ref, acc_ref):
    i = pl.program_id(0)
    j = pl.program_id(1)

    @pl.when(j == 0)
    def _():
        def body(rows):
            h = h_ref[rows, :]
            ms = jnp.mean(h * h, axis=-1, keepdims=True)
            hn_ref[rows, :] = ((h * lax.rsqrt(ms + RMS_EPS)) * gpre_ref[...]).astype(BF16)
            acc_ref[rows, :] = jnp.zeros((ROWS_EW, D_MODEL), F32)
        _row_tiles(TM_FFN, body)

    hn = hn_ref[...]
    g = jnp.dot(hn, wg_ref[...], preferred_element_type=F32)
    u = jnp.dot(hn, wu_ref[...].astype(BF16), preferred_element_type=F32)
    act = ((g * _sigmoid(g)) * u).astype(BF16)
    acc_ref[...] += jnp.dot(act, wd_ref[...].astype(BF16), preferred_element_type=F32)

    def finish(o_ref):
        def body(rows):
            y = acc_ref[rows, :]
            ms = jnp.mean(y * y, axis=-1, keepdims=True)
            o_ref[rows, :] = h_ref[rows, :] + (y * lax.rsqrt(ms + RMS_EPS)) * gpost_ref[...]
        _row_tiles(TM_FFN, body)

    last = j == pl.num_programs(1) - 1

    @pl.when(jnp.logical_and(last, i < n_first))
    def _():
        finish(oa_ref)

    @pl.when(jnp.logical_and(last, i >= n_first))
    def _():
        finish(ob_ref)


def _ffn(h, n_tokens_first, gpre, gpost, wg, wu, wd):
    T = h.shape[0]
    n_first = n_tokens_first // TM_FFN
    nbytes = (2 * 3 * TM_FFN * D_MODEL * 4 + TM_FFN * D_MODEL * 2 + 2 * 5 * D_MODEL * TF_FFN * 2
              + 3 * TM_FFN * TF_FFN * 4 + 2 * TM_FFN * D_MODEL * 4)
    return pl.pallas_call(
        functools.partial(_ffn_kernel, n_first),
        grid=(T // TM_FFN, D_FF // TF_FFN),
        in_specs=[
            pl.BlockSpec((TM_FFN, D_MODEL), lambda i, j: (i, 0)),
            pl.BlockSpec((1, D_MODEL), lambda i, j: (0, 0)),
            pl.BlockSpec((1, D_MODEL), lambda i, j: (0, 0)),
            pl.BlockSpec((D_MODEL, TF_FFN), lambda i, j: (0, j)),
            pl.BlockSpec((D_MODEL, TF_FFN), lambda i, j: (0, j)),
            pl.BlockSpec((TF_FFN, D_MODEL), lambda i, j: (j, 0)),
        ],
        out_specs=list(_two_source_specs(TM_FFN, n_first)),
        out_shape=[jax.ShapeDtypeStruct((n_tokens_first, D_MODEL), F32),
                   jax.ShapeDtypeStruct((T - n_tokens_first, D_MODEL), F32)],
        scratch_shapes=[pltpu.VMEM((TM_FFN, D_MODEL), BF16), pltpu.VMEM((TM_FFN, D_MODEL), F32)],
        compiler_params=pltpu.CompilerParams(
            dimension_semantics=("arbitrary", "arbitrary"),
            vmem_limit_bytes=_vmem_limit(nbytes)),
        name="swiglu_ffn",
    )(h, gpre, gpost, wg, wu, wd)


def _rope_tables(max_len):
    pos = jnp.arange(max_len, dtype=F32)
    inv_freq = ROPE_THETA ** (-jnp.arange(ROPE_HALF, dtype=F32) / ROPE_HALF)
    ang = pos[:, None] * inv_freq[None, :]
    cos, sin = jnp.cos(ang), jnp.sin(ang)
    rest = HEAD_DIM - ROPE_DIM
    cos_t = jnp.concatenate([cos, cos, jnp.ones((max_len, rest), F32)], axis=1)
    sa_t = jnp.concatenate([-sin, jnp.zeros((max_len, HEAD_DIM - ROPE_HALF), F32)], axis=1)
    sb_t = jnp.concatenate([jnp.zeros((max_len, ROPE_HALF), F32), sin, jnp.zeros((max_len, rest), F32)], axis=1)
    return cos_t, sa_t, sb_t


def _seq_groups(seq_shapes, seg):
    groups, first = [], 0
    for B, L in seq_shapes:
        assert L % seg == 0
        groups.append((first, L // seg))
        first += B * L // seg
    return tuple(groups)


def _encoder_layer(xa, xb, seq_shapes, tables, w_in, sink, rec_norm, lb, w_out, norm_mix_pre, norm_mix_post,
                   norm_ffn_pre, norm_ffn_post, w_gate, w_up, w_down):
    row = lambda v: v.astype(F32).reshape(1, -1)
    xn = _stream_norm(xa, xb, row(norm_mix_pre))

    assert all(off % TN_IN == 0 and w % TN_IN == 0 for off, w in SLAB_GROUPS) and OFF_ZF % TN_IN == 0
    gate_tiles = list(range(OFF_ZF // TN_IN, OFF_IR // TN_IN))
    slab_tiles = [t for off, w in SLAB_GROUPS for t in range(off // TN_IN, (off + w) // TN_IN)]
    kind_of = lambda t: ("rope" if t < OFF_K // TN_IN else "rope_k" if t == OFF_K // TN_IN
                         else "silu" if OFF_QR // TN_IN <= t < OFF_ZF // TN_IN else "plain")
    table_groups = _seq_groups(seq_shapes, TM_IN)
    slab = _in_projection(xn, w_in, tables, table_groups, slab_tiles, tuple(kind_of(t) for t in slab_tiles),
                          BF16, "in_projection")
    gates = _in_projection(xn, w_in, None, None, gate_tiles, ("plain",) * len(gate_tiles),
                           F32, "gate_projection", w_scale=-LOG2E)

    attn = _window_attention(slab, sink.astype(F32), _seq_groups(seq_shapes, TQ_ATT))
    rec_f, rec_b = _hgrn2(slab, gates, row(lb), _seq_groups(seq_shapes, SEG_REC))
    h = _out_projection(attn, rec_f, rec_b, slab, xa, xb, w_out.astype(BF16), row(rec_norm),
                        row(norm_mix_post))
    return _ffn(h, xa.shape[0], row(norm_ffn_pre), row(norm_ffn_post),
                w_gate.astype(BF16), w_up, w_down)


def kernel(x_prompt, x_sample, w_in, sink, rec_norm, lb_logits, w_out, norm_mix_pre, norm_mix_post,
           norm_ffn_pre, norm_ffn_post, w_gate, w_up, w_down):
    lb_all = jnp.cumsum(jax.nn.softmax(lb_logits.astype(F32), axis=0), axis=0)
    seq_shapes = (x_prompt.shape[:2], x_sample.shape[:2])
    xa = x_prompt.reshape(-1, D_MODEL)
    xb = x_sample.reshape(-1, D_MODEL)
    tables = _rope_tables(max(L for _, L in seq_shapes))
    for l in range(DEPTH):
        xa, xb = _encoder_layer(xa, xb, seq_shapes, tables, w_in[l], sink[l], rec_norm[l], lb_all[l], w_out[l],
                                norm_mix_pre[l], norm_mix_post[l], norm_ffn_pre[l], norm_ffn_post[l],
                                w_gate[l], w_up[l], w_down[l])
    return (xa.reshape(x_prompt.shape), xb.reshape(x_sample.shape))
```

```python
import functools

import jax
import jax.numpy as jnp
import numpy as np
from jax import lax
from jax.experimental import pallas as pl
from jax.experimental.pallas import tpu as pltpu

F32 = jnp.float32
BF16 = jnp.bfloat16

D_MODEL = 2048
DEPTH = 1
HEAD_DIM = 128
N_Q_HEADS = 16
N_KV_HEADS = 4
GROUP = N_Q_HEADS // N_KV_HEADS
WINDOW = 128
BLOCK = 128
ROPE_DIM = HEAD_DIM // 4
ROPE_HALF = ROPE_DIM // 2
ROPE_THETA = 500000.0
N_REC_HEADS = 16
REC_KEY_DIM = 128
REC_VAL_DIM = 128
CHUNK = 32
D_FF = -(-8 * D_MODEL // (3 * 256)) * 256
RMS_EPS = 1e-6
LOG2E = 1.4426950408889634
Q_SCALE = HEAD_DIM ** -0.5 * LOG2E

D_ATT = N_Q_HEADS * HEAD_DIM
D_KV = N_KV_HEADS * HEAD_DIM
D_REC_K = N_REC_HEADS * REC_KEY_DIM
D_REC_V = N_REC_HEADS * REC_VAL_DIM
SPLIT_SIZES = (D_ATT, D_KV, D_KV, D_REC_K, D_REC_K, D_REC_K, D_REC_V, D_REC_V, D_MODEL, D_MODEL)
D_IN = sum(SPLIT_SIZES)
(OFF_Q, OFF_K, OFF_V, OFF_QR, OFF_ZF, OFF_ZB, OFF_IR, OFF_GR, OFF_GA, OFF_GT) = (
    int(v) for v in np.concatenate([[0], np.cumsum(SPLIT_SIZES)[:-1]]))
SLAB_GROUPS = ((OFF_Q, D_ATT), (OFF_QR, D_REC_K), (OFF_IR, D_REC_V), (OFF_GR, D_REC_V), (OFF_GA, D_MODEL),
               (OFF_GT, D_MODEL), (OFF_K, 2 * D_KV))
A_Q, A_QR, A_IR, A_GR, A_GA, A_GT, A_K = (int(v) for v in np.cumsum([0] + [w for _, w in SLAB_GROUPS])[:-1])
A_V = A_K + D_KV

V7X_VMEM_CEILING = 56 * 1024 * 1024
LANES = 128

TM_NORM = 512
TM_IN, TN_IN = 1024, 1024
SUB_IN = 256
ROWS_EW = 256
SLAB_ATT = 32
TQ_ATT = 1024
SEG_REC = 512
HB_REC = 16
TM_OUT = 256
KC_OUT = 512
TM_FFN, TF_FFN = 512, 512


def _vmem_limit(nbytes):
    return int(min(V7X_VMEM_CEILING, nbytes * 1.25 + (4 << 20)))


def _sigmoid(x):
    return 1.0 / (1.0 + jnp.exp(-x))


def _dot_nt(a, b):
    return lax.dot_general(a, b, (((1,), (1,)), ((), ())), preferred_element_type=F32)


def _dot_tn(a, b):
    return lax.dot_general(a, b, (((0,), (0,)), ((), ())), preferred_element_type=F32)


def _rows(start, size):
    return pl.ds(start if isinstance(start, int) else pl.multiple_of(start, size), size)


def _row_tiles(n_rows, body):
    def step(r, c):
        body(_rows(r * ROWS_EW, ROWS_EW))
        return c
    lax.fori_loop(0, n_rows // ROWS_EW, step, 0)


def _segment_position(g, seq_groups):
    first, local, per = None, None, None
    for g0, n in reversed(seq_groups):
        loc = lax.rem(g - g0, n)
        fst = g - loc
        if first is None:
            first, local, per = fst, loc, n
        else:
            here = g < nxt
            first = jnp.where(here, fst, first)
            local = jnp.where(here, loc, local)
            per = jnp.where(here, n, per)
        nxt = g0
    return first, local, per


def _two_source_specs(tm, n_first):
    return (pl.BlockSpec((tm, D_MODEL), lambda i, *_: (jnp.minimum(i, n_first - 1), 0)),
            pl.BlockSpec((tm, D_MODEL), lambda i, *_: (jnp.maximum(i - n_first, 0), 0)))


def _norm_kernel(n_first, xa_ref, xb_ref, g_ref, o_ref):
    def run(x_ref):
        def body(rows):
            x = x_ref[rows, :]
            ms = jnp.mean(x * x, axis=-1, keepdims=True)
            o_ref[rows, :] = ((x * lax.rsqrt(ms + RMS_EPS)) * g_ref[...]).astype(o_ref.dtype)
        _row_tiles(TM_NORM, body)

    @pl.when(pl.program_id(0) < n_first)
    def _():
        run(xa_ref)

    @pl.when(pl.program_id(0) >= n_first)
    def _():
        run(xb_ref)


def _stream_norm(xa, xb, gain):
    T = xa.shape[0] + xb.shape[0]
    n_first = xa.shape[0] // TM_NORM
    nbytes = 2 * 2 * TM_NORM * D_MODEL * 4 + 2 * TM_NORM * D_MODEL * 2
    return pl.pallas_call(
        functools.partial(_norm_kernel, n_first),
        grid=(T // TM_NORM,),
        in_specs=[*_two_source_specs(TM_NORM, n_first), pl.BlockSpec((1, D_MODEL), lambda i: (0, 0))],
        out_specs=pl.BlockSpec((TM_NORM, D_MODEL), lambda i: (i, 0)),
        out_shape=jax.ShapeDtypeStruct((T, D_MODEL), BF16),
        compiler_params=pltpu.CompilerParams(
            dimension_semantics=("parallel",), vmem_limit_bytes=_vmem_limit(nbytes)),
        name="stream_norm",
    )(xa, xb, gain)


def _inproj_kernel(kinds, w_scale, col_tiles_ref, *refs):
    del col_tiles_ref
    with_epilogue = any(k != "plain" for k in kinds)
    if with_epilogue:
        xn_ref, cos_ref, sa_ref, sb_ref, w_ref, o_ref, wb_ref, y_ref = refs
    else:
        xn_ref, w_ref, o_ref, wb_ref = refs
    j = pl.program_id(0)

    @pl.when(pl.program_id(1) == 0)
    def _():
        def body(rows):
            w = w_ref[rows, :]
            wb_ref[rows, :] = (w if w_scale == 1.0 else w * w_scale).astype(BF16)
        _row_tiles(D_MODEL, body)

    def transform(kind, y, rows, lane_tile):
        if kind == "silu":
            return y * _sigmoid(y)
        if kind == "rope_k" and lane_tile >= N_KV_HEADS:
            return y
        y = (y * cos_ref[rows, :] + pltpu.roll(y, HEAD_DIM - ROPE_HALF, 1) * sa_ref[rows, :]
             + pltpu.roll(y, ROPE_HALF, 1) * sb_ref[rows, :])
        return y * Q_SCALE if kind == "rope" else y

    def epilogue(kind, sub):
        for r in range(TM_IN // ROWS_EW):
            rows = slice(r * ROWS_EW, (r + 1) * ROWS_EW)
            for cc in range(SUB_IN // LANES):
                lane_tile = sub * (SUB_IN // LANES) + cc
                y = y_ref[sub, rows, cc * LANES:(cc + 1) * LANES]
                o_ref[rows, lane_tile * LANES:(lane_tile + 1) * LANES] = (
                    transform(kind, y, rows, lane_tile).astype(o_ref.dtype))

    for kind in sorted(set(kinds)):
        tiles = [t for t, k in enumerate(kinds) if k == kind]
        cond = functools.reduce(jnp.logical_or, [j == t for t in tiles])

        @pl.when(cond)
        def _(kind=kind):
            if kind == "plain":
                o_ref[...] = jnp.dot(xn_ref[...], wb_ref[...], preferred_element_type=F32).astype(o_ref.dtype)
            else:
                n_sub = TN_IN // SUB_IN
                for sub in range(n_sub):
                    y_ref[sub] = jnp.dot(xn_ref[...], wb_ref[:, sub * SUB_IN:(sub + 1) * SUB_IN],
                                         preferred_element_type=F32)
                    if sub > 0:
                        epilogue(kind, sub - 1)
                epilogue(kind, n_sub - 1)


def _in_projection(xn, w_in, tables, table_groups, col_tiles, kinds, out_dtype, name, w_scale=1.0):
    T = xn.shape[0]
    n_col = len(col_tiles)
    col_tiles = jnp.asarray(col_tiles, jnp.int32)
    with_epilogue = any(k != "plain" for k in kinds)
    rope_tiles = [t for t, k in enumerate(kinds) if k.startswith("rope")]
    out_bytes = jnp.dtype(out_dtype).itemsize

    def table_map(j, i, ct):
        _, local, _ = _segment_position(i, table_groups)
        uses_tables = functools.reduce(jnp.logical_or, [j == t for t in rope_tiles])
        return (jnp.where(uses_tables, local, 0), 0)

    in_specs = [pl.BlockSpec((TM_IN, D_MODEL), lambda j, i, ct: (i, 0))]
    args = [xn]
    scratch = [pltpu.VMEM((D_MODEL, TN_IN), BF16)]
    nbytes = (2 * TM_IN * D_MODEL * 2 + 2 * D_MODEL * TN_IN * 4 + D_MODEL * TN_IN * 2
              + 2 * TM_IN * TN_IN * out_bytes + TM_IN * TN_IN * 4)
    if with_epilogue:
        in_specs += [pl.BlockSpec((TM_IN, LANES), table_map)] * 3
        args += list(tables)
        scratch.append(pltpu.VMEM((TN_IN // SUB_IN, TM_IN, SUB_IN), F32))
        nbytes += 2 * 3 * TM_IN * LANES * 4 + TM_IN * TN_IN * 4
    in_specs.append(pl.BlockSpec((D_MODEL, TN_IN), lambda j, i, ct: (0, ct[j])))
    args.append(w_in)
    return pl.pallas_call(
        functools.partial(_inproj_kernel, kinds, w_scale),
        grid_spec=pltpu.PrefetchScalarGridSpec(
            num_scalar_prefetch=1,
            grid=(n_col, T // TM_IN),
            in_specs=in_specs,
            out_specs=pl.BlockSpec((TM_IN, TN_IN), lambda j, i, ct: (i, j)),
            scratch_shapes=scratch),
        out_shape=jax.ShapeDtypeStruct((T, n_col * TN_IN), out_dtype),
        compiler_params=pltpu.CompilerParams(
            dimension_semantics=("arbitrary", "arbitrary"),
            vmem_limit_bytes=_vmem_limit(nbytes)),
        name=name,
    )(col_tiles, *args)


def _attn_kernel(seq_groups, sink_ref, q_ref, kc_ref, vc_ref, kp_ref, kn_ref, vp_ref, vn_ref,
                 ga_ref, o_ref, s_all, p_all, r_all):
    t = pl.program_id(0)
    h = pl.program_id(1)
    _, local, per = _segment_position(t, seq_groups)
    has_prev = local != 0
    has_next = local != per - 1

    k_all = jnp.concatenate([kp_ref[...], kc_ref[...], kn_ref[...]], axis=0)
    v_all = jnp.concatenate([vp_ref[...], vc_ref[...], vn_ref[...]], axis=0)
    qi = lax.broadcasted_iota(jnp.int32, (BLOCK, BLOCK), 0)
    kj = lax.broadcasted_iota(jnp.int32, (BLOCK, BLOCK), 1)
    neg = jnp.float32(-jnp.inf)
    bias_prev = jnp.where(kj >= qi, 0.0, neg)
    bias_next = jnp.where(kj <= qi, 0.0, neg)
    bias_prev0 = jnp.where(has_prev, bias_prev, neg)
    bias_next_last = jnp.where(has_next, bias_next, neg)
    n_blocks = TQ_ATT // BLOCK
    for b in range(n_blocks):
        rs = slice(b * BLOCK, (b + 1) * BLOCK)
        kb = k_all[b * BLOCK:(b + 3) * BLOCK]
        vb = v_all[b * BLOCK:(b + 3) * BLOCK]
        bp = bias_prev0 if b == 0 else bias_prev
        bn = bias_next_last if b == n_blocks - 1 else bias_next
        heads = [slice(g * HEAD_DIM, (g + 1) * HEAD_DIM) for g in range(GROUP)]
        s_ref, p_ref, r_ref = s_all.at[b], p_all.at[b], r_all.at[b]
        q4 = jnp.concatenate([q_ref[rs, cs] for cs in heads], axis=0)
        s_ref[...] = _dot_nt(q4, kb).reshape(GROUP, BLOCK, 3 * BLOCK)
        for g, cs in enumerate(heads):
            sink2 = sink_ref[GROUP * h + g] * LOG2E
            for r in range(BLOCK // SLAB_ATT):
                rr = slice(r * SLAB_ATT, (r + 1) * SLAB_ATT)
                s = s_ref[g, rr, :]
                s_p = s[:, :BLOCK] + bp[rr]
                s_c = s[:, BLOCK:2 * BLOCK]
                s_n = s[:, 2 * BLOCK:] + bn[rr]
                m = jnp.maximum(jnp.max(jnp.maximum(jnp.maximum(s_p, s_c), s_n), axis=-1, keepdims=True), sink2)
                p_p, p_c, p_n = jnp.exp2(s_p - m), jnp.exp2(s_c - m), jnp.exp2(s_n - m)
                denom = jnp.sum(p_p + p_c + p_n, axis=-1, keepdims=True) + jnp.exp2(sink2 - m)
                p_ref[g, rr, :] = jnp.concatenate([p_p, p_c, p_n], axis=1).astype(BF16)
                r_ref[g, rr, :] = jnp.broadcast_to(1.0 / denom, (SLAB_ATT, LANES))
        o4 = jnp.dot(p_ref[...].reshape(GROUP * BLOCK, 3 * BLOCK), vb, preferred_element_type=F32)
        for g, cs in enumerate(heads):
            o = o4[g * BLOCK:(g + 1) * BLOCK] * r_ref[g]
            o_ref[rs, cs] = (_sigmoid(ga_ref[rs, cs].astype(F32)) * o).astype(o_ref.dtype)


def _window_attention(slab, sink, seq_groups):
    T = slab.shape[0]
    n_tiles = T // TQ_ATT
    bpt = TQ_ATT // BLOCK
    last_blk = T // BLOCK - 1
    gw = GROUP * HEAD_DIM
    assert A_K % HEAD_DIM == 0 and A_V % HEAD_DIM == 0 and A_GA % gw == 0
    kcol, vcol, gcol = A_K // HEAD_DIM, A_V // HEAD_DIM, A_GA // gw
    nbytes = (2 * (3 * TQ_ATT * gw * 2 + 2 * TQ_ATT * HEAD_DIM * 2 + 4 * BLOCK * HEAD_DIM * 2)
              + bpt * GROUP * BLOCK * (3 * BLOCK * 6 + LANES * 4))
    return pl.pallas_call(
        functools.partial(_attn_kernel, seq_groups),
        grid=(n_tiles, N_KV_HEADS),
        in_specs=[
            pl.BlockSpec(memory_space=pltpu.SMEM),
            pl.BlockSpec((TQ_ATT, gw), lambda t, h: (t, h)),
            pl.BlockSpec((TQ_ATT, HEAD_DIM), lambda t, h: (t, kcol + h)),
            pl.BlockSpec((TQ_ATT, HEAD_DIM), lambda t, h: (t, vcol + h)),
            pl.BlockSpec((BLOCK, HEAD_DIM), lambda t, h: (jnp.maximum(t * bpt - 1, 0), kcol + h)),
            pl.BlockSpec((BLOCK, HEAD_DIM), lambda t, h: (jnp.minimum((t + 1) * bpt, last_blk), kcol + h)),
            pl.BlockSpec((BLOCK, HEAD_DIM), lambda t, h: (jnp.maximum(t * bpt - 1, 0), vcol + h)),
            pl.BlockSpec((BLOCK, HEAD_DIM), lambda t, h: (jnp.minimum((t + 1) * bpt, last_blk), vcol + h)),
            pl.BlockSpec((TQ_ATT, gw), lambda t, h: (t, gcol + h)),
        ],
        out_specs=pl.BlockSpec((TQ_ATT, gw), lambda t, h: (t, h)),
        out_shape=jax.ShapeDtypeStruct((T, D_ATT), BF16),
        scratch_shapes=[pltpu.VMEM((bpt, GROUP, BLOCK, 3 * BLOCK), F32),
                        pltpu.VMEM((bpt, GROUP, BLOCK, 3 * BLOCK), BF16),
                        pltpu.VMEM((bpt, GROUP, BLOCK, LANES), F32)],
        compiler_params=pltpu.CompilerParams(
            dimension_semantics=("parallel", "arbitrary"),
            vmem_limit_bytes=_vmem_limit(nbytes)),
        name="window_attention",
    )(sink, slab, slab, slab, slab, slab, slab, slab, slab)


def _hgrn_kernel(seq_groups, lb_ref, qf_ref, if_ref, zf_ref, qb_ref, ib_ref, zb_ref,
                 of_ref, ob_ref, st_ref, stb_ref, ds_ref, ops_ref, cross_ref, dec_ref, a_ref):
    g = pl.program_id(1)
    _, local, _ = _segment_position(g, seq_groups)

    @pl.when(local == 0)
    def _():
        st_ref[...] = jnp.zeros(st_ref.shape, F32)
        stb_ref[...] = jnp.zeros(stb_ref.shape, BF16)

    Q_IN, K_IN, Q_ST, K_ST, I_C = range(5)
    pair = 2 * CHUNK
    n_pairs = SEG_REC // pair
    rin = lax.broadcasted_iota(jnp.int32, (CHUNK, LANES), 0)
    ti = lax.broadcasted_iota(jnp.int32, (CHUNK, CHUNK), 0)
    si = lax.broadcasted_iota(jnp.int32, (CHUNK, CHUNK), 1)
    half = (slice(0, CHUNK), slice(CHUNK, pair))

    def prepare(d, hh, start, q_ref, i_ref, z_ref):
        cs = slice(hh * LANES, (hh + 1) * LANES)
        lb = lb_ref[:, cs]
        one_m_lb = 1.0 - lb
        q_in, k_in, k_st, q_dc, dec = [], [], [], [], []
        for t in range(2):
            rows = _rows(start + t * CHUNK, CHUNK)
            q = q_ref[rows, cs].astype(F32)
            gate = one_m_lb * (1.0 / (1.0 + jnp.exp2(z_ref[rows, cs])))
            k = one_m_lb - gate
            logf = jnp.log2(lb + gate)
            c = logf
            for s in (1, 2, 4, 8, 16):
                if d == 0:
                    c = c + jnp.where(rin >= s, pltpu.roll(c, s, 0), 0.0)
                else:
                    c = c + jnp.where(rin < CHUNK - s, pltpu.roll(c, CHUNK - s, 0), 0.0)
            last, mid = (CHUNK - 1, CHUNK // 2 - 1) if d == 0 else (0, CHUNK // 2)
            tot = c[last:last + 1, :]
            cref = c[mid:mid + 1, :]
            q_in.append(q * jnp.exp2(c - cref))
            k_in.append(k * jnp.exp2(cref - c))
            k_st.append(k_in[t] * jnp.exp2(tot - cref))
            q_dc.append(q_in[t] * jnp.exp2(cref))
            dec.append(jnp.exp2(tot))
            ops_ref[d, hh, Q_IN, half[t]] = q_in[t].astype(BF16)
            ops_ref[d, hh, K_IN, half[t]] = k_in[t].astype(BF16)
            ops_ref[d, hh, I_C, half[t]] = i_ref[rows, cs]
        ca, cb = (0, 1) if d == 0 else (1, 0)
        ops_ref[d, hh, Q_ST, half[ca]] = q_dc[ca].astype(BF16)
        ops_ref[d, hh, Q_ST, half[cb]] = (q_dc[cb] * dec[ca]).astype(BF16)
        ops_ref[d, hh, K_ST, half[ca]] = (k_st[ca] * dec[cb]).astype(BF16)
        ops_ref[d, hh, K_ST, half[cb]] = k_st[cb].astype(BF16)
        cross_ref[d, hh, 0] = q_dc[cb].astype(BF16)
        cross_ref[d, hh, 1] = k_st[ca].astype(BF16)
        dec_ref[d, hh] = jnp.broadcast_to(dec[0] * dec[1], (8, LANES))

    def intra(d, hh):
        for t in range(2):
            a_ref[d, hh, t] = _dot_nt(ops_ref[d, hh, Q_IN, half[t]], ops_ref[d, hh, K_IN, half[t]])
        a_ref[d, hh, 2] = _dot_nt(cross_ref[d, hh, 0], cross_ref[d, hh, 1])

    def output(d, hh, start, o_ref):
        cs = slice(hh * LANES, (hh + 1) * LANES)
        tri = (ti >= si) if d == 0 else (ti <= si)
        a_top = jnp.where(tri, a_ref[d, hh, 0], 0.0).astype(BF16)
        a_bot = jnp.where(tri, a_ref[d, hh, 1], 0.0).astype(BF16)
        a_x = a_ref[d, hh, 2].astype(BF16)
        i_top, i_bot = ops_ref[d, hh, I_C, half[0]], ops_ref[d, hh, I_C, half[1]]
        inter = jnp.dot(ops_ref[d, hh, Q_ST], stb_ref[d, hh], preferred_element_type=F32)
        if d == 0:
            both = jnp.dot(jnp.concatenate([a_top, a_x], axis=0), i_top, preferred_element_type=F32)
            o_top = both[:CHUNK] + inter[:CHUNK]
            o_bot = both[CHUNK:] + jnp.dot(a_bot, i_bot, preferred_element_type=F32) + inter[CHUNK:]
        else:
            both = jnp.dot(jnp.concatenate([a_x, a_bot], axis=0), i_bot, preferred_element_type=F32)
            o_top = both[:CHUNK] + jnp.dot(a_top, i_top, preferred_element_type=F32) + inter[:CHUNK]
            o_bot = both[CHUNK:] + inter[CHUNK:]
        o_ref[_rows(start, CHUNK), cs] = o_top.astype(o_ref.dtype)
        o_ref[_rows(start + CHUNK, CHUNK), cs] = o_bot.astype(o_ref.dtype)
        ds_ref[d, hh] = _dot_tn(ops_ref[d, hh, I_C], ops_ref[d, hh, K_ST])

    def update(d, hh):
        dec = dec_ref[d, hh, 0:1, :]
        for r in range(REC_VAL_DIM // CHUNK):
            rr = slice(r * CHUNK, (r + 1) * CHUNK)
            new = dec * st_ref[d, hh, rr, :] + ds_ref[d, hh, rr, :]
            st_ref[d, hh, rr, :] = new
        stb_ref[d, hh] = st_ref[d, hh].T.astype(BF16)

    def scan(n, carry):
        start_f = pl.multiple_of(n * pair, pair)
        start_b = pl.multiple_of((n_pairs - 1 - n) * pair, pair)
        for hh in range(HB_REC):
            prepare(0, hh, start_f, qf_ref, if_ref, zf_ref)
            prepare(1, hh, start_b, qb_ref, ib_ref, zb_ref)
        for hh in range(HB_REC):
            intra(0, hh)
            intra(1, hh)
        for hh in range(HB_REC):
            output(0, hh, start_f, of_ref)
            output(1, hh, start_b, ob_ref)
        for hh in range(HB_REC):
            update(0, hh)
            update(1, hh)
        return carry

    lax.fori_loop(0, n_pairs, scan, 0)


def _hgrn2(slab, gates, lb_row, seq_groups):
    T = slab.shape[0]
    w = HB_REC * LANES
    assert A_QR % w == 0 and A_IR % w == 0 and D_REC_K % w == 0

    def mirror(g):
        first, local, per = _segment_position(g, seq_groups)
        return first + per - 1 - local

    def fspec(off):
        return pl.BlockSpec((SEG_REC, w), lambda hb, g: (g, off // w + hb))

    def bspec(off):
        return pl.BlockSpec((SEG_REC, w), lambda hb, g: (mirror(g), off // w + hb))

    state = (2, HB_REC, REC_VAL_DIM, REC_KEY_DIM)
    nbytes = (2 * (4 * SEG_REC * w * 2 + 2 * SEG_REC * w * 4 + 2 * SEG_REC * w * 2)
              + 2 * HB_REC * (REC_VAL_DIM * REC_KEY_DIM * 10 + 12 * CHUNK * LANES * 2 + 8 * LANES * 4))
    out = jax.ShapeDtypeStruct((T, D_REC_V), BF16)
    return pl.pallas_call(
        functools.partial(_hgrn_kernel, seq_groups),
        grid=(N_REC_HEADS // HB_REC, T // SEG_REC),
        in_specs=[
            pl.BlockSpec((1, w), lambda hb, g: (0, hb)),
            fspec(A_QR), fspec(A_IR), fspec(0),
            bspec(A_QR), bspec(A_IR), bspec(D_REC_K),
        ],
        out_specs=[pl.BlockSpec((SEG_REC, w), lambda hb, g: (g, hb)),
                   pl.BlockSpec((SEG_REC, w), lambda hb, g: (mirror(g), hb))],
        out_shape=[out, out],
        scratch_shapes=[pltpu.VMEM(state, F32), pltpu.VMEM(state, BF16), pltpu.VMEM(state, F32),
                        pltpu.VMEM((2, HB_REC, 5, 2 * CHUNK, LANES), BF16),
                        pltpu.VMEM((2, HB_REC, 2, CHUNK, LANES), BF16),
                        pltpu.VMEM((2, HB_REC, 8, LANES), F32),
                        pltpu.VMEM((2, HB_REC, 3, CHUNK, CHUNK), F32)],
        compiler_params=pltpu.CompilerParams(
            dimension_semantics=("parallel", "arbitrary"),
            vmem_limit_bytes=_vmem_limit(nbytes)),
        name="hgrn2",
    )(lb_row, slab, slab, gates, slab, slab, gates)


def _outproj_kernel(n_first, a_ref, of_ref, ob_ref, grl_ref, grh_ref, gtl_ref, gth_ref, xa_ref, xb_ref,
                    w_ref, rg_ref, g_ref, gn_ref, h_ref, hn_ref, m_ref):
    half = D_REC_V // 2
    heads_per_chunk = KC_OUT // REC_VAL_DIM
    y = None
    for c in range(D_MODEL // KC_OUT):
        for hh in range(c * heads_per_chunk, (c + 1) * heads_per_chunk):
            cs = slice(hh * REC_VAL_DIM, (hh + 1) * REC_VAL_DIM)
            lo = hh * REC_VAL_DIM < half
            hs = cs if lo else slice(hh * REC_VAL_DIM - half, (hh + 1) * REC_VAL_DIM - half)
            rec = of_ref[:, cs].astype(F32) + ob_ref[:, cs].astype(F32)
            ms = jnp.mean(rec * rec, axis=-1, keepdims=True)
            r = (rec * lax.rsqrt(ms + RMS_EPS)) * rg_ref[:, cs]
            gr = (grl_ref if lo else grh_ref)[:, hs].astype(F32)
            gt = (gtl_ref if lo else gth_ref)[:, hs].astype(F32)
            r = _sigmoid(gt) * (r * (gr * _sigmoid(gr)))
            m_ref[:, cs] = (a_ref[:, cs].astype(F32) + r).astype(BF16)
        ks = slice(c * KC_OUT, (c + 1) * KC_OUT)
        part = jnp.dot(m_ref[:, ks], w_ref[ks, :], preferred_element_type=F32)
        y = part if y is None else y + part
    ms = jnp.mean(y * y, axis=-1, keepdims=True)
    h_ref[...] = (y * lax.rsqrt(ms + RMS_EPS)) * g_ref[...]

    @pl.when(pl.program_id(0) < n_first)
    def _():
        h_ref[...] += xa_ref[...]

    @pl.when(pl.program_id(0) >= n_first)
    def _():
        h_ref[...] += xb_ref[...]

    h = h_ref[...]
    ms = jnp.mean(h * h, axis=-1, keepdims=True)
    hn_ref[...] = ((h * lax.rsqrt(ms + RMS_EPS)) * gn_ref[...]).astype(BF16)


def _out_projection(attn, rec_f, rec_b, slab, xa, xb, w_bf16, rec_gain, gain, next_gain):
    T = attn.shape[0]
    n_first = xa.shape[0] // TM_OUT
    half = D_REC_V // 2
    assert A_GR % half == 0 and A_GT % half == 0
    tok = lambda i: (i, 0)
    nbytes = (2 * 3 * TM_OUT * D_MODEL * 2 + 2 * 4 * TM_OUT * half * 2 + 2 * 3 * TM_OUT * D_MODEL * 4
              + 2 * D_MODEL * D_MODEL * 2 + TM_OUT * D_MODEL * 2 + 2 * TM_OUT * D_MODEL * 4)
    return pl.pallas_call(
        functools.partial(_outproj_kernel, n_first),
        grid=(T // TM_OUT,),
        in_specs=[
            pl.BlockSpec((TM_OUT, D_MODEL), tok),
            pl.BlockSpec((TM_OUT, D_REC_V), tok),
            pl.BlockSpec((TM_OUT, D_REC_V), tok),
            pl.BlockSpec((TM_OUT, half), lambda i: (i, A_GR // half)),
            pl.BlockSpec((TM_OUT, half), lambda i: (i, A_GR // half + 1)),
            pl.BlockSpec((TM_OUT, half), lambda i: (i, A_GT // half)),
            pl.BlockSpec((TM_OUT, half), lambda i: (i, A_GT // half + 1)),
            *_two_source_specs(TM_OUT, n_first),
            pl.BlockSpec((D_MODEL, D_MODEL), lambda i: (0, 0)),
            pl.BlockSpec((1, D_REC_V), lambda i: (0, 0)),
            pl.BlockSpec((1, D_MODEL), lambda i: (0, 0)),
            pl.BlockSpec((1, D_MODEL), lambda i: (0, 0)),
        ],
        out_specs=[pl.BlockSpec((TM_OUT, D_MODEL), tok), pl.BlockSpec((TM_OUT, D_MODEL), tok)],
        out_shape=[jax.ShapeDtypeStruct((T, D_MODEL), F32), jax.ShapeDtypeStruct((T, D_MODEL), BF16)],
        scratch_shapes=[pltpu.VMEM((TM_OUT, D_MODEL), BF16)],
        compiler_params=pltpu.CompilerParams(
            dimension_semantics=("parallel",),
            vmem_limit_bytes=_vmem_limit(nbytes)),
        name="out_projection",
    )(attn, rec_f, rec_b, slab, slab, slab, slab, xa, xb, w_bf16, rec_gain, gain, next_gain)


def _ffn_kernel(n_first, h_ref, hn_ref, gpost_ref, wg_ref, wu_ref, wd_ref, oa_ref, ob_ref, acc_ref):
    i = pl.program_id(0)
    j = pl.program_id(1)

    @pl.when(j == 0)
    def _():
        def body(rows):
            acc_ref[rows, :] = jnp.zeros((ROWS_EW, D_MODEL), F32)
        _row_tiles(TM_FFN, body)

    hn = hn_ref[...]
    g = jnp.dot(hn, wg_ref[...], preferred_element_type=F32)
    u = jnp.dot(hn, wu_ref[...], preferred_element_type=F32)
    act = ((g * _sigmoid(g)) * u).astype(BF16)
    acc_ref[...] += jnp.dot(act, wd_ref[...].astype(BF16), preferred_element_type=F32)

    def finish(o_ref):
        def body(rows):
            y = acc_ref[rows, :]
            ms = jnp.mean(y * y, axis=-1, keepdims=True)
            o_ref[rows, :] = h_ref[rows, :] + (y * lax.rsqrt(ms + RMS_EPS)) * gpost_ref[...]
        _row_tiles(TM_FFN, body)

    last = j == pl.num_programs(1) - 1

    @pl.when(jnp.logical_and(last, i < n_first))
    def _():
        finish(oa_ref)

    @pl.when(jnp.logical_and(last, i >= n_first))
    def _():
        finish(ob_ref)


def _ffn(h, hn, n_tokens_first, gpost, wg, wu, wd):
    T = h.shape[0]
    n_first = n_tokens_first // TM_FFN
    nbytes = (2 * 3 * TM_FFN * D_MODEL * 4 + 2 * TM_FFN * D_MODEL * 2 + 2 * 4 * D_MODEL * TF_FFN * 2
              + 3 * TM_FFN * TF_FFN * 4 + 2 * TM_FFN * D_MODEL * 4)
    return pl.pallas_call(
        functools.partial(_ffn_kernel, n_first),
        grid=(T // TM_FFN, D_FF // TF_FFN),
        in_specs=[
            pl.BlockSpec((TM_FFN, D_MODEL), lambda i, j: (i, 0)),
            pl.BlockSpec((TM_FFN, D_MODEL), lambda i, j: (i, 0)),
            pl.BlockSpec((1, D_MODEL), lambda i, j: (0, 0)),
            pl.BlockSpec((D_MODEL, TF_FFN), lambda i, j: (0, j)),
            pl.BlockSpec((D_MODEL, TF_FFN), lambda i, j: (0, j)),
            pl.BlockSpec((TF_FFN, D_MODEL), lambda i, j: (j, 0)),
        ],
        out_specs=list(_two_source_specs(TM_FFN, n_first)),
        out_shape=[jax.ShapeDtypeStruct((n_tokens_first, D_MODEL), F32),
                   jax.ShapeDtypeStruct((T - n_tokens_first, D_MODEL), F32)],
        scratch_shapes=[pltpu.VMEM((TM_FFN, D_MODEL), F32)],
        compiler_params=pltpu.CompilerParams(
            dimension_semantics=("arbitrary", "arbitrary"),
            vmem_limit_bytes=_vmem_limit(nbytes)),
        name="swiglu_ffn",
    )(h, hn, gpost, wg, wu, wd)


def _rope_tables(max_len):
    pos = jnp.arange(max_len, dtype=F32)
    inv_freq = ROPE_THETA ** (-jnp.arange(ROPE_HALF, dtype=F32) / ROPE_HALF)
    ang = pos[:, None] * inv_freq[None, :]
    cos, sin = jnp.cos(ang), jnp.sin(ang)
    rest = HEAD_DIM - ROPE_DIM
    cos_t = jnp.concatenate([cos, cos, jnp.ones((max_len, rest), F32)], axis=1)
    sa_t = jnp.concatenate([-sin, jnp.zeros((max_len, HEAD_DIM - ROPE_HALF), F32)], axis=1)
    sb_t = jnp.concatenate([jnp.zeros((max_len, ROPE_HALF), F32), sin, jnp.zeros((max_len, rest), F32)], axis=1)
    return cos_t, sa_t, sb_t


def _seq_groups(seq_shapes, seg):
    groups, first = [], 0
    for B, L in seq_shapes:
        assert L % seg == 0
        groups.append((first, L // seg))
        first += B * L // seg
    return tuple(groups)


def _encoder_layer(xa, xb, seq_shapes, tables, w_in, sink, rec_norm, lb, w_out, norm_mix_pre, norm_mix_post,
                   norm_ffn_pre, norm_ffn_post, w_gate, w_up, w_down):
    row = lambda v: v.astype(F32).reshape(1, -1)
    xn = _stream_norm(xa, xb, row(norm_mix_pre))

    assert all(off % TN_IN == 0 and w % TN_IN == 0 for off, w in SLAB_GROUPS) and OFF_ZF % TN_IN == 0
    gate_tiles = list(range(OFF_ZF // TN_IN, OFF_IR // TN_IN))
    slab_tiles = [t for off, w in SLAB_GROUPS for t in range(off // TN_IN, (off + w) // TN_IN)]
    kind_of = lambda t: ("rope" if t < OFF_K // TN_IN else "rope_k" if t == OFF_K // TN_IN
                         else "silu" if OFF_QR // TN_IN <= t < OFF_ZF // TN_IN else "plain")
    table_groups = _seq_groups(seq_shapes, TM_IN)
    slab = _in_projection(xn, w_in, tables, table_groups, slab_tiles, tuple(kind_of(t) for t in slab_tiles),
                          BF16, "in_projection")
    gates = _in_projection(xn, w_in, None, None, gate_tiles, ("plain",) * len(gate_tiles),
                           F32, "gate_projection", w_scale=-LOG2E)

    attn = _window_attention(slab, sink.astype(F32), _seq_groups(seq_shapes, TQ_ATT))
    rec_f, rec_b = _hgrn2(slab, gates, row(lb), _seq_groups(seq_shapes, SEG_REC))
    h, hn = _out_projection(attn, rec_f, rec_b, slab, xa, xb, w_out.astype(BF16), row(rec_norm),
                            row(norm_mix_post), row(norm_ffn_pre))
    return _ffn(h, hn, xa.shape[0], row(norm_ffn_post),
                w_gate.astype(BF16), w_up.astype(BF16), w_down)


def kernel(x_prompt, x_sample, w_in, sink, rec_norm, lb_logits, w_out, norm_mix_pre, norm_mix_post,
           norm_ffn_pre, norm_ffn_post, w_gate, w_up, w_down):
    lb_all = jnp.cumsum(jax.nn.softmax(lb_logits.astype(F32), axis=0), axis=0)
    seq_shapes = (x_prompt.shape[:2], x_sample.shape[:2])
    xa = x_prompt.reshape(-1, D_MODEL)
    xb = x_sample.reshape(-1, D_MODEL)
    tables = _rope_tables(max(L for _, L in seq_shapes))
    for l in range(DEPTH):
        xa, xb = _encoder_layer(xa, xb, seq_shapes, tables, w_in[l], sink[l], rec_norm[l], lb_all[l], w_out[l],
                                norm_mix_pre[l], norm_mix_post[l], norm_ffn_pre[l], norm_ffn_post[l],
                                w_gate[l], w_up[l], w_down[l])
    return (xa.reshape(x_prompt.shape), xb.reshape(x_sample.shape))
```
